```python
import math
import jax
import jax.numpy as jnp
from jax import lax
import numpy as np

D_MODEL = 1024
BATCH = 8
SEQ = 4096
DEPTH = 2

SSD_HEADS = 8
SSD_HEAD_DIM = 64
SSD_INNER = SSD_HEADS * SSD_HEAD_DIM
SSD_GROUPS = 2
SSD_STATE = 64
SSD_CONV = 4
SSD_CHUNK = 128
CONV_CH = SSD_INNER + 2 * SSD_GROUPS * SSD_STATE
ATT_HEADS = 8
ATT_HEAD_DIM = 64
ATT_INNER = ATT_HEADS * ATT_HEAD_DIM
MOBA_BLOCK = 256
MOBA_TOPK = 3
MOBA_Q_CHUNK = 16
REL_BUCKETS = 32
REL_MAX_DIST = 128
MIX_WIDTH = SSD_INNER + ATT_INNER
IN_COLS = SSD_INNER + CONV_CH + SSD_HEADS + 3 * ATT_INNER
MOE_GROUPS = 4
MOE_EXPERTS_PER_GROUP = 8
MOE_EXPERTS = MOE_GROUPS * MOE_EXPERTS_PER_GROUP
MOE_TOPK = 2
MOE_HIDDEN = 512
MOE_BLOCK_ROWS = 128
EPS = 1e-6

kernel_name = "hymba_ssd_moba_hiermoe"


def rmsnorm(x, g):
    xf = x.astype(jnp.float32)
    y = xf * lax.rsqrt(jnp.mean(xf * xf, axis=-1, keepdims=True) + EPS)
    return (y * g.astype(jnp.float32)).astype(x.dtype)


def t5_bucket(dist):
    dist = jnp.maximum(dist, 0)
    max_exact = REL_BUCKETS // 2
    d = jnp.maximum(dist, max_exact).astype(jnp.float32)
    large = max_exact + (jnp.log(d / max_exact) / math.log(REL_MAX_DIST / max_exact)
                         * (REL_BUCKETS - max_exact)).astype(jnp.int32)
    large = jnp.minimum(large, REL_BUCKETS - 1)
    return jnp.where(dist < max_exact, dist, large)


def causal_dwconv(u, w, b):
    out = lax.conv_general_dilated(
        u, w[:, None, :].astype(u.dtype), window_strides=(1,),
        padding=[(SSD_CONV - 1, 0)], dimension_numbers=('NWC', 'WIO', 'NWC'),
        feature_group_count=u.shape[-1])
    return out + b.astype(u.dtype)


def ssd_mixer(z, xbc, dt_raw, conv_w, conv_b, dt_bias, a_log, d_skip, norm_g):
    f32 = jnp.float32
    bsz, seq, _ = z.shape
    G, E, P, N, L = SSD_GROUPS, SSD_HEADS // SSD_GROUPS, SSD_HEAD_DIM, SSD_STATE, SSD_CHUNK
    nc = seq // L
    xbc = jax.nn.silu(causal_dwconv(xbc, conv_w, conv_b)).astype(f32)
    xs = xbc[..., :SSD_INNER].reshape(bsz, seq, SSD_HEADS, P)
    b_in = xbc[..., SSD_INNER:SSD_INNER + G * N].reshape(bsz, nc, L, G, N)
    c_out = xbc[..., SSD_INNER + G * N:].reshape(bsz, nc, L, G, N)
    dt = jax.nn.softplus(dt_raw.astype(f32) + dt_bias.astype(f32))
    a_dt = dt * (-jnp.exp(a_log.astype(f32)))
    xdt = (xs * dt[..., None]).reshape(bsz, nc, L, G, E, P)
    a_cs = jnp.cumsum(a_dt.reshape(bsz, nc, L, G, E).transpose(0, 3, 4, 1, 2), axis=-1)
    causal = jnp.tril(jnp.ones((L, L), dtype=bool))
    l_mat = jnp.exp(jnp.where(causal, a_cs[..., :, None] - a_cs[..., None, :], -jnp.inf))
    cb = jnp.einsum('bclgn,bcsgn->bcgls', c_out, b_in)
    y_diag = jnp.einsum('bcgls,bgecls,bcsgep->bclgep', cb, l_mat, xdt)
    decay_states = jnp.exp(a_cs[..., -1:] - a_cs)
    states = jnp.einsum('bclgn,bgecl,bclgep->bcgepn', b_in, decay_states, xdt)
    chunk_decay = jnp.exp(a_cs[..., -1])

    def step(h, inp):
        st, dec = inp
        return h * dec[..., None, None] + st, h

    h0 = jnp.zeros((bsz, G, E, P, N), f32)
    _, h_in = lax.scan(step, h0, (jnp.moveaxis(states, 1, 0), jnp.moveaxis(chunk_decay, -1, 0)))
    h_in = jnp.moveaxis(h_in, 0, 1)
    y_off = jnp.einsum('bclgn,bcgepn,bgecl->bclgep', c_out, h_in, jnp.exp(a_cs))
    y = (y_diag + y_off).reshape(bsz, seq, SSD_HEADS, P) + d_skip.astype(f32)[:, None] * xs
    y = y.reshape(bsz, seq, SSD_INNER) * jax.nn.silu(z.astype(f32))
    return rmsnorm(y, norm_g).astype(z.dtype)


def moba_attention(q, k, v, q_g, k_g, rel_bias):
    f32 = jnp.float32
    bsz, seq, _ = q.shape
    H, hd, BS, QC = ATT_HEADS, ATT_HEAD_DIM, MOBA_BLOCK, MOBA_Q_CHUNK
    q = rmsnorm(q.reshape(bsz, seq, H, hd), q_g).transpose(0, 2, 1, 3)
    k = rmsnorm(k.reshape(bsz, seq, H, hd), k_g).transpose(0, 2, 1, 3)
    v = v.reshape(bsz, seq, H, hd).transpose(0, 2, 1, 3)
    s_pad = -(-seq // BS) * BS
    pad = [(0, 0), (0, 0), (0, s_pad - seq), (0, 0)]
    q, k, v = jnp.pad(q, pad), jnp.pad(k, pad), jnp.pad(v, pad)
    nb = s_pad // BS
    topk = min(MOBA_TOPK, nb)
    kb = k.reshape(bsz, H, nb, BS, hd)
    vb = v.reshape(bsz, H, nb, BS, hd)
    kmean = jnp.mean(kb.astype(f32), axis=3)
    gate = jnp.einsum('bhsd,bhnd->bhsn', q.astype(f32), kmean)
    past = jnp.arange(nb)[None, :] < (jnp.arange(s_pad) // BS)[:, None]
    gate = jnp.where(past, gate, -jnp.inf)
    _, sel = lax.top_k(gate, topk)
    scale = ATT_HEAD_DIM ** -0.5
    bi = jnp.arange(bsz)[:, None, None, None]
    hi = jnp.arange(H)[None, :, None, None]

    def chunk(c):
        q0 = c * QC
        qpos = q0 + jnp.arange(QC)
        qc = lax.dynamic_slice_in_dim(q, q0, QC, axis=2)
        selc = lax.dynamic_slice_in_dim(sel, q0, QC, axis=2)
        ksel = kb[bi, hi, selc]
        vsel = vb[bi, hi, selc]
        kpos = selc[..., None] * BS + jnp.arange(BS)
        dist = qpos[:, None, None] - kpos
        valid = jnp.arange(topk)[None, :] < (qpos // BS)[:, None]
        s_sel = jnp.einsum('bhqd,bhqknd->bhqkn', qc, ksel, preferred_element_type=f32) * scale
        s_sel = s_sel + rel_bias[t5_bucket(dist), hi[..., None]].astype(f32)
        s_sel = jnp.where(valid[:, :, None], s_sel, -jnp.inf)
        ob = q0 // BS
        kown = lax.dynamic_index_in_dim(kb, ob, axis=2, keepdims=False)
        vown = lax.dynamic_index_in_dim(vb, ob, axis=2, keepdims=False)
        dist_own = qpos[:, None] - (ob * BS + jnp.arange(BS))[None, :]
        s_own = jnp.einsum('bhqd,bhkd->bhqk', qc, kown, preferred_element_type=f32) * scale
        s_own = s_own + rel_bias[t5_bucket(dist_own)].transpose(2, 0, 1).astype(f32)
        s_own = jnp.where(dist_own >= 0, s_own, -jnp.inf)
        logits = jnp.concatenate([s_sel.reshape(bsz, H, QC, topk * BS), s_own], axis=-1)
        p = jax.nn.softmax(logits, axis=-1)
        p_sel = p[..., :topk * BS].reshape(bsz, H, QC, topk, BS).astype(v.dtype)
        p_own = p[..., topk * BS:].astype(v.dtype)
        return (jnp.einsum('bhqkn,bhqknd->bhqd', p_sel, vsel)
                + jnp.einsum('bhqk,bhkd->bhqd', p_own, vown))

    out = lax.map(chunk, jnp.arange(s_pad // QC))
    out = out.transpose(1, 2, 0, 3, 4).reshape(bsz, H, s_pad, hd)[:, :, :seq]
    return out.transpose(0, 2, 1, 3).reshape(bsz, seq, ATT_INNER)


def hier_moe(xn, w_r1, b_r1, w_r2, b_r2, w_gate, w_up, w_down):
    f32 = jnp.float32
    bsz, seq, d = xn.shape
    T = bsz * seq
    xt = xn.reshape(T, d)
    p1 = jax.nn.softmax(jnp.matmul(xt, w_r1).astype(f32) + b_r1.astype(f32), axis=-1)
    gval, gidx = lax.top_k(p1, 1)
    lg2 = (jnp.matmul(xt, w_r2).astype(f32) + b_r2.astype(f32)).reshape(T, MOE_GROUPS, MOE_EXPERTS_PER_GROUP)
    lg2 = jnp.take_along_axis(lg2, gidx[:, :, None], axis=1)[:, 0]
    ev, eidx = lax.top_k(jax.nn.softmax(lg2, axis=-1), MOE_TOPK)
    gates = gval * (ev / jnp.sum(ev, axis=-1, keepdims=True))
    experts = gidx * MOE_EXPERTS_PER_GROUP + eidx
    M = MOE_BLOCK_ROWS
    A = T * MOE_TOPK
    flat_e = experts.reshape(-1)
    flat_tok = jnp.arange(A) // MOE_TOPK
    flat_gate = gates.reshape(-1)
    order = jnp.argsort(flat_e)
    se, st, sg = flat_e[order], flat_tok[order], flat_gate[order]
    counts = jnp.bincount(flat_e, length=MOE_EXPERTS)
    starts = jnp.cumsum(counts) - counts
    padded = (counts + M - 1) // M * M
    ends = jnp.cumsum(padded)
    pstarts = ends - padded
    dest = pstarts[se] + jnp.arange(A) - starts[se]
    n_blk = -(-A // M) + MOE_EXPERTS
    P = n_blk * M
    slot_tok = jnp.full((P,), T, jnp.int32).at[dest].set(st)
    slot_gate = jnp.zeros((P,), f32).at[dest].set(sg)
    blk_exp = jnp.minimum(jnp.searchsorted(ends, jnp.arange(n_blk) * M, side='right'), MOE_EXPERTS - 1)
    xpad = jnp.concatenate([xt, jnp.zeros((1, d), xt.dtype)], axis=0)
    xb = xpad[slot_tok].reshape(n_blk, M, d)

    def expert_block(args):
        xblk, e = args
        hmid = jax.nn.silu(xblk @ w_gate[e]) * (xblk @ w_up[e])
        return hmid @ w_down[e]

    yb = lax.map(expert_block, (xb, blk_exp)).reshape(P, d)
    out = jnp.zeros((T + 1, d), f32).at[slot_tok].add(yb.astype(f32) * slot_gate[:, None])[:T]
    return out.reshape(bsz, seq, d).astype(xn.dtype)


def setup_inputs(seed: int = 0) -> dict:
    key = jax.random.key(seed)
    ks = jax.random.split(key, 24)
    nrm = jax.random.normal
    D = D_MODEL
    dt = jnp.exp(jax.random.uniform(ks[6], (DEPTH, SSD_HEADS)) * (math.log(0.1) - math.log(0.001)) + math.log(0.001))
    return {
        "x": nrm(ks[0], (BATCH, SEQ, D), jnp.float32),
        "rel_bias": 0.5 * nrm(ks[1], (REL_BUCKETS, ATT_HEADS), jnp.float32),
        "norm1_g": 1.0 + 0.01 * nrm(ks[2], (DEPTH, D), jnp.float32),
        "w_in": nrm(ks[3], (DEPTH, D, IN_COLS), jnp.float32) * D ** -0.5,
        "conv_w": nrm(ks[4], (DEPTH, SSD_CONV, CONV_CH), jnp.float32) * SSD_CONV ** -0.5,
        "conv_b": 0.01 * nrm(ks[5], (DEPTH, CONV_CH), jnp.float32),
        "dt_bias": dt + jnp.log(-jnp.expm1(-dt)),
        "a_log": jnp.log(jax.random.uniform(ks[7], (DEPTH, SSD_HEADS), minval=1.0, maxval=16.0)),
        "d_skip": 1.0 + 0.01 * nrm(ks[8], (DEPTH, SSD_HEADS), jnp.float32),
        "ssd_norm_g": 1.0 + 0.01 * nrm(ks[9], (DEPTH, SSD_INNER), jnp.float32),
        "q_norm_g": 1.0 + 0.01 * nrm(ks[10], (DEPTH, ATT_HEAD_DIM), jnp.float32),
        "k_norm_g": 1.0 + 0.01 * nrm(ks[11], (DEPTH, ATT_HEAD_DIM), jnp.float32),
        "w_out": nrm(ks[12], (DEPTH, MIX_WIDTH, D), jnp.float32) * MIX_WIDTH ** -0.5,
        "norm2_g": 1.0 + 0.01 * nrm(ks[13], (DEPTH, D), jnp.float32),
        "w_r1": nrm(ks[14], (DEPTH, D, MOE_GROUPS), jnp.float32) * D ** -0.5,
        "b_r1": 0.01 * nrm(ks[15], (DEPTH, MOE_GROUPS), jnp.float32),
        "w_r2": nrm(ks[16], (DEPTH, D, MOE_EXPERTS), jnp.float32) * D ** -0.5,
        "b_r2": 0.01 * nrm(ks[17], (DEPTH, MOE_EXPERTS), jnp.float32),
        "w_gate": nrm(ks[18], (DEPTH, MOE_EXPERTS, D, MOE_HIDDEN), jnp.float32) * D ** -0.5,
        "w_up": nrm(ks[19], (DEPTH, MOE_EXPERTS, D, MOE_HIDDEN), jnp.float32) * D ** -0.5,
        "w_down": nrm(ks[20], (DEPTH, MOE_EXPERTS, MOE_HIDDEN, D), jnp.float32) * MOE_HIDDEN ** -0.5,
    }


def reference(x, rel_bias, norm1_g, w_in, conv_w, conv_b, dt_bias, a_log, d_skip, ssd_norm_g,
              q_norm_g, k_norm_g, w_out, norm2_g, w_r1, b_r1, w_r2, b_r2, w_gate, w_up, w_down):
    cuts = [SSD_INNER, SSD_INNER + CONV_CH, SSD_INNER + CONV_CH + SSD_HEADS,
            SSD_INNER + CONV_CH + SSD_HEADS + ATT_INNER,
            SSD_INNER + CONV_CH + SSD_HEADS + 2 * ATT_INNER]
    h = x
    for l in range(DEPTH):
        u = jnp.matmul(rmsnorm(h, norm1_g[l]), w_in[l])
        z, xbc, dt_raw, q, k, v = jnp.split(u, cuts, axis=-1)
        y_ssd = ssd_mixer(z, xbc, dt_raw, conv_w[l], conv_b[l], dt_bias[l], a_log[l],
                          d_skip[l], ssd_norm_g[l])
        y_att = moba_attention(q, k, v, q_norm_g[l], k_norm_g[l], rel_bias)
        h = h + jnp.matmul(jnp.concatenate([y_ssd, y_att], axis=-1), w_out[l])
        h = h + hier_moe(rmsnorm(h, norm2_g[l]), w_r1[l], b_r1[l], w_r2[l], b_r2[l],
                         w_gate[l], w_up[l], w_down[l])
    return h
```

```python
import functools
import math

import numpy as np
import jax
import jax.numpy as jnp
from jax import lax
from jax.experimental import pallas as pl
from jax.experimental.pallas import tpu as pltpu

F32 = jnp.float32
BF16 = jnp.bfloat16
HIGHEST = lax.Precision.HIGHEST

LANES = 128
SUBLANES = 8

D_MODEL = 1024
SSD_HEADS = 8
SSD_HEAD_DIM = 64
SSD_INNER = SSD_HEADS * SSD_HEAD_DIM
SSD_GROUPS = 2
SSD_HEADS_PER_GROUP = SSD_HEADS // SSD_GROUPS
SSD_STATE = 64
SSD_CONV = 4
SSD_CHUNK = 128
CONV_CH = SSD_INNER + 2 * SSD_GROUPS * SSD_STATE
ATT_HEADS = 8
ATT_HEAD_DIM = 64
ATT_INNER = ATT_HEADS * ATT_HEAD_DIM
MOBA_BLOCK = 256
MOBA_TOPK = 3
REL_BUCKETS = 32
REL_MAX_DIST = 128
MOE_GROUPS = 4
MOE_EXPERTS_PER_GROUP = 8
MOE_EXPERTS = MOE_GROUPS * MOE_EXPERTS_PER_GROUP
MOE_HIDDEN = 512
EPS = 1e-6

MASK_VALUE = -1e30
IN_ROWS = 512
OUT_ROWS = 256
EXPERT_ROWS = 256
MOVE_ROWS = 256
VMEM_LIMIT = 56 * 1024 * 1024
ROUTER_LANE0 = MOE_GROUPS


def _silu(x):
    return x * (1.0 / (1.0 + jnp.exp(-x)))


def _softplus(x):
    return jnp.maximum(x, 0.0) + jnp.log(1.0 + jnp.exp(-jnp.abs(x)))


def _params(*sem):
    return pltpu.CompilerParams(dimension_semantics=sem, vmem_limit_bytes=VMEM_LIMIT)


def _inproj_body(h_ref, g_ref, wz_ref, wx_ref, wdt_ref, wdtt_ref, wq_ref, wk_ref, wv_ref,
                 z_ref, xbc_ref, dt_ref, dtt_ref, q_ref, k_ref, v_ref):
    x = h_ref[...]
    xn = x * lax.rsqrt(jnp.mean(x * x, axis=-1, keepdims=True) + EPS) * g_ref[...]
    xb = xn.astype(BF16)
    z_ref[...] = jnp.dot(xb, wz_ref[...], preferred_element_type=F32)
    xbc_ref[...] = jnp.dot(xb, wx_ref[...], preferred_element_type=F32)
    dt_ref[...] = jnp.dot(xb, wdt_ref[...], preferred_element_type=F32)
    dtt_ref[...] = lax.dot_general(wdtt_ref[...], xb, (((1,), (1,)), ((), ())),
                                   preferred_element_type=F32)
    q_ref[...] = jnp.dot(xb, wq_ref[...], preferred_element_type=F32)
    k_ref[...] = jnp.dot(xb, wk_ref[...], preferred_element_type=F32)
    v_ref[...] = jnp.dot(xb, wv_ref[...], preferred_element_type=F32)


def _inproj(h2, g, w_in):
    T = h2.shape[0]
    c0, c1, c2 = SSD_INNER, SSD_INNER + CONV_CH, SSD_INNER + CONV_CH + SSD_HEADS
    wb = w_in.astype(BF16)
    wz, wx, wdt = wb[:, :c0], wb[:, c0:c1], wb[:, c1:c2]
    wq, wk, wv = (wb[:, c2 + i * ATT_INNER:c2 + (i + 1) * ATT_INNER] for i in range(3))
    wdt_pad = jnp.pad(wdt, ((0, 0), (0, LANES - SSD_HEADS)))
    wdtt = wdt.T
    full = lambda a: pl.BlockSpec(a.shape, lambda i: (0,) * a.ndim)
    rows = lambda n: pl.BlockSpec((IN_ROWS, n), lambda i: (i, 0))
    ins = (h2, g.reshape(1, D_MODEL), wz, wx, wdt_pad, wdtt, wq, wk, wv)
    out_shape = (
        jax.ShapeDtypeStruct((T, SSD_INNER), F32), jax.ShapeDtypeStruct((T, CONV_CH), F32),
        jax.ShapeDtypeStruct((T, LANES), F32), jax.ShapeDtypeStruct((SSD_HEADS, T), F32),
        jax.ShapeDtypeStruct((T, ATT_INNER), F32), jax.ShapeDtypeStruct((T, ATT_INNER), F32),
        jax.ShapeDtypeStruct((T, ATT_INNER), F32))
    out_specs = (rows(SSD_INNER), rows(CONV_CH), rows(LANES),
                 pl.BlockSpec((SSD_HEADS, IN_ROWS), lambda i: (0, i)),
                 rows(ATT_INNER), rows(ATT_INNER), rows(ATT_INNER))
    return pl.pallas_call(
        _inproj_body, grid=(T // IN_ROWS,),
        in_specs=[rows(D_MODEL)] + [full(a) for a in ins[1:]],
        out_specs=out_specs, out_shape=out_shape,
        compiler_params=_params("arbitrary"), name="inproj")(*ins)


def _ssd_body(z_ref, xbc_ref, dt_ref, dtt_ref, cw_ref, cb_ref, dtb_ref, dtbt_ref, alog_ref,
              alogt_ref, dskip_ref, ng_ref, tril_ref, expand_ref, y_ref, ext_ref, state_ref):
    L, P, N, E = SSD_CHUNK, SSD_HEAD_DIM, SSD_STATE, SSD_HEADS_PER_GROUP
    c = pl.program_id(1)

    @pl.when(c == 0)
    def _():
        state_ref[...] = jnp.zeros_like(state_ref)
        ext_ref[0:SUBLANES, :] = jnp.zeros((SUBLANES, CONV_CH), F32)

    ext_ref[SUBLANES:SUBLANES + L, :] = xbc_ref[0]
    conv = jnp.zeros((L, CONV_CH), F32) + cb_ref[...]
    for j in range(SSD_CONV):
        off = SUBLANES - (SSD_CONV - 1) + j
        conv = conv + cw_ref[j:j + 1, :] * ext_ref[off:off + L, :]
    ext_ref[0:SUBLANES, :] = ext_ref[L:L + SUBLANES, :]
    act = _silu(conv)
    xs = act[:, :SSD_INNER]
    b_in = act[:, SSD_INNER:SSD_INNER + SSD_GROUPS * N]
    c_out = act[:, SSD_INNER + SSD_GROUPS * N:]

    tril = tril_ref[...]
    expand = expand_ref[...]
    dt = _softplus(dt_ref[0] + dtb_ref[...])
    a_dt = dt * (-jnp.exp(alog_ref[...]))
    a_cs = jnp.dot(tril, a_dt, precision=HIGHEST, preferred_element_type=F32)
    dt_t = _softplus(dtt_ref[...] + dtbt_ref[...])
    a_dt_t = dt_t * (-jnp.exp(alogt_ref[...]))
    a_cs_t = lax.dot_general(a_dt_t, tril, (((1,), (1,)), ((), ())), precision=HIGHEST,
                             preferred_element_type=F32)
    last = a_cs[L - 1:L, :]
    hdot = lambda a: jnp.dot(a, expand, precision=HIGHEST, preferred_element_type=F32)
    dt_x = hdot(dt)
    eacs_x = hdot(jnp.exp(a_cs))
    dec_x = hdot(jnp.exp(last - a_cs))
    cdec_x = hdot(jnp.broadcast_to(jnp.exp(last), (SUBLANES, LANES)))[0:1, :]
    xdt = xs * dt_x
    xdtd = (xdt * dec_x).astype(BF16)
    xdt_b = xdt.astype(BF16)
    row = lax.broadcasted_iota(jnp.int32, (L, L), 0)
    col = lax.broadcasted_iota(jnp.int32, (L, L), 1)
    causal = row >= col

    y_parts = []
    for g in range(SSD_GROUPS):
        bg = b_in[:, g * N:(g + 1) * N].astype(BF16)
        cg = c_out[:, g * N:(g + 1) * N].astype(BF16)
        cb = lax.dot_general(cg, bg, (((1,), (1,)), ((), ())), preferred_element_type=F32)
        st = state_ref[g]
        gs = slice(g * E * P, (g + 1) * E * P)
        y_off = jnp.dot(cg, st.astype(BF16), preferred_element_type=F32) * eacs_x[:, gs]
        for e in range(E):
            h = g * E + e
            diff = a_cs[:, h:h + 1] - a_cs_t[h:h + 1, :]
            l_mat = jnp.exp(jnp.where(causal, diff, -jnp.inf))
            m = (cb * l_mat).astype(BF16)
            y_diag = jnp.dot(m, xdt_b[:, h * P:(h + 1) * P], preferred_element_type=F32)
            y_parts.append(y_diag + y_off[:, e * P:(e + 1) * P])
        upd = lax.dot_general(bg, xdtd[:, gs], (((0,), (0,)), ((), ())),
                              preferred_element_type=F32)
        state_ref[g] = cdec_x[:, gs] * st + upd
    y = jnp.concatenate(y_parts, axis=-1) + dskip_ref[...] * xs
    y = y * _silu(z_ref[0])
    y = y * lax.rsqrt(jnp.mean(y * y, axis=-1, keepdims=True) + EPS) * ng_ref[...]
    y_ref[0] = y


def _ssd(z, xbc, dt, dtt, conv_w, conv_b, dt_bias, a_log, d_skip, norm_g):
    B, S, _ = z.shape
    L = SSD_CHUNK
    nc = S // L
    pad_row = lambda a: jnp.pad(a.reshape(1, SSD_HEADS), ((0, 0), (0, LANES - SSD_HEADS)))
    tril = jnp.asarray(np.tril(np.ones((L, L), np.float32)))
    expand_np = np.zeros((LANES, SSD_INNER), np.float32)
    for h in range(SSD_HEADS):
        expand_np[h, h * SSD_HEAD_DIM:(h + 1) * SSD_HEAD_DIM] = 1.0
    ins = (z, xbc, dt, dtt, conv_w, conv_b.reshape(1, CONV_CH), pad_row(dt_bias),
           dt_bias.reshape(SSD_HEADS, 1), pad_row(a_log), a_log.reshape(SSD_HEADS, 1),
           jnp.repeat(d_skip, SSD_HEAD_DIM).reshape(1, SSD_INNER), norm_g.reshape(1, SSD_INNER),
           tril, jnp.asarray(expand_np))
    full = lambda a: pl.BlockSpec(a.shape, lambda b, c: (0,) * a.ndim)
    chunk = lambda n: pl.BlockSpec((1, L, n), lambda b, c: (b, c, 0))
    in_specs = [chunk(SSD_INNER), chunk(CONV_CH), chunk(LANES),
                pl.BlockSpec((SSD_HEADS, L), lambda b, c: (0, b * nc + c))]
    in_specs += [full(a) for a in ins[4:]]
    return pl.pallas_call(
        _ssd_body, grid=(B, nc), in_specs=in_specs, out_specs=chunk(SSD_INNER),
        out_shape=jax.ShapeDtypeStruct((B, S, SSD_INNER), F32),
        scratch_shapes=[pltpu.VMEM((L + SUBLANES, CONV_CH), F32),
                        pltpu.VMEM((SSD_GROUPS, SSD_STATE, SSD_HEADS_PER_GROUP * SSD_HEAD_DIM), F32)],
        compiler_params=_params("arbitrary", "arbitrary"), name="ssd")(*ins)


def _bucket_tile():
    bs = MOBA_BLOCK
    dist = np.arange(bs)[:, None] - np.arange(2 * bs)[None, :] + bs
    max_exact = REL_BUCKETS // 2
    d = np.maximum(dist, max_exact).astype(np.float32)
    large = max_exact + (np.log(d / np.float32(max_exact)) / np.float32(math.log(REL_MAX_DIST / max_exact))
                         * np.float32(REL_BUCKETS - max_exact)).astype(np.int32)
    large = np.minimum(large, REL_BUCKETS - 1)
    bucket = np.where(dist < max_exact, dist, large)
    return np.where(dist >= 0, bucket, -1).astype(np.int32)


def _attn_body(rb_ref, q_ref, k_ref, v_ref, qg_ref, kg_ref, bucket_ref, o_ref,
               kaug_ref, vb_ref, kmean_ref, bias_ref):
    bs, hd = MOBA_BLOCK, ATT_HEAD_DIM
    nb = k_ref.shape[1] // bs
    hp = pl.program_id(1)
    qb = pl.program_id(2)
    scale = ATT_HEAD_DIM ** -0.5

    @pl.when(qb == 0)
    def _():
        vb_ref[...] = v_ref[0].astype(BF16)
        bucket = bucket_ref[...]
        blk = lax.broadcasted_iota(jnp.int32, (nb * bs, hd), 0) // bs
        lane = lax.broadcasted_iota(jnp.int32, (nb * bs, hd), 1)
        onehot = (blk == lane).astype(BF16)
        for h in range(2):
            kh = k_ref[0, :, h * hd:(h + 1) * hd]
            kn = kh * lax.rsqrt(jnp.mean(kh * kh, axis=-1, keepdims=True) + EPS) * kg_ref[...]
            kmean_ref[h] = jnp.mean(kn.reshape(nb, bs, hd), axis=1)
            kaug_ref[h] = jnp.concatenate([kn.astype(BF16), onehot], axis=-1)
            head = hp * 2 + h
            far = rb_ref[REL_BUCKETS - 1, head]
            bias = jnp.where(bucket < 0, MASK_VALUE, 0.0)
            for i in range(REL_BUCKETS - 1):
                bias = jnp.where(bucket == i, rb_ref[i, head] - far, bias)
            bias_ref[h] = bias

    q = q_ref[0]
    outs = []
    for h in range(2):
        qh = q[:, h * hd:(h + 1) * hd]
        qn = qh * lax.rsqrt(jnp.mean(qh * qh, axis=-1, keepdims=True) + EPS) * qg_ref[...]
        gate = lax.dot_general(qn, kmean_ref[h], (((1,), (1,)), ((), ())), precision=HIGHEST,
                               preferred_element_type=F32)
        colb = lax.broadcasted_iota(jnp.int32, (bs, nb), 1)
        past = colb < qb
        g = jnp.where(past, gate, -jnp.inf)
        rank = jnp.zeros((bs, nb), jnp.int32)
        for m in range(nb):
            gm = g[:, m:m + 1]
            beats = (gm > g) | ((gm == g) & (colb > m))
            rank = rank + beats.astype(jnp.int32)
        sel = (past & (rank < MOBA_TOPK)) | (colb == qb)
        selmask = jnp.where(sel, 0.0, MASK_VALUE)
        q_aug = jnp.concatenate(
            [qn * scale, selmask, jnp.zeros((bs, hd - nb), F32)], axis=-1).astype(BF16)

        def step(n, carry, bias):
            m_i, l_i, acc = carry
            start = pl.multiple_of(n * bs, bs)
            kblk = kaug_ref[h, pl.ds(start, bs), :]
            s = lax.dot_general(q_aug, kblk, (((1,), (1,)), ((), ())), preferred_element_type=F32)
            if bias is not None:
                s = s + bias
            m_new = jnp.maximum(m_i, jnp.max(s, axis=-1, keepdims=True))
            alpha = jnp.exp(m_i - m_new)
            p = jnp.exp(s - m_new)
            l_new = alpha * l_i + jnp.sum(p, axis=-1, keepdims=True)
            pv = jnp.dot(p.astype(BF16), vb_ref[pl.ds(start, bs), :], preferred_element_type=F32)
            return m_new, l_new, alpha * acc + pv

        init = (jnp.full((bs, 1), -jnp.inf, F32), jnp.zeros((bs, 1), F32),
                jnp.zeros((bs, 2 * hd), F32))
        carry = lax.fori_loop(0, jnp.maximum(qb - 1, 0), lambda n, c: step(n, c, None), init)
        no_prev = jnp.where(qb == 0, MASK_VALUE, 0.0)
        carry = step(jnp.maximum(qb - 1, 0), carry, bias_ref[h, :, 0:bs] + no_prev)
        _, l_i, acc = step(qb, carry, bias_ref[h, :, bs:2 * bs])
        outs.append(acc[:, h * hd:(h + 1) * hd] / l_i)
    o_ref[0] = jnp.concatenate(outs, axis=-1)


def _attention(q, k, v, q_g, k_g, rel_bias):
    B, S, _ = q.shape
    bs = MOBA_BLOCK
    nb = S // bs
    assert S % bs == 0 and nb <= ATT_HEAD_DIM
    pair = 2 * ATT_HEAD_DIM
    bucket = jnp.asarray(_bucket_tile())
    grid_spec = pltpu.PrefetchScalarGridSpec(
        num_scalar_prefetch=0, grid=(B, ATT_HEADS // 2, nb),
        in_specs=[
            pl.BlockSpec(memory_space=pltpu.SMEM),
            pl.BlockSpec((1, bs, pair), lambda b, hp, i: (b, i, hp)),
            pl.BlockSpec((1, S, pair), lambda b, hp, i: (b, 0, hp)),
            pl.BlockSpec((1, S, pair), lambda b, hp, i: (b, 0, hp)),
            pl.BlockSpec((1, ATT_HEAD_DIM), lambda b, hp, i: (0, 0)),
            pl.BlockSpec((1, ATT_HEAD_DIM), lambda b, hp, i: (0, 0)),
            pl.BlockSpec((bs, 2 * bs), lambda b, hp, i: (0, 0)),
        ],
        out_specs=pl.BlockSpec((1, bs, pair), lambda b, hp, i: (b, i, hp)),
        scratch_shapes=[pltpu.VMEM((2, S, pair), BF16), pltpu.VMEM((S, pair), BF16),
                        pltpu.VMEM((2, nb, ATT_HEAD_DIM), F32), pltpu.VMEM((2, bs, 2 * bs), F32)])
    return pl.pallas_call(
        _attn_body, grid_spec=grid_spec, out_shape=jax.ShapeDtypeStruct((B, S, ATT_INNER), F32),
        compiler_params=_params("arbitrary", "arbitrary", "arbitrary"), name="moba")(
            rel_bias, q, k, v, q_g.reshape(1, -1), k_g.reshape(1, -1), bucket)


def _outproj_body(h_ref, ys_ref, ya_ref, wa_ref, wb_ref, g_ref, wr_ref, br_ref, tri_ref,
                  hn_ref, xn_ref, route_ref, counts_ref, carry_ref):
    i = pl.program_id(0)

    @pl.when(i == 0)
    def _():
        carry_ref[...] = jnp.zeros_like(carry_ref)

    hn = (h_ref[...]
          + jnp.dot(ys_ref[...].astype(BF16), wa_ref[...], preferred_element_type=F32)
          + jnp.dot(ya_ref[...].astype(BF16), wb_ref[...], preferred_element_type=F32))
    hn_ref[...] = hn
    xn = hn * lax.rsqrt(jnp.mean(hn * hn, axis=-1, keepdims=True) + EPS) * g_ref[...]
    xn_ref[...] = xn
    logits = jnp.dot(xn, wr_ref[...], precision=HIGHEST, preferred_element_type=F32) + br_ref[...]
    rows = logits.shape[0]
    lane = lax.broadcasted_iota(jnp.int32, (rows, LANES), 1)
    first = lambda hit: jnp.min(jnp.where(hit, lane, LANES), axis=-1, keepdims=True)
    l1 = jnp.where(lane < MOE_GROUPS, logits, -jnp.inf)
    mx = jnp.max(l1, axis=-1, keepdims=True)
    gval = 1.0 / jnp.sum(jnp.exp(l1 - mx), axis=-1, keepdims=True)
    gidx = first(l1 == mx)
    e_lane = lane - ROUTER_LANE0
    in_grp = (e_lane >= 0) & (e_lane < MOE_EXPERTS) & ((e_lane >> 3) == gidx)
    l2 = jnp.where(in_grp, logits, -jnp.inf)
    m1 = jnp.max(l2, axis=-1, keepdims=True)
    i1 = first(l2 == m1)
    l2b = jnp.where(lane == i1, -jnp.inf, l2)
    m2 = jnp.max(l2b, axis=-1, keepdims=True)
    i2 = first(l2b == m2)
    r = jnp.exp(m2 - m1)
    g1 = gval / (1.0 + r)
    g2 = gval * r / (1.0 + r)
    oh1 = (lane == i1)
    oh2 = (lane == i2)
    both = jnp.where(oh1 | oh2, 1.0, 0.0)
    cx = jnp.dot(tri_ref[...], both.astype(BF16), preferred_element_type=F32) + carry_ref[...]
    rank1 = jnp.sum(jnp.where(oh1, cx, 0.0), axis=-1, keepdims=True)
    rank2 = jnp.sum(jnp.where(oh2, cx, 0.0), axis=-1, keepdims=True)
    carry_ref[...] = carry_ref[...] + jnp.sum(both, axis=0, keepdims=True)
    counts_ref[...] = carry_ref[...]
    cols = ((i1 - ROUTER_LANE0).astype(F32), (i2 - ROUTER_LANE0).astype(F32), g1, g2, rank1, rank2)
    route = jnp.zeros((rows, LANES), F32)
    for j, cval in enumerate(cols):
        route = jnp.where(lane == j, cval, route)
    route_ref[...] = route


def _outproj_route(h2, y_ssd, y_att, w_out, norm_g, w_r1, b_r1, w_r2, b_r2):
    T = h2.shape[0]
    R = OUT_ROWS
    wb = w_out.astype(BF16)
    pad = LANES - MOE_GROUPS - MOE_EXPERTS
    wr = jnp.pad(jnp.concatenate([w_r1, w_r2], axis=1), ((0, 0), (0, pad)))
    br = jnp.pad(jnp.concatenate([b_r1, b_r2]), (0, pad)).reshape(1, LANES)
    tri = jnp.asarray(np.tril(np.ones((R, R), np.float32), -1), dtype=BF16)
    ins = (h2, y_ssd, y_att, wb[:SSD_INNER], wb[SSD_INNER:], norm_g.reshape(1, D_MODEL), wr, br, tri)
    full = lambda a: pl.BlockSpec(a.shape, lambda i: (0,) * a.ndim)
    rows = lambda n: pl.BlockSpec((R, n), lambda i: (i, 0))
    return pl.pallas_call(
        _outproj_body, grid=(T // R,),
        in_specs=[rows(D_MODEL), rows(SSD_INNER), rows(ATT_INNER)] + [full(a) for a in ins[3:]],
        out_specs=(rows(D_MODEL), rows(D_MODEL), rows(LANES), pl.BlockSpec((1, LANES), lambda i: (0, 0))),
        out_shape=(jax.ShapeDtypeStruct((T, D_MODEL), F32), jax.ShapeDtypeStruct((T, D_MODEL), F32),
                   jax.ShapeDtypeStruct((T, LANES), F32), jax.ShapeDtypeStruct((1, LANES), F32)),
        scratch_shapes=[pltpu.VMEM((1, LANES), F32)],
        compiler_params=_params("arbitrary"), name="outproj_route")(*ins)


def _row_copy(src_ref, src_row, dst_ref, dst_row, sem):
    return pltpu.make_async_copy(src_ref.at[pl.ds(src_row, 1), :], dst_ref.at[pl.ds(dst_row, 1), :], sem)


def _dispatch_body(d1_ref, d2_ref, xn_ref, xb_in_ref, xb_ref, sem):
    del xb_in_ref
    base = pl.program_id(0) * MOVE_ROWS

    def start(t, _):
        _row_copy(xn_ref, base + t, xb_ref, d1_ref[t], sem).start()
        _row_copy(xn_ref, base + t, xb_ref, d2_ref[t], sem).start()
        return 0

    def wait(t, _):
        _row_copy(xn_ref, base + t, xb_ref, d1_ref[t], sem).wait()
        _row_copy(xn_ref, base + t, xb_ref, d2_ref[t], sem).wait()
        return 0

    lax.fori_loop(0, MOVE_ROWS, start, 0)
    lax.fori_loop(0, MOVE_ROWS, wait, 0)


def _dispatch(xn, dest1, dest2, n_slots):
    T = xn.shape[0]
    idx = pl.BlockSpec((MOVE_ROWS,), lambda i: (i,), memory_space=pltpu.SMEM)
    anyspec = pl.BlockSpec(memory_space=pl.ANY)
    return pl.pallas_call(
        _dispatch_body, grid=(T // MOVE_ROWS,),
        in_specs=[idx, idx, anyspec, anyspec], out_specs=anyspec,
        out_shape=jax.ShapeDtypeStruct((n_slots, D_MODEL), F32),
        scratch_shapes=[pltpu.SemaphoreType.DMA(())],
        input_output_aliases={3: 0},
        compiler_params=_params("arbitrary"), name="dispatch")(
            dest1, dest2, xn, jnp.zeros((n_slots, D_MODEL), F32))


def _expert_body(be_ref, nused_ref, xb_ref, wg_ref, wu_ref, wd_ref, yb_ref):
    i = pl.program_id(0)

    @pl.when(i < nused_ref[0])
    def _():
        x = xb_ref[...].astype(BF16)
        gate = jnp.dot(x, wg_ref[0], preferred_element_type=F32)
        up = jnp.dot(x, wu_ref[0], preferred_element_type=F32)
        hmid = (_silu(gate) * up).astype(BF16)
        yb_ref[...] = jnp.dot(hmid, wd_ref[0], preferred_element_type=F32)

    @pl.when(i >= nused_ref[0])
    def _():
        yb_ref[...] = jnp.zeros_like(yb_ref)


def _experts(xb, blk_exp, n_used, w_gate, w_up, w_down):
    n_slots = xb.shape[0]
    M = EXPERT_ROWS
    grid_spec = pltpu.PrefetchScalarGridSpec(
        num_scalar_prefetch=2, grid=(n_slots // M,),
        in_specs=[pl.BlockSpec((M, D_MODEL), lambda i, be, nu: (i, 0)),
                  pl.BlockSpec((1, D_MODEL, MOE_HIDDEN), lambda i, be, nu: (be[i], 0, 0)),
                  pl.BlockSpec((1, D_MODEL, MOE_HIDDEN), lambda i, be, nu: (be[i], 0, 0)),
                  pl.BlockSpec((1, MOE_HIDDEN, D_MODEL), lambda i, be, nu: (be[i], 0, 0))],
        out_specs=pl.BlockSpec((M, D_MODEL), lambda i, be, nu: (i, 0)))
    return pl.pallas_call(
        _expert_body, grid_spec=grid_spec, out_shape=jax.ShapeDtypeStruct((n_slots, D_MODEL), F32),
        compiler_params=_params("arbitrary"), name="experts")(
            blk_exp, n_used, xb, w_gate.astype(BF16), w_up.astype(BF16), w_down.astype(BF16))


def _combine_body(d1_ref, d2_ref, h_ref, route_ref, yb_ref, o_ref, buf_ref, sem):
    def start(t, _):
        _row_copy(yb_ref, d1_ref[t], buf_ref.at[0], t, sem).start()
        _row_copy(yb_ref, d2_ref[t], buf_ref.at[1], t, sem).start()
        return 0

    def wait(t, _):
        _row_copy(yb_ref, d1_ref[t], buf_ref.at[0], t, sem).wait()
        _row_copy(yb_ref, d2_ref[t], buf_ref.at[1], t, sem).wait()
        return 0

    lax.fori_loop(0, MOVE_ROWS, start, 0)
    lax.fori_loop(0, MOVE_ROWS, wait, 0)
    route = route_ref[...]
    g1 = route[:, 2:3]
    g2 = route[:, 3:4]
    o_ref[...] = h_ref[...] + (buf_ref[0] * g1 + buf_ref[1] * g2)


def _combine(h2, route, yb, dest1, dest2):
    T = h2.shape[0]
    R = MOVE_ROWS
    idx = pl.BlockSpec((R,), lambda i: (i,), memory_space=pltpu.SMEM)
    return pl.pallas_call(
        _combine_body, grid=(T // R,),
        in_specs=[idx, idx, pl.BlockSpec((R, D_MODEL), lambda i: (i, 0)),
                  pl.BlockSpec((R, LANES), lambda i: (i, 0)), pl.BlockSpec(memory_space=pl.ANY)],
        out_specs=pl.BlockSpec((R, D_MODEL), lambda i: (i, 0)),
        out_shape=jax.ShapeDtypeStruct((T, D_MODEL), F32),
        scratch_shapes=[pltpu.VMEM((2, R, D_MODEL), F32), pltpu.SemaphoreType.DMA(())],
        compiler_params=_params("arbitrary"), name="combine")(dest1, dest2, h2, route, yb)


def _moe(h2, xn, route, counts, w_gate, w_up, w_down):
    T = h2.shape[0]
    M = EXPERT_ROWS
    n_blk = (2 * T) // M + MOE_EXPERTS
    cnt = counts[0, ROUTER_LANE0:ROUTER_LANE0 + MOE_EXPERTS].astype(jnp.int32)
    padded = (cnt + M - 1) // M * M
    ends = jnp.cumsum(padded)
    pstarts = ends - padded
    e1 = route[:, 0].astype(jnp.int32)
    e2 = route[:, 1].astype(jnp.int32)
    dest1 = pstarts[e1] + route[:, 4].astype(jnp.int32)
    dest2 = pstarts[e2] + route[:, 5].astype(jnp.int32)
    n_used = (ends[-1] // M).astype(jnp.int32)
    blk = jnp.minimum(jnp.arange(n_blk, dtype=jnp.int32), n_used - 1) * M
    blk_exp = jnp.minimum(jnp.searchsorted(ends, blk, side='right'), MOE_EXPERTS - 1).astype(jnp.int32)
    xb = _dispatch(xn, dest1, dest2, n_blk * M)
    yb = _experts(xb, blk_exp, n_used.reshape(1), w_gate, w_up, w_down)
    return _combine(h2, route, yb, dest1, dest2)


def kernel(x, rel_bias, norm1_g, w_in, conv_w, conv_b, dt_bias, a_log, d_skip, ssd_norm_g,
           q_norm_g, k_norm_g, w_out, norm2_g, w_r1, b_r1, w_r2, b_r2, w_gate, w_up, w_down):
    B, S, D = x.shape
    T = B * S
    h2 = x.reshape(T, D)
    depth = w_in.shape[0]
    for l in range(depth):
        z, xbc, dt, dtt, q, k, v = _inproj(h2, norm1_g[l], w_in[l])
        r3 = lambda a: a.reshape(B, S, a.shape[-1])
        y_ssd = _ssd(r3(z), r3(xbc), r3(dt), dtt, conv_w[l], conv_b[l], dt_bias[l], a_log[l],
                     d_skip[l], ssd_norm_g[l])
        y_att = _attention(r3(q), r3(k), r3(v), q_norm_g[l], k_norm_g[l], rel_bias)
        h2, xn, route, counts = _outproj_route(
            h2, y_ssd.reshape(T, SSD_INNER), y_att.reshape(T, ATT_INNER), w_out[l], norm2_g[l],
            w_r1[l], b_r1[l], w_r2[l], b_r2[l])
        h2 = _moe(h2, xn, route, counts, w_gate[l], w_up[l], w_down[l])
    return h2.reshape(B, S, D)
```

```python
import functools
import math

import numpy as np
import jax
import jax.numpy as jnp
from jax import lax
from jax.experimental import pallas as pl
from jax.experimental.pallas import tpu as pltpu

F32 = jnp.float32
BF16 = jnp.bfloat16
HIGHEST = lax.Precision.HIGHEST

LANES = 128
SUBLANES = 8

D_MODEL = 1024
SSD_HEADS = 8
SSD_HEAD_DIM = 64
SSD_INNER = SSD_HEADS * SSD_HEAD_DIM
SSD_GROUPS = 2
SSD_HEADS_PER_GROUP = SSD_HEADS // SSD_GROUPS
SSD_STATE = 64
SSD_CONV = 4
SSD_CHUNK = 128
CONV_CH = SSD_INNER + 2 * SSD_GROUPS * SSD_STATE
ATT_HEADS = 8
ATT_HEAD_DIM = 64
ATT_INNER = ATT_HEADS * ATT_HEAD_DIM
MOBA_BLOCK = 256
MOBA_TOPK = 3
REL_BUCKETS = 32
REL_MAX_DIST = 128
MOE_GROUPS = 4
MOE_EXPERTS_PER_GROUP = 8
MOE_EXPERTS = MOE_GROUPS * MOE_EXPERTS_PER_GROUP
MOE_HIDDEN = 512
EPS = 1e-6

MASK_VALUE = -1e30
LOG2E = math.log2(math.e)
IN_ROWS = 512
OUT_ROWS = 256
EXPERT_ROWS = 256
MOVE_ROWS = 256
VMEM_LIMIT = 56 * 1024 * 1024
ROUTER_LANE0 = MOE_GROUPS


def _silu(x):
    return x * (1.0 / (1.0 + jnp.exp(-x)))


def _softplus(x):
    return jnp.maximum(x, 0.0) + jnp.log(1.0 + jnp.exp(-jnp.abs(x)))


def _params(*sem):
    return pltpu.CompilerParams(dimension_semantics=sem, vmem_limit_bytes=VMEM_LIMIT)


def _inproj_body(h_ref, g_ref, wz_ref, wx_ref, wdt_ref, wdtt_ref, wq_ref, wk_ref, wv_ref,
                 z_ref, xbc_ref, dt_ref, dtt_ref, q_ref, k_ref, v_ref):
    x = h_ref[...]
    xn = x * lax.rsqrt(jnp.mean(x * x, axis=-1, keepdims=True) + EPS) * g_ref[...]
    xb = xn.astype(BF16)
    z_ref[...] = jnp.dot(xb, wz_ref[...], preferred_element_type=F32)
    xbc_ref[...] = jnp.dot(xb, wx_ref[...], preferred_element_type=F32)
    dt_ref[...] = jnp.dot(xb, wdt_ref[...], preferred_element_type=F32)
    dtt_ref[...] = lax.dot_general(wdtt_ref[...], xb, (((1,), (1,)), ((), ())),
                                   preferred_element_type=F32)
    q_ref[...] = jnp.dot(xb, wq_ref[...], preferred_element_type=F32)
    k_ref[...] = jnp.dot(xb, wk_ref[...], preferred_element_type=F32)
    v_ref[...] = jnp.dot(xb, wv_ref[...], preferred_element_type=F32)


def _inproj(h2, g, w_in):
    T = h2.shape[0]
    c0, c1, c2 = SSD_INNER, SSD_INNER + CONV_CH, SSD_INNER + CONV_CH + SSD_HEADS
    wb = w_in.astype(BF16)
    wz, wx, wdt = wb[:, :c0], wb[:, c0:c1], wb[:, c1:c2]
    wq, wk, wv = (wb[:, c2 + i * ATT_INNER:c2 + (i + 1) * ATT_INNER] for i in range(3))
    wdt_pad = jnp.pad(wdt, ((0, 0), (0, LANES - SSD_HEADS)))
    wdtt = wdt.T
    full = lambda a: pl.BlockSpec(a.shape, lambda i: (0,) * a.ndim)
    rows = lambda n: pl.BlockSpec((IN_ROWS, n), lambda i: (i, 0))
    ins = (h2, g.reshape(1, D_MODEL), wz, wx, wdt_pad, wdtt, wq, wk, wv)
    out_shape = (
        jax.ShapeDtypeStruct((T, SSD_INNER), F32), jax.ShapeDtypeStruct((T, CONV_CH), F32),
        jax.ShapeDtypeStruct((T, LANES), F32), jax.ShapeDtypeStruct((SSD_HEADS, T), F32),
        jax.ShapeDtypeStruct((T, ATT_INNER), F32), jax.ShapeDtypeStruct((T, ATT_INNER), F32),
        jax.ShapeDtypeStruct((T, ATT_INNER), F32))
    out_specs = (rows(SSD_INNER), rows(CONV_CH), rows(LANES),
                 pl.BlockSpec((SSD_HEADS, IN_ROWS), lambda i: (0, i)),
                 rows(ATT_INNER), rows(ATT_INNER), rows(ATT_INNER))
    return pl.pallas_call(
        _inproj_body, grid=(T // IN_ROWS,),
        in_specs=[rows(D_MODEL)] + [full(a) for a in ins[1:]],
        out_specs=out_specs, out_shape=out_shape,
        compiler_params=_params("arbitrary"), name="inproj")(*ins)


def _ssd_body(z_ref, xbc_ref, dt_ref, dtt_ref, cw_ref, cb_ref, dtb_ref, dtbt_ref, alog_ref,
              alogt_ref, dskip_ref, ng_ref, tril_ref, expand_ref, y_ref, ext_ref, state_ref):
    L, P, N, E = SSD_CHUNK, SSD_HEAD_DIM, SSD_STATE, SSD_HEADS_PER_GROUP
    c = pl.program_id(1)

    @pl.when(c == 0)
    def _():
        state_ref[...] = jnp.zeros_like(state_ref)
        ext_ref[0:SUBLANES, :] = jnp.zeros((SUBLANES, CONV_CH), F32)

    ext_ref[SUBLANES:SUBLANES + L, :] = xbc_ref[0]
    conv = jnp.zeros((L, CONV_CH), F32) + cb_ref[...]
    for j in range(SSD_CONV):
        off = SUBLANES - (SSD_CONV - 1) + j
        conv = conv + cw_ref[j:j + 1, :] * ext_ref[off:off + L, :]
    ext_ref[0:SUBLANES, :] = ext_ref[L:L + SUBLANES, :]
    act = _silu(conv)
    xs = act[:, :SSD_INNER]
    b_in = act[:, SSD_INNER:SSD_INNER + SSD_GROUPS * N]
    c_out = act[:, SSD_INNER + SSD_GROUPS * N:]

    tril = tril_ref[...]
    expand = expand_ref[...]
    dt = _softplus(dt_ref[0] + dtb_ref[...])
    a_dt = dt * (-jnp.exp(alog_ref[...]))
    a_cs = jnp.dot(tril, a_dt, precision=HIGHEST, preferred_element_type=F32)
    dt_t = _softplus(dtt_ref[...] + dtbt_ref[...])
    a_dt_t = dt_t * (-jnp.exp(alogt_ref[...]))
    a_cs_t = lax.dot_general(a_dt_t, tril, (((1,), (1,)), ((), ())), precision=HIGHEST,
                             preferred_element_type=F32)
    last = a_cs[L - 1:L, :]
    hdot = lambda a: jnp.dot(a, expand, precision=HIGHEST, preferred_element_type=F32)
    dt_x = hdot(dt)
    eacs_x = hdot(jnp.exp(a_cs))
    dec_x = hdot(jnp.exp(last - a_cs))
    cdec_x = hdot(jnp.broadcast_to(jnp.exp(last), (SUBLANES, LANES)))[0:1, :]
    xdt = xs * dt_x
    xdtd = (xdt * dec_x).astype(BF16)
    xdt_b = xdt.astype(BF16)
    row = lax.broadcasted_iota(jnp.int32, (L, L), 0)
    col = lax.broadcasted_iota(jnp.int32, (L, L), 1)
    causal = row >= col

    y_parts = []
    for g in range(SSD_GROUPS):
        bg = b_in[:, g * N:(g + 1) * N].astype(BF16)
        cg = c_out[:, g * N:(g + 1) * N].astype(BF16)
        cb = lax.dot_general(cg, bg, (((1,), (1,)), ((), ())), preferred_element_type=F32)
        st = state_ref[g]
        gs = slice(g * E * P, (g + 1) * E * P)
        y_off = jnp.dot(cg, st.astype(BF16), preferred_element_type=F32) * eacs_x[:, gs]
        for e in range(E):
            h = g * E + e
            diff = a_cs[:, h:h + 1] - a_cs_t[h:h + 1, :]
            l_mat = jnp.exp(jnp.where(causal, diff, -jnp.inf))
            m = (cb * l_mat).astype(BF16)
            y_diag = jnp.dot(m, xdt_b[:, h * P:(h + 1) * P], preferred_element_type=F32)
            y_parts.append(y_diag + y_off[:, e * P:(e + 1) * P])
        upd = lax.dot_general(bg, xdtd[:, gs], (((0,), (0,)), ((), ())),
                              preferred_element_type=F32)
        state_ref[g] = cdec_x[:, gs] * st + upd
    y = jnp.concatenate(y_parts, axis=-1) + dskip_ref[...] * xs
    y = y * _silu(z_ref[0])
    y = y * lax.rsqrt(jnp.mean(y * y, axis=-1, keepdims=True) + EPS) * ng_ref[...]
    y_ref[0] = y


def _ssd(z, xbc, dt, dtt, conv_w, conv_b, dt_bias, a_log, d_skip, norm_g):
    B, S, _ = z.shape
    L = SSD_CHUNK
    nc = S // L
    pad_row = lambda a: jnp.pad(a.reshape(1, SSD_HEADS), ((0, 0), (0, LANES - SSD_HEADS)))
    tril = jnp.asarray(np.tril(np.ones((L, L), np.float32)))
    expand_np = np.zeros((LANES, SSD_INNER), np.float32)
    for h in range(SSD_HEADS):
        expand_np[h, h * SSD_HEAD_DIM:(h + 1) * SSD_HEAD_DIM] = 1.0
    ins = (z, xbc, dt, dtt, conv_w, conv_b.reshape(1, CONV_CH), pad_row(dt_bias),
           dt_bias.reshape(SSD_HEADS, 1), pad_row(a_log), a_log.reshape(SSD_HEADS, 1),
           jnp.repeat(d_skip, SSD_HEAD_DIM).reshape(1, SSD_INNER), norm_g.reshape(1, SSD_INNER),
           tril, jnp.asarray(expand_np))
    full = lambda a: pl.BlockSpec(a.shape, lambda b, c: (0,) * a.ndim)
    chunk = lambda n: pl.BlockSpec((1, L, n), lambda b, c: (b, c, 0))
    in_specs = [chunk(SSD_INNER), chunk(CONV_CH), chunk(LANES),
                pl.BlockSpec((SSD_HEADS, L), lambda b, c: (0, b * nc + c))]
    in_specs += [full(a) for a in ins[4:]]
    return pl.pallas_call(
        _ssd_body, grid=(B, nc), in_specs=in_specs, out_specs=chunk(SSD_INNER),
        out_shape=jax.ShapeDtypeStruct((B, S, SSD_INNER), F32),
        scratch_shapes=[pltpu.VMEM((L + SUBLANES, CONV_CH), F32),
                        pltpu.VMEM((SSD_GROUPS, SSD_STATE, SSD_HEADS_PER_GROUP * SSD_HEAD_DIM), F32)],
        compiler_params=_params("arbitrary", "arbitrary"), name="ssd")(*ins)


def _bucket_tile():
    bs = MOBA_BLOCK
    dist = np.arange(bs)[:, None] - np.arange(2 * bs)[None, :] + bs
    max_exact = REL_BUCKETS // 2
    d = np.maximum(dist, max_exact).astype(np.float32)
    large = max_exact + (np.log(d / np.float32(max_exact)) / np.float32(math.log(REL_MAX_DIST / max_exact))
                         * np.float32(REL_BUCKETS - max_exact)).astype(np.int32)
    large = np.minimum(large, REL_BUCKETS - 1)
    bucket = np.where(dist < max_exact, dist, large)
    return np.where(dist >= 0, bucket, -1).astype(np.int32)


def _bias_body(rb_ref, bucket_ref, o_ref):
    head = pl.program_id(0)
    bucket = bucket_ref[...]
    far = rb_ref[REL_BUCKETS - 1, head]
    bias = jnp.where(bucket < 0, MASK_VALUE, 0.0)
    for i in range(REL_BUCKETS - 1):
        bias = jnp.where(bucket == i, (rb_ref[i, head] - far) * LOG2E, bias)
    o_ref[0] = bias


def _bias_tiles(rel_bias):
    bs = MOBA_BLOCK
    bucket = jnp.asarray(_bucket_tile())
    return pl.pallas_call(
        _bias_body, grid=(ATT_HEADS,),
        in_specs=[pl.BlockSpec(memory_space=pltpu.SMEM), pl.BlockSpec((bs, 2 * bs), lambda h: (0, 0))],
        out_specs=pl.BlockSpec((1, bs, 2 * bs), lambda h: (h, 0, 0)),
        out_shape=jax.ShapeDtypeStruct((ATT_HEADS, bs, 2 * bs), F32),
        compiler_params=_params("arbitrary"), name="bias_tiles")(rel_bias, bucket)


def _attn_body(q_ref, k_ref, v_ref, qg_ref, kg_ref, bias_ref, o_ref,
               kaug_ref, vaug_ref, kmean_ref, qaug_ref, sa_ref, sb_ref, m_ref, acc_ref):
    bs, hd = MOBA_BLOCK, ATT_HEAD_DIM
    nb = k_ref.shape[1] // bs
    qb = pl.program_id(2)
    nt_dims = (((1,), (1,)), ((), ()))

    @pl.when(qb == 0)
    def _():
        blk = lax.broadcasted_iota(jnp.int32, (nb * bs, hd), 0) // bs
        lane = lax.broadcasted_iota(jnp.int32, (nb * bs, hd), 1)
        onehot = (blk == lane).astype(BF16)
        ones = jnp.ones((nb * bs, hd), BF16)
        for h in range(2):
            kh = k_ref[0, :, h * hd:(h + 1) * hd]
            kn = kh * lax.rsqrt(jnp.mean(kh * kh, axis=-1, keepdims=True) + EPS) * kg_ref[...]
            kmean_ref[h] = jnp.mean(kn.reshape(nb, bs, hd), axis=1)
            kaug_ref[h] = jnp.concatenate([kn.astype(BF16), onehot], axis=-1)
            vaug_ref[h] = jnp.concatenate([v_ref[0, :, h * hd:(h + 1) * hd].astype(BF16), ones], axis=-1)

    q = q_ref[0]
    for h in range(2):
        qh = q[:, h * hd:(h + 1) * hd]
        qn = qh * lax.rsqrt(jnp.mean(qh * qh, axis=-1, keepdims=True) + EPS) * qg_ref[...]
        gate = lax.dot_general(kmean_ref[h], qn, nt_dims, precision=HIGHEST,
                               preferred_element_type=F32)
        rowb = lax.broadcasted_iota(jnp.int32, (nb, bs), 0)
        past = rowb < qb
        g = jnp.where(past, gate, -jnp.inf)
        rank = jnp.zeros((nb, bs), jnp.int32)
        for m in range(nb):
            gm = g[m:m + 1, :]
            beats = (gm > g) | ((gm == g) & (rowb > m))
            rank = rank + beats.astype(jnp.int32)
        sel = (past & (rank < MOBA_TOPK)) | (rowb == qb)
        selmask = jnp.where(sel, 0.0, MASK_VALUE)
        mask_rows = jnp.concatenate(
            [jnp.zeros((hd, bs), F32), selmask, jnp.zeros((hd - nb, bs), F32)], axis=0)
        qs = jnp.concatenate([qn * (ATT_HEAD_DIM ** -0.5 * LOG2E), jnp.zeros((bs, hd), F32)], axis=-1)
        qaug_ref[h] = (qs + mask_rows.T).astype(BF16)
        m_ref[h] = jnp.full((bs, 2 * hd), -jnp.inf, F32)
        acc_ref[h] = jnp.zeros((bs, 2 * hd), F32)

    def scores(n, s_ref):
        start = pl.multiple_of(n * bs, bs)
        for h in range(2):
            s_ref[h] = lax.dot_general(qaug_ref[h], kaug_ref[h, pl.ds(start, bs), :], nt_dims,
                                       preferred_element_type=F32)

    def update(n, s_ref, bias):
        start = pl.multiple_of(n * bs, bs)
        for h in range(2):
            s = s_ref[h]
            if bias is not None:
                s = s + bias(h)
            m_i = m_ref[h]
            m_new = jnp.maximum(m_i, jnp.max(s, axis=-1, keepdims=True))
            alpha = jnp.exp2(m_i - m_new)
            p = jnp.exp2(s - jnp.concatenate([m_new, m_new], axis=-1)).astype(BF16)
            pv = jnp.dot(p, vaug_ref[h, pl.ds(start, bs), :], preferred_element_type=F32)
            acc_ref[h] = alpha * acc_ref[h] + pv
            m_ref[h] = m_new

    def far_pair(i, _):
        n = 2 * i
        scores(n + 1, sb_ref)
        update(n, sa_ref, None)
        scores(n + 2, sa_ref)
        update(n + 1, sb_ref, None)
        return 0

    n_far = jnp.maximum(qb - 1, 0)
    scores(0, sa_ref)
    lax.fori_loop(0, n_far >> 1, far_pair, 0)
    no_prev = jnp.where(qb == 0, MASK_VALUE, 0.0)
    prev_bias = lambda h: bias_ref[h, :, 0:bs] + no_prev
    own_bias = lambda h: bias_ref[h, :, bs:2 * bs]

    @pl.when((n_far & 1) == 1)
    def _():
        scores(n_far, sb_ref)
        update(n_far - 1, sa_ref, None)
        scores(qb, sa_ref)
        update(n_far, sb_ref, prev_bias)
        update(qb, sa_ref, own_bias)

    @pl.when((n_far & 1) == 0)
    def _():
        scores(qb, sb_ref)
        update(n_far, sa_ref, prev_bias)
        update(qb, sb_ref, own_bias)

    outs = []
    for h in range(2):
        acc = acc_ref[h]
        outs.append(acc[:, 0:hd] / acc[:, hd:hd + 1])
    o_ref[0] = jnp.concatenate(outs, axis=-1)


def _attention(q, k, v, q_g, k_g, bias_tiles):
    B, S, _ = q.shape
    bs = MOBA_BLOCK
    nb = S // bs
    assert S % bs == 0 and nb <= ATT_HEAD_DIM
    pair = 2 * ATT_HEAD_DIM
    return pl.pallas_call(
        _attn_body, grid=(B, ATT_HEADS // 2, nb),
        in_specs=[
            pl.BlockSpec((1, bs, pair), lambda b, hp, i: (b, i, hp)),
            pl.BlockSpec((1, S, pair), lambda b, hp, i: (b, 0, hp)),
            pl.BlockSpec((1, S, pair), lambda b, hp, i: (b, 0, hp)),
            pl.BlockSpec((1, ATT_HEAD_DIM), lambda b, hp, i: (0, 0)),
            pl.BlockSpec((1, ATT_HEAD_DIM), lambda b, hp, i: (0, 0)),
            pl.BlockSpec((2, bs, 2 * bs), lambda b, hp, i: (hp, 0, 0)),
        ],
        out_specs=pl.BlockSpec((1, bs, pair), lambda b, hp, i: (b, i, hp)),
        out_shape=jax.ShapeDtypeStruct((B, S, ATT_INNER), F32),
        scratch_shapes=[pltpu.VMEM((2, S, pair), BF16), pltpu.VMEM((2, S, pair), BF16),
                        pltpu.VMEM((2, nb, ATT_HEAD_DIM), F32), pltpu.VMEM((2, bs, pair), BF16),
                        pltpu.VMEM((2, bs, bs), F32), pltpu.VMEM((2, bs, bs), F32),
                        pltpu.VMEM((2, bs, pair), F32),
                        pltpu.VMEM((2, bs, pair), F32)],
        compiler_params=_params("arbitrary", "arbitrary", "arbitrary"), name="moba")(
            q, k, v, q_g.reshape(1, -1), k_g.reshape(1, -1), bias_tiles)


def _outproj_body(h_ref, ys_ref, ya_ref, wa_ref, wb_ref, g_ref, wr_ref, br_ref, tri_ref,
                  hn_ref, xn_ref, route_ref, counts_ref, carry_ref):
    i = pl.program_id(0)

    @pl.when(i == 0)
    def _():
        carry_ref[...] = jnp.zeros_like(carry_ref)

    hn = (h_ref[...]
          + jnp.dot(ys_ref[...].astype(BF16), wa_ref[...], preferred_element_type=F32)
          + jnp.dot(ya_ref[...].astype(BF16), wb_ref[...], preferred_element_type=F32))
    hn_ref[...] = hn
    xn = hn * lax.rsqrt(jnp.mean(hn * hn, axis=-1, keepdims=True) + EPS) * g_ref[...]
    xn_ref[...] = xn
    logits = jnp.dot(xn, wr_ref[...], precision=HIGHEST, preferred_element_type=F32) + br_ref[...]
    rows = logits.shape[0]
    lane = lax.broadcasted_iota(jnp.int32, (rows, LANES), 1)
    first = lambda hit: jnp.min(jnp.where(hit, lane, LANES), axis=-1, keepdims=True)
    l1 = jnp.where(lane < MOE_GROUPS, logits, -jnp.inf)
    mx = jnp.max(l1, axis=-1, keepdims=True)
    gval = 1.0 / jnp.sum(jnp.exp(l1 - mx), axis=-1, keepdims=True)
    gidx = first(l1 == mx)
    e_lane = lane - ROUTER_LANE0
    in_grp = (e_lane >= 0) & (e_lane < MOE_EXPERTS) & ((e_lane >> 3) == gidx)
    l2 = jnp.where(in_grp, logits, -jnp.inf)
    m1 = jnp.max(l2, axis=-1, keepdims=True)
    i1 = first(l2 == m1)
    l2b = jnp.where(lane == i1, -jnp.inf, l2)
    m2 = jnp.max(l2b, axis=-1, keepdims=True)
    i2 = first(l2b == m2)
    r = jnp.exp(m2 - m1)
    g1 = gval / (1.0 + r)
    g2 = gval * r / (1.0 + r)
    oh1 = (lane == i1)
    oh2 = (lane == i2)
    both = jnp.where(oh1 | oh2, 1.0, 0.0)
    cx = jnp.dot(tri_ref[...], both.astype(BF16), preferred_element_type=F32) + carry_ref[...]
    rank1 = jnp.sum(jnp.where(oh1, cx, 0.0), axis=-1, keepdims=True)
    rank2 = jnp.sum(jnp.where(oh2, cx, 0.0), axis=-1, keepdims=True)
    carry_ref[...] = carry_ref[...] + jnp.sum(both, axis=0, keepdims=True)
    counts_ref[...] = carry_ref[...]
    cols = ((i1 - ROUTER_LANE0).astype(F32), (i2 - ROUTER_LANE0).astype(F32), g1, g2, rank1, rank2)
    route = jnp.zeros((rows, LANES), F32)
    for j, cval in enumerate(cols):
        route = jnp.where(lane == j, cval, route)
    route_ref[...] = route


def _outproj_route(h2, y_ssd, y_att, w_out, norm_g, w_r1, b_r1, w_r2, b_r2):
    T = h2.shape[0]
    R = OUT_ROWS
    wb = w_out.astype(BF16)
    pad = LANES - MOE_GROUPS - MOE_EXPERTS
    wr = jnp.pad(jnp.concatenate([w_r1, w_r2], axis=1), ((0, 0), (0, pad)))
    br = jnp.pad(jnp.concatenate([b_r1, b_r2]), (0, pad)).reshape(1, LANES)
    tri = jnp.asarray(np.tril(np.ones((R, R), np.float32), -1), dtype=BF16)
    ins = (h2, y_ssd, y_att, wb[:SSD_INNER], wb[SSD_INNER:], norm_g.reshape(1, D_MODEL), wr, br, tri)
    full = lambda a: pl.BlockSpec(a.shape, lambda i: (0,) * a.ndim)
    rows = lambda n: pl.BlockSpec((R, n), lambda i: (i, 0))
    return pl.pallas_call(
        _outproj_body, grid=(T // R,),
        in_specs=[rows(D_MODEL), rows(SSD_INNER), rows(ATT_INNER)] + [full(a) for a in ins[3:]],
        out_specs=(rows(D_MODEL), rows(D_MODEL), rows(LANES), pl.BlockSpec((1, LANES), lambda i: (0, 0))),
        out_shape=(jax.ShapeDtypeStruct((T, D_MODEL), F32), jax.ShapeDtypeStruct((T, D_MODEL), F32),
                   jax.ShapeDtypeStruct((T, LANES), F32), jax.ShapeDtypeStruct((1, LANES), F32)),
        scratch_shapes=[pltpu.VMEM((1, LANES), F32)],
        compiler_params=_params("arbitrary"), name="outproj_route")(*ins)


def _row_copy(src_ref, src_row, dst_ref, dst_row, sem):
    return pltpu.make_async_copy(src_ref.at[pl.ds(src_row, 1), :], dst_ref.at[pl.ds(dst_row, 1), :], sem)


def _dispatch_body(d1_ref, d2_ref, xn_ref, xb_in_ref, xb_ref, sem):
    del xb_in_ref

    def start(t, _):
        _row_copy(xn_ref, t, xb_ref, d1_ref[t], sem).start()
        _row_copy(xn_ref, t, xb_ref, d2_ref[t], sem).start()
        return 0

    def wait(t, _):
        _row_copy(xn_ref, t, xb_ref, d1_ref[t], sem).wait()
        _row_copy(xn_ref, t, xb_ref, d2_ref[t], sem).wait()
        return 0

    lax.fori_loop(0, MOVE_ROWS, start, 0)
    lax.fori_loop(0, MOVE_ROWS, wait, 0)


def _dispatch(xn, dest1, dest2, n_slots):
    T = xn.shape[0]
    idx = pl.BlockSpec((MOVE_ROWS,), lambda i: (i,), memory_space=pltpu.SMEM)
    anyspec = pl.BlockSpec(memory_space=pl.ANY)
    return pl.pallas_call(
        _dispatch_body, grid=(T // MOVE_ROWS,),
        in_specs=[idx, idx, pl.BlockSpec((MOVE_ROWS, D_MODEL), lambda i: (i, 0)), anyspec],
        out_specs=anyspec,
        out_shape=jax.ShapeDtypeStruct((n_slots, D_MODEL), F32),
        scratch_shapes=[pltpu.SemaphoreType.DMA(())],
        input_output_aliases={3: 0},
        compiler_params=_params("arbitrary"), name="dispatch")(
            dest1, dest2, xn, jnp.zeros((n_slots, D_MODEL), F32))


def _expert_body(be_ref, nused_ref, xb_ref, wg_ref, wu_ref, wd_ref, yb_ref):
    i = pl.program_id(0)

    @pl.when(i < nused_ref[0])
    def _():
        x = xb_ref[...].astype(BF16)
        gate = jnp.dot(x, wg_ref[0], preferred_element_type=F32)
        up = jnp.dot(x, wu_ref[0], preferred_element_type=F32)
        hmid = (_silu(gate) * up).astype(BF16)
        yb_ref[...] = jnp.dot(hmid, wd_ref[0], preferred_element_type=F32)

    @pl.when(i >= nused_ref[0])
    def _():
        yb_ref[...] = jnp.zeros_like(yb_ref)


def _experts(xb, blk_exp, n_used, w_gate, w_up, w_down):
    n_slots = xb.shape[0]
    M = EXPERT_ROWS
    grid_spec = pltpu.PrefetchScalarGridSpec(
        num_scalar_prefetch=2, grid=(n_slots // M,),
        in_specs=[pl.BlockSpec((M, D_MODEL), lambda i, be, nu: (i, 0)),
                  pl.BlockSpec((1, D_MODEL, MOE_HIDDEN), lambda i, be, nu: (be[i], 0, 0)),
                  pl.BlockSpec((1, D_MODEL, MOE_HIDDEN), lambda i, be, nu: (be[i], 0, 0)),
                  pl.BlockSpec((1, MOE_HIDDEN, D_MODEL), lambda i, be, nu: (be[i], 0, 0))],
        out_specs=pl.BlockSpec((M, D_MODEL), lambda i, be, nu: (i, 0)))
    return pl.pallas_call(
        _expert_body, grid_spec=grid_spec, out_shape=jax.ShapeDtypeStruct((n_slots, D_MODEL), F32),
        compiler_params=_params("arbitrary"), name="experts")(
            blk_exp, n_used, xb, w_gate.astype(BF16), w_up.astype(BF16), w_down.astype(BF16))


def _combine_body(d1_ref, d2_ref, h_ref, route_ref, yb_ref, o_ref, buf_ref, sem):
    def start(t, _):
        _row_copy(yb_ref, d1_ref[t], buf_ref.at[0], t, sem).start()
        _row_copy(yb_ref, d2_ref[t], buf_ref.at[1], t, sem).start()
        return 0

    def wait(t, _):
        _row_copy(yb_ref, d1_ref[t], buf_ref.at[0], t, sem).wait()
        _row_copy(yb_ref, d2_ref[t], buf_ref.at[1], t, sem).wait()
        return 0

    lax.fori_loop(0, MOVE_ROWS, start, 0)
    lax.fori_loop(0, MOVE_ROWS, wait, 0)
    route = route_ref[...]
    g1 = route[:, 2:3]
    g2 = route[:, 3:4]
    o_ref[...] = h_ref[...] + (buf_ref[0] * g1 + buf_ref[1] * g2)


def _combine(h2, route, yb, dest1, dest2):
    T = h2.shape[0]
    R = MOVE_ROWS
    idx = pl.BlockSpec((R,), lambda i: (i,), memory_space=pltpu.SMEM)
    return pl.pallas_call(
        _combine_body, grid=(T // R,),
        in_specs=[idx, idx, pl.BlockSpec((R, D_MODEL), lambda i: (i, 0)),
                  pl.BlockSpec((R, LANES), lambda i: (i, 0)), pl.BlockSpec(memory_space=pl.ANY)],
        out_specs=pl.BlockSpec((R, D_MODEL), lambda i: (i, 0)),
        out_shape=jax.ShapeDtypeStruct((T, D_MODEL), F32),
        scratch_shapes=[pltpu.VMEM((2, R, D_MODEL), F32), pltpu.SemaphoreType.DMA(())],
        compiler_params=_params("arbitrary"), name="combine")(dest1, dest2, h2, route, yb)


def _moe(h2, xn, route, counts, w_gate, w_up, w_down):
    T = h2.shape[0]
    M = EXPERT_ROWS
    n_blk = (2 * T) // M + MOE_EXPERTS
    cnt = counts[0, ROUTER_LANE0:ROUTER_LANE0 + MOE_EXPERTS].astype(jnp.int32)
    padded = (cnt + M - 1) // M * M
    ends = jnp.cumsum(padded)
    pstarts = ends - padded
    e1 = route[:, 0].astype(jnp.int32)
    e2 = route[:, 1].astype(jnp.int32)
    dest1 = pstarts[e1] + route[:, 4].astype(jnp.int32)
    dest2 = pstarts[e2] + route[:, 5].astype(jnp.int32)
    n_used = (ends[-1] // M).astype(jnp.int32)
    blk = jnp.minimum(jnp.arange(n_blk, dtype=jnp.int32), n_used - 1) * M
    blk_exp = jnp.minimum(jnp.searchsorted(ends, blk, side='right'), MOE_EXPERTS - 1).astype(jnp.int32)
    xb = _dispatch(xn, dest1, dest2, n_blk * M)
    yb = _experts(xb, blk_exp, n_used.reshape(1), w_gate, w_up, w_down)
    return _combine(h2, route, yb, dest1, dest2)


def kernel(x, rel_bias, norm1_g, w_in, conv_w, conv_b, dt_bias, a_log, d_skip, ssd_norm_g,
           q_norm_g, k_norm_g, w_out, norm2_g, w_r1, b_r1, w_r2, b_r2, w_gate, w_up, w_down):
    B, S, D = x.shape
    T = B * S
    h2 = x.reshape(T, D)
    depth = w_in.shape[0]
    bias_tiles = _bias_tiles(rel_bias)
    for l in range(depth):
        z, xbc, dt, dtt, q, k, v = _inproj(h2, norm1_g[l], w_in[l])
        r3 = lambda a: a.reshape(B, S, a.shape[-1])
        y_ssd = _ssd(r3(z), r3(xbc), r3(dt), dtt, conv_w[l], conv_b[l], dt_bias[l], a_log[l],
                     d_skip[l], ssd_norm_g[l])
        y_att = _attention(r3(q), r3(k), r3(v), q_norm_g[l], k_norm_g[l], bias_tiles)
        h2, xn, route, counts = _outproj_route(
            h2, y_ssd.reshape(T, SSD_INNER), y_att.reshape(T, ATT_INNER), w_out[l], norm2_g[l],
            w_r1[l], b_r1[l], w_r2[l], b_r2[l])
        h2 = _moe(h2, xn, route, counts, w_gate[l], w_up[l], w_down[l])
    return h2.reshape(B, S, D)
```

```python
import functools
import math

import numpy as np
import jax
import jax.numpy as jnp
from jax import lax
from jax.experimental import pallas as pl
from jax.experimental.pallas import tpu as pltpu

F32 = jnp.float32
BF16 = jnp.bfloat16
HIGHEST = lax.Precision.HIGHEST

LANES = 128
SUBLANES = 8

D_MODEL = 1024
SSD_HEADS = 8
SSD_HEAD_DIM = 64
SSD_INNER = SSD_HEADS * SSD_HEAD_DIM
SSD_GROUPS = 2
SSD_HEADS_PER_GROUP = SSD_HEADS // SSD_GROUPS
SSD_STATE = 64
SSD_CONV = 4
SSD_CHUNK = 128
CONV_CH = SSD_INNER + 2 * SSD_GROUPS * SSD_STATE
ATT_HEADS = 8
ATT_HEAD_DIM = 64
ATT_INNER = ATT_HEADS * ATT_HEAD_DIM
MOBA_BLOCK = 256
MOBA_TOPK = 3
REL_BUCKETS = 32
REL_MAX_DIST = 128
MOE_GROUPS = 4
MOE_EXPERTS_PER_GROUP = 8
MOE_EXPERTS = MOE_GROUPS * MOE_EXPERTS_PER_GROUP
MOE_HIDDEN = 512
EPS = 1e-6

MASK_VALUE = -1e30
LOG2E = math.log2(math.e)
IN_ROWS = 512
OUT_ROWS = 256
EXPERT_ROWS = 256
MOVE_ROWS = 256
VMEM_LIMIT = 56 * 1024 * 1024
ROUTER_LANE0 = MOE_GROUPS


def _silu(x):
    return x * (1.0 / (1.0 + jnp.exp(-x)))


def _softplus(x):
    return jnp.maximum(x, 0.0) + jnp.log(1.0 + jnp.exp(-jnp.abs(x)))


def _params(*sem):
    return pltpu.CompilerParams(dimension_semantics=sem, vmem_limit_bytes=VMEM_LIMIT)


def _inproj_body(h_ref, g_ref, wz_ref, wx_ref, wdt_ref, wdtt_ref, wq_ref, wk_ref, wv_ref,
                 z_ref, xbc_ref, dt_ref, dtt_ref, q_ref, k_ref, v_ref):
    x = h_ref[...]
    xn = x * lax.rsqrt(jnp.mean(x * x, axis=-1, keepdims=True) + EPS) * g_ref[...]
    xb = xn.astype(BF16)
    z_ref[...] = jnp.dot(xb, wz_ref[...], preferred_element_type=F32)
    xbc_ref[...] = jnp.dot(xb, wx_ref[...], preferred_element_type=F32)
    dt_ref[...] = jnp.dot(xb, wdt_ref[...], preferred_element_type=F32)
    dtt_ref[...] = lax.dot_general(wdtt_ref[...], xb, (((1,), (1,)), ((), ())),
                                   preferred_element_type=F32)
    q_ref[...] = jnp.dot(xb, wq_ref[...], preferred_element_type=F32)
    k_ref[...] = jnp.dot(xb, wk_ref[...], preferred_element_type=F32)
    v_ref[...] = jnp.dot(xb, wv_ref[...], preferred_element_type=F32)


def _inproj(h2, g, w_in):
    T = h2.shape[0]
    c0, c1, c2 = SSD_INNER, SSD_INNER + CONV_CH, SSD_INNER + CONV_CH + SSD_HEADS
    wb = w_in.astype(BF16)
    wz, wx, wdt = wb[:, :c0], wb[:, c0:c1], wb[:, c1:c2]
    wq, wk, wv = (wb[:, c2 + i * ATT_INNER:c2 + (i + 1) * ATT_INNER] for i in range(3))
    wdt_pad = jnp.pad(wdt, ((0, 0), (0, LANES - SSD_HEADS)))
    wdtt = wdt.T
    full = lambda a: pl.BlockSpec(a.shape, lambda i: (0,) * a.ndim)
    rows = lambda n: pl.BlockSpec((IN_ROWS, n), lambda i: (i, 0))
    ins = (h2, g.reshape(1, D_MODEL), wz, wx, wdt_pad, wdtt, wq, wk, wv)
    out_shape = (
        jax.ShapeDtypeStruct((T, SSD_INNER), F32), jax.ShapeDtypeStruct((T, CONV_CH), F32),
        jax.ShapeDtypeStruct((T, LANES), F32), jax.ShapeDtypeStruct((SSD_HEADS, T), F32),
        jax.ShapeDtypeStruct((T, ATT_INNER), F32), jax.ShapeDtypeStruct((T, ATT_INNER), F32),
        jax.ShapeDtypeStruct((T, ATT_INNER), F32))
    out_specs = (rows(SSD_INNER), rows(CONV_CH), rows(LANES),
                 pl.BlockSpec((SSD_HEADS, IN_ROWS), lambda i: (0, i)),
                 rows(ATT_INNER), rows(ATT_INNER), rows(ATT_INNER))
    return pl.pallas_call(
        _inproj_body, grid=(T // IN_ROWS,),
        in_specs=[rows(D_MODEL)] + [full(a) for a in ins[1:]],
        out_specs=out_specs, out_shape=out_shape,
        compiler_params=_params("arbitrary"), name="inproj")(*ins)


def _ssd_body(z_ref, xbc_ref, dt_ref, dtt_ref, cw_ref, cb_ref, dtb_ref, dtbt_ref, alog_ref,
              alogt_ref, dskip_ref, ng_ref, tril_ref, expand_ref, y_ref, ext_ref, state_ref):
    L, P, N, E = SSD_CHUNK, SSD_HEAD_DIM, SSD_STATE, SSD_HEADS_PER_GROUP
    c = pl.program_id(1)

    @pl.when(c == 0)
    def _():
        state_ref[...] = jnp.zeros_like(state_ref)
        ext_ref[0:SUBLANES, :] = jnp.zeros((SUBLANES, CONV_CH), F32)

    ext_ref[SUBLANES:SUBLANES + L, :] = xbc_ref[0]
    conv = jnp.zeros((L, CONV_CH), F32) + cb_ref[...]
    for j in range(SSD_CONV):
        off = SUBLANES - (SSD_CONV - 1) + j
        conv = conv + cw_ref[j:j + 1, :] * ext_ref[off:off + L, :]
    ext_ref[0:SUBLANES, :] = ext_ref[L:L + SUBLANES, :]
    act = _silu(conv)
    xs = act[:, :SSD_INNER]
    b_in = act[:, SSD_INNER:SSD_INNER + SSD_GROUPS * N]
    c_out = act[:, SSD_INNER + SSD_GROUPS * N:]

    tril = tril_ref[...]
    expand = expand_ref[...]
    dt = _softplus(dt_ref[0] + dtb_ref[...])
    a_dt = dt * (-jnp.exp(alog_ref[...]))
    a_cs = jnp.dot(tril, a_dt, precision=HIGHEST, preferred_element_type=F32)
    dt_t = _softplus(dtt_ref[...] + dtbt_ref[...])
    a_dt_t = dt_t * (-jnp.exp(alogt_ref[...]))
    a_cs_t = lax.dot_general(a_dt_t, tril, (((1,), (1,)), ((), ())), precision=HIGHEST,
                             preferred_element_type=F32)
    last = a_cs[L - 1:L, :]
    hdot = lambda a: jnp.dot(a, expand, precision=HIGHEST, preferred_element_type=F32)
    dt_x = hdot(dt)
    eacs_x = hdot(jnp.exp(a_cs))
    dec_x = hdot(jnp.exp(last - a_cs))
    cdec_x = hdot(jnp.broadcast_to(jnp.exp(last), (SUBLANES, LANES)))[0:1, :]
    xdt = xs * dt_x
    xdtd = (xdt * dec_x).astype(BF16)
    xdt_b = xdt.astype(BF16)
    row = lax.broadcasted_iota(jnp.int32, (L, L), 0)
    col = lax.broadcasted_iota(jnp.int32, (L, L), 1)
    causal = row >= col

    y_parts = []
    for g in range(SSD_GROUPS):
        bg = b_in[:, g * N:(g + 1) * N].astype(BF16)
        cg = c_out[:, g * N:(g + 1) * N].astype(BF16)
        cb = lax.dot_general(cg, bg, (((1,), (1,)), ((), ())), preferred_element_type=F32)
        st = state_ref[g]
        gs = slice(g * E * P, (g + 1) * E * P)
        y_off = jnp.dot(cg, st.astype(BF16), preferred_element_type=F32) * eacs_x[:, gs]
        for e in range(E):
            h = g * E + e
            diff = a_cs[:, h:h + 1] - a_cs_t[h:h + 1, :]
            l_mat = jnp.exp(jnp.where(causal, diff, -jnp.inf))
            m = (cb * l_mat).astype(BF16)
            y_diag = jnp.dot(m, xdt_b[:, h * P:(h + 1) * P], preferred_element_type=F32)
            y_parts.append(y_diag + y_off[:, e * P:(e + 1) * P])
        upd = lax.dot_general(bg, xdtd[:, gs], (((0,), (0,)), ((), ())),
                              preferred_element_type=F32)
        state_ref[g] = cdec_x[:, gs] * st + upd
    y = jnp.concatenate(y_parts, axis=-1) + dskip_ref[...] * xs
    y = y * _silu(z_ref[0])
    y = y * lax.rsqrt(jnp.mean(y * y, axis=-1, keepdims=True) + EPS) * ng_ref[...]
    y_ref[0] = y


def _ssd(z, xbc, dt, dtt, conv_w, conv_b, dt_bias, a_log, d_skip, norm_g):
    B, S, _ = z.shape
    L = SSD_CHUNK
    nc = S // L
    pad_row = lambda a: jnp.pad(a.reshape(1, SSD_HEADS), ((0, 0), (0, LANES - SSD_HEADS)))
    tril = jnp.asarray(np.tril(np.ones((L, L), np.float32)))
    expand_np = np.zeros((LANES, SSD_INNER), np.float32)
    for h in range(SSD_HEADS):
        expand_np[h, h * SSD_HEAD_DIM:(h + 1) * SSD_HEAD_DIM] = 1.0
    ins = (z, xbc, dt, dtt, conv_w, conv_b.reshape(1, CONV_CH), pad_row(dt_bias),
           dt_bias.reshape(SSD_HEADS, 1), pad_row(a_log), a_log.reshape(SSD_HEADS, 1),
           jnp.repeat(d_skip, SSD_HEAD_DIM).reshape(1, SSD_INNER), norm_g.reshape(1, SSD_INNER),
           tril, jnp.asarray(expand_np))
    full = lambda a: pl.BlockSpec(a.shape, lambda b, c: (0,) * a.ndim)
    chunk = lambda n: pl.BlockSpec((1, L, n), lambda b, c: (b, c, 0))
    in_specs = [chunk(SSD_INNER), chunk(CONV_CH), chunk(LANES),
                pl.BlockSpec((SSD_HEADS, L), lambda b, c: (0, b * nc + c))]
    in_specs += [full(a) for a in ins[4:]]
    return pl.pallas_call(
        _ssd_body, grid=(B, nc), in_specs=in_specs, out_specs=chunk(SSD_INNER),
        out_shape=jax.ShapeDtypeStruct((B, S, SSD_INNER), F32),
        scratch_shapes=[pltpu.VMEM((L + SUBLANES, CONV_CH), F32),
                        pltpu.VMEM((SSD_GROUPS, SSD_STATE, SSD_HEADS_PER_GROUP * SSD_HEAD_DIM), F32)],
        compiler_params=_params("arbitrary", "arbitrary"), name="ssd")(*ins)


def _bucket_tile():
    bs = MOBA_BLOCK
    dist = np.arange(bs)[:, None] - np.arange(2 * bs)[None, :] + bs
    max_exact = REL_BUCKETS // 2
    d = np.maximum(dist, max_exact).astype(np.float32)
    large = max_exact + (np.log(d / np.float32(max_exact)) / np.float32(math.log(REL_MAX_DIST / max_exact))
                         * np.float32(REL_BUCKETS - max_exact)).astype(np.int32)
    large = np.minimum(large, REL_BUCKETS - 1)
    bucket = np.where(dist < max_exact, dist, large)
    return np.where(dist >= 0, bucket, -1).astype(np.int32)


def _bias_body(rb_ref, bucket_ref, o_ref):
    head = pl.program_id(0)
    bucket = bucket_ref[...]
    far = rb_ref[REL_BUCKETS - 1, head]
    bias = jnp.where(bucket < 0, MASK_VALUE, 0.0)
    for i in range(REL_BUCKETS - 1):
        bias = jnp.where(bucket == i, (rb_ref[i, head] - far) * LOG2E, bias)
    o_ref[0] = bias


def _bias_tiles(rel_bias):
    bs = MOBA_BLOCK
    bucket = jnp.asarray(_bucket_tile())
    return pl.pallas_call(
        _bias_body, grid=(ATT_HEADS,),
        in_specs=[pl.BlockSpec(memory_space=pltpu.SMEM), pl.BlockSpec((bs, 2 * bs), lambda h: (0, 0))],
        out_specs=pl.BlockSpec((1, bs, 2 * bs), lambda h: (h, 0, 0)),
        out_shape=jax.ShapeDtypeStruct((ATT_HEADS, bs, 2 * bs), F32),
        compiler_params=_params("arbitrary"), name="bias_tiles")(rel_bias, bucket)


def _attn_body(q_ref, k_ref, v_ref, qg_ref, kg_ref, bias_ref, o_ref,
               kaug_ref, vaug_ref, kmean_ref, qaug_ref, sa_ref, sb_ref, m_ref, acc_ref):
    bs, hd = MOBA_BLOCK, ATT_HEAD_DIM
    nb = k_ref.shape[1] // bs
    qb = pl.program_id(2)
    nt_dims = (((1,), (1,)), ((), ()))

    @pl.when(qb == 0)
    def _():
        blk = lax.broadcasted_iota(jnp.int32, (nb * bs, hd), 0) // bs
        lane = lax.broadcasted_iota(jnp.int32, (nb * bs, hd), 1)
        onehot = (blk == lane).astype(BF16)
        ones = jnp.ones((nb * bs, hd), BF16)
        for h in range(2):
            kh = k_ref[0, :, h * hd:(h + 1) * hd]
            kn = kh * lax.rsqrt(jnp.mean(kh * kh, axis=-1, keepdims=True) + EPS) * kg_ref[...]
            kmean_ref[h] = jnp.mean(kn.reshape(nb, bs, hd), axis=1)
            kaug_ref[h] = jnp.concatenate([kn.astype(BF16), onehot], axis=-1)
            vaug_ref[h] = jnp.concatenate([v_ref[0, :, h * hd:(h + 1) * hd].astype(BF16), ones], axis=-1)

    q = q_ref[0]
    for h in range(2):
        qh = q[:, h * hd:(h + 1) * hd]
        qn = qh * lax.rsqrt(jnp.mean(qh * qh, axis=-1, keepdims=True) + EPS) * qg_ref[...]
        gate = lax.dot_general(kmean_ref[h], qn, nt_dims, precision=HIGHEST,
                               preferred_element_type=F32)
        rowb = lax.broadcasted_iota(jnp.int32, (nb, bs), 0)
        past = rowb < qb
        g = jnp.where(past, gate, -jnp.inf)
        rank = jnp.zeros((nb, bs), jnp.int32)
        for m in range(nb):
            gm = g[m:m + 1, :]
            beats = (gm > g) | ((gm == g) & (rowb > m))
            rank = rank + beats.astype(jnp.int32)
        sel = (past & (rank < MOBA_TOPK)) | (rowb == qb)
        selmask = jnp.where(sel, 0.0, MASK_VALUE)
        mask_rows = jnp.concatenate(
            [jnp.zeros((hd, bs), F32), selmask, jnp.zeros((hd - nb, bs), F32)], axis=0)
        qs = jnp.concatenate([qn * (ATT_HEAD_DIM ** -0.5 * LOG2E), jnp.zeros((bs, hd), F32)], axis=-1)
        qaug_ref[h] = (qs + mask_rows.T).astype(BF16)
        m_ref[h] = jnp.full((bs, 2 * hd), -jnp.inf, F32)
        acc_ref[h] = jnp.zeros((bs, 2 * hd), F32)

    def scores(n, s_ref):
        start = pl.multiple_of(n * bs, bs)
        for h in range(2):
            s_ref[h] = lax.dot_general(qaug_ref[h], kaug_ref[h, pl.ds(start, bs), :], nt_dims,
                                       preferred_element_type=F32)

    def update(n, s_ref, bias):
        start = pl.multiple_of(n * bs, bs)
        for h in range(2):
            s = s_ref[h]
            if bias is not None:
                s = s + bias(h)
            m_i = m_ref[h]
            m_new = jnp.maximum(m_i, jnp.max(s, axis=-1, keepdims=True))
            alpha = jnp.exp2(m_i - m_new)
            p = jnp.exp2(s - jnp.concatenate([m_new, m_new], axis=-1)).astype(BF16)
            pv = jnp.dot(p, vaug_ref[h, pl.ds(start, bs), :], preferred_element_type=F32)
            acc_ref[h] = alpha * acc_ref[h] + pv
            m_ref[h] = m_new

    def far_pair(i, _):
        n = 2 * i
        scores(n + 1, sb_ref)
        update(n, sa_ref, None)
        scores(n + 2, sa_ref)
        update(n + 1, sb_ref, None)
        return 0

    n_far = jnp.maximum(qb - 1, 0)
    scores(0, sa_ref)
    lax.fori_loop(0, n_far >> 1, far_pair, 0)
    no_prev = jnp.where(qb == 0, MASK_VALUE, 0.0)
    prev_bias = lambda h: bias_ref[h, :, 0:bs] + no_prev
    own_bias = lambda h: bias_ref[h, :, bs:2 * bs]

    @pl.when((n_far & 1) == 1)
    def _():
        scores(n_far, sb_ref)
        update(n_far - 1, sa_ref, None)
        scores(qb, sa_ref)
        update(n_far, sb_ref, prev_bias)
        update(qb, sa_ref, own_bias)

    @pl.when((n_far & 1) == 0)
    def _():
        scores(qb, sb_ref)
        update(n_far, sa_ref, prev_bias)
        update(qb, sb_ref, own_bias)

    outs = []
    for h in range(2):
        acc = acc_ref[h]
        outs.append(acc[:, 0:hd] / acc[:, hd:hd + 1])
    o_ref[0] = jnp.concatenate(outs, axis=-1)


def _attention(q, k, v, q_g, k_g, bias_tiles):
    B, S, _ = q.shape
    bs = MOBA_BLOCK
    nb = S // bs
    assert S % bs == 0 and nb <= ATT_HEAD_DIM
    pair = 2 * ATT_HEAD_DIM
    return pl.pallas_call(
        _attn_body, grid=(B, ATT_HEADS // 2, nb),
        in_specs=[
            pl.BlockSpec((1, bs, pair), lambda b, hp, i: (b, i, hp)),
            pl.BlockSpec((1, S, pair), lambda b, hp, i: (b, 0, hp)),
            pl.BlockSpec((1, S, pair), lambda b, hp, i: (b, 0, hp)),
            pl.BlockSpec((1, ATT_HEAD_DIM), lambda b, hp, i: (0, 0)),
            pl.BlockSpec((1, ATT_HEAD_DIM), lambda b, hp, i: (0, 0)),
            pl.BlockSpec((2, bs, 2 * bs), lambda b, hp, i: (hp, 0, 0)),
        ],
        out_specs=pl.BlockSpec((1, bs, pair), lambda b, hp, i: (b, i, hp)),
        out_shape=jax.ShapeDtypeStruct((B, S, ATT_INNER), F32),
        scratch_shapes=[pltpu.VMEM((2, S, pair), BF16), pltpu.VMEM((2, S, pair), BF16),
                        pltpu.VMEM((2, nb, ATT_HEAD_DIM), F32), pltpu.VMEM((2, bs, pair), BF16),
                        pltpu.VMEM((2, bs, bs), F32), pltpu.VMEM((2, bs, bs), F32),
                        pltpu.VMEM((2, bs, pair), F32),
                        pltpu.VMEM((2, bs, pair), F32)],
        compiler_params=_params("arbitrary", "arbitrary", "arbitrary"), name="moba")(
            q, k, v, q_g.reshape(1, -1), k_g.reshape(1, -1), bias_tiles)


def _outproj_body(h_ref, ys_ref, ya_ref, wa_ref, wb_ref, g_ref, wrh_ref, wrl_ref, br_ref, tri_ref,
                  hn_ref, xn_ref, route_ref, routet_ref, counts_ref, carry_ref):
    i = pl.program_id(0)

    @pl.when(i == 0)
    def _():
        carry_ref[...] = jnp.zeros_like(carry_ref)

    hn = (h_ref[...]
          + jnp.dot(ys_ref[...].astype(BF16), wa_ref[...], preferred_element_type=F32)
          + jnp.dot(ya_ref[...].astype(BF16), wb_ref[...], preferred_element_type=F32))
    hn_ref[...] = hn
    xn = hn * lax.rsqrt(jnp.mean(hn * hn, axis=-1, keepdims=True) + EPS) * g_ref[...]
    for c in range(SUBLANES):
        xn_ref[:, c, :] = xn[:, c * LANES:(c + 1) * LANES]
    x_hi = xn.astype(BF16)
    x_lo = (xn - x_hi.astype(F32)).astype(BF16)
    logits = (jnp.dot(x_hi, wrh_ref[...], preferred_element_type=F32)
              + jnp.dot(x_lo, wrh_ref[...], preferred_element_type=F32)
              + jnp.dot(x_hi, wrl_ref[...], preferred_element_type=F32)) + br_ref[...]
    rows = logits.shape[0]
    lane = lax.broadcasted_iota(jnp.int32, (rows, LANES), 1)
    first = lambda hit: jnp.min(jnp.where(hit, lane, LANES), axis=-1, keepdims=True)
    l1 = jnp.where(lane < MOE_GROUPS, logits, -jnp.inf)
    mx = jnp.max(l1, axis=-1, keepdims=True)
    gval = 1.0 / jnp.sum(jnp.exp(l1 - mx), axis=-1, keepdims=True)
    gidx = first(l1 == mx)
    e_lane = lane - ROUTER_LANE0
    in_grp = (e_lane >= 0) & (e_lane < MOE_EXPERTS) & ((e_lane >> 3) == gidx)
    l2 = jnp.where(in_grp, logits, -jnp.inf)
    m1 = jnp.max(l2, axis=-1, keepdims=True)
    i1 = first(l2 == m1)
    l2b = jnp.where(lane == i1, -jnp.inf, l2)
    m2 = jnp.max(l2b, axis=-1, keepdims=True)
    i2 = first(l2b == m2)
    r = jnp.exp(m2 - m1)
    g1 = gval / (1.0 + r)
    g2 = gval * r / (1.0 + r)
    oh1 = (lane == i1)
    oh2 = (lane == i2)
    both = jnp.where(oh1 | oh2, 1.0, 0.0)
    cx = jnp.dot(tri_ref[...], both.astype(BF16), preferred_element_type=F32) + carry_ref[...]
    rank1 = jnp.sum(jnp.where(oh1, cx, 0.0), axis=-1, keepdims=True)
    rank2 = jnp.sum(jnp.where(oh2, cx, 0.0), axis=-1, keepdims=True)
    carry_ref[...] = carry_ref[...] + jnp.sum(both, axis=0, keepdims=True)
    counts_ref[...] = carry_ref[...]
    cols = ((i1 - ROUTER_LANE0).astype(F32), (i2 - ROUTER_LANE0).astype(F32), g1, g2, rank1, rank2)
    route = jnp.zeros((rows, LANES), F32)
    for j, cval in enumerate(cols):
        route = jnp.where(lane == j, cval, route)
    route_ref[...] = route
    routet_ref[...] = route.T[0:SUBLANES, :]


def _outproj_route(h2, y_ssd, y_att, w_out, norm_g, w_r1, b_r1, w_r2, b_r2):
    T = h2.shape[0]
    R = OUT_ROWS
    wb = w_out.astype(BF16)
    pad = LANES - MOE_GROUPS - MOE_EXPERTS
    wr = jnp.pad(jnp.concatenate([w_r1, w_r2], axis=1), ((0, 0), (0, pad)))
    wr_hi = wr.astype(BF16)
    wr_lo = (wr - wr_hi.astype(F32)).astype(BF16)
    br = jnp.pad(jnp.concatenate([b_r1, b_r2]), (0, pad)).reshape(1, LANES)
    tri = jnp.asarray(np.tril(np.ones((R, R), np.float32), -1), dtype=BF16)
    ins = (h2, y_ssd, y_att, wb[:SSD_INNER], wb[SSD_INNER:], norm_g.reshape(1, D_MODEL),
           wr_hi, wr_lo, br, tri)
    full = lambda a: pl.BlockSpec(a.shape, lambda i: (0,) * a.ndim)
    rows = lambda n: pl.BlockSpec((R, n), lambda i: (i, 0))
    return pl.pallas_call(
        _outproj_body, grid=(T // R,),
        in_specs=[rows(D_MODEL), rows(SSD_INNER), rows(ATT_INNER)] + [full(a) for a in ins[3:]],
        out_specs=(rows(D_MODEL), pl.BlockSpec((R, SUBLANES, LANES), lambda i: (i, 0, 0)), rows(LANES),
                   pl.BlockSpec((SUBLANES, R), lambda i: (0, i)), pl.BlockSpec((1, LANES), lambda i: (0, 0))),
        out_shape=(jax.ShapeDtypeStruct((T, D_MODEL), F32),
                   jax.ShapeDtypeStruct((T, SUBLANES, LANES), F32),
                   jax.ShapeDtypeStruct((T, LANES), F32), jax.ShapeDtypeStruct((SUBLANES, T), F32),
                   jax.ShapeDtypeStruct((1, LANES), F32)),
        scratch_shapes=[pltpu.VMEM((1, LANES), F32)],
        compiler_params=_params("arbitrary"), name="outproj_route")(*ins)


def _slots_body(routet_ref, pstart_ref, dest_ref):
    cols = routet_ref.shape[1]
    expert = lax.broadcasted_iota(jnp.int32, (MOE_EXPERTS, cols), 0).astype(F32)
    pstart = pstart_ref[...]
    for j in range(2):
        hit = expert == routet_ref[j:j + 1, :]
        base = jnp.sum(jnp.where(hit, pstart, 0.0), axis=0, keepdims=True)
        dest_ref[j:j + 1, :] = (base + routet_ref[4 + j:5 + j, :]).astype(jnp.int32)


def _slots(routet, pstarts):
    T = routet.shape[1]
    cols = min(T, 4096)
    assert T % cols == 0
    return pl.pallas_call(
        _slots_body, grid=(T // cols,),
        in_specs=[pl.BlockSpec((SUBLANES, cols), lambda i: (0, i)),
                  pl.BlockSpec((MOE_EXPERTS, 1), lambda i: (0, 0))],
        out_specs=pl.BlockSpec((2, cols), lambda i: (0, i)),
        out_shape=jax.ShapeDtypeStruct((2, T), jnp.int32),
        compiler_params=_params("arbitrary"), name="slots")(
            routet, pstarts.astype(F32).reshape(MOE_EXPERTS, 1))


def _tile_copy(src_ref, src_row, dst_ref, dst_row, sem):
    return pltpu.make_async_copy(src_ref.at[pl.ds(src_row, 1)], dst_ref.at[pl.ds(dst_row, 1)], sem)


def _dispatch_body(pad_ref, d1_ref, d2_ref, xn_ref, xb_ref, zero_ref, sem):
    n_blocks = xb_ref.shape[0] // EXPERT_ROWS
    n_used = pad_ref[MOE_EXPERTS]

    def zero_copy(row):
        return pltpu.make_async_copy(zero_ref, xb_ref.at[pl.ds(row, EXPERT_ROWS)], sem)

    @pl.when(pl.program_id(0) == 0)
    def _():
        zero_ref[...] = jnp.zeros_like(zero_ref)
        for e in range(MOE_EXPERTS):
            zero_copy(pad_ref[e]).start()
        for e in range(MOE_EXPERTS):
            zero_copy(pad_ref[e]).wait()
        for b in range(n_blocks - MOE_EXPERTS - 1, n_blocks):
            @pl.when(b >= n_used)
            def _():
                zero_copy(b * EXPERT_ROWS).start()
        for b in range(n_blocks - MOE_EXPERTS - 1, n_blocks):
            @pl.when(b >= n_used)
            def _():
                zero_copy(b * EXPERT_ROWS).wait()

    def start(t, _):
        _tile_copy(xn_ref, t, xb_ref, d1_ref[t], sem).start()
        _tile_copy(xn_ref, t, xb_ref, d2_ref[t], sem).start()
        return 0

    def wait(t, _):
        _tile_copy(xn_ref, t, xb_ref, d1_ref[t], sem).wait()
        _tile_copy(xn_ref, t, xb_ref, d2_ref[t], sem).wait()
        return 0

    lax.fori_loop(0, MOVE_ROWS, start, 0)
    lax.fori_loop(0, MOVE_ROWS, wait, 0)


def _dispatch(xn, dest1, dest2, pad_start, n_slots):
    T = xn.shape[0]
    idx = pl.BlockSpec((MOVE_ROWS,), lambda i, pad: (i,), memory_space=pltpu.SMEM)
    grid_spec = pltpu.PrefetchScalarGridSpec(
        num_scalar_prefetch=1, grid=(T // MOVE_ROWS,),
        in_specs=[idx, idx, pl.BlockSpec((MOVE_ROWS, SUBLANES, LANES), lambda i, pad: (i, 0, 0))],
        out_specs=pl.BlockSpec(memory_space=pl.ANY),
        scratch_shapes=[pltpu.VMEM((EXPERT_ROWS, SUBLANES, LANES), F32), pltpu.SemaphoreType.DMA(())])
    return pl.pallas_call(
        _dispatch_body, grid_spec=grid_spec,
        out_shape=jax.ShapeDtypeStruct((n_slots + EXPERT_ROWS, SUBLANES, LANES), F32),
        compiler_params=_params("arbitrary"), name="dispatch")(pad_start, dest1, dest2, xn)


def _expert_body(be_ref, nused_ref, xb_ref, wg_ref, wu_ref, wd_ref, yb_ref,
                 x_ref, wgb_ref, wub_ref, wdb_ref):
    i = pl.program_id(0)
    used = i < nused_ref[0]
    new_expert = (i == 0) | (be_ref[i] != be_ref[jnp.maximum(i - 1, 0)])

    @pl.when(used & new_expert)
    def _():
        wgb_ref[...] = wg_ref[0, 0].astype(BF16)
        wub_ref[...] = wu_ref[0, 0].astype(BF16)
        wdb_ref[...] = wd_ref[0, 0].astype(BF16)

    @pl.when(used)
    def _():
        for c in range(SUBLANES):
            x_ref[:, c * LANES:(c + 1) * LANES] = xb_ref[:, c, :].astype(BF16)
        x = x_ref[...]
        gate = jnp.dot(x, wgb_ref[...], preferred_element_type=F32)
        up = jnp.dot(x, wub_ref[...], preferred_element_type=F32)
        hmid = (_silu(gate) * up).astype(BF16)
        y = jnp.dot(hmid, wdb_ref[...], preferred_element_type=F32)
        for c in range(SUBLANES):
            yb_ref[:, c, :] = y[:, c * LANES:(c + 1) * LANES]

    @pl.when(jnp.logical_not(used))
    def _():
        yb_ref[...] = jnp.zeros_like(yb_ref)


def _experts(xb, blk_exp, n_used, n_blk, layer, w_gate, w_up, w_down):
    M = EXPERT_ROWS
    tile = lambda index: pl.BlockSpec((M, SUBLANES, LANES), index)
    weight = lambda a: pl.BlockSpec((1, 1) + a.shape[2:], lambda i, be, nu: (layer, be[i], 0, 0))
    grid_spec = pltpu.PrefetchScalarGridSpec(
        num_scalar_prefetch=2, grid=(n_blk,),
        in_specs=[tile(lambda i, be, nu: (jnp.minimum(i, nu[0] - 1), 0, 0)),
                  weight(w_gate), weight(w_up), weight(w_down)],
        out_specs=tile(lambda i, be, nu: (i, 0, 0)),
        scratch_shapes=[pltpu.VMEM((M, D_MODEL), BF16), pltpu.VMEM((D_MODEL, MOE_HIDDEN), BF16),
                        pltpu.VMEM((D_MODEL, MOE_HIDDEN), BF16), pltpu.VMEM((MOE_HIDDEN, D_MODEL), BF16)])
    return pl.pallas_call(
        _expert_body, grid_spec=grid_spec,
        out_shape=jax.ShapeDtypeStruct((n_blk * M, SUBLANES, LANES), F32),
        compiler_params=_params("arbitrary"), name="experts")(
            blk_exp, n_used, xb, w_gate, w_up, w_down)


def _combine_body(d1_ref, d2_ref, h_ref, route_ref, yb_ref, o_ref, buf_ref, sem):
    def start(t, _):
        _tile_copy(yb_ref, d1_ref[t], buf_ref.at[0], t, sem).start()
        _tile_copy(yb_ref, d2_ref[t], buf_ref.at[1], t, sem).start()
        return 0

    def wait(t, _):
        _tile_copy(yb_ref, d1_ref[t], buf_ref.at[0], t, sem).wait()
        _tile_copy(yb_ref, d2_ref[t], buf_ref.at[1], t, sem).wait()
        return 0

    lax.fori_loop(0, MOVE_ROWS, start, 0)
    lax.fori_loop(0, MOVE_ROWS, wait, 0)
    route = route_ref[...]
    g1 = route[:, 2:3]
    g2 = route[:, 3:4]
    for c in range(SUBLANES):
        cs = slice(c * LANES, (c + 1) * LANES)
        o_ref[:, cs] = h_ref[:, cs] + (buf_ref[0, :, c, :] * g1 + buf_ref[1, :, c, :] * g2)


def _combine(h2, route, yb, dest1, dest2):
    T = h2.shape[0]
    R = MOVE_ROWS
    idx = pl.BlockSpec((R,), lambda i: (i,), memory_space=pltpu.SMEM)
    return pl.pallas_call(
        _combine_body, grid=(T // R,),
        in_specs=[idx, idx, pl.BlockSpec((R, D_MODEL), lambda i: (i, 0)),
                  pl.BlockSpec((R, LANES), lambda i: (i, 0)), pl.BlockSpec(memory_space=pl.ANY)],
        out_specs=pl.BlockSpec((R, D_MODEL), lambda i: (i, 0)),
        out_shape=jax.ShapeDtypeStruct((T, D_MODEL), F32),
        scratch_shapes=[pltpu.VMEM((2, R, SUBLANES, LANES), F32), pltpu.SemaphoreType.DMA(())],
        compiler_params=_params("arbitrary"), name="combine")(dest1, dest2, h2, route, yb)


def _moe(h2, xn, route, routet, counts, layer, w_gate, w_up, w_down):
    T = h2.shape[0]
    M = EXPERT_ROWS
    n_blk = (2 * T) // M + MOE_EXPERTS
    cnt = counts[0, ROUTER_LANE0:ROUTER_LANE0 + MOE_EXPERTS].astype(jnp.int32)
    padded = (cnt + M - 1) // M * M
    ends = jnp.cumsum(padded)
    pstarts = ends - padded
    n_used = (ends[-1] // M).astype(jnp.int32)
    blk = jnp.minimum(jnp.arange(n_blk, dtype=jnp.int32), n_used - 1) * M
    blk_exp = jnp.sum((ends[None, :] <= blk[:, None]).astype(jnp.int32), axis=1)
    blk_exp = jnp.minimum(blk_exp, MOE_EXPERTS - 1)
    dest = _slots(routet, pstarts)
    pad_info = jnp.concatenate([pstarts + cnt, n_used.reshape(1)])
    xb = _dispatch(xn, dest[0], dest[1], pad_info, n_blk * M)
    yb = _experts(xb, blk_exp, n_used.reshape(1), n_blk, layer, w_gate, w_up, w_down)
    return _combine(h2, route, yb, dest[0], dest[1])


def kernel(x, rel_bias, norm1_g, w_in, conv_w, conv_b, dt_bias, a_log, d_skip, ssd_norm_g,
           q_norm_g, k_norm_g, w_out, norm2_g, w_r1, b_r1, w_r2, b_r2, w_gate, w_up, w_down):
    B, S, D = x.shape
    T = B * S
    h2 = x.reshape(T, D)
    depth = w_in.shape[0]
    bias_tiles = _bias_tiles(rel_bias)
    for l in range(depth):
        z, xbc, dt, dtt, q, k, v = _inproj(h2, norm1_g[l], w_in[l])
        r3 = lambda a: a.reshape(B, S, a.shape[-1])
        y_ssd = _ssd(r3(z), r3(xbc), r3(dt), dtt, conv_w[l], conv_b[l], dt_bias[l], a_log[l],
                     d_skip[l], ssd_norm_g[l])
        y_att = _attention(r3(q), r3(k), r3(v), q_norm_g[l], k_norm_g[l], bias_tiles)
        h2, xn, route, routet, counts = _outproj_route(
            h2, y_ssd.reshape(T, SSD_INNER), y_att.reshape(T, ATT_INNER), w_out[l], norm2_g[l],
            w_r1[l], b_r1[l], w_r2[l], b_r2[l])
        h2 = _moe(h2, xn, route, routet, counts, l, w_gate, w_up, w_down)
    return h2.reshape(B, S, D)
```

```python
import functools
import math

import numpy as np
import jax
import jax.numpy as jnp
from jax import lax
from jax.experimental import pallas as pl
from jax.experimental.pallas import tpu as pltpu

F32 = jnp.float32
BF16 = jnp.bfloat16
HIGHEST = lax.Precision.HIGHEST

LANES = 128
SUBLANES = 8

D_MODEL = 1024
SSD_HEADS = 8
SSD_HEAD_DIM = 64
SSD_INNER = SSD_HEADS * SSD_HEAD_DIM
SSD_GROUPS = 2
SSD_HEADS_PER_GROUP = SSD_HEADS // SSD_GROUPS
SSD_STATE = 64
SSD_CONV = 4
SSD_CHUNK = 128
CONV_CH = SSD_INNER + 2 * SSD_GROUPS * SSD_STATE
ATT_HEADS = 8
ATT_HEAD_DIM = 64
ATT_INNER = ATT_HEADS * ATT_HEAD_DIM
MOBA_BLOCK = 256
MOBA_TOPK = 3
REL_BUCKETS = 32
REL_MAX_DIST = 128
MOE_GROUPS = 4
MOE_EXPERTS_PER_GROUP = 8
MOE_EXPERTS = MOE_GROUPS * MOE_EXPERTS_PER_GROUP
MOE_HIDDEN = 512
EPS = 1e-6

MASK_VALUE = -1e30
LOG2E = math.log2(math.e)
IN_ROWS = 512
OUT_ROWS = 256
EXPERT_ROWS = 256
MOVE_ROWS = 256
VMEM_LIMIT = 56 * 1024 * 1024
ROUTER_LANE0 = MOE_GROUPS


def _silu(x):
    return x * (1.0 / (1.0 + jnp.exp(-x)))


def _softplus(x):
    return jnp.maximum(x, 0.0) + jnp.log(1.0 + jnp.exp(-jnp.abs(x)))


def _params(*sem):
    return pltpu.CompilerParams(dimension_semantics=sem, vmem_limit_bytes=VMEM_LIMIT)


def _inproj_body(h_ref, g_ref, wz_ref, wx_ref, wdt_ref, wdtt_ref, wq_ref, wk_ref, wv_ref,
                 z_ref, xbc_ref, dt_ref, dtt_ref, q_ref, k_ref, v_ref):
    x = h_ref[...]
    xn = x * lax.rsqrt(jnp.mean(x * x, axis=-1, keepdims=True) + EPS) * g_ref[...]
    xb = xn.astype(BF16)
    z_ref[...] = jnp.dot(xb, wz_ref[...], preferred_element_type=F32)
    xbc_ref[...] = jnp.dot(xb, wx_ref[...], preferred_element_type=F32)
    dt_ref[...] = jnp.dot(xb, wdt_ref[...], preferred_element_type=F32)
    dtt_ref[...] = lax.dot_general(wdtt_ref[...], xb, (((1,), (1,)), ((), ())),
                                   preferred_element_type=F32)
    q_ref[...] = jnp.dot(xb, wq_ref[...], preferred_element_type=F32)
    k_ref[...] = jnp.dot(xb, wk_ref[...], preferred_element_type=F32)
    v_ref[...] = jnp.dot(xb, wv_ref[...], preferred_element_type=F32)


def _inproj(h2, g, w_in):
    T = h2.shape[0]
    c0, c1, c2 = SSD_INNER, SSD_INNER + CONV_CH, SSD_INNER + CONV_CH + SSD_HEADS
    wb = w_in.astype(BF16)
    wz, wx, wdt = wb[:, :c0], wb[:, c0:c1], wb[:, c1:c2]
    wq, wk, wv = (wb[:, c2 + i * ATT_INNER:c2 + (i + 1) * ATT_INNER] for i in range(3))
    wdt_pad = jnp.pad(wdt, ((0, 0), (0, LANES - SSD_HEADS)))
    wdtt = wdt.T
    full = lambda a: pl.BlockSpec(a.shape, lambda i: (0,) * a.ndim)
    rows = lambda n: pl.BlockSpec((IN_ROWS, n), lambda i: (i, 0))
    ins = (h2, g.reshape(1, D_MODEL), wz, wx, wdt_pad, wdtt, wq, wk, wv)
    out_shape = (
        jax.ShapeDtypeStruct((T, SSD_INNER), F32), jax.ShapeDtypeStruct((T, CONV_CH), F32),
        jax.ShapeDtypeStruct((T, LANES), F32), jax.ShapeDtypeStruct((SSD_HEADS, T), F32),
        jax.ShapeDtypeStruct((T, ATT_INNER), F32), jax.ShapeDtypeStruct((T, ATT_INNER), F32),
        jax.ShapeDtypeStruct((T, ATT_INNER), F32))
    out_specs = (rows(SSD_INNER), rows(CONV_CH), rows(LANES),
                 pl.BlockSpec((SSD_HEADS, IN_ROWS), lambda i: (0, i)),
                 rows(ATT_INNER), rows(ATT_INNER), rows(ATT_INNER))
    return pl.pallas_call(
        _inproj_body, grid=(T // IN_ROWS,),
        in_specs=[rows(D_MODEL)] + [full(a) for a in ins[1:]],
        out_specs=out_specs, out_shape=out_shape,
        compiler_params=_params("arbitrary"), name="inproj")(*ins)


def _ssd_body(z_ref, xbc_ref, dt_ref, dtt_ref, cw_ref, cb_ref, dtb_ref, dtbt_ref, alog_ref,
              alogt_ref, dskip_ref, ng_ref, tril_ref, expand_ref, y_ref, ext_ref, state_ref):
    L, P, N, E = SSD_CHUNK, SSD_HEAD_DIM, SSD_STATE, SSD_HEADS_PER_GROUP
    c = pl.program_id(1)

    @pl.when(c == 0)
    def _():
        state_ref[...] = jnp.zeros_like(state_ref)
        ext_ref[0:SUBLANES, :] = jnp.zeros((SUBLANES, CONV_CH), F32)

    ext_ref[SUBLANES:SUBLANES + L, :] = xbc_ref[0]
    conv = jnp.zeros((L, CONV_CH), F32) + cb_ref[...]
    for j in range(SSD_CONV):
        off = SUBLANES - (SSD_CONV - 1) + j
        conv = conv + cw_ref[j:j + 1, :] * ext_ref[off:off + L, :]
    ext_ref[0:SUBLANES, :] = ext_ref[L:L + SUBLANES, :]
    act = _silu(conv)
    xs = act[:, :SSD_INNER]
    b_in = act[:, SSD_INNER:SSD_INNER + SSD_GROUPS * N]
    c_out = act[:, SSD_INNER + SSD_GROUPS * N:]

    tril = tril_ref[...]
    expand = expand_ref[...]
    dt = _softplus(dt_ref[0] + dtb_ref[...])
    a_dt = dt * (-jnp.exp(alog_ref[...]))
    a_cs = jnp.dot(tril, a_dt, precision=HIGHEST, preferred_element_type=F32)
    dt_t = _softplus(dtt_ref[...] + dtbt_ref[...])
    a_dt_t = dt_t * (-jnp.exp(alogt_ref[...]))
    a_cs_t = lax.dot_general(a_dt_t, tril, (((1,), (1,)), ((), ())), precision=HIGHEST,
                             preferred_element_type=F32)
    last = a_cs[L - 1:L, :]
    hdot = lambda a: jnp.dot(a, expand, precision=HIGHEST, preferred_element_type=F32)
    dt_x = hdot(dt)
    eacs_x = hdot(jnp.exp(a_cs))
    dec_x = hdot(jnp.exp(last - a_cs))
    cdec_x = hdot(jnp.broadcast_to(jnp.exp(last), (SUBLANES, LANES)))[0:1, :]
    xdt = xs * dt_x
    xdtd = (xdt * dec_x).astype(BF16)
    xdt_b = xdt.astype(BF16)
    row = lax.broadcasted_iota(jnp.int32, (L, L), 0)
    col = lax.broadcasted_iota(jnp.int32, (L, L), 1)
    causal = row >= col

    y_parts = []
    for g in range(SSD_GROUPS):
        bg = b_in[:, g * N:(g + 1) * N].astype(BF16)
        cg = c_out[:, g * N:(g + 1) * N].astype(BF16)
        cb = lax.dot_general(cg, bg, (((1,), (1,)), ((), ())), preferred_element_type=F32)
        st = state_ref[g]
        gs = slice(g * E * P, (g + 1) * E * P)
        y_off = jnp.dot(cg, st.astype(BF16), preferred_element_type=F32) * eacs_x[:, gs]
        for e in range(E):
            h = g * E + e
            diff = a_cs[:, h:h + 1] - a_cs_t[h:h + 1, :]
            l_mat = jnp.exp(jnp.where(causal, diff, -jnp.inf))
            m = (cb * l_mat).astype(BF16)
            y_diag = jnp.dot(m, xdt_b[:, h * P:(h + 1) * P], preferred_element_type=F32)
            y_parts.append(y_diag + y_off[:, e * P:(e + 1) * P])
        upd = lax.dot_general(bg, xdtd[:, gs], (((0,), (0,)), ((), ())),
                              preferred_element_type=F32)
        state_ref[g] = cdec_x[:, gs] * st + upd
    y = jnp.concatenate(y_parts, axis=-1) + dskip_ref[...] * xs
    y = y * _silu(z_ref[0])
    y = y * lax.rsqrt(jnp.mean(y * y, axis=-1, keepdims=True) + EPS) * ng_ref[...]
    y_ref[0] = y


def _ssd(z, xbc, dt, dtt, conv_w, conv_b, dt_bias, a_log, d_skip, norm_g):
    B, S, _ = z.shape
    L = SSD_CHUNK
    nc = S // L
    pad_row = lambda a: jnp.pad(a.reshape(1, SSD_HEADS), ((0, 0), (0, LANES - SSD_HEADS)))
    tril = jnp.asarray(np.tril(np.ones((L, L), np.float32)))
    expand_np = np.zeros((LANES, SSD_INNER), np.float32)
    for h in range(SSD_HEADS):
        expand_np[h, h * SSD_HEAD_DIM:(h + 1) * SSD_HEAD_DIM] = 1.0
    ins = (z, xbc, dt, dtt, conv_w, conv_b.reshape(1, CONV_CH), pad_row(dt_bias),
           dt_bias.reshape(SSD_HEADS, 1), pad_row(a_log), a_log.reshape(SSD_HEADS, 1),
           jnp.repeat(d_skip, SSD_HEAD_DIM).reshape(1, SSD_INNER), norm_g.reshape(1, SSD_INNER),
           tril, jnp.asarray(expand_np))
    full = lambda a: pl.BlockSpec(a.shape, lambda b, c: (0,) * a.ndim)
    chunk = lambda n: pl.BlockSpec((1, L, n), lambda b, c: (b, c, 0))
    in_specs = [chunk(SSD_INNER), chunk(CONV_CH), chunk(LANES),
                pl.BlockSpec((SSD_HEADS, L), lambda b, c: (0, b * nc + c))]
    in_specs += [full(a) for a in ins[4:]]
    return pl.pallas_call(
        _ssd_body, grid=(B, nc), in_specs=in_specs, out_specs=chunk(SSD_INNER),
        out_shape=jax.ShapeDtypeStruct((B, S, SSD_INNER), F32),
        scratch_shapes=[pltpu.VMEM((L + SUBLANES, CONV_CH), F32),
                        pltpu.VMEM((SSD_GROUPS, SSD_STATE, SSD_HEADS_PER_GROUP * SSD_HEAD_DIM), F32)],
        compiler_params=_params("arbitrary", "arbitrary"), name="ssd")(*ins)


def _bucket_tile():
    bs = MOBA_BLOCK
    dist = np.arange(bs)[:, None] - np.arange(2 * bs)[None, :] + bs
    max_exact = REL_BUCKETS // 2
    d = np.maximum(dist, max_exact).astype(np.float32)
    large = max_exact + (np.log(d / np.float32(max_exact)) / np.float32(math.log(REL_MAX_DIST / max_exact))
                         * np.float32(REL_BUCKETS - max_exact)).astype(np.int32)
    large = np.minimum(large, REL_BUCKETS - 1)
    bucket = np.where(dist < max_exact, dist, large)
    return np.where(dist >= 0, bucket, -1).astype(np.int32)


def _bias_body(rb_ref, bucket_ref, o_ref):
    head = pl.program_id(0)
    bucket = bucket_ref[...]
    far = rb_ref[REL_BUCKETS - 1, head]
    bias = jnp.where(bucket < 0, MASK_VALUE, 0.0)
    for i in range(REL_BUCKETS - 1):
        bias = jnp.where(bucket == i, (rb_ref[i, head] - far) * LOG2E, bias)
    o_ref[0] = bias


def _bias_tiles(rel_bias):
    bs = MOBA_BLOCK
    bucket = jnp.asarray(_bucket_tile())
    return pl.pallas_call(
        _bias_body, grid=(ATT_HEADS,),
        in_specs=[pl.BlockSpec(memory_space=pltpu.SMEM), pl.BlockSpec((bs, 2 * bs), lambda h: (0, 0))],
        out_specs=pl.BlockSpec((1, bs, 2 * bs), lambda h: (h, 0, 0)),
        out_shape=jax.ShapeDtypeStruct((ATT_HEADS, bs, 2 * bs), F32),
        compiler_params=_params("arbitrary"), name="bias_tiles")(rel_bias, bucket)


def _attn_body(q_ref, k_ref, v_ref, qg_ref, kg_ref, bias_ref, o_ref,
               kaug_ref, vaug_ref, kmean_ref, qaug_ref, sa_ref, sb_ref, m_ref, acc_ref):
    bs, hd = MOBA_BLOCK, ATT_HEAD_DIM
    nb = k_ref.shape[1] // bs
    qb = pl.program_id(2)
    nt_dims = (((1,), (1,)), ((), ()))

    @pl.when(qb == 0)
    def _():
        blk = lax.broadcasted_iota(jnp.int32, (nb * bs, hd), 0) // bs
        lane = lax.broadcasted_iota(jnp.int32, (nb * bs, hd), 1)
        onehot = (blk == lane).astype(BF16)
        ones = jnp.ones((nb * bs, hd), BF16)
        for h in range(2):
            kh = k_ref[0, :, h * hd:(h + 1) * hd]
            kn = kh * lax.rsqrt(jnp.mean(kh * kh, axis=-1, keepdims=True) + EPS) * kg_ref[...]
            kmean_ref[h] = jnp.mean(kn.reshape(nb, bs, hd), axis=1)
            kaug_ref[h] = jnp.concatenate([kn.astype(BF16), onehot], axis=-1)
            vaug_ref[h] = jnp.concatenate([v_ref[0, :, h * hd:(h + 1) * hd].astype(BF16), ones], axis=-1)

    q = q_ref[0]
    for h in range(2):
        qh = q[:, h * hd:(h + 1) * hd]
        qn = qh * lax.rsqrt(jnp.mean(qh * qh, axis=-1, keepdims=True) + EPS) * qg_ref[...]
        gate = lax.dot_general(kmean_ref[h], qn, nt_dims, precision=HIGHEST,
                               preferred_element_type=F32)
        rowb = lax.broadcasted_iota(jnp.int32, (nb, bs), 0)
        past = rowb < qb
        g = jnp.where(past, gate, -jnp.inf)
        rank = jnp.zeros((nb, bs), jnp.int32)
        for m in range(nb):
            gm = g[m:m + 1, :]
            beats = (gm > g) | ((gm == g) & (rowb > m))
            rank = rank + beats.astype(jnp.int32)
        sel = (past & (rank < MOBA_TOPK)) | (rowb == qb)
        selmask = jnp.where(sel, 0.0, MASK_VALUE)
        mask_rows = jnp.concatenate(
            [jnp.zeros((hd, bs), F32), selmask, jnp.zeros((hd - nb, bs), F32)], axis=0)
        qs = jnp.concatenate([qn * (ATT_HEAD_DIM ** -0.5 * LOG2E), jnp.zeros((bs, hd), F32)], axis=-1)
        qaug_ref[h] = (qs + mask_rows.T).astype(BF16)
        m_ref[h] = jnp.full((bs, 2 * hd), -jnp.inf, F32)
        acc_ref[h] = jnp.zeros((bs, 2 * hd), F32)

    def scores(n, s_ref):
        start = pl.multiple_of(n * bs, bs)
        for h in range(2):
            s_ref[h] = lax.dot_general(qaug_ref[h], kaug_ref[h, pl.ds(start, bs), :], nt_dims,
                                       preferred_element_type=F32)

    def update(n, s_ref, bias):
        start = pl.multiple_of(n * bs, bs)
        for h in range(2):
            s = s_ref[h]
            if bias is not None:
                s = s + bias(h)
            m_i = m_ref[h]
            m_new = jnp.maximum(m_i, jnp.max(s, axis=-1, keepdims=True))
            alpha = jnp.exp2(m_i - m_new)
            p = jnp.exp2(s - jnp.concatenate([m_new, m_new], axis=-1)).astype(BF16)
            pv = jnp.dot(p, vaug_ref[h, pl.ds(start, bs), :], preferred_element_type=F32)
            acc_ref[h] = alpha * acc_ref[h] + pv
            m_ref[h] = m_new

    def far_pair(i, _):
        n = 2 * i
        scores(n + 1, sb_ref)
        update(n, sa_ref, None)
        scores(n + 2, sa_ref)
        update(n + 1, sb_ref, None)
        return 0

    n_far = jnp.maximum(qb - 1, 0)
    scores(0, sa_ref)
    lax.fori_loop(0, n_far >> 1, far_pair, 0)
    no_prev = jnp.where(qb == 0, MASK_VALUE, 0.0)
    prev_bias = lambda h: bias_ref[h, :, 0:bs] + no_prev
    own_bias = lambda h: bias_ref[h, :, bs:2 * bs]

    @pl.when((n_far & 1) == 1)
    def _():
        scores(n_far, sb_ref)
        update(n_far - 1, sa_ref, None)
        scores(qb, sa_ref)
        update(n_far, sb_ref, prev_bias)
        update(qb, sa_ref, own_bias)

    @pl.when((n_far & 1) == 0)
    def _():
        scores(qb, sb_ref)
        update(n_far, sa_ref, prev_bias)
        update(qb, sb_ref, own_bias)

    outs = []
    for h in range(2):
        acc = acc_ref[h]
        outs.append(acc[:, 0:hd] / acc[:, hd:hd + 1])
    o_ref[0] = jnp.concatenate(outs, axis=-1)


def _attention(q, k, v, q_g, k_g, bias_tiles):
    B, S, _ = q.shape
    bs = MOBA_BLOCK
    nb = S // bs
    assert S % bs == 0 and nb <= ATT_HEAD_DIM
    pair = 2 * ATT_HEAD_DIM
    return pl.pallas_call(
        _attn_body, grid=(B, ATT_HEADS // 2, nb),
        in_specs=[
            pl.BlockSpec((1, bs, pair), lambda b, hp, i: (b, i, hp)),
            pl.BlockSpec((1, S, pair), lambda b, hp, i: (b, 0, hp)),
            pl.BlockSpec((1, S, pair), lambda b, hp, i: (b, 0, hp)),
            pl.BlockSpec((1, ATT_HEAD_DIM), lambda b, hp, i: (0, 0)),
            pl.BlockSpec((1, ATT_HEAD_DIM), lambda b, hp, i: (0, 0)),
            pl.BlockSpec((2, bs, 2 * bs), lambda b, hp, i: (hp, 0, 0)),
        ],
        out_specs=pl.BlockSpec((1, bs, pair), lambda b, hp, i: (b, i, hp)),
        out_shape=jax.ShapeDtypeStruct((B, S, ATT_INNER), F32),
        scratch_shapes=[pltpu.VMEM((2, S, pair), BF16), pltpu.VMEM((2, S, pair), BF16),
                        pltpu.VMEM((2, nb, ATT_HEAD_DIM), F32), pltpu.VMEM((2, bs, pair), BF16),
                        pltpu.VMEM((2, bs, bs), F32), pltpu.VMEM((2, bs, bs), F32),
                        pltpu.VMEM((2, bs, pair), F32),
                        pltpu.VMEM((2, bs, pair), F32)],
        compiler_params=_params("arbitrary", "arbitrary", "arbitrary"), name="moba")(
            q, k, v, q_g.reshape(1, -1), k_g.reshape(1, -1), bias_tiles)


def _outproj_body(h_ref, ys_ref, ya_ref, wa_ref, wb_ref, g_ref, wrh_ref, wrl_ref, br_ref, tri_ref,
                  hn_ref, xn_ref, route_ref, routet_ref, counts_ref, carry_ref):
    i = pl.program_id(0)

    @pl.when(i == 0)
    def _():
        carry_ref[...] = jnp.zeros_like(carry_ref)

    hn = (h_ref[...]
          + jnp.dot(ys_ref[...].astype(BF16), wa_ref[...], preferred_element_type=F32)
          + jnp.dot(ya_ref[...].astype(BF16), wb_ref[...], preferred_element_type=F32))
    hn_ref[...] = hn
    xn = hn * lax.rsqrt(jnp.mean(hn * hn, axis=-1, keepdims=True) + EPS) * g_ref[...]
    for c in range(SUBLANES):
        xn_ref[pl.ds(c, xn.shape[0], stride=SUBLANES), :] = xn[:, c * LANES:(c + 1) * LANES]
    x_hi = xn.astype(BF16)
    x_lo = (xn - x_hi.astype(F32)).astype(BF16)
    logits = (jnp.dot(x_hi, wrh_ref[...], preferred_element_type=F32)
              + jnp.dot(x_lo, wrh_ref[...], preferred_element_type=F32)
              + jnp.dot(x_hi, wrl_ref[...], preferred_element_type=F32)) + br_ref[...]
    rows = logits.shape[0]
    lane = lax.broadcasted_iota(jnp.int32, (rows, LANES), 1)
    first = lambda hit: jnp.min(jnp.where(hit, lane, LANES), axis=-1, keepdims=True)
    l1 = jnp.where(lane < MOE_GROUPS, logits, -jnp.inf)
    mx = jnp.max(l1, axis=-1, keepdims=True)
    gval = 1.0 / jnp.sum(jnp.exp(l1 - mx), axis=-1, keepdims=True)
    gidx = first(l1 == mx)
    e_lane = lane - ROUTER_LANE0
    in_grp = (e_lane >= 0) & (e_lane < MOE_EXPERTS) & ((e_lane >> 3) == gidx)
    l2 = jnp.where(in_grp, logits, -jnp.inf)
    m1 = jnp.max(l2, axis=-1, keepdims=True)
    i1 = first(l2 == m1)
    l2b = jnp.where(lane == i1, -jnp.inf, l2)
    m2 = jnp.max(l2b, axis=-1, keepdims=True)
    i2 = first(l2b == m2)
    r = jnp.exp(m2 - m1)
    g1 = gval / (1.0 + r)
    g2 = gval * r / (1.0 + r)
    oh1 = (lane == i1)
    oh2 = (lane == i2)
    both = jnp.where(oh1 | oh2, 1.0, 0.0)
    cx = jnp.dot(tri_ref[...], both.astype(BF16), preferred_element_type=F32) + carry_ref[...]
    rank1 = jnp.sum(jnp.where(oh1, cx, 0.0), axis=-1, keepdims=True)
    rank2 = jnp.sum(jnp.where(oh2, cx, 0.0), axis=-1, keepdims=True)
    carry_ref[...] = carry_ref[...] + jnp.sum(both, axis=0, keepdims=True)
    counts_ref[...] = carry_ref[...]
    cols = ((i1 - ROUTER_LANE0).astype(F32), (i2 - ROUTER_LANE0).astype(F32), g1, g2, rank1, rank2)
    route = jnp.zeros((rows, LANES), F32)
    for j, cval in enumerate(cols):
        route = jnp.where(lane == j, cval, route)
    route_ref[...] = route
    routet_ref[...] = route.T[0:SUBLANES, :]


def _outproj_route(h2, y_ssd, y_att, w_out, norm_g, w_r1, b_r1, w_r2, b_r2):
    T = h2.shape[0]
    R = OUT_ROWS
    wb = w_out.astype(BF16)
    pad = LANES - MOE_GROUPS - MOE_EXPERTS
    wr = jnp.pad(jnp.concatenate([w_r1, w_r2], axis=1), ((0, 0), (0, pad)))
    wr_hi = wr.astype(BF16)
    wr_lo = (wr - wr_hi.astype(F32)).astype(BF16)
    br = jnp.pad(jnp.concatenate([b_r1, b_r2]), (0, pad)).reshape(1, LANES)
    tri = jnp.asarray(np.tril(np.ones((R, R), np.float32), -1), dtype=BF16)
    ins = (h2, y_ssd, y_att, wb[:SSD_INNER], wb[SSD_INNER:], norm_g.reshape(1, D_MODEL),
           wr_hi, wr_lo, br, tri)
    full = lambda a: pl.BlockSpec(a.shape, lambda i: (0,) * a.ndim)
    rows = lambda n: pl.BlockSpec((R, n), lambda i: (i, 0))
    return pl.pallas_call(
        _outproj_body, grid=(T // R,),
        in_specs=[rows(D_MODEL), rows(SSD_INNER), rows(ATT_INNER)] + [full(a) for a in ins[3:]],
        out_specs=(rows(D_MODEL), pl.BlockSpec((R * SUBLANES, LANES), lambda i: (i, 0)), rows(LANES),
                   pl.BlockSpec((SUBLANES, R), lambda i: (0, i)), pl.BlockSpec((1, LANES), lambda i: (0, 0))),
        out_shape=(jax.ShapeDtypeStruct((T, D_MODEL), F32),
                   jax.ShapeDtypeStruct((T * SUBLANES, LANES), F32),
                   jax.ShapeDtypeStruct((T, LANES), F32), jax.ShapeDtypeStruct((SUBLANES, T), F32),
                   jax.ShapeDtypeStruct((1, LANES), F32)),
        scratch_shapes=[pltpu.VMEM((1, LANES), F32)],
        compiler_params=_params("arbitrary"), name="outproj_route")(*ins)


def _slots_body(routet_ref, pstart_ref, dest_ref):
    cols = routet_ref.shape[1]
    expert = lax.broadcasted_iota(jnp.int32, (MOE_EXPERTS, cols), 0).astype(F32)
    pstart = pstart_ref[...]
    for j in range(2):
        hit = expert == routet_ref[j:j + 1, :]
        base = jnp.sum(jnp.where(hit, pstart, 0.0), axis=0, keepdims=True)
        dest_ref[j:j + 1, :] = (base + routet_ref[4 + j:5 + j, :]).astype(jnp.int32)


def _slots(routet, pstarts):
    T = routet.shape[1]
    cols = min(T, 4096)
    assert T % cols == 0
    return pl.pallas_call(
        _slots_body, grid=(T // cols,),
        in_specs=[pl.BlockSpec((SUBLANES, cols), lambda i: (0, i)),
                  pl.BlockSpec((MOE_EXPERTS, 1), lambda i: (0, 0))],
        out_specs=pl.BlockSpec((2, cols), lambda i: (0, i)),
        out_shape=jax.ShapeDtypeStruct((2, T), jnp.int32),
        compiler_params=_params("arbitrary"), name="slots")(
            routet, pstarts.astype(F32).reshape(MOE_EXPERTS, 1))


def _tile_copy(src_ref, src_row, dst_ref, dst_row, sem):
    tile = lambda row: pl.ds(pl.multiple_of(row * SUBLANES, SUBLANES), SUBLANES)
    return pltpu.make_async_copy(src_ref.at[tile(src_row)], dst_ref.at[tile(dst_row)], sem)


def _dispatch_body(pad_ref, d1_ref, d2_ref, xn_ref, xb_ref, zero_ref, sem):
    block = EXPERT_ROWS * SUBLANES
    n_blocks = xb_ref.shape[0] // block
    n_used = pad_ref[MOE_EXPERTS]

    def zero_copy(row):
        start = pl.multiple_of(row * SUBLANES, SUBLANES)
        return pltpu.make_async_copy(zero_ref, xb_ref.at[pl.ds(start, block)], sem)

    @pl.when(pl.program_id(0) == 0)
    def _():
        zero_ref[...] = jnp.zeros_like(zero_ref)
        for e in range(MOE_EXPERTS):
            zero_copy(pad_ref[e]).start()
        for e in range(MOE_EXPERTS):
            zero_copy(pad_ref[e]).wait()
        for b in range(n_blocks - MOE_EXPERTS - 1, n_blocks):
            @pl.when(b >= n_used)
            def _():
                zero_copy(b * EXPERT_ROWS).start()
        for b in range(n_blocks - MOE_EXPERTS - 1, n_blocks):
            @pl.when(b >= n_used)
            def _():
                zero_copy(b * EXPERT_ROWS).wait()

    def start(t, _):
        _tile_copy(xn_ref, t, xb_ref, d1_ref[t], sem).start()
        _tile_copy(xn_ref, t, xb_ref, d2_ref[t], sem).start()
        return 0

    def wait(t, _):
        _tile_copy(xn_ref, t, xb_ref, d1_ref[t], sem).wait()
        _tile_copy(xn_ref, t, xb_ref, d2_ref[t], sem).wait()
        return 0

    lax.fori_loop(0, MOVE_ROWS, start, 0)
    lax.fori_loop(0, MOVE_ROWS, wait, 0)


def _dispatch(xn, dest1, dest2, pad_start, n_slots):
    T = dest1.shape[0]
    idx = pl.BlockSpec((MOVE_ROWS,), lambda i, pad: (i,), memory_space=pltpu.SMEM)
    grid_spec = pltpu.PrefetchScalarGridSpec(
        num_scalar_prefetch=1, grid=(T // MOVE_ROWS,),
        in_specs=[idx, idx, pl.BlockSpec((MOVE_ROWS * SUBLANES, LANES), lambda i, pad: (i, 0))],
        out_specs=pl.BlockSpec(memory_space=pl.ANY),
        scratch_shapes=[pltpu.VMEM((EXPERT_ROWS * SUBLANES, LANES), F32), pltpu.SemaphoreType.DMA(())])
    return pl.pallas_call(
        _dispatch_body, grid_spec=grid_spec,
        out_shape=jax.ShapeDtypeStruct(((n_slots + EXPERT_ROWS) * SUBLANES, LANES), F32),
        compiler_params=_params("arbitrary"), name="dispatch")(pad_start, dest1, dest2, xn)


def _expert_body(be_ref, nused_ref, xb_ref, wg_ref, wu_ref, wd_ref, yb_ref,
                 x_ref, wgb_ref, wub_ref, wdb_ref):
    i = pl.program_id(0)
    used = i < nused_ref[0]
    new_expert = (i == 0) | (be_ref[i] != be_ref[jnp.maximum(i - 1, 0)])

    @pl.when(used & new_expert)
    def _():
        wgb_ref[...] = wg_ref[0, 0].astype(BF16)
        wub_ref[...] = wu_ref[0, 0].astype(BF16)
        wdb_ref[...] = wd_ref[0, 0].astype(BF16)

    @pl.when(used)
    def _():
        for c in range(SUBLANES):
            x_ref[:, c * LANES:(c + 1) * LANES] = xb_ref[pl.ds(c, EXPERT_ROWS, stride=SUBLANES), :].astype(BF16)
        x = x_ref[...]
        gate = jnp.dot(x, wgb_ref[...], preferred_element_type=F32)
        up = jnp.dot(x, wub_ref[...], preferred_element_type=F32)
        hmid = (_silu(gate) * up).astype(BF16)
        y = jnp.dot(hmid, wdb_ref[...], preferred_element_type=F32)
        for c in range(SUBLANES):
            yb_ref[pl.ds(c, EXPERT_ROWS, stride=SUBLANES), :] = y[:, c * LANES:(c + 1) * LANES]

    @pl.when(jnp.logical_not(used))
    def _():
        yb_ref[...] = jnp.zeros_like(yb_ref)


def _experts(xb, blk_exp, n_used, n_blk, layer, w_gate, w_up, w_down):
    M = EXPERT_ROWS
    tile = lambda index: pl.BlockSpec((M * SUBLANES, LANES), index)
    weight = lambda a: pl.BlockSpec((1, 1) + a.shape[2:], lambda i, be, nu: (layer, be[i], 0, 0))
    grid_spec = pltpu.PrefetchScalarGridSpec(
        num_scalar_prefetch=2, grid=(n_blk,),
        in_specs=[tile(lambda i, be, nu: (jnp.minimum(i, nu[0] - 1), 0)),
                  weight(w_gate), weight(w_up), weight(w_down)],
        out_specs=tile(lambda i, be, nu: (i, 0)),
        scratch_shapes=[pltpu.VMEM((M, D_MODEL), BF16), pltpu.VMEM((D_MODEL, MOE_HIDDEN), BF16),
                        pltpu.VMEM((D_MODEL, MOE_HIDDEN), BF16), pltpu.VMEM((MOE_HIDDEN, D_MODEL), BF16)])
    return pl.pallas_call(
        _expert_body, grid_spec=grid_spec,
        out_shape=jax.ShapeDtypeStruct((n_blk * M * SUBLANES, LANES), F32),
        compiler_params=_params("arbitrary"), name="experts")(
            blk_exp, n_used, xb, w_gate, w_up, w_down)


def _combine_body(d1_ref, d2_ref, h_ref, route_ref, yb_ref, o_ref, buf_ref, sem):
    def start(t, _):
        _tile_copy(yb_ref, d1_ref[t], buf_ref.at[0], t, sem).start()
        _tile_copy(yb_ref, d2_ref[t], buf_ref.at[1], t, sem).start()
        return 0

    def wait(t, _):
        _tile_copy(yb_ref, d1_ref[t], buf_ref.at[0], t, sem).wait()
        _tile_copy(yb_ref, d2_ref[t], buf_ref.at[1], t, sem).wait()
        return 0

    lax.fori_loop(0, MOVE_ROWS, start, 0)
    lax.fori_loop(0, MOVE_ROWS, wait, 0)
    route = route_ref[...]
    g1 = route[:, 2:3]
    g2 = route[:, 3:4]
    for c in range(SUBLANES):
        cs = slice(c * LANES, (c + 1) * LANES)
        rows = pl.ds(c, MOVE_ROWS, stride=SUBLANES)
        o_ref[:, cs] = h_ref[:, cs] + (buf_ref[0, rows, :] * g1 + buf_ref[1, rows, :] * g2)


def _combine(h2, route, yb, dest1, dest2):
    T = h2.shape[0]
    R = MOVE_ROWS
    idx = pl.BlockSpec((R,), lambda i: (i,), memory_space=pltpu.SMEM)
    return pl.pallas_call(
        _combine_body, grid=(T // R,),
        in_specs=[idx, idx, pl.BlockSpec((R, D_MODEL), lambda i: (i, 0)),
                  pl.BlockSpec((R, LANES), lambda i: (i, 0)), pl.BlockSpec(memory_space=pl.ANY)],
        out_specs=pl.BlockSpec((R, D_MODEL), lambda i: (i, 0)),
        out_shape=jax.ShapeDtypeStruct((T, D_MODEL), F32),
        scratch_shapes=[pltpu.VMEM((2, R * SUBLANES, LANES), F32), pltpu.SemaphoreType.DMA(())],
        compiler_params=_params("arbitrary"), name="combine")(dest1, dest2, h2, route, yb)


def _moe(h2, xn, route, routet, counts, layer, w_gate, w_up, w_down):
    T = h2.shape[0]
    M = EXPERT_ROWS
    n_blk = (2 * T) // M + MOE_EXPERTS
    cnt = counts[0, ROUTER_LANE0:ROUTER_LANE0 + MOE_EXPERTS].astype(jnp.int32)
    padded = (cnt + M - 1) // M * M
    ends = jnp.cumsum(padded)
    pstarts = ends - padded
    n_used = (ends[-1] // M).astype(jnp.int32)
    blk = jnp.minimum(jnp.arange(n_blk, dtype=jnp.int32), n_used - 1) * M
    blk_exp = jnp.sum((ends[None, :] <= blk[:, None]).astype(jnp.int32), axis=1)
    blk_exp = jnp.minimum(blk_exp, MOE_EXPERTS - 1)
    dest = _slots(routet, pstarts)
    pad_info = jnp.concatenate([pstarts + cnt, n_used.reshape(1)])
    xb = _dispatch(xn, dest[0], dest[1], pad_info, n_blk * M)
    yb = _experts(xb, blk_exp, n_used.reshape(1), n_blk, layer, w_gate, w_up, w_down)
    return _combine(h2, route, yb, dest[0], dest[1])


def kernel(x, rel_bias, norm1_g, w_in, conv_w, conv_b, dt_bias, a_log, d_skip, ssd_norm_g,
           q_norm_g, k_norm_g, w_out, norm2_g, w_r1, b_r1, w_r2, b_r2, w_gate, w_up, w_down):
    B, S, D = x.shape
    T = B * S
    h2 = x.reshape(T, D)
    depth = w_in.shape[0]
    bias_tiles = _bias_tiles(rel_bias)
    for l in range(depth):
        z, xbc, dt, dtt, q, k, v = _inproj(h2, norm1_g[l], w_in[l])
        r3 = lambda a: a.reshape(B, S, a.shape[-1])
        y_ssd = _ssd(r3(z), r3(xbc), r3(dt), dtt, conv_w[l], conv_b[l], dt_bias[l], a_log[l],
                     d_skip[l], ssd_norm_g[l])
        y_att = _attention(r3(q), r3(k), r3(v), q_norm_g[l], k_norm_g[l], bias_tiles)
        h2, xn, route, routet, counts = _outproj_route(
            h2, y_ssd.reshape(T, SSD_INNER), y_att.reshape(T, ATT_INNER), w_out[l], norm2_g[l],
            w_r1[l], b_r1[l], w_r2[l], b_r2[l])
        h2 = _moe(h2, xn, route, routet, counts, l, w_gate, w_up, w_down)
    return h2.reshape(B, S, D)
```

```python
import functools
import math

import numpy as np
import jax
import jax.numpy as jnp
from jax import lax
from jax.experimental import pallas as pl
from jax.experimental.pallas import tpu as pltpu

F32 = jnp.float32
BF16 = jnp.bfloat16
HIGHEST = lax.Precision.HIGHEST

LANES = 128
SUBLANES = 8

D_MODEL = 1024
SSD_HEADS = 8
SSD_HEAD_DIM = 64
SSD_INNER = SSD_HEADS * SSD_HEAD_DIM
SSD_GROUPS = 2
SSD_HEADS_PER_GROUP = SSD_HEADS // SSD_GROUPS
SSD_STATE = 64
SSD_CONV = 4
SSD_CHUNK = 128
CONV_CH = SSD_INNER + 2 * SSD_GROUPS * SSD_STATE
ATT_HEADS = 8
ATT_HEAD_DIM = 64
ATT_INNER = ATT_HEADS * ATT_HEAD_DIM
MOBA_BLOCK = 256
MOBA_TOPK = 3
REL_BUCKETS = 32
REL_MAX_DIST = 128
MOE_GROUPS = 4
MOE_EXPERTS_PER_GROUP = 8
MOE_EXPERTS = MOE_GROUPS * MOE_EXPERTS_PER_GROUP
MOE_HIDDEN = 512
EPS = 1e-6

MASK_VALUE = -1e30
LOG2E = math.log2(math.e)
IN_ROWS = 512
OUT_ROWS = 512
EXPERT_ROWS = 512
MOVE_ROWS = 256
VMEM_LIMIT = 56 * 1024 * 1024
ROUTER_LANE0 = MOE_GROUPS


def _silu(x):
    return x * (1.0 / (1.0 + jnp.exp(-x)))


def _softplus(x):
    return jnp.maximum(x, 0.0) + jnp.log(1.0 + jnp.exp(-jnp.abs(x)))


def _params(*sem):
    return pltpu.CompilerParams(dimension_semantics=sem, vmem_limit_bytes=VMEM_LIMIT)


def _inproj_body(h_ref, g_ref, wz_ref, wx_ref, wdt_ref, wdtt_ref, wq_ref, wk_ref, wv_ref,
                 z_ref, xbc_ref, dt_ref, dtt_ref, q_ref, k_ref, v_ref):
    x = h_ref[...]
    xn = x * lax.rsqrt(jnp.mean(x * x, axis=-1, keepdims=True) + EPS) * g_ref[...]
    xb = xn.astype(BF16)
    z_ref[...] = jnp.dot(xb, wz_ref[...], preferred_element_type=F32)
    xbc_ref[...] = jnp.dot(xb, wx_ref[...], preferred_element_type=F32)
    dt_ref[...] = jnp.dot(xb, wdt_ref[...], preferred_element_type=F32)
    dtt_ref[...] = lax.dot_general(wdtt_ref[...], xb, (((1,), (1,)), ((), ())),
                                   preferred_element_type=F32)
    q_ref[...] = jnp.dot(xb, wq_ref[...], preferred_element_type=F32)
    k_ref[...] = jnp.dot(xb, wk_ref[...], preferred_element_type=F32)
    v_ref[...] = jnp.dot(xb, wv_ref[...], preferred_element_type=F32)


def _inproj(h2, g, w_in):
    T = h2.shape[0]
    c0, c1, c2 = SSD_INNER, SSD_INNER + CONV_CH, SSD_INNER + CONV_CH + SSD_HEADS
    wb = w_in.astype(BF16)
    wz, wx, wdt = wb[:, :c0], wb[:, c0:c1], wb[:, c1:c2]
    wq, wk, wv = (wb[:, c2 + i * ATT_INNER:c2 + (i + 1) * ATT_INNER] for i in range(3))
    wdt_pad = jnp.pad(wdt, ((0, 0), (0, LANES - SSD_HEADS)))
    wdtt = wdt.T
    full = lambda a: pl.BlockSpec(a.shape, lambda i: (0,) * a.ndim)
    rows = lambda n: pl.BlockSpec((IN_ROWS, n), lambda i: (i, 0))
    ins = (h2, g.reshape(1, D_MODEL), wz, wx, wdt_pad, wdtt, wq, wk, wv)
    out_shape = (
        jax.ShapeDtypeStruct((T, SSD_INNER), F32), jax.ShapeDtypeStruct((T, CONV_CH), F32),
        jax.ShapeDtypeStruct((T, LANES), F32), jax.ShapeDtypeStruct((SSD_HEADS, T), F32),
        jax.ShapeDtypeStruct((T, ATT_INNER), F32), jax.ShapeDtypeStruct((T, ATT_INNER), F32),
        jax.ShapeDtypeStruct((T, ATT_INNER), F32))
    out_specs = (rows(SSD_INNER), rows(CONV_CH), rows(LANES),
                 pl.BlockSpec((SSD_HEADS, IN_ROWS), lambda i: (0, i)),
                 rows(ATT_INNER), rows(ATT_INNER), rows(ATT_INNER))
    return pl.pallas_call(
        _inproj_body, grid=(T // IN_ROWS,),
        in_specs=[rows(D_MODEL)] + [full(a) for a in ins[1:]],
        out_specs=out_specs, out_shape=out_shape,
        compiler_params=_params("arbitrary"), name="inproj")(*ins)


def _ssd_body(z_ref, xbc_ref, dt_ref, dtt_ref, cw_ref, cb_ref, dtb_ref, dtbt_ref, alog_ref,
              alogt_ref, dskip_ref, ng_ref, tril_ref, expand_ref, y_ref, ext_ref, state_ref):
    L, P, N, E = SSD_CHUNK, SSD_HEAD_DIM, SSD_STATE, SSD_HEADS_PER_GROUP
    c = pl.program_id(1)

    @pl.when(c == 0)
    def _():
        state_ref[...] = jnp.zeros_like(state_ref)
        ext_ref[0:SUBLANES, :] = jnp.zeros((SUBLANES, CONV_CH), F32)

    ext_ref[SUBLANES:SUBLANES + L, :] = xbc_ref[0]
    conv = jnp.zeros((L, CONV_CH), F32) + cb_ref[...]
    for j in range(SSD_CONV):
        off = SUBLANES - (SSD_CONV - 1) + j
        conv = conv + cw_ref[j:j + 1, :] * ext_ref[off:off + L, :]
    ext_ref[0:SUBLANES, :] = ext_ref[L:L + SUBLANES, :]
    act = _silu(conv)
    xs = act[:, :SSD_INNER]
    b_in = act[:, SSD_INNER:SSD_INNER + SSD_GROUPS * N]
    c_out = act[:, SSD_INNER + SSD_GROUPS * N:]

    tril = tril_ref[...]
    expand = expand_ref[...]
    dt = _softplus(dt_ref[0] + dtb_ref[...])
    a_dt = dt * (-jnp.exp(alog_ref[...]))
    a_cs = jnp.dot(tril, a_dt, precision=HIGHEST, preferred_element_type=F32)
    dt_t = _softplus(dtt_ref[...] + dtbt_ref[...])
    a_dt_t = dt_t * (-jnp.exp(alogt_ref[...]))
    a_cs_t = lax.dot_general(a_dt_t, tril, (((1,), (1,)), ((), ())), precision=HIGHEST,
                             preferred_element_type=F32)
    last = a_cs[L - 1:L, :]
    def hdot(a):
        hi = a.astype(BF16)
        lo = (a - hi.astype(F32)).astype(BF16)
        return (jnp.dot(hi, expand, preferred_element_type=F32)
                + jnp.dot(lo, expand, preferred_element_type=F32))
    dt_x = hdot(dt)
    eacs_x = hdot(jnp.exp(a_cs))
    dec_x = hdot(jnp.exp(last - a_cs))
    cdec_x = hdot(jnp.broadcast_to(jnp.exp(last), (SUBLANES, LANES)))[0:1, :]
    xdt = xs * dt_x
    xdtd = (xdt * dec_x).astype(BF16)
    xdt_b = xdt.astype(BF16)
    row = lax.broadcasted_iota(jnp.int32, (L, L), 0)
    col = lax.broadcasted_iota(jnp.int32, (L, L), 1)
    causal = row >= col

    y_parts = []
    for g in range(SSD_GROUPS):
        bg = b_in[:, g * N:(g + 1) * N].astype(BF16)
        cg = c_out[:, g * N:(g + 1) * N].astype(BF16)
        cb = lax.dot_general(cg, bg, (((1,), (1,)), ((), ())), preferred_element_type=F32)
        st = state_ref[g]
        gs = slice(g * E * P, (g + 1) * E * P)
        y_off = jnp.dot(cg, st.astype(BF16), preferred_element_type=F32) * eacs_x[:, gs]
        for e in range(E):
            h = g * E + e
            diff = a_cs[:, h:h + 1] - a_cs_t[h:h + 1, :]
            l_mat = jnp.exp(jnp.where(causal, diff, -jnp.inf))
            m = (cb * l_mat).astype(BF16)
            y_diag = jnp.dot(m, xdt_b[:, h * P:(h + 1) * P], preferred_element_type=F32)
            y_parts.append(y_diag + y_off[:, e * P:(e + 1) * P])
        upd = lax.dot_general(bg, xdtd[:, gs], (((0,), (0,)), ((), ())),
                              preferred_element_type=F32)
        state_ref[g] = cdec_x[:, gs] * st + upd
    y = jnp.concatenate(y_parts, axis=-1) + dskip_ref[...] * xs
    y = y * _silu(z_ref[0])
    y = y * lax.rsqrt(jnp.mean(y * y, axis=-1, keepdims=True) + EPS) * ng_ref[...]
    y_ref[0] = y


def _ssd(z, xbc, dt, dtt, conv_w, conv_b, dt_bias, a_log, d_skip, norm_g):
    B, S, _ = z.shape
    L = SSD_CHUNK
    nc = S // L
    pad_row = lambda a: jnp.pad(a.reshape(1, SSD_HEADS), ((0, 0), (0, LANES - SSD_HEADS)))
    tril = jnp.asarray(np.tril(np.ones((L, L), np.float32)))
    expand_np = np.zeros((LANES, SSD_INNER), np.float32)
    for h in range(SSD_HEADS):
        expand_np[h, h * SSD_HEAD_DIM:(h + 1) * SSD_HEAD_DIM] = 1.0
    ins = (z, xbc, dt, dtt, conv_w, conv_b.reshape(1, CONV_CH), pad_row(dt_bias),
           dt_bias.reshape(SSD_HEADS, 1), pad_row(a_log), a_log.reshape(SSD_HEADS, 1),
           jnp.repeat(d_skip, SSD_HEAD_DIM).reshape(1, SSD_INNER), norm_g.reshape(1, SSD_INNER),
           tril, jnp.asarray(expand_np, dtype=BF16))
    full = lambda a: pl.BlockSpec(a.shape, lambda b, c: (0,) * a.ndim)
    chunk = lambda n: pl.BlockSpec((1, L, n), lambda b, c: (b, c, 0))
    in_specs = [chunk(SSD_INNER), chunk(CONV_CH), chunk(LANES),
                pl.BlockSpec((SSD_HEADS, L), lambda b, c: (0, b * nc + c))]
    in_specs += [full(a) for a in ins[4:]]
    return pl.pallas_call(
        _ssd_body, grid=(B, nc), in_specs=in_specs, out_specs=chunk(SSD_INNER),
        out_shape=jax.ShapeDtypeStruct((B, S, SSD_INNER), F32),
        scratch_shapes=[pltpu.VMEM((L + SUBLANES, CONV_CH), F32),
                        pltpu.VMEM((SSD_GROUPS, SSD_STATE, SSD_HEADS_PER_GROUP * SSD_HEAD_DIM), F32)],
        compiler_params=_params("arbitrary", "arbitrary"), name="ssd")(*ins)


def _bucket_tile():
    bs = MOBA_BLOCK
    dist = np.arange(bs)[:, None] - np.arange(2 * bs)[None, :] + bs
    max_exact = REL_BUCKETS // 2
    d = np.maximum(dist, max_exact).astype(np.float32)
    large = max_exact + (np.log(d / np.float32(max_exact)) / np.float32(math.log(REL_MAX_DIST / max_exact))
                         * np.float32(REL_BUCKETS - max_exact)).astype(np.int32)
    large = np.minimum(large, REL_BUCKETS - 1)
    bucket = np.where(dist < max_exact, dist, large)
    return np.where(dist >= 0, bucket, -1).astype(np.int32)


def _bias_body(rb_ref, bucket_ref, o_ref):
    head = pl.program_id(0)
    bucket = bucket_ref[...]
    far = rb_ref[REL_BUCKETS - 1, head]
    bias = jnp.where(bucket < 0, MASK_VALUE, 0.0)
    for i in range(REL_BUCKETS - 1):
        bias = jnp.where(bucket == i, (rb_ref[i, head] - far) * LOG2E, bias)
    o_ref[0] = bias


def _bias_tiles(rel_bias):
    bs = MOBA_BLOCK
    bucket = jnp.asarray(_bucket_tile())
    return pl.pallas_call(
        _bias_body, grid=(ATT_HEADS,),
        in_specs=[pl.BlockSpec(memory_space=pltpu.SMEM), pl.BlockSpec((bs, 2 * bs), lambda h: (0, 0))],
        out_specs=pl.BlockSpec((1, bs, 2 * bs), lambda h: (h, 0, 0)),
        out_shape=jax.ShapeDtypeStruct((ATT_HEADS, bs, 2 * bs), F32),
        compiler_params=_params("arbitrary"), name="bias_tiles")(rel_bias, bucket)


def _attn_body(q_ref, k_ref, v_ref, qg_ref, kg_ref, bias_ref, o_ref,
               kaug_ref, vaug_ref, kmean_ref, qaug_ref, sa_ref, sb_ref, m_ref, acc_ref):
    bs, hd = MOBA_BLOCK, ATT_HEAD_DIM
    nb = k_ref.shape[1] // bs
    qb = pl.program_id(2)
    nt_dims = (((1,), (1,)), ((), ()))

    @pl.when(qb == 0)
    def _():
        blk = lax.broadcasted_iota(jnp.int32, (nb * bs, hd), 0) // bs
        lane = lax.broadcasted_iota(jnp.int32, (nb * bs, hd), 1)
        onehot = (blk == lane).astype(BF16)
        ones = jnp.ones((nb * bs, hd), BF16)
        for h in range(2):
            kh = k_ref[0, :, h * hd:(h + 1) * hd]
            kn = kh * lax.rsqrt(jnp.mean(kh * kh, axis=-1, keepdims=True) + EPS) * kg_ref[...]
            kmean_ref[h] = jnp.mean(kn.reshape(nb, bs, hd), axis=1)
            kaug_ref[h] = jnp.concatenate([kn.astype(BF16), onehot], axis=-1)
            vaug_ref[h] = jnp.concatenate([v_ref[0, :, h * hd:(h + 1) * hd].astype(BF16), ones], axis=-1)
            kmean_b = kmean_ref[h].astype(BF16)
            rowb = lax.broadcasted_iota(jnp.int32, (nb, bs), 0)
            for j in range(nb):
                rows = slice(j * bs, (j + 1) * bs)
                qh = q_ref[0, rows, h * hd:(h + 1) * hd]
                qn = qh * lax.rsqrt(jnp.mean(qh * qh, axis=-1, keepdims=True) + EPS) * qg_ref[...]
                if j <= MOBA_TOPK:
                    sel = rowb <= j
                else:
                    gate = lax.dot_general(kmean_b, qn.astype(BF16), nt_dims,
                                           preferred_element_type=F32)
                    rank = jnp.zeros((nb, bs), jnp.int32)
                    for m in range(j):
                        gm = gate[m:m + 1, :]
                        beats = (gm > gate) | ((gm == gate) & (rowb > m))
                        rank = rank + beats.astype(jnp.int32)
                    sel = ((rowb < j) & (rank < MOBA_TOPK)) | (rowb == j)
                selmask = jnp.where(sel, 0.0, MASK_VALUE)
                mask_rows = jnp.concatenate(
                    [jnp.zeros((hd, bs), F32), selmask, jnp.zeros((hd - nb, bs), F32)], axis=0)
                qs = jnp.concatenate(
                    [qn * (ATT_HEAD_DIM ** -0.5 * LOG2E), jnp.zeros((bs, hd), F32)], axis=-1)
                qaug_ref[h, rows, :] = (qs + mask_rows.T).astype(BF16)

    q0 = pl.multiple_of(qb * bs, bs)
    for h in range(2):
        m_ref[h] = jnp.full((bs, 2 * hd), -jnp.inf, F32)
        acc_ref[h] = jnp.zeros((bs, 2 * hd), F32)

    def scores(n, s_ref):
        start = pl.multiple_of(n * bs, bs)
        for h in range(2):
            s_ref[h] = lax.dot_general(qaug_ref[h, pl.ds(q0, bs), :], kaug_ref[h, pl.ds(start, bs), :], nt_dims,
                                       preferred_element_type=F32)

    def update(n, s_ref, bias):
        start = pl.multiple_of(n * bs, bs)
        for h in range(2):
            s = s_ref[h]
            if bias is not None:
                s = s + bias(h)
            m_i = m_ref[h]
            m_new = jnp.maximum(m_i, jnp.max(s, axis=-1, keepdims=True))
            alpha = jnp.exp2(m_i - m_new)
            p = jnp.exp2(s - jnp.concatenate([m_new, m_new], axis=-1)).astype(BF16)
            pv = jnp.dot(p, vaug_ref[h, pl.ds(start, bs), :], preferred_element_type=F32)
            acc_ref[h] = alpha * acc_ref[h] + pv
            m_ref[h] = m_new

    def far_pair(i, _):
        n = 2 * i
        scores(n + 1, sb_ref)
        update(n, sa_ref, None)
        scores(n + 2, sa_ref)
        update(n + 1, sb_ref, None)
        return 0

    n_far = jnp.maximum(qb - 1, 0)
    scores(0, sa_ref)
    lax.fori_loop(0, n_far >> 1, far_pair, 0)
    no_prev = jnp.where(qb == 0, MASK_VALUE, 0.0)
    prev_bias = lambda h: bias_ref[h, :, 0:bs] + no_prev
    own_bias = lambda h: bias_ref[h, :, bs:2 * bs]

    @pl.when((n_far & 1) == 1)
    def _():
        scores(n_far, sb_ref)
        update(n_far - 1, sa_ref, None)
        scores(qb, sa_ref)
        update(n_far, sb_ref, prev_bias)
        update(qb, sa_ref, own_bias)

    @pl.when((n_far & 1) == 0)
    def _():
        scores(qb, sb_ref)
        update(n_far, sa_ref, prev_bias)
        update(qb, sb_ref, own_bias)

    outs = []
    for h in range(2):
        acc = acc_ref[h]
        outs.append(acc[:, 0:hd] / acc[:, hd:hd + 1])
    o_ref[0] = jnp.concatenate(outs, axis=-1)


def _attention(q, k, v, q_g, k_g, bias_tiles):
    B, S, _ = q.shape
    bs = MOBA_BLOCK
    nb = S // bs
    assert S % bs == 0 and nb <= ATT_HEAD_DIM
    pair = 2 * ATT_HEAD_DIM
    return pl.pallas_call(
        _attn_body, grid=(B, ATT_HEADS // 2, nb),
        in_specs=[
            pl.BlockSpec((1, S, pair), lambda b, hp, i: (b, 0, hp)),
            pl.BlockSpec((1, S, pair), lambda b, hp, i: (b, 0, hp)),
            pl.BlockSpec((1, S, pair), lambda b, hp, i: (b, 0, hp)),
            pl.BlockSpec((1, ATT_HEAD_DIM), lambda b, hp, i: (0, 0)),
            pl.BlockSpec((1, ATT_HEAD_DIM), lambda b, hp, i: (0, 0)),
            pl.BlockSpec((2, bs, 2 * bs), lambda b, hp, i: (hp, 0, 0)),
        ],
        out_specs=pl.BlockSpec((1, bs, pair), lambda b, hp, i: (b, i, hp)),
        out_shape=jax.ShapeDtypeStruct((B, S, ATT_INNER), F32),
        scratch_shapes=[pltpu.VMEM((2, S, pair), BF16), pltpu.VMEM((2, S, pair), BF16),
                        pltpu.VMEM((2, nb, ATT_HEAD_DIM), F32), pltpu.VMEM((2, S, pair), BF16),
                        pltpu.VMEM((2, bs, bs), F32), pltpu.VMEM((2, bs, bs), F32),
                        pltpu.VMEM((2, bs, pair), F32),
                        pltpu.VMEM((2, bs, pair), F32)],
        compiler_params=_params("arbitrary", "arbitrary", "arbitrary"), name="moba")(
            q, k, v, q_g.reshape(1, -1), k_g.reshape(1, -1), bias_tiles)


def _outproj_body(h_ref, ys_ref, ya_ref, wa_ref, wb_ref, g_ref, wrh_ref, wrl_ref, br_ref, tri_ref,
                  hn_ref, xn_ref, route_ref, routet_ref, counts_ref, carry_ref):
    i = pl.program_id(0)

    @pl.when(i == 0)
    def _():
        carry_ref[...] = jnp.zeros_like(carry_ref)

    hn = (h_ref[...]
          + jnp.dot(ys_ref[...].astype(BF16), wa_ref[...], preferred_element_type=F32)
          + jnp.dot(ya_ref[...].astype(BF16), wb_ref[...], preferred_element_type=F32))
    hn_ref[...] = hn
    xn = hn * lax.rsqrt(jnp.mean(hn * hn, axis=-1, keepdims=True) + EPS) * g_ref[...]
    for c in range(SUBLANES):
        xn_ref[pl.ds(c, xn.shape[0], stride=SUBLANES), :] = xn[:, c * LANES:(c + 1) * LANES]
    x_hi = xn.astype(BF16)
    x_lo = (xn - x_hi.astype(F32)).astype(BF16)
    logits = (jnp.dot(x_hi, wrh_ref[...], preferred_element_type=F32)
              + jnp.dot(x_lo, wrh_ref[...], preferred_element_type=F32)
              + jnp.dot(x_hi, wrl_ref[...], preferred_element_type=F32)) + br_ref[...]
    rows = logits.shape[0]
    lane = lax.broadcasted_iota(jnp.int32, (rows, LANES), 1)
    first = lambda hit: jnp.min(jnp.where(hit, lane, LANES), axis=-1, keepdims=True)
    l1 = jnp.where(lane < MOE_GROUPS, logits, -jnp.inf)
    mx = jnp.max(l1, axis=-1, keepdims=True)
    gval = 1.0 / jnp.sum(jnp.exp(l1 - mx), axis=-1, keepdims=True)
    gidx = first(l1 == mx)
    e_lane = lane - ROUTER_LANE0
    in_grp = (e_lane >= 0) & (e_lane < MOE_EXPERTS) & ((e_lane >> 3) == gidx)
    l2 = jnp.where(in_grp, logits, -jnp.inf)
    m1 = jnp.max(l2, axis=-1, keepdims=True)
    i1 = first(l2 == m1)
    l2b = jnp.where(lane == i1, -jnp.inf, l2)
    m2 = jnp.max(l2b, axis=-1, keepdims=True)
    i2 = first(l2b == m2)
    r = jnp.exp(m2 - m1)
    g1 = gval / (1.0 + r)
    g2 = gval * r / (1.0 + r)
    oh1 = (lane == i1)
    oh2 = (lane == i2)
    both = jnp.where(oh1 | oh2, 1.0, 0.0)
    cx = jnp.dot(tri_ref[...], both.astype(BF16), preferred_element_type=F32) + carry_ref[...]
    rank1 = jnp.sum(jnp.where(oh1, cx, 0.0), axis=-1, keepdims=True)
    rank2 = jnp.sum(jnp.where(oh2, cx, 0.0), axis=-1, keepdims=True)
    carry_ref[...] = carry_ref[...] + jnp.sum(both, axis=0, keepdims=True)
    counts_ref[...] = carry_ref[...]
    cols = ((i1 - ROUTER_LANE0).astype(F32), (i2 - ROUTER_LANE0).astype(F32), g1, g2, rank1, rank2)
    route = jnp.zeros((rows, LANES), F32)
    for j, cval in enumerate(cols):
        route = jnp.where(lane == j, cval, route)
    route_ref[...] = route
    routet_ref[...] = route.T[0:SUBLANES, :]


def _outproj_route(h2, y_ssd, y_att, w_out, norm_g, w_r1, b_r1, w_r2, b_r2):
    T = h2.shape[0]
    R = OUT_ROWS
    wb = w_out.astype(BF16)
    pad = LANES - MOE_GROUPS - MOE_EXPERTS
    wr = jnp.pad(jnp.concatenate([w_r1, w_r2], axis=1), ((0, 0), (0, pad)))
    wr_hi = wr.astype(BF16)
    wr_lo = (wr - wr_hi.astype(F32)).astype(BF16)
    br = jnp.pad(jnp.concatenate([b_r1, b_r2]), (0, pad)).reshape(1, LANES)
    tri = jnp.asarray(np.tril(np.ones((R, R), np.float32), -1), dtype=BF16)
    ins = (h2, y_ssd, y_att, wb[:SSD_INNER], wb[SSD_INNER:], norm_g.reshape(1, D_MODEL),
           wr_hi, wr_lo, br, tri)
    full = lambda a: pl.BlockSpec(a.shape, lambda i: (0,) * a.ndim)
    rows = lambda n: pl.BlockSpec((R, n), lambda i: (i, 0))
    return pl.pallas_call(
        _outproj_body, grid=(T // R,),
        in_specs=[rows(D_MODEL), rows(SSD_INNER), rows(ATT_INNER)] + [full(a) for a in ins[3:]],
        out_specs=(rows(D_MODEL), pl.BlockSpec((R * SUBLANES, LANES), lambda i: (i, 0)), rows(LANES),
                   pl.BlockSpec((SUBLANES, R), lambda i: (0, i)), pl.BlockSpec((1, LANES), lambda i: (0, 0))),
        out_shape=(jax.ShapeDtypeStruct((T, D_MODEL), F32),
                   jax.ShapeDtypeStruct((T * SUBLANES, LANES), F32),
                   jax.ShapeDtypeStruct((T, LANES), F32), jax.ShapeDtypeStruct((SUBLANES, T), F32),
                   jax.ShapeDtypeStruct((1, LANES), F32)),
        scratch_shapes=[pltpu.VMEM((1, LANES), F32)],
        compiler_params=_params("arbitrary"), name="outproj_route")(*ins)


def _slots_body(routet_ref, pstart_ref, dest_ref):
    cols = routet_ref.shape[1]
    expert = lax.broadcasted_iota(jnp.int32, (MOE_EXPERTS, cols), 0).astype(F32)
    pstart = pstart_ref[...]
    for j in range(2):
        hit = expert == routet_ref[j:j + 1, :]
        base = jnp.sum(jnp.where(hit, pstart, 0.0), axis=0, keepdims=True)
        dest_ref[j:j + 1, :] = (base + routet_ref[4 + j:5 + j, :]).astype(jnp.int32)


def _slots(routet, pstarts):
    T = routet.shape[1]
    cols = min(T, 4096)
    assert T % cols == 0
    return pl.pallas_call(
        _slots_body, grid=(T // cols,),
        in_specs=[pl.BlockSpec((SUBLANES, cols), lambda i: (0, i)),
                  pl.BlockSpec((MOE_EXPERTS, 1), lambda i: (0, 0))],
        out_specs=pl.BlockSpec((2, cols), lambda i: (0, i)),
        out_shape=jax.ShapeDtypeStruct((2, T), jnp.int32),
        compiler_params=_params("arbitrary"), name="slots")(
            routet, pstarts.astype(F32).reshape(MOE_EXPERTS, 1))


def _tile_copy(src_ref, src_row, dst_ref, dst_row, sem):
    tile = lambda row: pl.ds(pl.multiple_of(row * SUBLANES, SUBLANES), SUBLANES)
    return pltpu.make_async_copy(src_ref.at[tile(src_row)], dst_ref.at[tile(dst_row)], sem)


def _dispatch_body(pad_ref, d1_ref, d2_ref, xn_ref, xb_ref, zero_ref, stage_ref, sem, lsem, ssem):
    block = EXPERT_ROWS * SUBLANES
    n_blocks = xb_ref.shape[0] // block
    n_used = pad_ref[MOE_EXPERTS]

    def zero_copy(row):
        start = pl.multiple_of(row * SUBLANES, SUBLANES)
        return pltpu.make_async_copy(zero_ref, xb_ref.at[pl.ds(start, block)], sem)

    @pl.when(pl.program_id(0) == 0)
    def _():
        zero_ref[...] = jnp.zeros_like(zero_ref)
        for e in range(MOE_EXPERTS):
            zero_copy(pad_ref[e]).start()
        for e in range(MOE_EXPERTS):
            zero_copy(pad_ref[e]).wait()
        for b in range(n_blocks - MOE_EXPERTS - 1, n_blocks):
            @pl.when(b >= n_used)
            def _():
                zero_copy(b * EXPERT_ROWS).start()
        for b in range(n_blocks - MOE_EXPERTS - 1, n_blocks):
            @pl.when(b >= n_used)
            def _():
                zero_copy(b * EXPERT_ROWS).wait()

    i = pl.program_id(0)
    n = pl.num_programs(0)
    rows = MOVE_ROWS * SUBLANES

    def load(step, slot):
        src = xn_ref.at[pl.ds(pl.multiple_of(step * rows, rows), rows)]
        return pltpu.make_async_copy(src, stage_ref.at[slot], lsem.at[slot])

    def wait_scatters(slot):
        for _ in range(2):
            pltpu.make_async_copy(stage_ref.at[slot], stage_ref.at[slot], ssem.at[slot]).wait()

    @pl.when(i == 0)
    def _():
        load(0, 0).start()

    @pl.when(i >= 1)
    def _():
        wait_scatters((i - 1) & 1)

    @pl.when(i + 1 < n)
    def _():
        load(i + 1, (i + 1) & 1).start()

    slot = i & 1
    load(i, slot).wait()
    base = i * MOVE_ROWS

    def scatter(t, _):
        _tile_copy(stage_ref.at[slot], t, xb_ref, d1_ref[base + t], ssem.at[slot]).start()
        _tile_copy(stage_ref.at[slot], t, xb_ref, d2_ref[base + t], ssem.at[slot]).start()
        return 0

    lax.fori_loop(0, MOVE_ROWS, scatter, 0)

    @pl.when(i == n - 1)
    def _():
        wait_scatters(slot)


def _dispatch(xn, dest1, dest2, pad_start, n_slots):
    T = dest1.shape[0]
    grid_spec = pltpu.PrefetchScalarGridSpec(
        num_scalar_prefetch=3, grid=(T // MOVE_ROWS,),
        in_specs=[pl.BlockSpec(memory_space=pl.ANY)],
        out_specs=pl.BlockSpec(memory_space=pl.ANY),
        scratch_shapes=[pltpu.VMEM((EXPERT_ROWS * SUBLANES, LANES), F32),
                        pltpu.VMEM((2, MOVE_ROWS * SUBLANES, LANES), F32),
                        pltpu.SemaphoreType.DMA(()), pltpu.SemaphoreType.DMA((2,)),
                        pltpu.SemaphoreType.DMA((2,))])
    return pl.pallas_call(
        _dispatch_body, grid_spec=grid_spec,
        out_shape=jax.ShapeDtypeStruct(((n_slots + EXPERT_ROWS) * SUBLANES, LANES), F32),
        compiler_params=_params("arbitrary"), name="dispatch")(pad_start, dest1, dest2, xn)


def _expert_body(be_ref, nused_ref, xb_ref, wg_ref, wu_ref, wd_ref, yb_ref,
                 x_ref, wgb_ref, wub_ref, wdb_ref):
    i = pl.program_id(0)
    used = i < nused_ref[0]
    new_expert = (i == 0) | (be_ref[i] != be_ref[jnp.maximum(i - 1, 0)])

    @pl.when(used & new_expert)
    def _():
        wgb_ref[...] = wg_ref[0, 0].astype(BF16)
        wub_ref[...] = wu_ref[0, 0].astype(BF16)
        wdb_ref[...] = wd_ref[0, 0].astype(BF16)

    @pl.when(used)
    def _():
        for c in range(SUBLANES):
            x_ref[:, c * LANES:(c + 1) * LANES] = xb_ref[pl.ds(c, EXPERT_ROWS, stride=SUBLANES), :].astype(BF16)
        x = x_ref[...]
        gate = jnp.dot(x, wgb_ref[...], preferred_element_type=F32)
        up = jnp.dot(x, wub_ref[...], preferred_element_type=F32)
        hmid = (_silu(gate) * up).astype(BF16)
        y = jnp.dot(hmid, wdb_ref[...], preferred_element_type=F32)
        for c in range(SUBLANES):
            yb_ref[pl.ds(c, EXPERT_ROWS, stride=SUBLANES), :] = y[:, c * LANES:(c + 1) * LANES]

    @pl.when(jnp.logical_not(used))
    def _():
        yb_ref[...] = jnp.zeros_like(yb_ref)


def _experts(xb, blk_exp, n_used, n_blk, layer, w_gate, w_up, w_down):
    M = EXPERT_ROWS
    tile = lambda index: pl.BlockSpec((M * SUBLANES, LANES), index)
    weight = lambda a: pl.BlockSpec((1, 1) + a.shape[2:], lambda i, be, nu: (layer, be[i], 0, 0))
    grid_spec = pltpu.PrefetchScalarGridSpec(
        num_scalar_prefetch=2, grid=(n_blk,),
        in_specs=[tile(lambda i, be, nu: (jnp.minimum(i, nu[0] - 1), 0)),
                  weight(w_gate), weight(w_up), weight(w_down)],
        out_specs=tile(lambda i, be, nu: (i, 0)),
        scratch_shapes=[pltpu.VMEM((M, D_MODEL), BF16), pltpu.VMEM((D_MODEL, MOE_HIDDEN), BF16),
                        pltpu.VMEM((D_MODEL, MOE_HIDDEN), BF16), pltpu.VMEM((MOE_HIDDEN, D_MODEL), BF16)])
    return pl.pallas_call(
        _expert_body, grid_spec=grid_spec,
        out_shape=jax.ShapeDtypeStruct((n_blk * M * SUBLANES, LANES), F32),
        compiler_params=_params("arbitrary"), name="experts")(
            blk_exp, n_used, xb, w_gate, w_up, w_down)


def _combine_body(d1_ref, d2_ref, h_ref, route_ref, yb_ref, o_ref, buf_ref, sem):
    i = pl.program_id(0)
    n = pl.num_programs(0)

    def gather(step, slot):
        base = step * MOVE_ROWS

        def body(t, _):
            _tile_copy(yb_ref, d1_ref[base + t], buf_ref.at[slot, 0], t, sem.at[slot]).start()
            _tile_copy(yb_ref, d2_ref[base + t], buf_ref.at[slot, 1], t, sem.at[slot]).start()
            return 0

        lax.fori_loop(0, MOVE_ROWS, body, 0)

    @pl.when(i == 0)
    def _():
        gather(0, 0)

    @pl.when(i + 1 < n)
    def _():
        gather(i + 1, (i + 1) & 1)

    slot = i & 1
    pltpu.make_async_copy(buf_ref.at[slot], buf_ref.at[slot], sem.at[slot]).wait()
    route = route_ref[...]
    g1 = route[:, 2:3]
    g2 = route[:, 3:4]
    for c in range(SUBLANES):
        cs = slice(c * LANES, (c + 1) * LANES)
        rows = pl.ds(c, MOVE_ROWS, stride=SUBLANES)
        o_ref[:, cs] = h_ref[:, cs] + (buf_ref[slot, 0, rows, :] * g1 + buf_ref[slot, 1, rows, :] * g2)


def _combine(h2, route, yb, dest1, dest2):
    T = h2.shape[0]
    R = MOVE_ROWS
    grid_spec = pltpu.PrefetchScalarGridSpec(
        num_scalar_prefetch=2, grid=(T // R,),
        in_specs=[pl.BlockSpec((R, D_MODEL), lambda i, d1, d2: (i, 0)),
                  pl.BlockSpec((R, LANES), lambda i, d1, d2: (i, 0)), pl.BlockSpec(memory_space=pl.ANY)],
        out_specs=pl.BlockSpec((R, D_MODEL), lambda i, d1, d2: (i, 0)),
        scratch_shapes=[pltpu.VMEM((2, 2, R * SUBLANES, LANES), F32), pltpu.SemaphoreType.DMA((2,))])
    return pl.pallas_call(
        _combine_body, grid_spec=grid_spec, out_shape=jax.ShapeDtypeStruct((T, D_MODEL), F32),
        compiler_params=_params("arbitrary"), name="combine")(dest1, dest2, h2, route, yb)


def _moe(h2, xn, route, routet, counts, layer, w_gate, w_up, w_down):
    T = h2.shape[0]
    M = EXPERT_ROWS
    n_blk = (2 * T) // M + MOE_EXPERTS
    cnt = counts[0, ROUTER_LANE0:ROUTER_LANE0 + MOE_EXPERTS].astype(jnp.int32)
    padded = (cnt + M - 1) // M * M
    ends = jnp.cumsum(padded)
    pstarts = ends - padded
    n_used = (ends[-1] // M).astype(jnp.int32)
    blk = jnp.minimum(jnp.arange(n_blk, dtype=jnp.int32), n_used - 1) * M
    blk_exp = jnp.sum((ends[None, :] <= blk[:, None]).astype(jnp.int32), axis=1)
    blk_exp = jnp.minimum(blk_exp, MOE_EXPERTS - 1)
    dest = _slots(routet, pstarts)
    pad_info = jnp.concatenate([pstarts + cnt, n_used.reshape(1)])
    xb = _dispatch(xn, dest[0], dest[1], pad_info, n_blk * M)
    yb = _experts(xb, blk_exp, n_used.reshape(1), n_blk, layer, w_gate, w_up, w_down)
    return _combine(h2, route, yb, dest[0], dest[1])


def kernel(x, rel_bias, norm1_g, w_in, conv_w, conv_b, dt_bias, a_log, d_skip, ssd_norm_g,
           q_norm_g, k_norm_g, w_out, norm2_g, w_r1, b_r1, w_r2, b_r2, w_gate, w_up, w_down):
    B, S, D = x.shape
    T = B * S
    h2 = x.reshape(T, D)
    depth = w_in.shape[0]
    bias_tiles = _bias_tiles(rel_bias)
    for l in range(depth):
        z, xbc, dt, dtt, q, k, v = _inproj(h2, norm1_g[l], w_in[l])
        r3 = lambda a: a.reshape(B, S, a.shape[-1])
        y_ssd = _ssd(r3(z), r3(xbc), r3(dt), dtt, conv_w[l], conv_b[l], dt_bias[l], a_log[l],
                     d_skip[l], ssd_norm_g[l])
        y_att = _attention(r3(q), r3(k), r3(v), q_norm_g[l], k_norm_g[l], bias_tiles)
        h2, xn, route, routet, counts = _outproj_route(
            h2, y_ssd.reshape(T, SSD_INNER), y_att.reshape(T, ATT_INNER), w_out[l], norm2_g[l],
            w_r1[l], b_r1[l], w_r2[l], b_r2[l])
        h2 = _moe(h2, xn, route, routet, counts, l, w_gate, w_up, w_down)
    return h2.reshape(B, S, D)
```

```python
import functools
import math

import numpy as np
import jax
import jax.numpy as jnp
from jax import lax
from jax.experimental import pallas as pl
from jax.experimental.pallas import tpu as pltpu

F32 = jnp.float32
BF16 = jnp.bfloat16
HIGHEST = lax.Precision.HIGHEST

LANES = 128
SUBLANES = 8

D_MODEL = 1024
SSD_HEADS = 8
SSD_HEAD_DIM = 64
SSD_INNER = SSD_HEADS * SSD_HEAD_DIM
SSD_GROUPS = 2
SSD_HEADS_PER_GROUP = SSD_HEADS // SSD_GROUPS
SSD_STATE = 64
SSD_CONV = 4
SSD_CHUNK = 128
CONV_CH = SSD_INNER + 2 * SSD_GROUPS * SSD_STATE
ATT_HEADS = 8
ATT_HEAD_DIM = 64
ATT_INNER = ATT_HEADS * ATT_HEAD_DIM
MOBA_BLOCK = 256
MOBA_TOPK = 3
REL_BUCKETS = 32
REL_MAX_DIST = 128
MOE_GROUPS = 4
MOE_EXPERTS_PER_GROUP = 8
MOE_EXPERTS = MOE_GROUPS * MOE_EXPERTS_PER_GROUP
MOE_HIDDEN = 512
EPS = 1e-6

MASK_VALUE = -1e30
LOG2E = math.log2(math.e)
IN_ROWS = 512
SSD_STEP_CHUNKS = 2
OUT_ROWS = 512
EXPERT_ROWS = 512
MOVE_ROWS = 256
VMEM_LIMIT = 56 * 1024 * 1024
ROUTER_LANE0 = MOE_GROUPS


def _silu(x):
    return x * (1.0 / (1.0 + jnp.exp(-x)))


def _softplus(x):
    return jnp.maximum(x, 0.0) + jnp.log(1.0 + jnp.exp(-jnp.abs(x)))


def _params(*sem):
    return pltpu.CompilerParams(dimension_semantics=sem, vmem_limit_bytes=VMEM_LIMIT)


def _inproj_body(h_ref, g_ref, wz_ref, wx_ref, wdt_ref, wq_ref, wk_ref, wv_ref,
                 z_ref, xbc_ref, dt_ref, dtt_ref, q_ref, k_ref, v_ref):
    x = h_ref[...]
    xn = x * lax.rsqrt(jnp.mean(x * x, axis=-1, keepdims=True) + EPS) * g_ref[...]
    xb = xn.astype(BF16)
    z_ref[...] = jnp.dot(xb, wz_ref[...], preferred_element_type=F32)
    xbc_ref[...] = jnp.dot(xb, wx_ref[...], preferred_element_type=F32)
    dt = jnp.dot(xb, wdt_ref[...], preferred_element_type=F32)
    dt_ref[...] = dt
    dtt_ref[...] = dt.T[0:SSD_HEADS, :]
    q_ref[...] = jnp.dot(xb, wq_ref[...], preferred_element_type=F32)
    k_ref[...] = jnp.dot(xb, wk_ref[...], preferred_element_type=F32)
    v_ref[...] = jnp.dot(xb, wv_ref[...], preferred_element_type=F32)


def _inproj(h2, g, w_in):
    T = h2.shape[0]
    c0, c1, c2 = SSD_INNER, SSD_INNER + CONV_CH, SSD_INNER + CONV_CH + SSD_HEADS
    wb = w_in.astype(BF16)
    wz, wx, wdt = wb[:, :c0], wb[:, c0:c1], wb[:, c1:c2]
    wq, wk, wv = (wb[:, c2 + i * ATT_INNER:c2 + (i + 1) * ATT_INNER] for i in range(3))
    wdt_pad = jnp.pad(wdt, ((0, 0), (0, LANES - SSD_HEADS)))
    full = lambda a: pl.BlockSpec(a.shape, lambda i: (0,) * a.ndim)
    rows = lambda n: pl.BlockSpec((IN_ROWS, n), lambda i: (i, 0))
    ins = (h2, g.reshape(1, D_MODEL), wz, wx, wdt_pad, wq, wk, wv)
    out_shape = (
        jax.ShapeDtypeStruct((T, SSD_INNER), F32), jax.ShapeDtypeStruct((T, CONV_CH), F32),
        jax.ShapeDtypeStruct((T, LANES), F32), jax.ShapeDtypeStruct((SSD_HEADS, T), F32),
        jax.ShapeDtypeStruct((T, ATT_INNER), F32), jax.ShapeDtypeStruct((T, ATT_INNER), F32),
        jax.ShapeDtypeStruct((T, ATT_INNER), F32))
    out_specs = (rows(SSD_INNER), rows(CONV_CH), rows(LANES),
                 pl.BlockSpec((SSD_HEADS, IN_ROWS), lambda i: (0, i)),
                 rows(ATT_INNER), rows(ATT_INNER), rows(ATT_INNER))
    return pl.pallas_call(
        _inproj_body, grid=(T // IN_ROWS,),
        in_specs=[rows(D_MODEL)] + [full(a) for a in ins[1:]],
        out_specs=out_specs, out_shape=out_shape,
        compiler_params=_params("arbitrary"), name="inproj")(*ins)


def _ssd_body(z_ref, xbc_ref, dt_ref, dtt_ref, cw_ref, cb_ref, dtb_ref, dtbt_ref, alog_ref,
              alogt_ref, dskip_ref, ng_ref, tril_ref, expand_ref, y_ref, ext_ref, state_ref):
    L, P, N, E = SSD_CHUNK, SSD_HEAD_DIM, SSD_STATE, SSD_HEADS_PER_GROUP
    R = SSD_STEP_CHUNKS * L
    c = pl.program_id(1)

    @pl.when(c == 0)
    def _():
        state_ref[...] = jnp.zeros_like(state_ref)
        ext_ref[0:SUBLANES, :] = jnp.zeros((SUBLANES, CONV_CH), F32)

    ext_ref[SUBLANES:SUBLANES + R, :] = xbc_ref[0]
    tril = tril_ref[...]
    expand = expand_ref[...]
    row = lax.broadcasted_iota(jnp.int32, (L, L), 0)
    col = lax.broadcasted_iota(jnp.int32, (L, L), 1)
    causal = row >= col
    for sub in range(SSD_STEP_CHUNKS):
        _ssd_chunk(sub * L, z_ref, dt_ref, dtt_ref, cw_ref, cb_ref, dtb_ref, dtbt_ref, alog_ref,
                   alogt_ref, dskip_ref, ng_ref, tril, expand, causal, y_ref, ext_ref, state_ref)
    ext_ref[0:SUBLANES, :] = ext_ref[R:R + SUBLANES, :]


def _ssd_chunk(r0, z_ref, dt_ref, dtt_ref, cw_ref, cb_ref, dtb_ref, dtbt_ref, alog_ref, alogt_ref,
               dskip_ref, ng_ref, tril, expand, causal, y_ref, ext_ref, state_ref):
    L, P, N, E = SSD_CHUNK, SSD_HEAD_DIM, SSD_STATE, SSD_HEADS_PER_GROUP
    conv = jnp.zeros((L, CONV_CH), F32) + cb_ref[...]
    for j in range(SSD_CONV):
        off = r0 + SUBLANES - (SSD_CONV - 1) + j
        conv = conv + cw_ref[j:j + 1, :] * ext_ref[off:off + L, :]
    act = _silu(conv)
    xs = act[:, :SSD_INNER]
    b_in = act[:, SSD_INNER:SSD_INNER + SSD_GROUPS * N]
    c_out = act[:, SSD_INNER + SSD_GROUPS * N:]

    dt = _softplus(dt_ref[0, r0:r0 + L, :] + dtb_ref[...])
    a_dt = dt * (-jnp.exp(alog_ref[...]))
    a_cs = jnp.dot(tril, a_dt, precision=HIGHEST, preferred_element_type=F32)
    dt_t = _softplus(dtt_ref[:, r0:r0 + L] + dtbt_ref[...])
    a_dt_t = dt_t * (-jnp.exp(alogt_ref[...]))
    a_cs_t = lax.dot_general(a_dt_t, tril, (((1,), (1,)), ((), ())), precision=HIGHEST,
                             preferred_element_type=F32)
    last = a_cs[L - 1:L, :]
    def hdot(a):
        hi = a.astype(BF16)
        lo = (a - hi.astype(F32)).astype(BF16)
        return (jnp.dot(hi, expand, preferred_element_type=F32)
                + jnp.dot(lo, expand, preferred_element_type=F32))
    dt_x = hdot(dt)
    eacs_x = hdot(jnp.exp(a_cs))
    dec_x = hdot(jnp.exp(last - a_cs))
    cdec_x = hdot(jnp.broadcast_to(jnp.exp(last), (SUBLANES, LANES)))[0:1, :]
    xdt = xs * dt_x
    xdtd = (xdt * dec_x).astype(BF16)
    xdt_b = xdt.astype(BF16)

    y_parts = []
    for g in range(SSD_GROUPS):
        bg = b_in[:, g * N:(g + 1) * N].astype(BF16)
        cg = c_out[:, g * N:(g + 1) * N].astype(BF16)
        cb = lax.dot_general(cg, bg, (((1,), (1,)), ((), ())), preferred_element_type=F32)
        st = state_ref[g]
        gs = slice(g * E * P, (g + 1) * E * P)
        y_off = jnp.dot(cg, st.astype(BF16), preferred_element_type=F32) * eacs_x[:, gs]
        for e in range(E):
            h = g * E + e
            diff = a_cs[:, h:h + 1] - a_cs_t[h:h + 1, :]
            l_mat = jnp.exp(jnp.where(causal, diff, -jnp.inf))
            m = (cb * l_mat).astype(BF16)
            y_diag = jnp.dot(m, xdt_b[:, h * P:(h + 1) * P], preferred_element_type=F32)
            y_parts.append(y_diag + y_off[:, e * P:(e + 1) * P])
        upd = lax.dot_general(bg, xdtd[:, gs], (((0,), (0,)), ((), ())),
                              preferred_element_type=F32)
        state_ref[g] = cdec_x[:, gs] * st + upd
    y = jnp.concatenate(y_parts, axis=-1) + dskip_ref[...] * xs
    y = y * _silu(z_ref[0, r0:r0 + L, :])
    y = y * lax.rsqrt(jnp.mean(y * y, axis=-1, keepdims=True) + EPS) * ng_ref[...]
    y_ref[0, r0:r0 + L, :] = y


def _ssd(z, xbc, dt, dtt, conv_w, conv_b, dt_bias, a_log, d_skip, norm_g):
    B, S, _ = z.shape
    L = SSD_CHUNK
    R = SSD_STEP_CHUNKS * L
    nc = S // R
    pad_row = lambda a: jnp.pad(a.reshape(1, SSD_HEADS), ((0, 0), (0, LANES - SSD_HEADS)))
    tril = jnp.asarray(np.tril(np.ones((L, L), np.float32)))
    expand_np = np.zeros((LANES, SSD_INNER), np.float32)
    for h in range(SSD_HEADS):
        expand_np[h, h * SSD_HEAD_DIM:(h + 1) * SSD_HEAD_DIM] = 1.0
    ins = (z, xbc, dt, dtt, conv_w, conv_b.reshape(1, CONV_CH), pad_row(dt_bias),
           dt_bias.reshape(SSD_HEADS, 1), pad_row(a_log), a_log.reshape(SSD_HEADS, 1),
           jnp.repeat(d_skip, SSD_HEAD_DIM).reshape(1, SSD_INNER), norm_g.reshape(1, SSD_INNER),
           tril, jnp.asarray(expand_np, dtype=BF16))
    full = lambda a: pl.BlockSpec(a.shape, lambda b, c: (0,) * a.ndim)
    chunk = lambda n: pl.BlockSpec((1, R, n), lambda b, c: (b, c, 0))
    in_specs = [chunk(SSD_INNER), chunk(CONV_CH), chunk(LANES),
                pl.BlockSpec((SSD_HEADS, R), lambda b, c: (0, b * nc + c))]
    in_specs += [full(a) for a in ins[4:]]
    return pl.pallas_call(
        _ssd_body, grid=(B, nc), in_specs=in_specs, out_specs=chunk(SSD_INNER),
        out_shape=jax.ShapeDtypeStruct((B, S, SSD_INNER), F32),
        scratch_shapes=[pltpu.VMEM((R + SUBLANES, CONV_CH), F32),
                        pltpu.VMEM((SSD_GROUPS, SSD_STATE, SSD_HEADS_PER_GROUP * SSD_HEAD_DIM), F32)],
        compiler_params=_params("arbitrary", "arbitrary"), name="ssd")(*ins)


def _bucket_tile():
    bs = MOBA_BLOCK
    dist = np.arange(bs)[:, None] - np.arange(2 * bs)[None, :] + bs
    max_exact = REL_BUCKETS // 2
    d = np.maximum(dist, max_exact).astype(np.float32)
    large = max_exact + (np.log(d / np.float32(max_exact)) / np.float32(math.log(REL_MAX_DIST / max_exact))
                         * np.float32(REL_BUCKETS - max_exact)).astype(np.int32)
    large = np.minimum(large, REL_BUCKETS - 1)
    bucket = np.where(dist < max_exact, dist, large)
    return np.where(dist >= 0, bucket, -1).astype(np.int32)


def _bias_body(rb_ref, bucket_ref, o_ref):
    head = pl.program_id(0)
    bucket = bucket_ref[...]
    far = rb_ref[REL_BUCKETS - 1, head]
    bias = jnp.where(bucket < 0, MASK_VALUE, 0.0)
    for i in range(REL_BUCKETS - 1):
        bias = jnp.where(bucket == i, (rb_ref[i, head] - far) * LOG2E, bias)
    o_ref[0] = bias


def _bias_tiles(rel_bias):
    bs = MOBA_BLOCK
    bucket = jnp.asarray(_bucket_tile())
    return pl.pallas_call(
        _bias_body, grid=(ATT_HEADS,),
        in_specs=[pl.BlockSpec(memory_space=pltpu.SMEM), pl.BlockSpec((bs, 2 * bs), lambda h: (0, 0))],
        out_specs=pl.BlockSpec((1, bs, 2 * bs), lambda h: (h, 0, 0)),
        out_shape=jax.ShapeDtypeStruct((ATT_HEADS, bs, 2 * bs), F32),
        compiler_params=_params("arbitrary"), name="bias_tiles")(rel_bias, bucket)


def _attn_body(q_ref, k_ref, v_ref, qg_ref, kg_ref, bias_ref, o_ref,
               kaug_ref, vaug_ref, qaug_ref, sa_ref, sb_ref, m_ref, acc_ref):
    bs, hd = MOBA_BLOCK, ATT_HEAD_DIM
    nb = k_ref.shape[1] // bs
    qb = pl.program_id(2)
    nt_dims = (((1,), (1,)), ((), ()))

    @pl.when(qb == 0)
    def _():
        S = nb * bs
        pair = 2 * hd
        r = lax.broadcasted_iota(jnp.int32, (pair, pair), 0) // hd
        c = lax.broadcasted_iota(jnp.int32, (pair, pair), 1) // hd
        same_head = (r == c).astype(BF16)

        def head_norm(x, gain):
            ms = jnp.dot((x * x).astype(BF16), same_head, preferred_element_type=F32) * (1.0 / hd)
            return x * lax.rsqrt(ms + EPS) * gain

        kn = head_norm(k_ref[0], kg_ref[...])
        kmean = jnp.mean(kn.reshape(nb, bs, pair), axis=1)
        kn_b = kn.astype(BF16)
        v_b = v_ref[0].astype(BF16)
        lane = lax.broadcasted_iota(jnp.int32, (S, pair), 1)
        blk = lax.broadcasted_iota(jnp.int32, (S, pair), 0) // bs
        lane_m = lax.broadcasted_iota(jnp.int32, (nb, pair), 1)
        rowb = lax.broadcasted_iota(jnp.int32, (nb, bs), 0)
        lane_q = lax.broadcasted_iota(jnp.int32, (bs, pair), 1)
        kmean_b = []
        for h in range(2):
            own = (lane // hd) == h
            other0 = (1 - h) * hd
            kaug_ref[h] = jnp.where(own, kn_b, (lane - other0 == blk).astype(BF16))
            vaug_ref[h] = jnp.where(own, v_b, jnp.ones((S, pair), BF16))
            kmean_b.append(jnp.where((lane_m // hd) == h, kmean, 0.0).astype(BF16))
        for j in range(nb):
            rows = slice(j * bs, (j + 1) * bs)
            qn = head_norm(q_ref[0, rows, :], qg_ref[...])
            qn_b = qn.astype(BF16)
            qs = qn * (ATT_HEAD_DIM ** -0.5 * LOG2E)
            for h in range(2):
                if j <= MOBA_TOPK:
                    sel = rowb <= j
                else:
                    gate = lax.dot_general(kmean_b[h], qn_b, nt_dims,
                                           preferred_element_type=F32)
                    rank = jnp.zeros((nb, bs), jnp.int32)
                    for m in range(j):
                        gm = gate[m:m + 1, :]
                        beats = (gm > gate) | ((gm == gate) & (rowb > m))
                        rank = rank + beats.astype(jnp.int32)
                    sel = ((rowb < j) & (rank < MOBA_TOPK)) | (rowb == j)
                selmask = jnp.where(sel, 0.0, MASK_VALUE)
                pieces = [selmask, jnp.zeros((pair - nb, bs), F32)]
                if h == 0:
                    pieces = [jnp.zeros((hd, bs), F32), selmask, jnp.zeros((hd - nb, bs), F32)]
                mask_t = jnp.concatenate(pieces, axis=0).T
                qaug_ref[h, rows, :] = jnp.where((lane_q // hd) == h, qs, mask_t).astype(BF16)

    q0 = pl.multiple_of(qb * bs, bs)
    for h in range(2):
        m_ref[h] = jnp.full((bs, 2 * hd), -jnp.inf, F32)
        acc_ref[h] = jnp.zeros((bs, 2 * hd), F32)

    def scores(n, s_ref):
        start = pl.multiple_of(n * bs, bs)
        for h in range(2):
            s_ref[h] = lax.dot_general(qaug_ref[h, pl.ds(q0, bs), :], kaug_ref[h, pl.ds(start, bs), :], nt_dims,
                                       preferred_element_type=F32)

    def update(n, s_ref, bias):
        start = pl.multiple_of(n * bs, bs)
        for h in range(2):
            s = s_ref[h]
            if bias is not None:
                s = s + bias(h)
            m_i = m_ref[h]
            m_new = jnp.maximum(m_i, jnp.max(s, axis=-1, keepdims=True))
            alpha = jnp.exp2(m_i - m_new)
            p = jnp.exp2(s - jnp.concatenate([m_new, m_new], axis=-1)).astype(BF16)
            pv = jnp.dot(p, vaug_ref[h, pl.ds(start, bs), :], preferred_element_type=F32)
            acc_ref[h] = alpha * acc_ref[h] + pv
            m_ref[h] = m_new

    def far_pair(i, _):
        n = 2 * i
        scores(n + 1, sb_ref)
        update(n, sa_ref, None)
        scores(n + 2, sa_ref)
        update(n + 1, sb_ref, None)
        return 0

    n_far = jnp.maximum(qb - 1, 0)
    scores(0, sa_ref)
    lax.fori_loop(0, n_far >> 1, far_pair, 0)
    no_prev = jnp.where(qb == 0, MASK_VALUE, 0.0)
    prev_bias = lambda h: bias_ref[h, :, 0:bs] + no_prev
    own_bias = lambda h: bias_ref[h, :, bs:2 * bs]

    @pl.when((n_far & 1) == 1)
    def _():
        scores(n_far, sb_ref)
        update(n_far - 1, sa_ref, None)
        scores(qb, sa_ref)
        update(n_far, sb_ref, prev_bias)
        update(qb, sa_ref, own_bias)

    @pl.when((n_far & 1) == 0)
    def _():
        scores(qb, sb_ref)
        update(n_far, sa_ref, prev_bias)
        update(qb, sb_ref, own_bias)

    ratio = [acc_ref[h] / pltpu.roll(acc_ref[h], hd, 1) for h in range(2)]
    lane_o = lax.broadcasted_iota(jnp.int32, (bs, 2 * hd), 1)
    o_ref[0] = jnp.where(lane_o < hd, ratio[0], ratio[1])


def _attention(q, k, v, q_g, k_g, bias_tiles):
    B, S, _ = q.shape
    bs = MOBA_BLOCK
    nb = S // bs
    assert S % bs == 0 and nb <= ATT_HEAD_DIM
    pair = 2 * ATT_HEAD_DIM
    return pl.pallas_call(
        _attn_body, grid=(B, ATT_HEADS // 2, nb),
        in_specs=[
            pl.BlockSpec((1, S, pair), lambda b, hp, i: (b, 0, hp)),
            pl.BlockSpec((1, S, pair), lambda b, hp, i: (b, 0, hp)),
            pl.BlockSpec((1, S, pair), lambda b, hp, i: (b, 0, hp)),
            pl.BlockSpec((1, pair), lambda b, hp, i: (0, 0)),
            pl.BlockSpec((1, pair), lambda b, hp, i: (0, 0)),
            pl.BlockSpec((2, bs, 2 * bs), lambda b, hp, i: (hp, 0, 0)),
        ],
        out_specs=pl.BlockSpec((1, bs, pair), lambda b, hp, i: (b, i, hp)),
        out_shape=jax.ShapeDtypeStruct((B, S, ATT_INNER), F32),
        scratch_shapes=[pltpu.VMEM((2, S, pair), BF16), pltpu.VMEM((2, S, pair), BF16),
                        pltpu.VMEM((2, S, pair), BF16),
                        pltpu.VMEM((2, bs, bs), F32), pltpu.VMEM((2, bs, bs), F32),
                        pltpu.VMEM((2, bs, pair), F32),
                        pltpu.VMEM((2, bs, pair), F32)],
        compiler_params=_params("arbitrary", "arbitrary", "arbitrary"), name="moba")(
            q, k, v, jnp.tile(q_g, 2).reshape(1, pair), jnp.tile(k_g, 2).reshape(1, pair), bias_tiles)


def _outproj_body(h_ref, ys_ref, ya_ref, wa_ref, wb_ref, g_ref, wrh_ref, wrl_ref, br_ref, tri_ref,
                  hn_ref, xn_ref, route_ref, routet_ref, counts_ref, carry_ref):
    i = pl.program_id(0)

    @pl.when(i == 0)
    def _():
        carry_ref[...] = jnp.zeros_like(carry_ref)

    hn = (h_ref[...]
          + jnp.dot(ys_ref[...].astype(BF16), wa_ref[...], preferred_element_type=F32)
          + jnp.dot(ya_ref[...].astype(BF16), wb_ref[...], preferred_element_type=F32))
    hn_ref[...] = hn
    xn = hn * lax.rsqrt(jnp.mean(hn * hn, axis=-1, keepdims=True) + EPS) * g_ref[...]
    for c in range(SUBLANES):
        xn_ref[pl.ds(c, xn.shape[0], stride=SUBLANES), :] = xn[:, c * LANES:(c + 1) * LANES]
    x_hi = xn.astype(BF16)
    x_lo = (xn - x_hi.astype(F32)).astype(BF16)
    logits = (jnp.dot(x_hi, wrh_ref[...], preferred_element_type=F32)
              + jnp.dot(x_lo, wrh_ref[...], preferred_element_type=F32)
              + jnp.dot(x_hi, wrl_ref[...], preferred_element_type=F32)) + br_ref[...]
    rows = logits.shape[0]
    lane = lax.broadcasted_iota(jnp.int32, (rows, LANES), 1)
    first = lambda hit: jnp.min(jnp.where(hit, lane, LANES), axis=-1, keepdims=True)
    l1 = jnp.where(lane < MOE_GROUPS, logits, -jnp.inf)
    mx = jnp.max(l1, axis=-1, keepdims=True)
    gval = 1.0 / jnp.sum(jnp.exp(l1 - mx), axis=-1, keepdims=True)
    gidx = first(l1 == mx)
    e_lane = lane - ROUTER_LANE0
    in_grp = (e_lane >= 0) & (e_lane < MOE_EXPERTS) & ((e_lane >> 3) == gidx)
    l2 = jnp.where(in_grp, logits, -jnp.inf)
    m1 = jnp.max(l2, axis=-1, keepdims=True)
    i1 = first(l2 == m1)
    l2b = jnp.where(lane == i1, -jnp.inf, l2)
    m2 = jnp.max(l2b, axis=-1, keepdims=True)
    i2 = first(l2b == m2)
    r = jnp.exp(m2 - m1)
    g1 = gval / (1.0 + r)
    g2 = gval * r / (1.0 + r)
    oh1 = (lane == i1)
    oh2 = (lane == i2)
    both = jnp.where(oh1 | oh2, 1.0, 0.0)
    cx = jnp.dot(tri_ref[...], both.astype(BF16), preferred_element_type=F32) + carry_ref[...]
    rank1 = jnp.sum(jnp.where(oh1, cx, 0.0), axis=-1, keepdims=True)
    rank2 = jnp.sum(jnp.where(oh2, cx, 0.0), axis=-1, keepdims=True)
    carry_ref[...] = carry_ref[...] + jnp.sum(both, axis=0, keepdims=True)
    counts_ref[...] = carry_ref[...]
    cols = ((i1 - ROUTER_LANE0).astype(F32), (i2 - ROUTER_LANE0).astype(F32), g1, g2, rank1, rank2)
    route = jnp.zeros((rows, LANES), F32)
    for j, cval in enumerate(cols):
        route = jnp.where(lane == j, cval, route)
    route_ref[...] = route
    routet_ref[...] = route.T[0:SUBLANES, :]


def _outproj_route(h2, y_ssd, y_att, w_out, norm_g, w_r1, b_r1, w_r2, b_r2):
    T = h2.shape[0]
    R = OUT_ROWS
    wb = w_out.astype(BF16)
    pad = LANES - MOE_GROUPS - MOE_EXPERTS
    wr = jnp.pad(jnp.concatenate([w_r1, w_r2], axis=1), ((0, 0), (0, pad)))
    wr_hi = wr.astype(BF16)
    wr_lo = (wr - wr_hi.astype(F32)).astype(BF16)
    br = jnp.pad(jnp.concatenate([b_r1, b_r2]), (0, pad)).reshape(1, LANES)
    tri = jnp.asarray(np.tril(np.ones((R, R), np.float32), -1), dtype=BF16)
    ins = (h2, y_ssd, y_att, wb[:SSD_INNER], wb[SSD_INNER:], norm_g.reshape(1, D_MODEL),
           wr_hi, wr_lo, br, tri)
    full = lambda a: pl.BlockSpec(a.shape, lambda i: (0,) * a.ndim)
    rows = lambda n: pl.BlockSpec((R, n), lambda i: (i, 0))
    return pl.pallas_call(
        _outproj_body, grid=(T // R,),
        in_specs=[rows(D_MODEL), rows(SSD_INNER), rows(ATT_INNER)] + [full(a) for a in ins[3:]],
        out_specs=(rows(D_MODEL), pl.BlockSpec((R * SUBLANES, LANES), lambda i: (i, 0)), rows(LANES),
                   pl.BlockSpec((SUBLANES, R), lambda i: (0, i)), pl.BlockSpec((1, LANES), lambda i: (0, 0))),
        out_shape=(jax.ShapeDtypeStruct((T, D_MODEL), F32),
                   jax.ShapeDtypeStruct((T * SUBLANES, LANES), F32),
                   jax.ShapeDtypeStruct((T, LANES), F32), jax.ShapeDtypeStruct((SUBLANES, T), F32),
                   jax.ShapeDtypeStruct((1, LANES), F32)),
        scratch_shapes=[pltpu.VMEM((1, LANES), F32)],
        compiler_params=_params("arbitrary"), name="outproj_route")(*ins)


def _slots_body(routet_ref, pstart_ref, dest_ref):
    cols = routet_ref.shape[1]
    expert = lax.broadcasted_iota(jnp.int32, (MOE_EXPERTS, cols), 0).astype(F32)
    pstart = pstart_ref[...]
    for j in range(2):
        hit = expert == routet_ref[j:j + 1, :]
        base = jnp.sum(jnp.where(hit, pstart, 0.0), axis=0, keepdims=True)
        dest_ref[j:j + 1, :] = (base + routet_ref[4 + j:5 + j, :]).astype(jnp.int32)


def _slots(routet, pstarts):
    T = routet.shape[1]
    cols = min(T, 4096)
    assert T % cols == 0
    return pl.pallas_call(
        _slots_body, grid=(T // cols,),
        in_specs=[pl.BlockSpec((SUBLANES, cols), lambda i: (0, i)),
                  pl.BlockSpec((MOE_EXPERTS, 1), lambda i: (0, 0))],
        out_specs=pl.BlockSpec((2, cols), lambda i: (0, i)),
        out_shape=jax.ShapeDtypeStruct((2, T), jnp.int32),
        compiler_params=_params("arbitrary"), name="slots")(
            routet, pstarts.astype(F32).reshape(MOE_EXPERTS, 1))


def _tile_copy(src_ref, src_row, dst_ref, dst_row, sem):
    tile = lambda row: pl.ds(pl.multiple_of(row * SUBLANES, SUBLANES), SUBLANES)
    return pltpu.make_async_copy(src_ref.at[tile(src_row)], dst_ref.at[tile(dst_row)], sem)


def _dispatch_body(pad_ref, d1_ref, d2_ref, xn_ref, xb_ref, zero_ref, stage_ref, sem, lsem, ssem):
    block = EXPERT_ROWS * SUBLANES
    n_blocks = xb_ref.shape[0] // block
    n_used = pad_ref[MOE_EXPERTS]

    def zero_copy(row):
        start = pl.multiple_of(row * SUBLANES, SUBLANES)
        return pltpu.make_async_copy(zero_ref, xb_ref.at[pl.ds(start, block)], sem)

    @pl.when(pl.program_id(0) == 0)
    def _():
        zero_ref[...] = jnp.zeros_like(zero_ref)
        for e in range(MOE_EXPERTS):
            zero_copy(pad_ref[e]).start()
        for e in range(MOE_EXPERTS):
            zero_copy(pad_ref[e]).wait()
        for b in range(n_blocks - MOE_EXPERTS - 1, n_blocks):
            @pl.when(b >= n_used)
            def _():
                zero_copy(b * EXPERT_ROWS).start()
        for b in range(n_blocks - MOE_EXPERTS - 1, n_blocks):
            @pl.when(b >= n_used)
            def _():
                zero_copy(b * EXPERT_ROWS).wait()

    i = pl.program_id(0)
    n = pl.num_programs(0)
    rows = MOVE_ROWS * SUBLANES

    def load(step, slot):
        src = xn_ref.at[pl.ds(pl.multiple_of(step * rows, rows), rows)]
        return pltpu.make_async_copy(src, stage_ref.at[slot], lsem.at[slot])

    def wait_scatters(slot):
        for _ in range(2):
            pltpu.make_async_copy(stage_ref.at[slot], stage_ref.at[slot], ssem.at[slot]).wait()

    @pl.when(i == 0)
    def _():
        load(0, 0).start()

    @pl.when(i >= 1)
    def _():
        wait_scatters((i - 1) & 1)

    @pl.when(i + 1 < n)
    def _():
        load(i + 1, (i + 1) & 1).start()

    slot = i & 1
    load(i, slot).wait()
    base = i * MOVE_ROWS

    def scatter(t, _):
        _tile_copy(stage_ref.at[slot], t, xb_ref, d1_ref[base + t], ssem.at[slot]).start()
        _tile_copy(stage_ref.at[slot], t, xb_ref, d2_ref[base + t], ssem.at[slot]).start(priority=1)
        return 0

    lax.fori_loop(0, MOVE_ROWS, scatter, 0)

    @pl.when(i == n - 1)
    def _():
        wait_scatters(slot)


def _dispatch(xn, dest1, dest2, pad_start, n_slots):
    T = dest1.shape[0]
    grid_spec = pltpu.PrefetchScalarGridSpec(
        num_scalar_prefetch=3, grid=(T // MOVE_ROWS,),
        in_specs=[pl.BlockSpec(memory_space=pl.ANY)],
        out_specs=pl.BlockSpec(memory_space=pl.ANY),
        scratch_shapes=[pltpu.VMEM((EXPERT_ROWS * SUBLANES, LANES), F32),
                        pltpu.VMEM((2, MOVE_ROWS * SUBLANES, LANES), F32),
                        pltpu.SemaphoreType.DMA(()), pltpu.SemaphoreType.DMA((2,)),
                        pltpu.SemaphoreType.DMA((2,))])
    return pl.pallas_call(
        _dispatch_body, grid_spec=grid_spec,
        out_shape=jax.ShapeDtypeStruct(((n_slots + EXPERT_ROWS) * SUBLANES, LANES), F32),
        compiler_params=_params("arbitrary"), name="dispatch")(pad_start, dest1, dest2, xn)


def _expert_body(be_ref, nused_ref, xb_ref, wg_ref, wu_ref, wd_ref, yb_ref,
                 x_ref, wgb_ref, wub_ref, wdb_ref):
    i = pl.program_id(0)
    used = i < nused_ref[0]
    new_expert = (i == 0) | (be_ref[i] != be_ref[jnp.maximum(i - 1, 0)])

    @pl.when(used & new_expert)
    def _():
        wgb_ref[...] = wg_ref[0, 0].astype(BF16)
        wub_ref[...] = wu_ref[0, 0].astype(BF16)
        wdb_ref[...] = wd_ref[0, 0].astype(BF16)

    @pl.when(used)
    def _():
        for c in range(SUBLANES):
            x_ref[:, c * LANES:(c + 1) * LANES] = xb_ref[pl.ds(c, EXPERT_ROWS, stride=SUBLANES), :].astype(BF16)
        x = x_ref[...]
        gate = jnp.dot(x, wgb_ref[...], preferred_element_type=F32)
        up = jnp.dot(x, wub_ref[...], preferred_element_type=F32)
        hmid = (_silu(gate) * up).astype(BF16)
        y = jnp.dot(hmid, wdb_ref[...], preferred_element_type=F32)
        for c in range(SUBLANES):
            yb_ref[pl.ds(c, EXPERT_ROWS, stride=SUBLANES), :] = y[:, c * LANES:(c + 1) * LANES]

    @pl.when(jnp.logical_not(used))
    def _():
        yb_ref[...] = jnp.zeros_like(yb_ref)


def _experts(xb, blk_exp, n_used, n_blk, layer, w_gate, w_up, w_down):
    M = EXPERT_ROWS
    tile = lambda index: pl.BlockSpec((M * SUBLANES, LANES), index)
    weight = lambda a: pl.BlockSpec((1, 1) + a.shape[2:], lambda i, be, nu: (layer, be[i], 0, 0))
    grid_spec = pltpu.PrefetchScalarGridSpec(
        num_scalar_prefetch=2, grid=(n_blk,),
        in_specs=[tile(lambda i, be, nu: (jnp.minimum(i, nu[0] - 1), 0)),
                  weight(w_gate), weight(w_up), weight(w_down)],
        out_specs=tile(lambda i, be, nu: (i, 0)),
        scratch_shapes=[pltpu.VMEM((M, D_MODEL), BF16), pltpu.VMEM((D_MODEL, MOE_HIDDEN), BF16),
                        pltpu.VMEM((D_MODEL, MOE_HIDDEN), BF16), pltpu.VMEM((MOE_HIDDEN, D_MODEL), BF16)])
    return pl.pallas_call(
        _expert_body, grid_spec=grid_spec,
        out_shape=jax.ShapeDtypeStruct((n_blk * M * SUBLANES, LANES), F32),
        compiler_params=_params("arbitrary"), name="experts")(
            blk_exp, n_used, xb, w_gate, w_up, w_down)


def _combine_body(d1_ref, d2_ref, h_ref, route_ref, yb_ref, o_ref, buf_ref, sem):
    i = pl.program_id(0)
    n = pl.num_programs(0)

    def gather(step, slot):
        base = step * MOVE_ROWS

        def body(t, _):
            _tile_copy(yb_ref, d1_ref[base + t], buf_ref.at[slot, 0], t, sem.at[slot]).start()
            _tile_copy(yb_ref, d2_ref[base + t], buf_ref.at[slot, 1], t, sem.at[slot]).start(priority=1)
            return 0

        lax.fori_loop(0, MOVE_ROWS, body, 0)

    @pl.when(i == 0)
    def _():
        gather(0, 0)

    @pl.when(i + 1 < n)
    def _():
        gather(i + 1, (i + 1) & 1)

    slot = i & 1
    pltpu.make_async_copy(buf_ref.at[slot], buf_ref.at[slot], sem.at[slot]).wait()
    route = route_ref[...]
    g1 = route[:, 2:3]
    g2 = route[:, 3:4]
    for c in range(SUBLANES):
        cs = slice(c * LANES, (c + 1) * LANES)
        rows = pl.ds(c, MOVE_ROWS, stride=SUBLANES)
        o_ref[:, cs] = h_ref[:, cs] + (buf_ref[slot, 0, rows, :] * g1 + buf_ref[slot, 1, rows, :] * g2)


def _combine(h2, route, yb, dest1, dest2):
    T = h2.shape[0]
    R = MOVE_ROWS
    grid_spec = pltpu.PrefetchScalarGridSpec(
        num_scalar_prefetch=2, grid=(T // R,),
        in_specs=[pl.BlockSpec((R, D_MODEL), lambda i, d1, d2: (i, 0)),
                  pl.BlockSpec((R, LANES), lambda i, d1, d2: (i, 0)), pl.BlockSpec(memory_space=pl.ANY)],
        out_specs=pl.BlockSpec((R, D_MODEL), lambda i, d1, d2: (i, 0)),
        scratch_shapes=[pltpu.VMEM((2, 2, R * SUBLANES, LANES), F32), pltpu.SemaphoreType.DMA((2,))])
    return pl.pallas_call(
        _combine_body, grid_spec=grid_spec, out_shape=jax.ShapeDtypeStruct((T, D_MODEL), F32),
        compiler_params=_params("arbitrary"), name="combine")(dest1, dest2, h2, route, yb)


def _moe(h2, xn, route, routet, counts, layer, w_gate, w_up, w_down):
    T = h2.shape[0]
    M = EXPERT_ROWS
    n_blk = (2 * T) // M + MOE_EXPERTS
    cnt = counts[0, ROUTER_LANE0:ROUTER_LANE0 + MOE_EXPERTS].astype(jnp.int32)
    padded = (cnt + M - 1) // M * M
    ends = jnp.cumsum(padded)
    pstarts = ends - padded
    n_used = (ends[-1] // M).astype(jnp.int32)
    blk = jnp.minimum(jnp.arange(n_blk, dtype=jnp.int32), n_used - 1) * M
    blk_exp = jnp.sum((ends[None, :] <= blk[:, None]).astype(jnp.int32), axis=1)
    blk_exp = jnp.minimum(blk_exp, MOE_EXPERTS - 1)
    dest = _slots(routet, pstarts)
    pad_info = jnp.concatenate([pstarts + cnt, n_used.reshape(1)])
    xb = _dispatch(xn, dest[0], dest[1], pad_info, n_blk * M)
    yb = _experts(xb, blk_exp, n_used.reshape(1), n_blk, layer, w_gate, w_up, w_down)
    return _combine(h2, route, yb, dest[0], dest[1])


def kernel(x, rel_bias, norm1_g, w_in, conv_w, conv_b, dt_bias, a_log, d_skip, ssd_norm_g,
           q_norm_g, k_norm_g, w_out, norm2_g, w_r1, b_r1, w_r2, b_r2, w_gate, w_up, w_down):
    B, S, D = x.shape
    T = B * S
    h2 = x.reshape(T, D)
    depth = w_in.shape[0]
    bias_tiles = _bias_tiles(rel_bias)
    for l in range(depth):
        z, xbc, dt, dtt, q, k, v = _inproj(h2, norm1_g[l], w_in[l])
        r3 = lambda a: a.reshape(B, S, a.shape[-1])
        y_ssd = _ssd(r3(z), r3(xbc), r3(dt), dtt, conv_w[l], conv_b[l], dt_bias[l], a_log[l],
                     d_skip[l], ssd_norm_g[l])
        y_att = _attention(r3(q), r3(k), r3(v), q_norm_g[l], k_norm_g[l], bias_tiles)
        h2, xn, route, routet, counts = _outproj_route(
            h2, y_ssd.reshape(T, SSD_INNER), y_att.reshape(T, ATT_INNER), w_out[l], norm2_g[l],
            w_r1[l], b_r1[l], w_r2[l], b_r2[l])
        h2 = _moe(h2, xn, route, routet, counts, l, w_gate, w_up, w_down)
    return h2.reshape(B, S, D)
```

```python
import functools
import math

import numpy as np
import jax
import jax.numpy as jnp
from jax import lax
from jax.experimental import pallas as pl
from jax.experimental.pallas import tpu as pltpu

F32 = jnp.float32
BF16 = jnp.bfloat16
HIGHEST = lax.Precision.HIGHEST

LANES = 128
SUBLANES = 8

D_MODEL = 1024
SSD_HEADS = 8
SSD_HEAD_DIM = 64
SSD_INNER = SSD_HEADS * SSD_HEAD_DIM
SSD_GROUPS = 2
SSD_HEADS_PER_GROUP = SSD_HEADS // SSD_GROUPS
SSD_STATE = 64
SSD_CONV = 4
SSD_CHUNK = 128
CONV_CH = SSD_INNER + 2 * SSD_GROUPS * SSD_STATE
ATT_HEADS = 8
ATT_HEAD_DIM = 64
ATT_INNER = ATT_HEADS * ATT_HEAD_DIM
MOBA_BLOCK = 256
MOBA_TOPK = 3
REL_BUCKETS = 32
REL_MAX_DIST = 128
MOE_GROUPS = 4
MOE_EXPERTS_PER_GROUP = 8
MOE_EXPERTS = MOE_GROUPS * MOE_EXPERTS_PER_GROUP
MOE_HIDDEN = 512
EPS = 1e-6

MASK_VALUE = -1e30
LOG2E = math.log2(math.e)
IN_ROWS = 512
SSD_STEP_CHUNKS = 2
OUT_ROWS = 512
EXPERT_ROWS = 512
MOVE_ROWS = 256
VMEM_LIMIT = 56 * 1024 * 1024
ROUTER_LANE0 = MOE_GROUPS


def _silu(x):
    return x * (1.0 / (1.0 + jnp.exp(-x)))


def _softplus(x):
    return jnp.maximum(x, 0.0) + jnp.log(1.0 + jnp.exp(-jnp.abs(x)))


def _params(*sem):
    return pltpu.CompilerParams(dimension_semantics=sem, vmem_limit_bytes=VMEM_LIMIT)


def _inproj_body(h_ref, g_ref, wz_ref, wx_ref, wdt_ref, wq_ref, wk_ref, wv_ref,
                 z_ref, xbc_ref, dt_ref, dtt_ref, q_ref, k_ref, v_ref):
    x = h_ref[...]
    xn = x * lax.rsqrt(jnp.mean(x * x, axis=-1, keepdims=True) + EPS) * g_ref[...]
    xb = xn.astype(BF16)
    z_ref[...] = jnp.dot(xb, wz_ref[...], preferred_element_type=F32)
    xbc_ref[...] = jnp.dot(xb, wx_ref[...], preferred_element_type=F32)
    dt = jnp.dot(xb, wdt_ref[...], preferred_element_type=F32)
    dt_ref[...] = dt
    dtt_ref[...] = dt.T[0:SSD_HEADS, :]
    q_ref[...] = jnp.dot(xb, wq_ref[...], preferred_element_type=F32)
    k_ref[...] = jnp.dot(xb, wk_ref[...], preferred_element_type=F32)
    v_ref[...] = jnp.dot(xb, wv_ref[...], preferred_element_type=F32)


def _inproj(h2, g, w_in):
    T = h2.shape[0]
    c0, c1, c2 = SSD_INNER, SSD_INNER + CONV_CH, SSD_INNER + CONV_CH + SSD_HEADS
    wb = w_in.astype(BF16)
    wz, wx, wdt = wb[:, :c0], wb[:, c0:c1], wb[:, c1:c2]
    wq, wk, wv = (wb[:, c2 + i * ATT_INNER:c2 + (i + 1) * ATT_INNER] for i in range(3))
    wdt_pad = jnp.pad(wdt, ((0, 0), (0, LANES - SSD_HEADS)))
    full = lambda a: pl.BlockSpec(a.shape, lambda i: (0,) * a.ndim)
    rows = lambda n: pl.BlockSpec((IN_ROWS, n), lambda i: (i, 0))
    ins = (h2, g.reshape(1, D_MODEL), wz, wx, wdt_pad, wq, wk, wv)
    out_shape = (
        jax.ShapeDtypeStruct((T, SSD_INNER), F32), jax.ShapeDtypeStruct((T, CONV_CH), F32),
        jax.ShapeDtypeStruct((T, LANES), F32), jax.ShapeDtypeStruct((SSD_HEADS, T), F32),
        jax.ShapeDtypeStruct((T, ATT_INNER), F32), jax.ShapeDtypeStruct((T, ATT_INNER), F32),
        jax.ShapeDtypeStruct((T, ATT_INNER), F32))
    out_specs = (rows(SSD_INNER), rows(CONV_CH), rows(LANES),
                 pl.BlockSpec((SSD_HEADS, IN_ROWS), lambda i: (0, i)),
                 rows(ATT_INNER), rows(ATT_INNER), rows(ATT_INNER))
    return pl.pallas_call(
        _inproj_body, grid=(T // IN_ROWS,),
        in_specs=[rows(D_MODEL)] + [full(a) for a in ins[1:]],
        out_specs=out_specs, out_shape=out_shape,
        compiler_params=_params("arbitrary"), name="inproj")(*ins)


def _ssd_body(z_ref, xbc_ref, dt_ref, dtt_ref, cw_ref, cb_ref, dtb_ref, dtbt_ref, alog_ref,
              alogt_ref, dskip_ref, ng_ref, tril_ref, expand_ref, y_ref, ext_ref, state_ref):
    L, P, N, E = SSD_CHUNK, SSD_HEAD_DIM, SSD_STATE, SSD_HEADS_PER_GROUP
    R = SSD_STEP_CHUNKS * L
    c = pl.program_id(1)

    @pl.when(c == 0)
    def _():
        state_ref[...] = jnp.zeros_like(state_ref)
        ext_ref[0:SUBLANES, :] = jnp.zeros((SUBLANES, CONV_CH), F32)

    ext_ref[SUBLANES:SUBLANES + R, :] = xbc_ref[0]
    tril = tril_ref[...]
    expand = expand_ref[...]
    row = lax.broadcasted_iota(jnp.int32, (L, L), 0)
    col = lax.broadcasted_iota(jnp.int32, (L, L), 1)
    causal = row >= col
    for sub in range(SSD_STEP_CHUNKS):
        _ssd_chunk(sub * L, z_ref, dt_ref, dtt_ref, cw_ref, cb_ref, dtb_ref, dtbt_ref, alog_ref,
                   alogt_ref, dskip_ref, ng_ref, tril, expand, causal, y_ref, ext_ref, state_ref)
    ext_ref[0:SUBLANES, :] = ext_ref[R:R + SUBLANES, :]


def _ssd_chunk(r0, z_ref, dt_ref, dtt_ref, cw_ref, cb_ref, dtb_ref, dtbt_ref, alog_ref, alogt_ref,
               dskip_ref, ng_ref, tril, expand, causal, y_ref, ext_ref, state_ref):
    L, P, N, E = SSD_CHUNK, SSD_HEAD_DIM, SSD_STATE, SSD_HEADS_PER_GROUP
    conv = jnp.zeros((L, CONV_CH), F32) + cb_ref[...]
    for j in range(SSD_CONV):
        off = r0 + SUBLANES - (SSD_CONV - 1) + j
        conv = conv + cw_ref[j:j + 1, :] * ext_ref[off:off + L, :]
    act = _silu(conv)
    xs = act[:, :SSD_INNER]
    b_in = act[:, SSD_INNER:SSD_INNER + SSD_GROUPS * N]
    c_out = act[:, SSD_INNER + SSD_GROUPS * N:]

    dt = _softplus(dt_ref[0, r0:r0 + L, :] + dtb_ref[...])
    a_dt = dt * (-jnp.exp(alog_ref[...]))
    a_cs = jnp.dot(tril, a_dt, precision=HIGHEST, preferred_element_type=F32)
    dt_t = _softplus(dtt_ref[:, r0:r0 + L] + dtbt_ref[...])
    a_dt_t = dt_t * (-jnp.exp(alogt_ref[...]))
    a_cs_t = lax.dot_general(a_dt_t, tril, (((1,), (1,)), ((), ())), precision=HIGHEST,
                             preferred_element_type=F32)
    last = a_cs[L - 1:L, :]
    def hdot(a):
        hi = a.astype(BF16)
        lo = (a - hi.astype(F32)).astype(BF16)
        return (jnp.dot(hi, expand, preferred_element_type=F32)
                + jnp.dot(lo, expand, preferred_element_type=F32))
    dt_x = hdot(dt)
    eacs_x = hdot(jnp.exp(a_cs))
    dec_x = hdot(jnp.exp(last - a_cs))
    cdec_x = hdot(jnp.broadcast_to(jnp.exp(last), (SUBLANES, LANES)))[0:1, :]
    xdt = xs * dt_x
    xdtd = (xdt * dec_x).astype(BF16)
    xdt_b = xdt.astype(BF16)

    y_parts = []
    for g in range(SSD_GROUPS):
        bg = b_in[:, g * N:(g + 1) * N].astype(BF16)
        cg = c_out[:, g * N:(g + 1) * N].astype(BF16)
        cb = lax.dot_general(cg, bg, (((1,), (1,)), ((), ())), preferred_element_type=F32)
        st = state_ref[g]
        gs = slice(g * E * P, (g + 1) * E * P)
        y_off = jnp.dot(cg, st.astype(BF16), preferred_element_type=F32) * eacs_x[:, gs]
        for e in range(E):
            h = g * E + e
            diff = a_cs[:, h:h + 1] - a_cs_t[h:h + 1, :]
            l_mat = jnp.exp(jnp.where(causal, diff, -jnp.inf))
            m = (cb * l_mat).astype(BF16)
            y_diag = jnp.dot(m, xdt_b[:, h * P:(h + 1) * P], preferred_element_type=F32)
            y_parts.append(y_diag + y_off[:, e * P:(e + 1) * P])
        upd = lax.dot_general(bg, xdtd[:, gs], (((0,), (0,)), ((), ())),
                              preferred_element_type=F32)
        state_ref[g] = cdec_x[:, gs] * st + upd
    y = jnp.concatenate(y_parts, axis=-1) + dskip_ref[...] * xs
    y = y * _silu(z_ref[0, r0:r0 + L, :])
    y = y * lax.rsqrt(jnp.mean(y * y, axis=-1, keepdims=True) + EPS) * ng_ref[...]
    y_ref[0, r0:r0 + L, :] = y


def _ssd(z, xbc, dt, dtt, conv_w, conv_b, dt_bias, a_log, d_skip, norm_g):
    B, S, _ = z.shape
    L = SSD_CHUNK
    R = SSD_STEP_CHUNKS * L
    nc = S // R
    pad_row = lambda a: jnp.pad(a.reshape(1, SSD_HEADS), ((0, 0), (0, LANES - SSD_HEADS)))
    tril = jnp.asarray(np.tril(np.ones((L, L), np.float32)))
    expand_np = np.zeros((LANES, SSD_INNER), np.float32)
    for h in range(SSD_HEADS):
        expand_np[h, h * SSD_HEAD_DIM:(h + 1) * SSD_HEAD_DIM] = 1.0
    ins = (z, xbc, dt, dtt, conv_w, conv_b.reshape(1, CONV_CH), pad_row(dt_bias),
           dt_bias.reshape(SSD_HEADS, 1), pad_row(a_log), a_log.reshape(SSD_HEADS, 1),
           jnp.repeat(d_skip, SSD_HEAD_DIM).reshape(1, SSD_INNER), norm_g.reshape(1, SSD_INNER),
           tril, jnp.asarray(expand_np, dtype=BF16))
    full = lambda a: pl.BlockSpec(a.shape, lambda b, c: (0,) * a.ndim)
    chunk = lambda n: pl.BlockSpec((1, R, n), lambda b, c: (b, c, 0))
    in_specs = [chunk(SSD_INNER), chunk(CONV_CH), chunk(LANES),
                pl.BlockSpec((SSD_HEADS, R), lambda b, c: (0, b * nc + c))]
    in_specs += [full(a) for a in ins[4:]]
    return pl.pallas_call(
        _ssd_body, grid=(B, nc), in_specs=in_specs, out_specs=chunk(SSD_INNER),
        out_shape=jax.ShapeDtypeStruct((B, S, SSD_INNER), F32),
        scratch_shapes=[pltpu.VMEM((R + SUBLANES, CONV_CH), F32),
                        pltpu.VMEM((SSD_GROUPS, SSD_STATE, SSD_HEADS_PER_GROUP * SSD_HEAD_DIM), F32)],
        compiler_params=_params("arbitrary", "arbitrary"), name="ssd")(*ins)


def _bucket_tile():
    bs = MOBA_BLOCK
    dist = np.arange(bs)[:, None] - np.arange(2 * bs)[None, :] + bs
    max_exact = REL_BUCKETS // 2
    d = np.maximum(dist, max_exact).astype(np.float32)
    large = max_exact + (np.log(d / np.float32(max_exact)) / np.float32(math.log(REL_MAX_DIST / max_exact))
                         * np.float32(REL_BUCKETS - max_exact)).astype(np.int32)
    large = np.minimum(large, REL_BUCKETS - 1)
    bucket = np.where(dist < max_exact, dist, large)
    return np.where(dist >= 0, bucket, -1).astype(np.int32)


def _bias_body(rb_ref, bucket_ref, o_ref):
    head = pl.program_id(0)
    bucket = bucket_ref[...]
    far = rb_ref[REL_BUCKETS - 1, head]
    bias = jnp.where(bucket < 0, MASK_VALUE, 0.0)
    for i in range(REL_BUCKETS - 1):
        bias = jnp.where(bucket == i, (rb_ref[i, head] - far) * LOG2E, bias)
    o_ref[0] = bias


def _bias_tiles(rel_bias):
    bs = MOBA_BLOCK
    bucket = jnp.asarray(_bucket_tile())
    return pl.pallas_call(
        _bias_body, grid=(ATT_HEADS,),
        in_specs=[pl.BlockSpec(memory_space=pltpu.SMEM), pl.BlockSpec((bs, 2 * bs), lambda h: (0, 0))],
        out_specs=pl.BlockSpec((1, bs, 2 * bs), lambda h: (h, 0, 0)),
        out_shape=jax.ShapeDtypeStruct((ATT_HEADS, bs, 2 * bs), F32),
        compiler_params=_params("arbitrary"), name="bias_tiles")(rel_bias, bucket)


def _attn_body(q_ref, k_ref, v_ref, qg_ref, kg_ref, bias_ref, o_ref,
               kaug_ref, vaug_ref, qaug_ref, sa_ref, sb_ref, m_ref, acc_ref):
    bs, hd = MOBA_BLOCK, ATT_HEAD_DIM
    nb = k_ref.shape[1] // bs
    jp = pl.program_id(2)
    nt_dims = (((1,), (1,)), ((), ()))

    @pl.when(jp == 0)
    def _():
        S = nb * bs
        pair = 2 * hd
        r = lax.broadcasted_iota(jnp.int32, (pair, pair), 0) // hd
        c = lax.broadcasted_iota(jnp.int32, (pair, pair), 1) // hd
        same_head = (r == c).astype(BF16)

        def head_norm(x, gain):
            ms = jnp.dot((x * x).astype(BF16), same_head, preferred_element_type=F32) * (1.0 / hd)
            return x * lax.rsqrt(ms + EPS) * gain

        kn = head_norm(k_ref[0], kg_ref[...])
        kmean = jnp.mean(kn.reshape(nb, bs, pair), axis=1)
        kn_b = kn.astype(BF16)
        v_b = v_ref[0].astype(BF16)
        lane = lax.broadcasted_iota(jnp.int32, (S, pair), 1)
        blk = lax.broadcasted_iota(jnp.int32, (S, pair), 0) // bs
        lane_m = lax.broadcasted_iota(jnp.int32, (nb, pair), 1)
        rowb = lax.broadcasted_iota(jnp.int32, (nb, bs), 0)
        lane_q = lax.broadcasted_iota(jnp.int32, (bs, pair), 1)
        kmean_b = []
        for h in range(2):
            own = (lane // hd) == h
            other0 = (1 - h) * hd
            kaug_ref[h] = jnp.where(own, kn_b, (lane - other0 == blk).astype(BF16))
            vaug_ref[h] = jnp.where(own, v_b, jnp.ones((S, pair), BF16))
            kmean_b.append(jnp.where((lane_m // hd) == h, kmean, 0.0).astype(BF16))
        for j in range(nb):
            rows = slice(j * bs, (j + 1) * bs)
            qn = head_norm(q_ref[0, rows, :], qg_ref[...])
            qn_b = qn.astype(BF16)
            qs = qn * (ATT_HEAD_DIM ** -0.5 * LOG2E)
            for h in range(2):
                if j <= MOBA_TOPK:
                    sel = rowb <= j
                else:
                    gate = lax.dot_general(kmean_b[h], qn_b, nt_dims,
                                           preferred_element_type=F32)
                    rank = jnp.zeros((nb, bs), jnp.int32)
                    for m in range(j):
                        gm = gate[m:m + 1, :]
                        beats = (gm > gate) | ((gm == gate) & (rowb > m))
                        rank = rank + beats.astype(jnp.int32)
                    sel = ((rowb < j) & (rank < MOBA_TOPK)) | (rowb == j)
                selmask = jnp.where(sel, 0.0, MASK_VALUE)
                pieces = [selmask, jnp.zeros((pair - nb, bs), F32)]
                if h == 0:
                    pieces = [jnp.zeros((hd, bs), F32), selmask, jnp.zeros((hd - nb, bs), F32)]
                mask_t = jnp.concatenate(pieces, axis=0).T
                qaug_ref[h, rows, :] = jnp.where((lane_q // hd) == h, qs, mask_t).astype(BF16)

    qt = 2 * bs
    q0 = pl.multiple_of(jp * qt, qt)
    all_rows = slice(0, qt)
    late_rows = slice(bs, qt)
    for h in range(2):
        m_ref[h] = jnp.full((qt, 2 * hd), -jnp.inf, F32)
        acc_ref[h] = jnp.zeros((qt, 2 * hd), F32)

    def scores(n, s_ref, rows=all_rows):
        start = pl.multiple_of(n * bs, bs)
        q_rows = pl.ds(q0 + rows.start, rows.stop - rows.start)
        for h in range(2):
            s_ref[h, rows, :] = lax.dot_general(qaug_ref[h, q_rows, :], kaug_ref[h, pl.ds(start, bs), :],
                                                nt_dims, preferred_element_type=F32)

    def update(n, s_ref, bias, rows=all_rows):
        start = pl.multiple_of(n * bs, bs)
        for r0 in range(rows.start, rows.stop, bs):
            sub = slice(r0, r0 + bs)
            for h in range(2):
                s = s_ref[h, sub, :]
                tile = None if bias is None else bias(h, r0 // bs)
                if tile is not None:
                    s = s + tile
                m_i = m_ref[h, sub, :]
                m_new = jnp.maximum(m_i, jnp.max(s, axis=-1, keepdims=True))
                alpha = jnp.exp2(m_i - m_new)
                p = jnp.exp2(s - jnp.concatenate([m_new, m_new], axis=-1)).astype(BF16)
                pv = jnp.dot(p, vaug_ref[h, pl.ds(start, bs), :], preferred_element_type=F32)
                acc_ref[h, sub, :] = alpha * acc_ref[h, sub, :] + pv
                m_ref[h, sub, :] = m_new

    def far_pair(i, _):
        n = 2 * i
        scores(n + 1, sb_ref)
        update(n, sa_ref, None)
        scores(n + 2, sa_ref)
        update(n + 1, sb_ref, None)
        return 0

    n_far = jnp.maximum(2 * jp - 1, 0)
    scores(0, sa_ref)
    lax.fori_loop(0, n_far >> 1, far_pair, 0)
    prev = lambda h: bias_ref[h, :, 0:bs]
    own = lambda h, qblk=1: bias_ref[h, :, bs:2 * bs]
    bias_p = lambda h, qblk: prev(h) if qblk == 0 else None
    bias_o = lambda h, qblk: own(h) if qblk == 0 else prev(h)
    blk_p = jnp.maximum(2 * jp - 1, 0)
    blk_o = 2 * jp
    blk_n = 2 * jp + 1

    @pl.when(jp > 0)
    def _():
        scores(blk_p, sb_ref)
        update(n_far - 1, sa_ref, None)
        scores(blk_o, sa_ref)
        update(blk_p, sb_ref, bias_p)
        scores(blk_n, sb_ref, late_rows)
        update(blk_o, sa_ref, bias_o)
        update(blk_n, sb_ref, own, late_rows)

    @pl.when(jp == 0)
    def _():
        scores(blk_n, sb_ref, late_rows)
        update(blk_o, sa_ref, bias_o)
        update(blk_n, sb_ref, own, late_rows)

    ratio = [acc_ref[h] / pltpu.roll(acc_ref[h], hd, 1) for h in range(2)]
    lane_o = lax.broadcasted_iota(jnp.int32, (qt, 2 * hd), 1)
    o_ref[0] = jnp.where(lane_o < hd, ratio[0], ratio[1])


def _attention(q, k, v, q_g, k_g, bias_tiles):
    B, S, _ = q.shape
    bs = MOBA_BLOCK
    nb = S // bs
    assert S % (2 * bs) == 0 and nb <= ATT_HEAD_DIM
    pair = 2 * ATT_HEAD_DIM
    return pl.pallas_call(
        _attn_body, grid=(B, ATT_HEADS // 2, nb // 2),
        in_specs=[
            pl.BlockSpec((1, S, pair), lambda b, hp, i: (b, 0, hp)),
            pl.BlockSpec((1, S, pair), lambda b, hp, i: (b, 0, hp)),
            pl.BlockSpec((1, S, pair), lambda b, hp, i: (b, 0, hp)),
            pl.BlockSpec((1, pair), lambda b, hp, i: (0, 0)),
            pl.BlockSpec((1, pair), lambda b, hp, i: (0, 0)),
            pl.BlockSpec((2, bs, 2 * bs), lambda b, hp, i: (hp, 0, 0)),
        ],
        out_specs=pl.BlockSpec((1, 2 * bs, pair), lambda b, hp, i: (b, i, hp)),
        out_shape=jax.ShapeDtypeStruct((B, S, ATT_INNER), F32),
        scratch_shapes=[pltpu.VMEM((2, S, pair), BF16), pltpu.VMEM((2, S, pair), BF16),
                        pltpu.VMEM((2, S, pair), BF16),
                        pltpu.VMEM((2, 2 * bs, bs), F32), pltpu.VMEM((2, 2 * bs, bs), F32),
                        pltpu.VMEM((2, 2 * bs, pair), F32),
                        pltpu.VMEM((2, 2 * bs, pair), F32)],
        compiler_params=_params("arbitrary", "arbitrary", "arbitrary"), name="moba")(
            q, k, v, jnp.tile(q_g, 2).reshape(1, pair), jnp.tile(k_g, 2).reshape(1, pair), bias_tiles)


def _outproj_body(h_ref, ys_ref, ya_ref, wa_ref, wb_ref, g_ref, wrh_ref, wrl_ref, br_ref, tri_ref,
                  hn_ref, xn_ref, route_ref, routet_ref, counts_ref, carry_ref):
    i = pl.program_id(0)

    @pl.when(i == 0)
    def _():
        carry_ref[...] = jnp.zeros_like(carry_ref)

    hn = (h_ref[...]
          + jnp.dot(ys_ref[...].astype(BF16), wa_ref[...], preferred_element_type=F32)
          + jnp.dot(ya_ref[...].astype(BF16), wb_ref[...], preferred_element_type=F32))
    hn_ref[...] = hn
    xn = hn * lax.rsqrt(jnp.mean(hn * hn, axis=-1, keepdims=True) + EPS) * g_ref[...]
    for c in range(SUBLANES):
        xn_ref[pl.ds(c, xn.shape[0], stride=SUBLANES), :] = xn[:, c * LANES:(c + 1) * LANES]
    x_hi = xn.astype(BF16)
    x_lo = (xn - x_hi.astype(F32)).astype(BF16)
    hi_terms = jnp.dot(x_hi, wrl_ref[...], preferred_element_type=F32)
    logits = (hi_terms[:, :LANES] + hi_terms[:, LANES:]
              + jnp.dot(x_lo, wrh_ref[...], preferred_element_type=F32)) + br_ref[...]
    rows = logits.shape[0]
    lane = lax.broadcasted_iota(jnp.int32, (rows, LANES), 1)
    first = lambda hit: jnp.min(jnp.where(hit, lane, LANES), axis=-1, keepdims=True)
    l1 = jnp.where(lane < MOE_GROUPS, logits, -jnp.inf)
    mx = jnp.max(l1, axis=-1, keepdims=True)
    gval = 1.0 / jnp.sum(jnp.exp(l1 - mx), axis=-1, keepdims=True)
    gidx = first(l1 == mx)
    e_lane = lane - ROUTER_LANE0
    in_grp = (e_lane >= 0) & (e_lane < MOE_EXPERTS) & ((e_lane >> 3) == gidx)
    l2 = jnp.where(in_grp, logits, -jnp.inf)
    m1 = jnp.max(l2, axis=-1, keepdims=True)
    i1 = first(l2 == m1)
    l2b = jnp.where(lane == i1, -jnp.inf, l2)
    m2 = jnp.max(l2b, axis=-1, keepdims=True)
    i2 = first(l2b == m2)
    r = jnp.exp(m2 - m1)
    g1 = gval / (1.0 + r)
    g2 = gval * r / (1.0 + r)
    oh1 = (lane == i1)
    oh2 = (lane == i2)
    both = jnp.where(oh1 | oh2, 1.0, 0.0)
    cx = jnp.dot(tri_ref[...], both.astype(BF16), preferred_element_type=F32) + carry_ref[...]
    rank1 = jnp.sum(jnp.where(oh1, cx, 0.0), axis=-1, keepdims=True)
    rank2 = jnp.sum(jnp.where(oh2, cx, 0.0), axis=-1, keepdims=True)
    carry_ref[...] = carry_ref[...] + jnp.sum(both, axis=0, keepdims=True)
    counts_ref[...] = carry_ref[...]
    cols = ((i1 - ROUTER_LANE0).astype(F32), (i2 - ROUTER_LANE0).astype(F32), g1, g2, rank1, rank2)
    route = jnp.zeros((rows, LANES), F32)
    for j, cval in enumerate(cols):
        route = jnp.where(lane == j, cval, route)
    route_ref[...] = route
    routet_ref[...] = route.T[0:SUBLANES, :]


def _outproj_route(h2, y_ssd, y_att, w_out, norm_g, w_r1, b_r1, w_r2, b_r2):
    T = h2.shape[0]
    R = OUT_ROWS
    wb = w_out.astype(BF16)
    pad = LANES - MOE_GROUPS - MOE_EXPERTS
    wr = jnp.pad(jnp.concatenate([w_r1, w_r2], axis=1), ((0, 0), (0, pad)))
    wr_hi = wr.astype(BF16)
    wr_lo = (wr - wr_hi.astype(F32)).astype(BF16)
    br = jnp.pad(jnp.concatenate([b_r1, b_r2]), (0, pad)).reshape(1, LANES)
    tri = jnp.asarray(np.tril(np.ones((R, R), np.float32), -1), dtype=BF16)
    ins = (h2, y_ssd, y_att, wb[:SSD_INNER], wb[SSD_INNER:], norm_g.reshape(1, D_MODEL),
           wr_hi, jnp.concatenate([wr_hi, wr_lo], axis=1), br, tri)
    full = lambda a: pl.BlockSpec(a.shape, lambda i: (0,) * a.ndim)
    rows = lambda n: pl.BlockSpec((R, n), lambda i: (i, 0))
    return pl.pallas_call(
        _outproj_body, grid=(T // R,),
        in_specs=[rows(D_MODEL), rows(SSD_INNER), rows(ATT_INNER)] + [full(a) for a in ins[3:]],
        out_specs=(rows(D_MODEL), pl.BlockSpec((R * SUBLANES, LANES), lambda i: (i, 0)), rows(LANES),
                   pl.BlockSpec((SUBLANES, R), lambda i: (0, i)), pl.BlockSpec((1, LANES), lambda i: (0, 0))),
        out_shape=(jax.ShapeDtypeStruct((T, D_MODEL), F32),
                   jax.ShapeDtypeStruct((T * SUBLANES, LANES), F32),
                   jax.ShapeDtypeStruct((T, LANES), F32), jax.ShapeDtypeStruct((SUBLANES, T), F32),
                   jax.ShapeDtypeStruct((1, LANES), F32)),
        scratch_shapes=[pltpu.VMEM((1, LANES), F32)],
        compiler_params=_params("arbitrary"), name="outproj_route")(*ins)


def _slots_body(routet_ref, pstart_ref, dest_ref):
    cols = routet_ref.shape[1]
    expert = lax.broadcasted_iota(jnp.int32, (MOE_EXPERTS, cols), 0).astype(F32)
    pstart = pstart_ref[...]
    for j in range(2):
        hit = expert == routet_ref[j:j + 1, :]
        base = jnp.sum(jnp.where(hit, pstart, 0.0), axis=0, keepdims=True)
        dest_ref[j:j + 1, :] = (base + routet_ref[4 + j:5 + j, :]).astype(jnp.int32)


def _slots(routet, pstarts):
    T = routet.shape[1]
    cols = min(T, 4096)
    assert T % cols == 0
    return pl.pallas_call(
        _slots_body, grid=(T // cols,),
        in_specs=[pl.BlockSpec((SUBLANES, cols), lambda i: (0, i)),
                  pl.BlockSpec((MOE_EXPERTS, 1), lambda i: (0, 0))],
        out_specs=pl.BlockSpec((2, cols), lambda i: (0, i)),
        out_shape=jax.ShapeDtypeStruct((2, T), jnp.int32),
        compiler_params=_params("arbitrary"), name="slots")(
            routet, pstarts.astype(F32).reshape(MOE_EXPERTS, 1))


def _tile_copy(src_ref, src_row, dst_ref, dst_row, sem):
    tile = lambda row: pl.ds(pl.multiple_of(row * SUBLANES, SUBLANES), SUBLANES)
    return pltpu.make_async_copy(src_ref.at[tile(src_row)], dst_ref.at[tile(dst_row)], sem)


def _dispatch_body(pad_ref, d1_ref, d2_ref, xn_ref, xb_ref, zero_ref, stage_ref, sem, lsem, ssem):
    block = EXPERT_ROWS * SUBLANES
    n_blocks = xb_ref.shape[0] // block
    n_used = pad_ref[MOE_EXPERTS]

    def zero_copy(row):
        start = pl.multiple_of(row * SUBLANES, SUBLANES)
        return pltpu.make_async_copy(zero_ref, xb_ref.at[pl.ds(start, block)], sem)

    @pl.when(pl.program_id(0) == 0)
    def _():
        zero_ref[...] = jnp.zeros_like(zero_ref)
        for e in range(MOE_EXPERTS):
            zero_copy(pad_ref[e]).start()
        for e in range(MOE_EXPERTS):
            zero_copy(pad_ref[e]).wait()
        for b in range(n_blocks - MOE_EXPERTS - 1, n_blocks):
            @pl.when(b >= n_used)
            def _():
                zero_copy(b * EXPERT_ROWS).start()
        for b in range(n_blocks - MOE_EXPERTS - 1, n_blocks):
            @pl.when(b >= n_used)
            def _():
                zero_copy(b * EXPERT_ROWS).wait()

    i = pl.program_id(0)
    n = pl.num_programs(0)
    rows = MOVE_ROWS * SUBLANES

    def load(step, slot):
        src = xn_ref.at[pl.ds(pl.multiple_of(step * rows, rows), rows)]
        return pltpu.make_async_copy(src, stage_ref.at[slot], lsem.at[slot])

    def wait_scatters(slot):
        for _ in range(2):
            pltpu.make_async_copy(stage_ref.at[slot], stage_ref.at[slot], ssem.at[slot]).wait()

    @pl.when(i == 0)
    def _():
        load(0, 0).start()

    @pl.when(i >= 1)
    def _():
        wait_scatters((i - 1) & 1)

    @pl.when(i + 1 < n)
    def _():
        load(i + 1, (i + 1) & 1).start()

    slot = i & 1
    load(i, slot).wait()
    base = i * MOVE_ROWS

    def scatter(t, _):
        _tile_copy(stage_ref.at[slot], t, xb_ref, d1_ref[base + t], ssem.at[slot]).start()
        _tile_copy(stage_ref.at[slot], t, xb_ref, d2_ref[base + t], ssem.at[slot]).start(priority=1)
        return 0

    lax.fori_loop(0, MOVE_ROWS, scatter, 0)

    @pl.when(i == n - 1)
    def _():
        wait_scatters(slot)


def _dispatch(xn, dest1, dest2, pad_start, n_slots):
    T = dest1.shape[0]
    grid_spec = pltpu.PrefetchScalarGridSpec(
        num_scalar_prefetch=3, grid=(T // MOVE_ROWS,),
        in_specs=[pl.BlockSpec(memory_space=pl.ANY)],
        out_specs=pl.BlockSpec(memory_space=pl.ANY),
        scratch_shapes=[pltpu.VMEM((EXPERT_ROWS * SUBLANES, LANES), F32),
                        pltpu.VMEM((2, MOVE_ROWS * SUBLANES, LANES), F32),
                        pltpu.SemaphoreType.DMA(()), pltpu.SemaphoreType.DMA((2,)),
                        pltpu.SemaphoreType.DMA((2,))])
    return pl.pallas_call(
        _dispatch_body, grid_spec=grid_spec,
        out_shape=jax.ShapeDtypeStruct(((n_slots + EXPERT_ROWS) * SUBLANES, LANES), F32),
        compiler_params=_params("arbitrary"), name="dispatch")(pad_start, dest1, dest2, xn)


def _expert_body(be_ref, nused_ref, xb_ref, wg_ref, wu_ref, wd_ref, yb_ref,
                 x_ref, wgb_ref, wub_ref, wdb_ref):
    i = pl.program_id(0)
    used = i < nused_ref[0]
    new_expert = (i == 0) | (be_ref[i] != be_ref[jnp.maximum(i - 1, 0)])

    @pl.when(used & new_expert)
    def _():
        wgb_ref[...] = wg_ref[0, 0].astype(BF16)
        wub_ref[...] = wu_ref[0, 0].astype(BF16)
        wdb_ref[...] = wd_ref[0, 0].astype(BF16)

    @pl.when(used)
    def _():
        for c in range(SUBLANES):
            x_ref[:, c * LANES:(c + 1) * LANES] = xb_ref[pl.ds(c, EXPERT_ROWS, stride=SUBLANES), :].astype(BF16)
        x = x_ref[...]
        gate = jnp.dot(x, wgb_ref[...], preferred_element_type=F32)
        up = jnp.dot(x, wub_ref[...], preferred_element_type=F32)
        hmid = (_silu(gate) * up).astype(BF16)
        y = jnp.dot(hmid, wdb_ref[...], preferred_element_type=F32)
        for c in range(SUBLANES):
            yb_ref[pl.ds(c, EXPERT_ROWS, stride=SUBLANES), :] = y[:, c * LANES:(c + 1) * LANES]

    @pl.when(jnp.logical_not(used))
    def _():
        yb_ref[...] = jnp.zeros_like(yb_ref)


def _experts(xb, blk_exp, n_used, n_blk, layer, w_gate, w_up, w_down):
    M = EXPERT_ROWS
    tile = lambda index: pl.BlockSpec((M * SUBLANES, LANES), index)
    weight = lambda a: pl.BlockSpec((1, 1) + a.shape[2:], lambda i, be, nu: (layer, be[i], 0, 0))
    grid_spec = pltpu.PrefetchScalarGridSpec(
        num_scalar_prefetch=2, grid=(n_blk,),
        in_specs=[tile(lambda i, be, nu: (jnp.minimum(i, nu[0] - 1), 0)),
                  weight(w_gate), weight(w_up), weight(w_down)],
        out_specs=tile(lambda i, be, nu: (i, 0)),
        scratch_shapes=[pltpu.VMEM((M, D_MODEL), BF16), pltpu.VMEM((D_MODEL, MOE_HIDDEN), BF16),
                        pltpu.VMEM((D_MODEL, MOE_HIDDEN), BF16), pltpu.VMEM((MOE_HIDDEN, D_MODEL), BF16)])
    return pl.pallas_call(
        _expert_body, grid_spec=grid_spec,
        out_shape=jax.ShapeDtypeStruct((n_blk * M * SUBLANES, LANES), F32),
        compiler_params=_params("arbitrary"), name="experts")(
            blk_exp, n_used, xb, w_gate, w_up, w_down)


def _combine_body(d1_ref, d2_ref, h_ref, route_ref, yb_ref, o_ref, buf_ref, sem):
    i = pl.program_id(0)
    n = pl.num_programs(0)

    def gather(step, slot):
        base = step * MOVE_ROWS

        def body(t, _):
            _tile_copy(yb_ref, d1_ref[base + t], buf_ref.at[slot, 0], t, sem.at[slot]).start()
            _tile_copy(yb_ref, d2_ref[base + t], buf_ref.at[slot, 1], t, sem.at[slot]).start(priority=1)
            return 0

        lax.fori_loop(0, MOVE_ROWS, body, 0)

    @pl.when(i == 0)
    def _():
        gather(0, 0)

    @pl.when(i + 1 < n)
    def _():
        gather(i + 1, (i + 1) & 1)

    slot = i & 1
    pltpu.make_async_copy(buf_ref.at[slot], buf_ref.at[slot], sem.at[slot]).wait()
    route = route_ref[...]
    g1 = route[:, 2:3]
    g2 = route[:, 3:4]
    for c in range(SUBLANES):
        cs = slice(c * LANES, (c + 1) * LANES)
        rows = pl.ds(c, MOVE_ROWS, stride=SUBLANES)
        o_ref[:, cs] = h_ref[:, cs] + (buf_ref[slot, 0, rows, :] * g1 + buf_ref[slot, 1, rows, :] * g2)


def _combine(h2, route, yb, dest1, dest2):
    T = h2.shape[0]
    R = MOVE_ROWS
    grid_spec = pltpu.PrefetchScalarGridSpec(
        num_scalar_prefetch=2, grid=(T // R,),
        in_specs=[pl.BlockSpec((R, D_MODEL), lambda i, d1, d2: (i, 0)),
                  pl.BlockSpec((R, LANES), lambda i, d1, d2: (i, 0)), pl.BlockSpec(memory_space=pl.ANY)],
        out_specs=pl.BlockSpec((R, D_MODEL), lambda i, d1, d2: (i, 0)),
        scratch_shapes=[pltpu.VMEM((2, 2, R * SUBLANES, LANES), F32), pltpu.SemaphoreType.DMA((2,))])
    return pl.pallas_call(
        _combine_body, grid_spec=grid_spec, out_shape=jax.ShapeDtypeStruct((T, D_MODEL), F32),
        compiler_params=_params("arbitrary"), name="combine")(dest1, dest2, h2, route, yb)


def _moe(h2, xn, route, routet, counts, layer, w_gate, w_up, w_down):
    T = h2.shape[0]
    M = EXPERT_ROWS
    n_blk = (2 * T) // M + MOE_EXPERTS
    cnt = counts[0, ROUTER_LANE0:ROUTER_LANE0 + MOE_EXPERTS].astype(jnp.int32)
    padded = (cnt + M - 1) // M * M
    ends = jnp.cumsum(padded)
    pstarts = ends - padded
    n_used = (ends[-1] // M).astype(jnp.int32)
    blk = jnp.minimum(jnp.arange(n_blk, dtype=jnp.int32), n_used - 1) * M
    blk_exp = jnp.sum((ends[None, :] <= blk[:, None]).astype(jnp.int32), axis=1)
    blk_exp = jnp.minimum(blk_exp, MOE_EXPERTS - 1)
    dest = _slots(routet, pstarts)
    pad_info = jnp.concatenate([pstarts + cnt, n_used.reshape(1)])
    xb = _dispatch(xn, dest[0], dest[1], pad_info, n_blk * M)
    yb = _experts(xb, blk_exp, n_used.reshape(1), n_blk, layer, w_gate, w_up, w_down)
    return _combine(h2, route, yb, dest[0], dest[1])


def kernel(x, rel_bias, norm1_g, w_in, conv_w, conv_b, dt_bias, a_log, d_skip, ssd_norm_g,
           q_norm_g, k_norm_g, w_out, norm2_g, w_r1, b_r1, w_r2, b_r2, w_gate, w_up, w_down):
    B, S, D = x.shape
    T = B * S
    h2 = x.reshape(T, D)
    depth = w_in.shape[0]
    bias_tiles = _bias_tiles(rel_bias)
    for l in range(depth):
        z, xbc, dt, dtt, q, k, v = _inproj(h2, norm1_g[l], w_in[l])
        r3 = lambda a: a.reshape(B, S, a.shape[-1])
        y_ssd = _ssd(r3(z), r3(xbc), r3(dt), dtt, conv_w[l], conv_b[l], dt_bias[l], a_log[l],
                     d_skip[l], ssd_norm_g[l])
        y_att = _attention(r3(q), r3(k), r3(v), q_norm_g[l], k_norm_g[l], bias_tiles)
        h2, xn, route, routet, counts = _outproj_route(
            h2, y_ssd.reshape(T, SSD_INNER), y_att.reshape(T, ATT_INNER), w_out[l], norm2_g[l],
            w_r1[l], b_r1[l], w_r2[l], b_r2[l])
        h2 = _moe(h2, xn, route, routet, counts, l, w_gate, w_up, w_down)
    return h2.reshape(B, S, D)
```

```python
import functools
import math

import numpy as np
import jax
import jax.numpy as jnp
from jax import lax
from jax.experimental import pallas as pl
from jax.experimental.pallas import tpu as pltpu

F32 = jnp.float32
BF16 = jnp.bfloat16
HIGHEST = lax.Precision.HIGHEST

LANES = 128
SUBLANES = 8

D_MODEL = 1024
SSD_HEADS = 8
SSD_HEAD_DIM = 64
SSD_INNER = SSD_HEADS * SSD_HEAD_DIM
SSD_GROUPS = 2
SSD_HEADS_PER_GROUP = SSD_HEADS // SSD_GROUPS
SSD_STATE = 64
SSD_CONV = 4
SSD_CHUNK = 128
CONV_CH = SSD_INNER + 2 * SSD_GROUPS * SSD_STATE
ATT_HEADS = 8
ATT_HEAD_DIM = 64
ATT_INNER = ATT_HEADS * ATT_HEAD_DIM
MOBA_BLOCK = 256
MOBA_TOPK = 3
REL_BUCKETS = 32
REL_MAX_DIST = 128
MOE_GROUPS = 4
MOE_EXPERTS_PER_GROUP = 8
MOE_EXPERTS = MOE_GROUPS * MOE_EXPERTS_PER_GROUP
MOE_HIDDEN = 512
EPS = 1e-6

MASK_VALUE = -1e30
LOG2E = math.log2(math.e)
IN_ROWS = 512
SSD_STEP_CHUNKS = 2
OUT_ROWS = 512
EXPERT_ROWS = 512
MOVE_ROWS = 256
VMEM_LIMIT = 56 * 1024 * 1024
ROUTER_LANE0 = MOE_GROUPS


def _silu(x):
    return x * (1.0 / (1.0 + jnp.exp(-x)))


def _softplus(x):
    return jnp.maximum(x, 0.0) + jnp.log(1.0 + jnp.exp(-jnp.abs(x)))


def _params(*sem):
    return pltpu.CompilerParams(dimension_semantics=sem, vmem_limit_bytes=VMEM_LIMIT)


def _inproj_body(h_ref, g_ref, wz_ref, wx_ref, wdt_ref, wq_ref, wk_ref, wv_ref,
                 z_ref, xbc_ref, dt_ref, dtt_ref, q_ref, k_ref, v_ref):
    x = h_ref[...]
    xn = x * lax.rsqrt(jnp.mean(x * x, axis=-1, keepdims=True) + EPS) * g_ref[...]
    xb = xn.astype(BF16)
    z_ref[...] = jnp.dot(xb, wz_ref[...], preferred_element_type=F32)
    xbc_ref[...] = jnp.dot(xb, wx_ref[...], preferred_element_type=F32)
    dt = jnp.dot(xb, wdt_ref[...], preferred_element_type=F32)
    dt_ref[...] = dt
    dtt_ref[...] = dt.T[0:SSD_HEADS, :]
    q_ref[...] = jnp.dot(xb, wq_ref[...], preferred_element_type=F32)
    k_ref[...] = jnp.dot(xb, wk_ref[...], preferred_element_type=F32)
    v_ref[...] = jnp.dot(xb, wv_ref[...], preferred_element_type=F32)


def _inproj(h2, g, w_in):
    T = h2.shape[0]
    c0, c1, c2 = SSD_INNER, SSD_INNER + CONV_CH, SSD_INNER + CONV_CH + SSD_HEADS
    wb = w_in.astype(BF16)
    wz, wx, wdt = wb[:, :c0], wb[:, c0:c1], wb[:, c1:c2]
    wq, wk, wv = (wb[:, c2 + i * ATT_INNER:c2 + (i + 1) * ATT_INNER] for i in range(3))
    wdt_pad = jnp.pad(wdt, ((0, 0), (0, LANES - SSD_HEADS)))
    full = lambda a: pl.BlockSpec(a.shape, lambda i: (0,) * a.ndim)
    rows = lambda n: pl.BlockSpec((IN_ROWS, n), lambda i: (i, 0))
    ins = (h2, g.reshape(1, D_MODEL), wz, wx, wdt_pad, wq, wk, wv)
    out_shape = (
        jax.ShapeDtypeStruct((T, SSD_INNER), F32), jax.ShapeDtypeStruct((T, CONV_CH), F32),
        jax.ShapeDtypeStruct((T, LANES), F32), jax.ShapeDtypeStruct((SSD_HEADS, T), F32),
        jax.ShapeDtypeStruct((T, ATT_INNER), F32), jax.ShapeDtypeStruct((T, ATT_INNER), F32),
        jax.ShapeDtypeStruct((T, ATT_INNER), F32))
    out_specs = (rows(SSD_INNER), rows(CONV_CH), rows(LANES),
                 pl.BlockSpec((SSD_HEADS, IN_ROWS), lambda i: (0, i)),
                 rows(ATT_INNER), rows(ATT_INNER), rows(ATT_INNER))
    return pl.pallas_call(
        _inproj_body, grid=(T // IN_ROWS,),
        in_specs=[rows(D_MODEL)] + [full(a) for a in ins[1:]],
        out_specs=out_specs, out_shape=out_shape,
        compiler_params=_params("arbitrary"), name="inproj")(*ins)


def _ssd_body(z_ref, xbc_ref, dt_ref, dtt_ref, cw_ref, cb_ref, dtb_ref, dtbt_ref, alog_ref,
              alogt_ref, dskip_ref, ng_ref, tril_ref, expand_ref, y_ref, ext_ref, state_ref):
    L, P, N, E = SSD_CHUNK, SSD_HEAD_DIM, SSD_STATE, SSD_HEADS_PER_GROUP
    R = SSD_STEP_CHUNKS * L
    c = pl.program_id(1)

    @pl.when(c == 0)
    def _():
        state_ref[...] = jnp.zeros_like(state_ref)
        ext_ref[0:SUBLANES, :] = jnp.zeros((SUBLANES, CONV_CH), F32)

    ext_ref[SUBLANES:SUBLANES + R, :] = xbc_ref[0]
    tril = tril_ref[...]
    expand = expand_ref[...]
    row = lax.broadcasted_iota(jnp.int32, (L, L), 0)
    col = lax.broadcasted_iota(jnp.int32, (L, L), 1)
    causal = row >= col
    for sub in range(SSD_STEP_CHUNKS):
        _ssd_chunk(sub * L, z_ref, dt_ref, dtt_ref, cw_ref, cb_ref, dtb_ref, dtbt_ref, alog_ref,
                   alogt_ref, dskip_ref, ng_ref, tril, expand, causal, y_ref, ext_ref, state_ref)
    ext_ref[0:SUBLANES, :] = ext_ref[R:R + SUBLANES, :]


def _ssd_chunk(r0, z_ref, dt_ref, dtt_ref, cw_ref, cb_ref, dtb_ref, dtbt_ref, alog_ref, alogt_ref,
               dskip_ref, ng_ref, tril, expand, causal, y_ref, ext_ref, state_ref):
    L, P, N, E = SSD_CHUNK, SSD_HEAD_DIM, SSD_STATE, SSD_HEADS_PER_GROUP
    conv = jnp.zeros((L, CONV_CH), F32) + cb_ref[...]
    for j in range(SSD_CONV):
        off = r0 + SUBLANES - (SSD_CONV - 1) + j
        conv = conv + cw_ref[j:j + 1, :] * ext_ref[off:off + L, :]
    act = _silu(conv)
    xs = act[:, :SSD_INNER]
    b_in = act[:, SSD_INNER:SSD_INNER + SSD_GROUPS * N]
    c_out = act[:, SSD_INNER + SSD_GROUPS * N:]

    dt = _softplus(dt_ref[0, r0:r0 + L, :] + dtb_ref[...])
    a_dt = dt * (-jnp.exp(alog_ref[...]))
    a_cs = jnp.dot(tril, a_dt, precision=HIGHEST, preferred_element_type=F32)
    dt_t = _softplus(dtt_ref[:, r0:r0 + L] + dtbt_ref[...])
    a_dt_t = dt_t * (-jnp.exp(alogt_ref[...]))
    a_cs_t = lax.dot_general(a_dt_t, tril, (((1,), (1,)), ((), ())), precision=HIGHEST,
                             preferred_element_type=F32)
    last = a_cs[L - 1:L, :]
    def hdot(a):
        hi = a.astype(BF16)
        lo = (a - hi.astype(F32)).astype(BF16)
        return (jnp.dot(hi, expand, preferred_element_type=F32)
                + jnp.dot(lo, expand, preferred_element_type=F32))
    dt_x = hdot(dt)
    eacs_x = hdot(jnp.exp(a_cs))
    dec_x = hdot(jnp.exp(last - a_cs))
    cdec_x = hdot(jnp.broadcast_to(jnp.exp(last), (SUBLANES, LANES)))[0:1, :]
    xdt = xs * dt_x
    xdtd = (xdt * dec_x).astype(BF16)
    xdt_b = xdt.astype(BF16)

    y_parts = []
    for g in range(SSD_GROUPS):
        bg = b_in[:, g * N:(g + 1) * N].astype(BF16)
        cg = c_out[:, g * N:(g + 1) * N].astype(BF16)
        cb = lax.dot_general(cg, bg, (((1,), (1,)), ((), ())), preferred_element_type=F32)
        st = state_ref[g]
        gs = slice(g * E * P, (g + 1) * E * P)
        y_off = jnp.dot(cg, st.astype(BF16), preferred_element_type=F32) * eacs_x[:, gs]
        for e in range(E):
            h = g * E + e
            diff = a_cs[:, h:h + 1] - a_cs_t[h:h + 1, :]
            l_mat = jnp.exp(jnp.where(causal, diff, -jnp.inf))
            m = (cb * l_mat).astype(BF16)
            y_diag = jnp.dot(m, xdt_b[:, h * P:(h + 1) * P], preferred_element_type=F32)
            y_parts.append(y_diag + y_off[:, e * P:(e + 1) * P])
        upd = lax.dot_general(bg, xdtd[:, gs], (((0,), (0,)), ((), ())),
                              preferred_element_type=F32)
        state_ref[g] = cdec_x[:, gs] * st + upd
    y = jnp.concatenate(y_parts, axis=-1) + dskip_ref[...] * xs
    y = y * _silu(z_ref[0, r0:r0 + L, :])
    y = y * lax.rsqrt(jnp.mean(y * y, axis=-1, keepdims=True) + EPS) * ng_ref[...]
    y_ref[0, r0:r0 + L, :] = y


def _ssd(z, xbc, dt, dtt, conv_w, conv_b, dt_bias, a_log, d_skip, norm_g):
    B, S, _ = z.shape
    L = SSD_CHUNK
    R = SSD_STEP_CHUNKS * L
    nc = S // R
    pad_row = lambda a: jnp.pad(a.reshape(1, SSD_HEADS), ((0, 0), (0, LANES - SSD_HEADS)))
    tril = jnp.asarray(np.tril(np.ones((L, L), np.float32)))
    expand_np = np.zeros((LANES, SSD_INNER), np.float32)
    for h in range(SSD_HEADS):
        expand_np[h, h * SSD_HEAD_DIM:(h + 1) * SSD_HEAD_DIM] = 1.0
    ins = (z, xbc, dt, dtt, conv_w, conv_b.reshape(1, CONV_CH), pad_row(dt_bias),
           dt_bias.reshape(SSD_HEADS, 1), pad_row(a_log), a_log.reshape(SSD_HEADS, 1),
           jnp.repeat(d_skip, SSD_HEAD_DIM).reshape(1, SSD_INNER), norm_g.reshape(1, SSD_INNER),
           tril, jnp.asarray(expand_np, dtype=BF16))
    full = lambda a: pl.BlockSpec(a.shape, lambda b, c: (0,) * a.ndim)
    chunk = lambda n: pl.BlockSpec((1, R, n), lambda b, c: (b, c, 0))
    in_specs = [chunk(SSD_INNER), chunk(CONV_CH), chunk(LANES),
                pl.BlockSpec((SSD_HEADS, R), lambda b, c: (0, b * nc + c))]
    in_specs += [full(a) for a in ins[4:]]
    return pl.pallas_call(
        _ssd_body, grid=(B, nc), in_specs=in_specs, out_specs=chunk(SSD_INNER),
        out_shape=jax.ShapeDtypeStruct((B, S, SSD_INNER), F32),
        scratch_shapes=[pltpu.VMEM((R + SUBLANES, CONV_CH), F32),
                        pltpu.VMEM((SSD_GROUPS, SSD_STATE, SSD_HEADS_PER_GROUP * SSD_HEAD_DIM), F32)],
        compiler_params=_params("arbitrary", "arbitrary"), name="ssd")(*ins)


def _bucket_tile():
    bs = MOBA_BLOCK
    dist = np.arange(bs)[:, None] - np.arange(2 * bs)[None, :] + bs
    max_exact = REL_BUCKETS // 2
    d = np.maximum(dist, max_exact).astype(np.float32)
    large = max_exact + (np.log(d / np.float32(max_exact)) / np.float32(math.log(REL_MAX_DIST / max_exact))
                         * np.float32(REL_BUCKETS - max_exact)).astype(np.int32)
    large = np.minimum(large, REL_BUCKETS - 1)
    bucket = np.where(dist < max_exact, dist, large)
    return np.where(dist >= 0, bucket, -1).astype(np.int32)


def _bias_body(rb_ref, bucket_ref, o_ref):
    head = pl.program_id(0)
    bucket = bucket_ref[...]
    far = rb_ref[REL_BUCKETS - 1, head]
    bias = jnp.where(bucket < 0, MASK_VALUE, 0.0)
    for i in range(REL_BUCKETS - 1):
        bias = jnp.where(bucket == i, (rb_ref[i, head] - far) * LOG2E, bias)
    o_ref[0] = bias


def _bias_tiles(rel_bias):
    bs = MOBA_BLOCK
    bucket = jnp.asarray(_bucket_tile())
    return pl.pallas_call(
        _bias_body, grid=(ATT_HEADS,),
        in_specs=[pl.BlockSpec(memory_space=pltpu.SMEM), pl.BlockSpec((bs, 2 * bs), lambda h: (0, 0))],
        out_specs=pl.BlockSpec((1, bs, 2 * bs), lambda h: (h, 0, 0)),
        out_shape=jax.ShapeDtypeStruct((ATT_HEADS, bs, 2 * bs), F32),
        compiler_params=_params("arbitrary"), name="bias_tiles")(rel_bias, bucket)


def _attn_body(q_ref, k_ref, v_ref, qg_ref, kg_ref, bias_ref, o_ref,
               kaug_ref, vaug_ref, qaug_ref, sa_ref, sb_ref, m_ref, acc_ref):
    bs, hd = MOBA_BLOCK, ATT_HEAD_DIM
    nb = k_ref.shape[1] // bs
    jp = pl.program_id(2)
    nt_dims = (((1,), (1,)), ((), ()))

    @pl.when(jp == 0)
    def _():
        S = nb * bs
        pair = 2 * hd
        r = lax.broadcasted_iota(jnp.int32, (pair, pair), 0) // hd
        c = lax.broadcasted_iota(jnp.int32, (pair, pair), 1) // hd
        same_head = (r == c).astype(BF16)

        def head_norm(x, gain):
            ms = jnp.dot((x * x).astype(BF16), same_head, preferred_element_type=F32) * (1.0 / hd)
            return x * lax.rsqrt(ms + EPS) * gain

        kn = head_norm(k_ref[0], kg_ref[...])
        kmean = jnp.mean(kn.reshape(nb, bs, pair), axis=1)
        kn_b = kn.astype(BF16)
        v_b = v_ref[0].astype(BF16)
        lane = lax.broadcasted_iota(jnp.int32, (S, pair), 1)
        blk = lax.broadcasted_iota(jnp.int32, (S, pair), 0) // bs
        lane_m = lax.broadcasted_iota(jnp.int32, (nb, pair), 1)
        rowb = lax.broadcasted_iota(jnp.int32, (nb, bs), 0)
        lane_q = lax.broadcasted_iota(jnp.int32, (bs, pair), 1)
        kmean_b = []
        for h in range(2):
            own = (lane // hd) == h
            other0 = (1 - h) * hd
            kaug_ref[h] = jnp.where(own, kn_b, (lane - other0 == blk).astype(BF16))
            vaug_ref[h] = jnp.where(own, v_b, jnp.ones((S, pair), BF16))
            kmean_b.append(jnp.where((lane_m // hd) == h, kmean, 0.0).astype(BF16))
        for j in range(nb):
            rows = slice(j * bs, (j + 1) * bs)
            qn = head_norm(q_ref[0, rows, :], qg_ref[...])
            qn_b = qn.astype(BF16)
            qs = qn * (ATT_HEAD_DIM ** -0.5 * LOG2E)
            for h in range(2):
                if j <= MOBA_TOPK:
                    sel = rowb <= j
                else:
                    gate = lax.dot_general(kmean_b[h], qn_b, nt_dims,
                                           preferred_element_type=F32)
                    rank = jnp.zeros((nb, bs), jnp.int32)
                    for m in range(j):
                        gm = gate[m:m + 1, :]
                        beats = (gm > gate) | ((gm == gate) & (rowb > m))
                        rank = rank + beats.astype(jnp.int32)
                    sel = ((rowb < j) & (rank < MOBA_TOPK)) | (rowb == j)
                selmask = jnp.where(sel, 0.0, MASK_VALUE)
                pieces = [selmask, jnp.zeros((pair - nb, bs), F32)]
                if h == 0:
                    pieces = [jnp.zeros((hd, bs), F32), selmask, jnp.zeros((hd - nb, bs), F32)]
                mask_t = jnp.concatenate(pieces, axis=0).T
                qaug_ref[h, rows, :] = jnp.where((lane_q // hd) == h, qs, mask_t).astype(BF16)

    qt = 2 * bs
    q0 = pl.multiple_of(jp * qt, qt)
    all_rows = slice(0, qt)
    late_rows = slice(bs, qt)
    for h in range(2):
        m_ref[h] = jnp.full((qt, 2 * hd), -jnp.inf, F32)
        acc_ref[h] = jnp.zeros((qt, 2 * hd), F32)

    def scores(n, s_ref, rows=all_rows):
        start = pl.multiple_of(n * bs, bs)
        q_rows = pl.ds(q0 + rows.start, rows.stop - rows.start)
        for h in range(2):
            s_ref[h, rows, :] = lax.dot_general(qaug_ref[h, q_rows, :], kaug_ref[h, pl.ds(start, bs), :],
                                                nt_dims, preferred_element_type=F32)

    def update(n, s_ref, bias, rows=all_rows):
        start = pl.multiple_of(n * bs, bs)
        for r0 in range(rows.start, rows.stop, bs):
            sub = slice(r0, r0 + bs)
            for h in range(2):
                s = s_ref[h, sub, :]
                tile = None if bias is None else bias(h, r0 // bs)
                if tile is not None:
                    s = s + tile
                m_i = m_ref[h, sub, :]
                m_new = jnp.maximum(m_i, jnp.max(s, axis=-1, keepdims=True))
                alpha = jnp.exp2(m_i - m_new)
                p = jnp.exp2(s - jnp.concatenate([m_new, m_new], axis=-1)).astype(BF16)
                pv = jnp.dot(p, vaug_ref[h, pl.ds(start, bs), :], preferred_element_type=F32)
                acc_ref[h, sub, :] = alpha * acc_ref[h, sub, :] + pv
                m_ref[h, sub, :] = m_new

    def far_pair(i, _):
        n = 2 * i
        scores(n + 1, sb_ref)
        update(n, sa_ref, None)
        scores(n + 2, sa_ref)
        update(n + 1, sb_ref, None)
        return 0

    n_far = jnp.maximum(2 * jp - 1, 0)
    scores(0, sa_ref)
    lax.fori_loop(0, n_far >> 1, far_pair, 0)
    prev = lambda h: bias_ref[h, :, 0:bs]
    own = lambda h, qblk=1: bias_ref[h, :, bs:2 * bs]
    bias_p = lambda h, qblk: prev(h) if qblk == 0 else None
    bias_o = lambda h, qblk: own(h) if qblk == 0 else prev(h)
    blk_p = jnp.maximum(2 * jp - 1, 0)
    blk_o = 2 * jp
    blk_n = 2 * jp + 1

    @pl.when(jp > 0)
    def _():
        scores(blk_p, sb_ref)
        update(n_far - 1, sa_ref, None)
        scores(blk_o, sa_ref)
        update(blk_p, sb_ref, bias_p)
        scores(blk_n, sb_ref, late_rows)
        update(blk_o, sa_ref, bias_o)
        update(blk_n, sb_ref, own, late_rows)

    @pl.when(jp == 0)
    def _():
        scores(blk_n, sb_ref, late_rows)
        update(blk_o, sa_ref, bias_o)
        update(blk_n, sb_ref, own, late_rows)

    ratio = [acc_ref[h] / pltpu.roll(acc_ref[h], hd, 1) for h in range(2)]
    lane_o = lax.broadcasted_iota(jnp.int32, (qt, 2 * hd), 1)
    o_ref[0] = jnp.where(lane_o < hd, ratio[0], ratio[1])


def _attention(q, k, v, q_g, k_g, bias_tiles):
    B, S, _ = q.shape
    bs = MOBA_BLOCK
    nb = S // bs
    assert S % (2 * bs) == 0 and nb <= ATT_HEAD_DIM
    pair = 2 * ATT_HEAD_DIM
    return pl.pallas_call(
        _attn_body, grid=(B, ATT_HEADS // 2, nb // 2),
        in_specs=[
            pl.BlockSpec((1, S, pair), lambda b, hp, i: (b, 0, hp)),
            pl.BlockSpec((1, S, pair), lambda b, hp, i: (b, 0, hp)),
            pl.BlockSpec((1, S, pair), lambda b, hp, i: (b, 0, hp)),
            pl.BlockSpec((1, pair), lambda b, hp, i: (0, 0)),
            pl.BlockSpec((1, pair), lambda b, hp, i: (0, 0)),
            pl.BlockSpec((2, bs, 2 * bs), lambda b, hp, i: (hp, 0, 0)),
        ],
        out_specs=pl.BlockSpec((1, 2 * bs, pair), lambda b, hp, i: (b, i, hp)),
        out_shape=jax.ShapeDtypeStruct((B, S, ATT_INNER), F32),
        scratch_shapes=[pltpu.VMEM((2, S, pair), BF16), pltpu.VMEM((2, S, pair), BF16),
                        pltpu.VMEM((2, S, pair), BF16),
                        pltpu.VMEM((2, 2 * bs, bs), F32), pltpu.VMEM((2, 2 * bs, bs), F32),
                        pltpu.VMEM((2, 2 * bs, pair), F32),
                        pltpu.VMEM((2, 2 * bs, pair), F32)],
        compiler_params=_params("arbitrary", "arbitrary", "arbitrary"), name="moba")(
            q, k, v, jnp.tile(q_g, 2).reshape(1, pair), jnp.tile(k_g, 2).reshape(1, pair), bias_tiles)


def _outproj_body(h_ref, ys_ref, ya_ref, wa_ref, wb_ref, g_ref, wrh_ref, wrl_ref, br_ref, tri_ref,
                  hn_ref, xn_ref, route_ref, routet_ref, counts_ref, carry_ref):
    i = pl.program_id(0)

    @pl.when(i == 0)
    def _():
        carry_ref[...] = jnp.zeros_like(carry_ref)

    hn = (h_ref[...]
          + jnp.dot(ys_ref[...].astype(BF16), wa_ref[...], preferred_element_type=F32)
          + jnp.dot(ya_ref[...].astype(BF16), wb_ref[...], preferred_element_type=F32))
    hn_ref[...] = hn
    xn = hn * lax.rsqrt(jnp.mean(hn * hn, axis=-1, keepdims=True) + EPS) * g_ref[...]
    for c in range(SUBLANES):
        xn_ref[pl.ds(c, xn.shape[0], stride=SUBLANES), :] = xn[:, c * LANES:(c + 1) * LANES]
    x_hi = xn.astype(BF16)
    x_lo = (xn - x_hi.astype(F32)).astype(BF16)
    hi_terms = jnp.dot(x_hi, wrl_ref[...], preferred_element_type=F32)
    logits = (hi_terms[:, :LANES] + hi_terms[:, LANES:]
              + jnp.dot(x_lo, wrh_ref[...], preferred_element_type=F32)) + br_ref[...]
    rows = logits.shape[0]
    lane = lax.broadcasted_iota(jnp.int32, (rows, LANES), 1)
    first = lambda hit: jnp.min(jnp.where(hit, lane, LANES), axis=-1, keepdims=True)
    l1 = jnp.where(lane < MOE_GROUPS, logits, -jnp.inf)
    mx = jnp.max(l1, axis=-1, keepdims=True)
    gval = 1.0 / jnp.sum(jnp.exp(l1 - mx), axis=-1, keepdims=True)
    gidx = first(l1 == mx)
    e_lane = lane - ROUTER_LANE0
    in_grp = (e_lane >= 0) & (e_lane < MOE_EXPERTS) & ((e_lane >> 3) == gidx)
    l2 = jnp.where(in_grp, logits, -jnp.inf)
    m1 = jnp.max(l2, axis=-1, keepdims=True)
    i1 = first(l2 == m1)
    l2b = jnp.where(lane == i1, -jnp.inf, l2)
    m2 = jnp.max(l2b, axis=-1, keepdims=True)
    i2 = first(l2b == m2)
    r = jnp.exp(m2 - m1)
    g1 = gval / (1.0 + r)
    g2 = gval * r / (1.0 + r)
    oh1 = (lane == i1)
    oh2 = (lane == i2)
    both = jnp.where(oh1 | oh2, 1.0, 0.0)
    cx = jnp.dot(tri_ref[...], both.astype(BF16), preferred_element_type=F32) + carry_ref[...]
    rank1 = jnp.sum(jnp.where(oh1, cx, 0.0), axis=-1, keepdims=True)
    rank2 = jnp.sum(jnp.where(oh2, cx, 0.0), axis=-1, keepdims=True)
    carry_ref[...] = carry_ref[...] + jnp.sum(both, axis=0, keepdims=True)
    counts_ref[...] = carry_ref[...]
    cols = ((i1 - ROUTER_LANE0).astype(F32), (i2 - ROUTER_LANE0).astype(F32), g1, g2, rank1, rank2)
    route = jnp.zeros((rows, LANES), F32)
    for j, cval in enumerate(cols):
        route = jnp.where(lane == j, cval, route)
    route_ref[...] = route
    routet_ref[...] = route.T[0:SUBLANES, :]


def _outproj_route(h2, y_ssd, y_att, w_out, norm_g, w_r1, b_r1, w_r2, b_r2):
    T = h2.shape[0]
    R = OUT_ROWS
    wb = w_out.astype(BF16)
    pad = LANES - MOE_GROUPS - MOE_EXPERTS
    wr = jnp.pad(jnp.concatenate([w_r1, w_r2], axis=1), ((0, 0), (0, pad)))
    wr_hi = wr.astype(BF16)
    wr_lo = (wr - wr_hi.astype(F32)).astype(BF16)
    br = jnp.pad(jnp.concatenate([b_r1, b_r2]), (0, pad)).reshape(1, LANES)
    tri = jnp.asarray(np.tril(np.ones((R, R), np.float32), -1), dtype=BF16)
    ins = (h2, y_ssd, y_att, wb[:SSD_INNER], wb[SSD_INNER:], norm_g.reshape(1, D_MODEL),
           wr_hi, jnp.concatenate([wr_hi, wr_lo], axis=1), br, tri)
    full = lambda a: pl.BlockSpec(a.shape, lambda i: (0,) * a.ndim)
    rows = lambda n: pl.BlockSpec((R, n), lambda i: (i, 0))
    return pl.pallas_call(
        _outproj_body, grid=(T // R,),
        in_specs=[rows(D_MODEL), rows(SSD_INNER), rows(ATT_INNER)] + [full(a) for a in ins[3:]],
        out_specs=(rows(D_MODEL), pl.BlockSpec((R * SUBLANES, LANES), lambda i: (i, 0)), rows(LANES),
                   pl.BlockSpec((SUBLANES, R), lambda i: (0, i)), pl.BlockSpec((1, LANES), lambda i: (0, 0))),
        out_shape=(jax.ShapeDtypeStruct((T, D_MODEL), F32),
                   jax.ShapeDtypeStruct((T * SUBLANES, LANES), F32),
                   jax.ShapeDtypeStruct((T, LANES), F32), jax.ShapeDtypeStruct((SUBLANES, T), F32),
                   jax.ShapeDtypeStruct((1, LANES), F32)),
        scratch_shapes=[pltpu.VMEM((1, LANES), F32)],
        compiler_params=_params("arbitrary"), name="outproj_route")(*ins)


def _slots_body(routet_ref, pstart_ref, dest_ref):
    cols = routet_ref.shape[1]
    expert = lax.broadcasted_iota(jnp.int32, (MOE_EXPERTS, cols), 0).astype(F32)
    pstart = pstart_ref[...]
    for j in range(2):
        hit = expert == routet_ref[j:j + 1, :]
        base = jnp.sum(jnp.where(hit, pstart, 0.0), axis=0, keepdims=True)
        dest_ref[j:j + 1, :] = (base + routet_ref[4 + j:5 + j, :]).astype(jnp.int32)


def _slots(routet, pstarts):
    T = routet.shape[1]
    cols = min(T, 4096)
    assert T % cols == 0
    return pl.pallas_call(
        _slots_body, grid=(T // cols,),
        in_specs=[pl.BlockSpec((SUBLANES, cols), lambda i: (0, i)),
                  pl.BlockSpec((MOE_EXPERTS, 1), lambda i: (0, 0))],
        out_specs=pl.BlockSpec((2, cols), lambda i: (0, i)),
        out_shape=jax.ShapeDtypeStruct((2, T), jnp.int32),
        compiler_params=_params("arbitrary"), name="slots")(
            routet, pstarts.astype(F32).reshape(MOE_EXPERTS, 1))


def _tile_copy(src_ref, src_row, dst_ref, dst_row, sem):
    tile = lambda row: pl.ds(pl.multiple_of(row * SUBLANES, SUBLANES), SUBLANES)
    return pltpu.make_async_copy(src_ref.at[tile(src_row)], dst_ref.at[tile(dst_row)], sem)


def _expert_body(be_ref, nused_ref, st_ref, xn_ref, wg_ref, wu_ref, wd_ref, yb_ref,
                 xg_ref, x_ref, wgb_ref, wub_ref, wdb_ref, gsem):
    M = EXPERT_ROWS
    i = pl.program_id(0)
    n_used = nused_ref[0]
    used = i < n_used
    slot = i & 1
    new_expert = (i == 0) | (be_ref[i] != be_ref[jnp.maximum(i - 1, 0)])

    def gather(block, dst_slot):
        base = block * M
        for r in range(M):
            _tile_copy(xn_ref, st_ref[base + r], xg_ref.at[dst_slot], r,
                       gsem.at[dst_slot]).start(priority=r % 2)

    def wait_gather(src_slot):
        pltpu.make_async_copy(xg_ref.at[src_slot], xg_ref.at[src_slot], gsem.at[src_slot]).wait()

    def stage(src_slot):
        for c in range(SUBLANES):
            x_ref[:, c * LANES:(c + 1) * LANES] = (
                xg_ref[src_slot, pl.ds(c, M, stride=SUBLANES), :].astype(BF16))

    def compute():
        x = x_ref[...]
        gate = jnp.dot(x, wgb_ref[...], preferred_element_type=F32)
        up = jnp.dot(x, wub_ref[...], preferred_element_type=F32)
        hmid = (_silu(gate) * up).astype(BF16)
        y = jnp.dot(hmid, wdb_ref[...], preferred_element_type=F32)
        for c in range(SUBLANES):
            yb_ref[pl.ds(c, M, stride=SUBLANES), :] = y[:, c * LANES:(c + 1) * LANES]

    @pl.when(i == 0)
    def _():
        gather(0, 0)

    @pl.when(used & new_expert)
    def _():
        wgb_ref[...] = wg_ref[0, 0].astype(BF16)
        wub_ref[...] = wu_ref[0, 0].astype(BF16)
        wdb_ref[...] = wd_ref[0, 0].astype(BF16)

    @pl.when(used & (i + 1 < n_used))
    def _():
        wait_gather(slot)
        stage(slot)
        gather(i + 1, 1 - slot)
        compute()

    @pl.when(used & (i + 1 >= n_used))
    def _():
        wait_gather(slot)
        stage(slot)
        compute()

    @pl.when(jnp.logical_not(used))
    def _():
        yb_ref[...] = jnp.zeros_like(yb_ref)


def _slot_tokens_body(d1_ref, d2_ref, st_ref, zero_ref, sem):
    zero_ref[...] = jnp.zeros_like(zero_ref)
    clear = pltpu.make_async_copy(zero_ref, st_ref, sem)
    clear.start()
    clear.wait()

    def place(t, _):
        st_ref[d1_ref[t]] = t
        st_ref[d2_ref[t]] = t
        return 0

    lax.fori_loop(0, d1_ref.shape[0], place, 0, unroll=8)


def _slot_tokens(dest1, dest2, n_slots):
    grid_spec = pltpu.PrefetchScalarGridSpec(
        num_scalar_prefetch=2, grid=(1,), in_specs=[],
        out_specs=pl.BlockSpec(memory_space=pltpu.SMEM),
        scratch_shapes=[pltpu.VMEM((n_slots,), jnp.int32), pltpu.SemaphoreType.DMA(())])
    return pl.pallas_call(
        _slot_tokens_body, grid_spec=grid_spec,
        out_shape=jax.ShapeDtypeStruct((n_slots,), jnp.int32),
        compiler_params=_params("arbitrary"), name="slot_tokens")(dest1, dest2)


def _experts(xn, slot_tok, blk_exp, n_used, n_blk, layer, w_gate, w_up, w_down):
    M = EXPERT_ROWS
    tile = lambda index: pl.BlockSpec((M * SUBLANES, LANES), index)
    weight = lambda a: pl.BlockSpec((1, 1) + a.shape[2:], lambda i, be, nu, st: (layer, be[i], 0, 0))
    grid_spec = pltpu.PrefetchScalarGridSpec(
        num_scalar_prefetch=3, grid=(n_blk,),
        in_specs=[pl.BlockSpec(memory_space=pl.ANY), weight(w_gate), weight(w_up), weight(w_down)],
        out_specs=tile(lambda i, be, nu, st: (i, 0)),
        scratch_shapes=[pltpu.VMEM((2, M * SUBLANES, LANES), F32),
                        pltpu.VMEM((M, D_MODEL), BF16), pltpu.VMEM((D_MODEL, MOE_HIDDEN), BF16),
                        pltpu.VMEM((D_MODEL, MOE_HIDDEN), BF16), pltpu.VMEM((MOE_HIDDEN, D_MODEL), BF16),
                        pltpu.SemaphoreType.DMA((2,))])
    return pl.pallas_call(
        _expert_body, grid_spec=grid_spec,
        out_shape=jax.ShapeDtypeStruct((n_blk * M * SUBLANES, LANES), F32),
        compiler_params=_params("arbitrary"), name="experts")(
            blk_exp, n_used, slot_tok, xn, w_gate, w_up, w_down)


def _combine_body(d1_ref, d2_ref, h_ref, route_ref, yb_ref, o_ref, buf_ref, sem):
    i = pl.program_id(0)
    n = pl.num_programs(0)

    def gather(step, slot):
        base = step * MOVE_ROWS

        def body(t, _):
            _tile_copy(yb_ref, d1_ref[base + t], buf_ref.at[slot, 0], t, sem.at[slot]).start()
            _tile_copy(yb_ref, d2_ref[base + t], buf_ref.at[slot, 1], t, sem.at[slot]).start(priority=1)
            return 0

        lax.fori_loop(0, MOVE_ROWS, body, 0)

    @pl.when(i == 0)
    def _():
        gather(0, 0)

    @pl.when(i + 1 < n)
    def _():
        gather(i + 1, (i + 1) & 1)

    slot = i & 1
    pltpu.make_async_copy(buf_ref.at[slot], buf_ref.at[slot], sem.at[slot]).wait()
    route = route_ref[...]
    g1 = route[:, 2:3]
    g2 = route[:, 3:4]
    for c in range(SUBLANES):
        cs = slice(c * LANES, (c + 1) * LANES)
        rows = pl.ds(c, MOVE_ROWS, stride=SUBLANES)
        o_ref[:, cs] = h_ref[:, cs] + (buf_ref[slot, 0, rows, :] * g1 + buf_ref[slot, 1, rows, :] * g2)


def _combine(h2, route, yb, dest1, dest2):
    T = h2.shape[0]
    R = MOVE_ROWS
    grid_spec = pltpu.PrefetchScalarGridSpec(
        num_scalar_prefetch=2, grid=(T // R,),
        in_specs=[pl.BlockSpec((R, D_MODEL), lambda i, d1, d2: (i, 0)),
                  pl.BlockSpec((R, LANES), lambda i, d1, d2: (i, 0)), pl.BlockSpec(memory_space=pl.ANY)],
        out_specs=pl.BlockSpec((R, D_MODEL), lambda i, d1, d2: (i, 0)),
        scratch_shapes=[pltpu.VMEM((2, 2, R * SUBLANES, LANES), F32), pltpu.SemaphoreType.DMA((2,))])
    return pl.pallas_call(
        _combine_body, grid_spec=grid_spec, out_shape=jax.ShapeDtypeStruct((T, D_MODEL), F32),
        compiler_params=_params("arbitrary"), name="combine")(dest1, dest2, h2, route, yb)


def _moe(h2, xn, route, routet, counts, layer, w_gate, w_up, w_down):
    T = h2.shape[0]
    M = EXPERT_ROWS
    n_blk = (2 * T) // M + MOE_EXPERTS
    cnt = counts[0, ROUTER_LANE0:ROUTER_LANE0 + MOE_EXPERTS].astype(jnp.int32)
    padded = (cnt + M - 1) // M * M
    ends = jnp.cumsum(padded)
    pstarts = ends - padded
    n_used = (ends[-1] // M).astype(jnp.int32)
    blk = jnp.minimum(jnp.arange(n_blk, dtype=jnp.int32), n_used - 1) * M
    blk_exp = jnp.sum((ends[None, :] <= blk[:, None]).astype(jnp.int32), axis=1)
    blk_exp = jnp.minimum(blk_exp, MOE_EXPERTS - 1)
    dest = _slots(routet, pstarts)
    slot_tok = _slot_tokens(dest[0], dest[1], n_blk * M)
    yb = _experts(xn, slot_tok, blk_exp, n_used.reshape(1), n_blk, layer, w_gate, w_up, w_down)
    return _combine(h2, route, yb, dest[0], dest[1])


def kernel(x, rel_bias, norm1_g, w_in, conv_w, conv_b, dt_bias, a_log, d_skip, ssd_norm_g,
           q_norm_g, k_norm_g, w_out, norm2_g, w_r1, b_r1, w_r2, b_r2, w_gate, w_up, w_down):
    B, S, D = x.shape
    T = B * S
    h2 = x.reshape(T, D)
    depth = w_in.shape[0]
    bias_tiles = _bias_tiles(rel_bias)
    for l in range(depth):
        z, xbc, dt, dtt, q, k, v = _inproj(h2, norm1_g[l], w_in[l])
        r3 = lambda a: a.reshape(B, S, a.shape[-1])
        y_ssd = _ssd(r3(z), r3(xbc), r3(dt), dtt, conv_w[l], conv_b[l], dt_bias[l], a_log[l],
                     d_skip[l], ssd_norm_g[l])
        y_att = _attention(r3(q), r3(k), r3(v), q_norm_g[l], k_norm_g[l], bias_tiles)
        h2, xn, route, routet, counts = _outproj_route(
            h2, y_ssd.reshape(T, SSD_INNER), y_att.reshape(T, ATT_INNER), w_out[l], norm2_g[l],
            w_r1[l], b_r1[l], w_r2[l], b_r2[l])
        h2 = _moe(h2, xn, route, routet, counts, l, w_gate, w_up, w_down)
    return h2.reshape(B, S, D)
```

```python
import functools
import math

import numpy as np
import jax
import jax.numpy as jnp
from jax import lax
from jax.experimental import pallas as pl
from jax.experimental.pallas import tpu as pltpu

F32 = jnp.float32
BF16 = jnp.bfloat16
HIGHEST = lax.Precision.HIGHEST

LANES = 128
SUBLANES = 8

D_MODEL = 1024
SSD_HEADS = 8
SSD_HEAD_DIM = 64
SSD_INNER = SSD_HEADS * SSD_HEAD_DIM
SSD_GROUPS = 2
SSD_HEADS_PER_GROUP = SSD_HEADS // SSD_GROUPS
SSD_STATE = 64
SSD_CONV = 4
SSD_CHUNK = 128
CONV_CH = SSD_INNER + 2 * SSD_GROUPS * SSD_STATE
ATT_HEADS = 8
ATT_HEAD_DIM = 64
ATT_INNER = ATT_HEADS * ATT_HEAD_DIM
MOBA_BLOCK = 256
MOBA_TOPK = 3
REL_BUCKETS = 32
REL_MAX_DIST = 128
MOE_GROUPS = 4
MOE_EXPERTS_PER_GROUP = 8
MOE_EXPERTS = MOE_GROUPS * MOE_EXPERTS_PER_GROUP
MOE_HIDDEN = 512
EPS = 1e-6

MASK_VALUE = -1e30
LOG2E = math.log2(math.e)
IN_ROWS = 512
SSD_STEP_CHUNKS = 2
OUT_ROWS = 512
EXPERT_ROWS = 512
MOVE_ROWS = 256
VMEM_LIMIT = 56 * 1024 * 1024
ROUTER_LANE0 = MOE_GROUPS


def _silu(x):
    return x * (1.0 / (1.0 + jnp.exp(-x)))


def _softplus(x):
    return jnp.maximum(x, 0.0) + jnp.log(1.0 + jnp.exp(-jnp.abs(x)))


def _params(*sem):
    return pltpu.CompilerParams(dimension_semantics=sem, vmem_limit_bytes=VMEM_LIMIT)


def _inproj_body(h_ref, g_ref, wz_ref, wx_ref, wdt_ref, wq_ref, wk_ref, wv_ref,
                 z_ref, xbc_ref, dt_ref, dtt_ref, q_ref, k_ref, v_ref):
    x = h_ref[...]
    xn = x * lax.rsqrt(jnp.mean(x * x, axis=-1, keepdims=True) + EPS) * g_ref[...]
    xb = xn.astype(BF16)
    z_ref[...] = jnp.dot(xb, wz_ref[...], preferred_element_type=F32)
    xbc_ref[...] = jnp.dot(xb, wx_ref[...], preferred_element_type=F32)
    dt = jnp.dot(xb, wdt_ref[...], preferred_element_type=F32)
    dt_ref[...] = dt
    dtt_ref[...] = dt.T[0:SSD_HEADS, :]
    q_ref[...] = jnp.dot(xb, wq_ref[...], preferred_element_type=F32)
    k_ref[...] = jnp.dot(xb, wk_ref[...], preferred_element_type=F32)
    v_ref[...] = jnp.dot(xb, wv_ref[...], preferred_element_type=F32)


def _inproj(h2, g, w_in):
    T = h2.shape[0]
    c0, c1, c2 = SSD_INNER, SSD_INNER + CONV_CH, SSD_INNER + CONV_CH + SSD_HEADS
    wb = w_in.astype(BF16)
    wz, wx, wdt = wb[:, :c0], wb[:, c0:c1], wb[:, c1:c2]
    wq, wk, wv = (wb[:, c2 + i * ATT_INNER:c2 + (i + 1) * ATT_INNER] for i in range(3))
    wdt_pad = jnp.pad(wdt, ((0, 0), (0, LANES - SSD_HEADS)))
    full = lambda a: pl.BlockSpec(a.shape, lambda i: (0,) * a.ndim)
    rows = lambda n: pl.BlockSpec((IN_ROWS, n), lambda i: (i, 0))
    ins = (h2, g.reshape(1, D_MODEL), wz, wx, wdt_pad, wq, wk, wv)
    out_shape = (
        jax.ShapeDtypeStruct((T, SSD_INNER), F32), jax.ShapeDtypeStruct((T, CONV_CH), F32),
        jax.ShapeDtypeStruct((T, LANES), F32), jax.ShapeDtypeStruct((SSD_HEADS, T), F32),
        jax.ShapeDtypeStruct((T, ATT_INNER), F32), jax.ShapeDtypeStruct((T, ATT_INNER), F32),
        jax.ShapeDtypeStruct((T, ATT_INNER), F32))
    out_specs = (rows(SSD_INNER), rows(CONV_CH), rows(LANES),
                 pl.BlockSpec((SSD_HEADS, IN_ROWS), lambda i: (0, i)),
                 rows(ATT_INNER), rows(ATT_INNER), rows(ATT_INNER))
    return pl.pallas_call(
        _inproj_body, grid=(T // IN_ROWS,),
        in_specs=[rows(D_MODEL)] + [full(a) for a in ins[1:]],
        out_specs=out_specs, out_shape=out_shape,
        compiler_params=_params("arbitrary"), name="inproj")(*ins)


def _ssd_body(z_ref, xbc_ref, dt_ref, dtt_ref, cw_ref, cb_ref, dtb_ref, dtbt_ref, alog_ref,
              alogt_ref, dskip_ref, ng_ref, tril_ref, expand_ref, y_ref, ext_ref, state_ref):
    L, P, N, E = SSD_CHUNK, SSD_HEAD_DIM, SSD_STATE, SSD_HEADS_PER_GROUP
    R = SSD_STEP_CHUNKS * L
    c = pl.program_id(1)

    @pl.when(c == 0)
    def _():
        state_ref[...] = jnp.zeros_like(state_ref)
        ext_ref[0:SUBLANES, :] = jnp.zeros((SUBLANES, CONV_CH), F32)

    ext_ref[SUBLANES:SUBLANES + R, :] = xbc_ref[0]
    tril = tril_ref[...]
    expand = expand_ref[...]
    row = lax.broadcasted_iota(jnp.int32, (L, L), 0)
    col = lax.broadcasted_iota(jnp.int32, (L, L), 1)
    causal = row >= col
    for sub in range(SSD_STEP_CHUNKS):
        _ssd_chunk(sub * L, z_ref, dt_ref, dtt_ref, cw_ref, cb_ref, dtb_ref, dtbt_ref, alog_ref,
                   alogt_ref, dskip_ref, ng_ref, tril, expand, causal, y_ref, ext_ref, state_ref)
    ext_ref[0:SUBLANES, :] = ext_ref[R:R + SUBLANES, :]


def _ssd_chunk(r0, z_ref, dt_ref, dtt_ref, cw_ref, cb_ref, dtb_ref, dtbt_ref, alog_ref, alogt_ref,
               dskip_ref, ng_ref, tril, expand, causal, y_ref, ext_ref, state_ref):
    L, P, N, E = SSD_CHUNK, SSD_HEAD_DIM, SSD_STATE, SSD_HEADS_PER_GROUP
    conv = jnp.zeros((L, CONV_CH), F32) + cb_ref[...]
    for j in range(SSD_CONV):
        off = r0 + SUBLANES - (SSD_CONV - 1) + j
        conv = conv + cw_ref[j:j + 1, :] * ext_ref[off:off + L, :]
    act = _silu(conv)
    xs = act[:, :SSD_INNER]
    b_in = act[:, SSD_INNER:SSD_INNER + SSD_GROUPS * N]
    c_out = act[:, SSD_INNER + SSD_GROUPS * N:]

    dt = _softplus(dt_ref[0, r0:r0 + L, :] + dtb_ref[...])
    a_dt = dt * (-jnp.exp(alog_ref[...]))
    a_cs = jnp.dot(tril, a_dt, precision=HIGHEST, preferred_element_type=F32)
    dt_t = _softplus(dtt_ref[:, r0:r0 + L] + dtbt_ref[...])
    a_dt_t = dt_t * (-jnp.exp(alogt_ref[...]))
    a_cs_t = lax.dot_general(a_dt_t, tril, (((1,), (1,)), ((), ())), precision=HIGHEST,
                             preferred_element_type=F32)
    last = a_cs[L - 1:L, :]
    def hdot(a):
        hi = a.astype(BF16)
        lo = (a - hi.astype(F32)).astype(BF16)
        return (jnp.dot(hi, expand, preferred_element_type=F32)
                + jnp.dot(lo, expand, preferred_element_type=F32))
    dt_x = hdot(dt)
    eacs_x = hdot(jnp.exp(a_cs))
    dec_x = hdot(jnp.exp(last - a_cs))
    cdec_x = hdot(jnp.broadcast_to(jnp.exp(last), (SUBLANES, LANES)))[0:1, :]
    xdt = xs * dt_x
    xdtd = (xdt * dec_x).astype(BF16)
    xdt_b = xdt.astype(BF16)

    y_parts = []
    for g in range(SSD_GROUPS):
        bg = b_in[:, g * N:(g + 1) * N].astype(BF16)
        cg = c_out[:, g * N:(g + 1) * N].astype(BF16)
        cb = lax.dot_general(cg, bg, (((1,), (1,)), ((), ())), preferred_element_type=F32)
        st = state_ref[g]
        gs = slice(g * E * P, (g + 1) * E * P)
        y_off = jnp.dot(cg, st.astype(BF16), preferred_element_type=F32) * eacs_x[:, gs]
        for e in range(E):
            h = g * E + e
            diff = a_cs[:, h:h + 1] - a_cs_t[h:h + 1, :]
            l_mat = jnp.exp(jnp.where(causal, diff, -jnp.inf))
            m = (cb * l_mat).astype(BF16)
            y_diag = jnp.dot(m, xdt_b[:, h * P:(h + 1) * P], preferred_element_type=F32)
            y_parts.append(y_diag + y_off[:, e * P:(e + 1) * P])
        upd = lax.dot_general(bg, xdtd[:, gs], (((0,), (0,)), ((), ())),
                              preferred_element_type=F32)
        state_ref[g] = cdec_x[:, gs] * st + upd
    y = jnp.concatenate(y_parts, axis=-1) + dskip_ref[...] * xs
    y = y * _silu(z_ref[0, r0:r0 + L, :])
    y = y * lax.rsqrt(jnp.mean(y * y, axis=-1, keepdims=True) + EPS) * ng_ref[...]
    y_ref[0, r0:r0 + L, :] = y


def _ssd(z, xbc, dt, dtt, conv_w, conv_b, dt_bias, a_log, d_skip, norm_g):
    B, S, _ = z.shape
    L = SSD_CHUNK
    R = SSD_STEP_CHUNKS * L
    nc = S // R
    pad_row = lambda a: jnp.pad(a.reshape(1, SSD_HEADS), ((0, 0), (0, LANES - SSD_HEADS)))
    tril = jnp.asarray(np.tril(np.ones((L, L), np.float32)))
    expand_np = np.zeros((LANES, SSD_INNER), np.float32)
    for h in range(SSD_HEADS):
        expand_np[h, h * SSD_HEAD_DIM:(h + 1) * SSD_HEAD_DIM] = 1.0
    ins = (z, xbc, dt, dtt, conv_w, conv_b.reshape(1, CONV_CH), pad_row(dt_bias),
           dt_bias.reshape(SSD_HEADS, 1), pad_row(a_log), a_log.reshape(SSD_HEADS, 1),
           jnp.repeat(d_skip, SSD_HEAD_DIM).reshape(1, SSD_INNER), norm_g.reshape(1, SSD_INNER),
           tril, jnp.asarray(expand_np, dtype=BF16))
    full = lambda a: pl.BlockSpec(a.shape, lambda b, c: (0,) * a.ndim)
    chunk = lambda n: pl.BlockSpec((1, R, n), lambda b, c: (b, c, 0))
    in_specs = [chunk(SSD_INNER), chunk(CONV_CH), chunk(LANES),
                pl.BlockSpec((SSD_HEADS, R), lambda b, c: (0, b * nc + c))]
    in_specs += [full(a) for a in ins[4:]]
    return pl.pallas_call(
        _ssd_body, grid=(B, nc), in_specs=in_specs, out_specs=chunk(SSD_INNER),
        out_shape=jax.ShapeDtypeStruct((B, S, SSD_INNER), F32),
        scratch_shapes=[pltpu.VMEM((R + SUBLANES, CONV_CH), F32),
                        pltpu.VMEM((SSD_GROUPS, SSD_STATE, SSD_HEADS_PER_GROUP * SSD_HEAD_DIM), F32)],
        compiler_params=_params("arbitrary", "arbitrary"), name="ssd")(*ins)


def _bucket_tile():
    bs = MOBA_BLOCK
    dist = np.arange(bs)[:, None] - np.arange(2 * bs)[None, :] + bs
    max_exact = REL_BUCKETS // 2
    d = np.maximum(dist, max_exact).astype(np.float32)
    large = max_exact + (np.log(d / np.float32(max_exact)) / np.float32(math.log(REL_MAX_DIST / max_exact))
                         * np.float32(REL_BUCKETS - max_exact)).astype(np.int32)
    large = np.minimum(large, REL_BUCKETS - 1)
    bucket = np.where(dist < max_exact, dist, large)
    return np.where(dist >= 0, bucket, -1).astype(np.int32)


def _bias_body(rb_ref, bucket_ref, o_ref):
    head = pl.program_id(0)
    bucket = bucket_ref[...]
    far = rb_ref[REL_BUCKETS - 1, head]
    bias = jnp.where(bucket < 0, MASK_VALUE, 0.0)
    for i in range(REL_BUCKETS - 1):
        bias = jnp.where(bucket == i, (rb_ref[i, head] - far) * LOG2E, bias)
    o_ref[0] = bias


def _bias_tiles(rel_bias):
    bs = MOBA_BLOCK
    bucket = jnp.asarray(_bucket_tile())
    return pl.pallas_call(
        _bias_body, grid=(ATT_HEADS,),
        in_specs=[pl.BlockSpec(memory_space=pltpu.SMEM), pl.BlockSpec((bs, 2 * bs), lambda h: (0, 0))],
        out_specs=pl.BlockSpec((1, bs, 2 * bs), lambda h: (h, 0, 0)),
        out_shape=jax.ShapeDtypeStruct((ATT_HEADS, bs, 2 * bs), F32),
        compiler_params=_params("arbitrary"), name="bias_tiles")(rel_bias, bucket)


def _attn_body(q_ref, k_ref, v_ref, qg_ref, kg_ref, bias_ref, o_ref,
               kaug_ref, vaug_ref, qaug_ref, sa_ref, sb_ref, m_ref, acc_ref):
    bs, hd = MOBA_BLOCK, ATT_HEAD_DIM
    nb = k_ref.shape[1] // bs
    jp = pl.program_id(2)
    nt_dims = (((1,), (1,)), ((), ()))

    @pl.when(jp == 0)
    def _():
        S = nb * bs
        pair = 2 * hd
        r = lax.broadcasted_iota(jnp.int32, (pair, pair), 0) // hd
        c = lax.broadcasted_iota(jnp.int32, (pair, pair), 1) // hd
        same_head = (r == c).astype(BF16)

        def head_norm(x, gain):
            ms = jnp.dot((x * x).astype(BF16), same_head, preferred_element_type=F32) * (1.0 / hd)
            return x * lax.rsqrt(ms + EPS) * gain

        kn = head_norm(k_ref[0], kg_ref[...])
        kmean = jnp.mean(kn.reshape(nb, bs, pair), axis=1)
        kn_b = kn.astype(BF16)
        v_b = v_ref[0].astype(BF16)
        lane = lax.broadcasted_iota(jnp.int32, (S, pair), 1)
        blk = lax.broadcasted_iota(jnp.int32, (S, pair), 0) // bs
        lane_m = lax.broadcasted_iota(jnp.int32, (nb, pair), 1)
        rowb = lax.broadcasted_iota(jnp.int32, (nb, bs), 0)
        lane_q = lax.broadcasted_iota(jnp.int32, (bs, pair), 1)
        kmean_b = []
        for h in range(2):
            own = (lane // hd) == h
            other0 = (1 - h) * hd
            kaug_ref[h] = jnp.where(own, kn_b, (lane - other0 == blk).astype(BF16))
            vaug_ref[h] = jnp.where(own, v_b, jnp.ones((S, pair), BF16))
            kmean_b.append(jnp.where((lane_m // hd) == h, kmean, 0.0).astype(BF16))
        for j in range(nb):
            rows = slice(j * bs, (j + 1) * bs)
            qn = head_norm(q_ref[0, rows, :], qg_ref[...])
            qn_b = qn.astype(BF16)
            qs = qn * (ATT_HEAD_DIM ** -0.5 * LOG2E)
            for h in range(2):
                if j <= MOBA_TOPK:
                    sel = rowb <= j
                else:
                    gate = lax.dot_general(kmean_b[h], qn_b, nt_dims,
                                           preferred_element_type=F32)
                    rank = jnp.zeros((nb, bs), jnp.int32)
                    for m in range(j):
                        gm = gate[m:m + 1, :]
                        beats = (gm > gate) | ((gm == gate) & (rowb > m))
                        rank = rank + beats.astype(jnp.int32)
                    sel = ((rowb < j) & (rank < MOBA_TOPK)) | (rowb == j)
                selmask = jnp.where(sel, 0.0, MASK_VALUE)
                pieces = [selmask, jnp.zeros((pair - nb, bs), F32)]
                if h == 0:
                    pieces = [jnp.zeros((hd, bs), F32), selmask, jnp.zeros((hd - nb, bs), F32)]
                mask_t = jnp.concatenate(pieces, axis=0).T
                qaug_ref[h, rows, :] = jnp.where((lane_q // hd) == h, qs, mask_t).astype(BF16)

    qt = 2 * bs
    q0 = pl.multiple_of(jp * qt, qt)
    all_rows = slice(0, qt)
    late_rows = slice(bs, qt)
    for h in range(2):
        m_ref[h] = jnp.full((qt, 2 * hd), -jnp.inf, F32)
        acc_ref[h] = jnp.zeros((qt, 2 * hd), F32)

    def scores(n, s_ref, rows=all_rows):
        start = pl.multiple_of(n * bs, bs)
        q_rows = pl.ds(q0 + rows.start, rows.stop - rows.start)
        for h in range(2):
            s_ref[h, rows, :] = lax.dot_general(qaug_ref[h, q_rows, :], kaug_ref[h, pl.ds(start, bs), :],
                                                nt_dims, preferred_element_type=F32)

    def update(n, s_ref, bias, rows=all_rows):
        start = pl.multiple_of(n * bs, bs)
        for r0 in range(rows.start, rows.stop, bs):
            sub = slice(r0, r0 + bs)
            for h in range(2):
                s = s_ref[h, sub, :]
                tile = None if bias is None else bias(h, r0 // bs)
                if tile is not None:
                    s = s + tile
                m_i = m_ref[h, sub, :]
                m_new = jnp.maximum(m_i, jnp.max(s, axis=-1, keepdims=True))
                alpha = jnp.exp2(m_i - m_new)
                p = jnp.exp2((s - jnp.concatenate([m_new, m_new], axis=-1)).astype(BF16))
                pv = jnp.dot(p, vaug_ref[h, pl.ds(start, bs), :], preferred_element_type=F32)
                acc_ref[h, sub, :] = alpha * acc_ref[h, sub, :] + pv
                m_ref[h, sub, :] = m_new

    def far_pair(i, _):
        n = 2 * i
        scores(n + 1, sb_ref)
        update(n, sa_ref, None)
        scores(n + 2, sa_ref)
        update(n + 1, sb_ref, None)
        return 0

    n_far = jnp.maximum(2 * jp - 1, 0)
    scores(0, sa_ref)
    lax.fori_loop(0, n_far >> 1, far_pair, 0)
    prev = lambda h: bias_ref[h, :, 0:bs]
    own = lambda h, qblk=1: bias_ref[h, :, bs:2 * bs]
    bias_p = lambda h, qblk: prev(h) if qblk == 0 else None
    bias_o = lambda h, qblk: own(h) if qblk == 0 else prev(h)
    blk_p = jnp.maximum(2 * jp - 1, 0)
    blk_o = 2 * jp
    blk_n = 2 * jp + 1

    @pl.when(jp > 0)
    def _():
        scores(blk_p, sb_ref)
        update(n_far - 1, sa_ref, None)
        scores(blk_o, sa_ref)
        update(blk_p, sb_ref, bias_p)
        scores(blk_n, sb_ref, late_rows)
        update(blk_o, sa_ref, bias_o)
        update(blk_n, sb_ref, own, late_rows)

    @pl.when(jp == 0)
    def _():
        scores(blk_n, sb_ref, late_rows)
        update(blk_o, sa_ref, bias_o)
        update(blk_n, sb_ref, own, late_rows)

    ratio = [acc_ref[h] / pltpu.roll(acc_ref[h], hd, 1) for h in range(2)]
    lane_o = lax.broadcasted_iota(jnp.int32, (qt, 2 * hd), 1)
    o_ref[0] = jnp.where(lane_o < hd, ratio[0], ratio[1])


def _attention(q, k, v, q_g, k_g, bias_tiles):
    B, S, _ = q.shape
    bs = MOBA_BLOCK
    nb = S // bs
    assert S % (2 * bs) == 0 and nb <= ATT_HEAD_DIM
    pair = 2 * ATT_HEAD_DIM
    return pl.pallas_call(
        _attn_body, grid=(B, ATT_HEADS // 2, nb // 2),
        in_specs=[
            pl.BlockSpec((1, S, pair), lambda b, hp, i: (b, 0, hp)),
            pl.BlockSpec((1, S, pair), lambda b, hp, i: (b, 0, hp)),
            pl.BlockSpec((1, S, pair), lambda b, hp, i: (b, 0, hp)),
            pl.BlockSpec((1, pair), lambda b, hp, i: (0, 0)),
            pl.BlockSpec((1, pair), lambda b, hp, i: (0, 0)),
            pl.BlockSpec((2, bs, 2 * bs), lambda b, hp, i: (hp, 0, 0)),
        ],
        out_specs=pl.BlockSpec((1, 2 * bs, pair), lambda b, hp, i: (b, i, hp)),
        out_shape=jax.ShapeDtypeStruct((B, S, ATT_INNER), F32),
        scratch_shapes=[pltpu.VMEM((2, S, pair), BF16), pltpu.VMEM((2, S, pair), BF16),
                        pltpu.VMEM((2, S, pair), BF16),
                        pltpu.VMEM((2, 2 * bs, bs), F32), pltpu.VMEM((2, 2 * bs, bs), F32),
                        pltpu.VMEM((2, 2 * bs, pair), F32),
                        pltpu.VMEM((2, 2 * bs, pair), F32)],
        compiler_params=_params("arbitrary", "arbitrary", "arbitrary"), name="moba")(
            q, k, v, jnp.tile(q_g, 2).reshape(1, pair), jnp.tile(k_g, 2).reshape(1, pair), bias_tiles)


def _outproj_body(h_ref, ys_ref, ya_ref, wa_ref, wb_ref, g_ref, wrh_ref, wrl_ref, br_ref, tri_ref,
                  hn_ref, xn_ref, route_ref, routet_ref, counts_ref, carry_ref):
    i = pl.program_id(0)

    @pl.when(i == 0)
    def _():
        carry_ref[...] = jnp.zeros_like(carry_ref)

    hn = (h_ref[...]
          + jnp.dot(ys_ref[...].astype(BF16), wa_ref[...], preferred_element_type=F32)
          + jnp.dot(ya_ref[...].astype(BF16), wb_ref[...], preferred_element_type=F32))
    hn_ref[...] = hn
    xn = hn * lax.rsqrt(jnp.mean(hn * hn, axis=-1, keepdims=True) + EPS) * g_ref[...]
    for c in range(SUBLANES):
        xn_ref[pl.ds(c, xn.shape[0], stride=SUBLANES), :] = xn[:, c * LANES:(c + 1) * LANES]
    x_hi = xn.astype(BF16)
    x_lo = (xn - x_hi.astype(F32)).astype(BF16)
    hi_terms = jnp.dot(x_hi, wrl_ref[...], preferred_element_type=F32)
    logits = (hi_terms[:, :LANES] + hi_terms[:, LANES:]
              + jnp.dot(x_lo, wrh_ref[...], preferred_element_type=F32)) + br_ref[...]
    rows = logits.shape[0]
    lane = lax.broadcasted_iota(jnp.int32, (rows, LANES), 1)
    first = lambda hit: jnp.min(jnp.where(hit, lane, LANES), axis=-1, keepdims=True)
    l1 = jnp.where(lane < MOE_GROUPS, logits, -jnp.inf)
    mx = jnp.max(l1, axis=-1, keepdims=True)
    gval = 1.0 / jnp.sum(jnp.exp(l1 - mx), axis=-1, keepdims=True)
    gidx = first(l1 == mx)
    e_lane = lane - ROUTER_LANE0
    in_grp = (e_lane >= 0) & (e_lane < MOE_EXPERTS) & ((e_lane >> 3) == gidx)
    l2 = jnp.where(in_grp, logits, -jnp.inf)
    m1 = jnp.max(l2, axis=-1, keepdims=True)
    i1 = first(l2 == m1)
    l2b = jnp.where(lane == i1, -jnp.inf, l2)
    m2 = jnp.max(l2b, axis=-1, keepdims=True)
    i2 = first(l2b == m2)
    r = jnp.exp(m2 - m1)
    g1 = gval / (1.0 + r)
    g2 = gval * r / (1.0 + r)
    oh1 = (lane == i1)
    oh2 = (lane == i2)
    both = jnp.where(oh1 | oh2, 1.0, 0.0)
    cx = jnp.dot(tri_ref[...], both.astype(BF16), preferred_element_type=F32) + carry_ref[...]
    rank1 = jnp.sum(jnp.where(oh1, cx, 0.0), axis=-1, keepdims=True)
    rank2 = jnp.sum(jnp.where(oh2, cx, 0.0), axis=-1, keepdims=True)
    carry_ref[...] = carry_ref[...] + jnp.sum(both, axis=0, keepdims=True)
    counts_ref[...] = carry_ref[...]
    cols = ((i1 - ROUTER_LANE0).astype(F32), (i2 - ROUTER_LANE0).astype(F32), g1, g2, rank1, rank2)
    route = jnp.zeros((rows, LANES), F32)
    for j, cval in enumerate(cols):
        route = jnp.where(lane == j, cval, route)
    route_ref[...] = route
    routet_ref[...] = route.T[0:SUBLANES, :]


def _outproj_route(h2, y_ssd, y_att, w_out, norm_g, w_r1, b_r1, w_r2, b_r2):
    T = h2.shape[0]
    R = OUT_ROWS
    wb = w_out.astype(BF16)
    pad = LANES - MOE_GROUPS - MOE_EXPERTS
    wr = jnp.pad(jnp.concatenate([w_r1, w_r2], axis=1), ((0, 0), (0, pad)))
    wr_hi = wr.astype(BF16)
    wr_lo = (wr - wr_hi.astype(F32)).astype(BF16)
    br = jnp.pad(jnp.concatenate([b_r1, b_r2]), (0, pad)).reshape(1, LANES)
    tri = jnp.asarray(np.tril(np.ones((R, R), np.float32), -1), dtype=BF16)
    ins = (h2, y_ssd, y_att, wb[:SSD_INNER], wb[SSD_INNER:], norm_g.reshape(1, D_MODEL),
           wr_hi, jnp.concatenate([wr_hi, wr_lo], axis=1), br, tri)
    full = lambda a: pl.BlockSpec(a.shape, lambda i: (0,) * a.ndim)
    rows = lambda n: pl.BlockSpec((R, n), lambda i: (i, 0))
    return pl.pallas_call(
        _outproj_body, grid=(T // R,),
        in_specs=[rows(D_MODEL), rows(SSD_INNER), rows(ATT_INNER)] + [full(a) for a in ins[3:]],
        out_specs=(rows(D_MODEL), pl.BlockSpec((R * SUBLANES, LANES), lambda i: (i, 0)), rows(LANES),
                   pl.BlockSpec((SUBLANES, R), lambda i: (0, i)), pl.BlockSpec((1, LANES), lambda i: (0, 0))),
        out_shape=(jax.ShapeDtypeStruct((T, D_MODEL), F32),
                   jax.ShapeDtypeStruct((T * SUBLANES, LANES), F32),
                   jax.ShapeDtypeStruct((T, LANES), F32), jax.ShapeDtypeStruct((SUBLANES, T), F32),
                   jax.ShapeDtypeStruct((1, LANES), F32)),
        scratch_shapes=[pltpu.VMEM((1, LANES), F32)],
        compiler_params=_params("arbitrary"), name="outproj_route")(*ins)


def _slots_body(routet_ref, pstart_ref, dest_ref):
    cols = routet_ref.shape[1]
    expert = lax.broadcasted_iota(jnp.int32, (MOE_EXPERTS, cols), 0).astype(F32)
    pstart = pstart_ref[...]
    for j in range(2):
        hit = expert == routet_ref[j:j + 1, :]
        base = jnp.sum(jnp.where(hit, pstart, 0.0), axis=0, keepdims=True)
        dest_ref[j:j + 1, :] = (base + routet_ref[4 + j:5 + j, :]).astype(jnp.int32)


def _slots(routet, pstarts):
    T = routet.shape[1]
    cols = min(T, 4096)
    assert T % cols == 0
    return pl.pallas_call(
        _slots_body, grid=(T // cols,),
        in_specs=[pl.BlockSpec((SUBLANES, cols), lambda i: (0, i)),
                  pl.BlockSpec((MOE_EXPERTS, 1), lambda i: (0, 0))],
        out_specs=pl.BlockSpec((2, cols), lambda i: (0, i)),
        out_shape=jax.ShapeDtypeStruct((2, T), jnp.int32),
        compiler_params=_params("arbitrary"), name="slots")(
            routet, pstarts.astype(F32).reshape(MOE_EXPERTS, 1))


def _tile_copy(src_ref, src_row, dst_ref, dst_row, sem):
    tile = lambda row: pl.ds(pl.multiple_of(row * SUBLANES, SUBLANES), SUBLANES)
    return pltpu.make_async_copy(src_ref.at[tile(src_row)], dst_ref.at[tile(dst_row)], sem)


def _dispatch_body(pad_ref, d1_ref, d2_ref, xn_ref, xb_ref, zero_ref, stage_ref, sem, lsem, ssem):
    block = EXPERT_ROWS * SUBLANES
    n_blocks = xb_ref.shape[0] // block
    n_used = pad_ref[MOE_EXPERTS]

    def zero_copy(row):
        start = pl.multiple_of(row * SUBLANES, SUBLANES)
        return pltpu.make_async_copy(zero_ref, xb_ref.at[pl.ds(start, block)], sem)

    @pl.when(pl.program_id(0) == 0)
    def _():
        zero_ref[...] = jnp.zeros_like(zero_ref)
        for e in range(MOE_EXPERTS):
            zero_copy(pad_ref[e]).start()
        for e in range(MOE_EXPERTS):
            zero_copy(pad_ref[e]).wait()
        for b in range(n_blocks - MOE_EXPERTS - 1, n_blocks):
            @pl.when(b >= n_used)
            def _():
                zero_copy(b * EXPERT_ROWS).start()
        for b in range(n_blocks - MOE_EXPERTS - 1, n_blocks):
            @pl.when(b >= n_used)
            def _():
                zero_copy(b * EXPERT_ROWS).wait()

    i = pl.program_id(0)
    n = pl.num_programs(0)
    rows = MOVE_ROWS * SUBLANES

    def load(step, slot):
        src = xn_ref.at[pl.ds(pl.multiple_of(step * rows, rows), rows)]
        return pltpu.make_async_copy(src, stage_ref.at[slot], lsem.at[slot])

    def wait_scatters(slot):
        for _ in range(2):
            pltpu.make_async_copy(stage_ref.at[slot], stage_ref.at[slot], ssem.at[slot]).wait()

    @pl.when(i == 0)
    def _():
        load(0, 0).start()

    @pl.when(i >= 1)
    def _():
        wait_scatters((i - 1) & 1)

    @pl.when(i + 1 < n)
    def _():
        load(i + 1, (i + 1) & 1).start()

    slot = i & 1
    load(i, slot).wait()
    base = i * MOVE_ROWS

    def scatter(t, _):
        _tile_copy(stage_ref.at[slot], t, xb_ref, d1_ref[base + t], ssem.at[slot]).start()
        _tile_copy(stage_ref.at[slot], t, xb_ref, d2_ref[base + t], ssem.at[slot]).start(priority=1)
        return 0

    lax.fori_loop(0, MOVE_ROWS, scatter, 0)

    @pl.when(i == n - 1)
    def _():
        wait_scatters(slot)


def _dispatch(xn, dest1, dest2, pad_start, n_slots):
    T = dest1.shape[0]
    grid_spec = pltpu.PrefetchScalarGridSpec(
        num_scalar_prefetch=3, grid=(T // MOVE_ROWS,),
        in_specs=[pl.BlockSpec(memory_space=pl.ANY)],
        out_specs=pl.BlockSpec(memory_space=pl.ANY),
        scratch_shapes=[pltpu.VMEM((EXPERT_ROWS * SUBLANES, LANES), F32),
                        pltpu.VMEM((2, MOVE_ROWS * SUBLANES, LANES), F32),
                        pltpu.SemaphoreType.DMA(()), pltpu.SemaphoreType.DMA((2,)),
                        pltpu.SemaphoreType.DMA((2,))])
    return pl.pallas_call(
        _dispatch_body, grid_spec=grid_spec,
        out_shape=jax.ShapeDtypeStruct(((n_slots + EXPERT_ROWS) * SUBLANES, LANES), F32),
        compiler_params=_params("arbitrary"), name="dispatch")(pad_start, dest1, dest2, xn)


def _expert_body(be_ref, nused_ref, xb_ref, wg_ref, wu_ref, wd_ref, yb_ref,
                 x_ref, wgb_ref, wub_ref, wdb_ref):
    i = pl.program_id(0)
    used = i < nused_ref[0]
    new_expert = (i == 0) | (be_ref[i] != be_ref[jnp.maximum(i - 1, 0)])

    @pl.when(used & new_expert)
    def _():
        wgb_ref[...] = wg_ref[0, 0].astype(BF16)
        wub_ref[...] = wu_ref[0, 0].astype(BF16)
        wdb_ref[...] = wd_ref[0, 0].astype(BF16)

    @pl.when(used)
    def _():
        for c in range(SUBLANES):
            x_ref[:, c * LANES:(c + 1) * LANES] = xb_ref[pl.ds(c, EXPERT_ROWS, stride=SUBLANES), :].astype(BF16)
        x = x_ref[...]
        gate = jnp.dot(x, wgb_ref[...], preferred_element_type=F32)
        up = jnp.dot(x, wub_ref[...], preferred_element_type=F32)
        hmid = (_silu(gate) * up).astype(BF16)
        y = jnp.dot(hmid, wdb_ref[...], preferred_element_type=F32)
        for c in range(SUBLANES):
            yb_ref[pl.ds(c, EXPERT_ROWS, stride=SUBLANES), :] = y[:, c * LANES:(c + 1) * LANES]

    @pl.when(jnp.logical_not(used))
    def _():
        yb_ref[...] = jnp.zeros_like(yb_ref)


def _experts(xb, blk_exp, n_used, n_blk, layer, w_gate, w_up, w_down):
    M = EXPERT_ROWS
    tile = lambda index: pl.BlockSpec((M * SUBLANES, LANES), index)
    weight = lambda a: pl.BlockSpec((1, 1) + a.shape[2:], lambda i, be, nu: (layer, be[i], 0, 0))
    grid_spec = pltpu.PrefetchScalarGridSpec(
        num_scalar_prefetch=2, grid=(n_blk,),
        in_specs=[tile(lambda i, be, nu: (jnp.minimum(i, nu[0] - 1), 0)),
                  weight(w_gate), weight(w_up), weight(w_down)],
        out_specs=tile(lambda i, be, nu: (i, 0)),
        scratch_shapes=[pltpu.VMEM((M, D_MODEL), BF16), pltpu.VMEM((D_MODEL, MOE_HIDDEN), BF16),
                        pltpu.VMEM((D_MODEL, MOE_HIDDEN), BF16), pltpu.VMEM((MOE_HIDDEN, D_MODEL), BF16)])
    return pl.pallas_call(
        _expert_body, grid_spec=grid_spec,
        out_shape=jax.ShapeDtypeStruct((n_blk * M * SUBLANES, LANES), F32),
        compiler_params=_params("arbitrary"), name="experts")(
            blk_exp, n_used, xb, w_gate, w_up, w_down)


def _combine_body(d1_ref, d2_ref, h_ref, route_ref, yb_ref, o_ref, buf_ref, sem):
    i = pl.program_id(0)
    n = pl.num_programs(0)

    def gather(step, slot):
        base = step * MOVE_ROWS

        def body(t, _):
            _tile_copy(yb_ref, d1_ref[base + t], buf_ref.at[slot, 0], t, sem.at[slot]).start()
            _tile_copy(yb_ref, d2_ref[base + t], buf_ref.at[slot, 1], t, sem.at[slot]).start(priority=1)
            return 0

        lax.fori_loop(0, MOVE_ROWS, body, 0)

    @pl.when(i == 0)
    def _():
        gather(0, 0)

    @pl.when(i + 1 < n)
    def _():
        gather(i + 1, (i + 1) & 1)

    slot = i & 1
    pltpu.make_async_copy(buf_ref.at[slot], buf_ref.at[slot], sem.at[slot]).wait()
    route = route_ref[...]
    g1 = route[:, 2:3]
    g2 = route[:, 3:4]
    for c in range(SUBLANES):
        cs = slice(c * LANES, (c + 1) * LANES)
        rows = pl.ds(c, MOVE_ROWS, stride=SUBLANES)
        o_ref[:, cs] = h_ref[:, cs] + (buf_ref[slot, 0, rows, :] * g1 + buf_ref[slot, 1, rows, :] * g2)


def _combine(h2, route, yb, dest1, dest2):
    T = h2.shape[0]
    R = MOVE_ROWS
    grid_spec = pltpu.PrefetchScalarGridSpec(
        num_scalar_prefetch=2, grid=(T // R,),
        in_specs=[pl.BlockSpec((R, D_MODEL), lambda i, d1, d2: (i, 0)),
                  pl.BlockSpec((R, LANES), lambda i, d1, d2: (i, 0)), pl.BlockSpec(memory_space=pl.ANY)],
        out_specs=pl.BlockSpec((R, D_MODEL), lambda i, d1, d2: (i, 0)),
        scratch_shapes=[pltpu.VMEM((2, 2, R * SUBLANES, LANES), F32), pltpu.SemaphoreType.DMA((2,))])
    return pl.pallas_call(
        _combine_body, grid_spec=grid_spec, out_shape=jax.ShapeDtypeStruct((T, D_MODEL), F32),
        compiler_params=_params("arbitrary"), name="combine")(dest1, dest2, h2, route, yb)


def _moe(h2, xn, route, routet, counts, layer, w_gate, w_up, w_down):
    T = h2.shape[0]
    M = EXPERT_ROWS
    n_blk = (2 * T) // M + MOE_EXPERTS
    cnt = counts[0, ROUTER_LANE0:ROUTER_LANE0 + MOE_EXPERTS].astype(jnp.int32)
    padded = (cnt + M - 1) // M * M
    ends = jnp.cumsum(padded)
    pstarts = ends - padded
    n_used = (ends[-1] // M).astype(jnp.int32)
    blk = jnp.minimum(jnp.arange(n_blk, dtype=jnp.int32), n_used - 1) * M
    blk_exp = jnp.sum((ends[None, :] <= blk[:, None]).astype(jnp.int32), axis=1)
    blk_exp = jnp.minimum(blk_exp, MOE_EXPERTS - 1)
    dest = _slots(routet, pstarts)
    pad_info = jnp.concatenate([pstarts + cnt, n_used.reshape(1)])
    xb = _dispatch(xn, dest[0], dest[1], pad_info, n_blk * M)
    yb = _experts(xb, blk_exp, n_used.reshape(1), n_blk, layer, w_gate, w_up, w_down)
    return _combine(h2, route, yb, dest[0], dest[1])


def kernel(x, rel_bias, norm1_g, w_in, conv_w, conv_b, dt_bias, a_log, d_skip, ssd_norm_g,
           q_norm_g, k_norm_g, w_out, norm2_g, w_r1, b_r1, w_r2, b_r2, w_gate, w_up, w_down):
    B, S, D = x.shape
    T = B * S
    h2 = x.reshape(T, D)
    depth = w_in.shape[0]
    bias_tiles = _bias_tiles(rel_bias)
    for l in range(depth):
        z, xbc, dt, dtt, q, k, v = _inproj(h2, norm1_g[l], w_in[l])
        r3 = lambda a: a.reshape(B, S, a.shape[-1])
        y_ssd = _ssd(r3(z), r3(xbc), r3(dt), dtt, conv_w[l], conv_b[l], dt_bias[l], a_log[l],
                     d_skip[l], ssd_norm_g[l])
        y_att = _attention(r3(q), r3(k), r3(v), q_norm_g[l], k_norm_g[l], bias_tiles)
        h2, xn, route, routet, counts = _outproj_route(
            h2, y_ssd.reshape(T, SSD_INNER), y_att.reshape(T, ATT_INNER), w_out[l], norm2_g[l],
            w_r1[l], b_r1[l], w_r2[l], b_r2[l])
        h2 = _moe(h2, xn, route, routet, counts, l, w_gate, w_up, w_down)
    return h2.reshape(B, S, D)
```

```python
import math

import numpy as np
import jax
import jax.numpy as jnp
from jax import lax
from jax.experimental import pallas as pl
from jax.experimental.pallas import tpu as pltpu

F32 = jnp.float32
BF16 = jnp.bfloat16
HIGHEST = lax.Precision.HIGHEST

LANES = 128
SUBLANES = 8

D_MODEL = 1024
SSD_HEADS = 8
SSD_HEAD_DIM = 64
SSD_INNER = SSD_HEADS * SSD_HEAD_DIM
SSD_GROUPS = 2
SSD_HEADS_PER_GROUP = SSD_HEADS // SSD_GROUPS
SSD_STATE = 64
SSD_CONV = 4
SSD_CHUNK = 128
CONV_CH = SSD_INNER + 2 * SSD_GROUPS * SSD_STATE
ATT_HEADS = 8
ATT_HEAD_DIM = 64
ATT_INNER = ATT_HEADS * ATT_HEAD_DIM
MOBA_BLOCK = 256
MOBA_TOPK = 3
REL_BUCKETS = 32
REL_MAX_DIST = 128
MOE_GROUPS = 4
MOE_EXPERTS_PER_GROUP = 8
MOE_EXPERTS = MOE_GROUPS * MOE_EXPERTS_PER_GROUP
MOE_HIDDEN = 512
EPS = 1e-6

MASK_VALUE = -1e30
LOG2E = math.log2(math.e)
IN_ROWS = 512
SSD_STEP_CHUNKS = 2
OUT_ROWS = 512
EXPERT_ROWS = 512
MOVE_ROWS = 256
VMEM_LIMIT = 56 * 1024 * 1024
ROUTER_LANE0 = MOE_GROUPS


def _silu(x):
    return x * (1.0 / (1.0 + jnp.exp(-x)))


def _softplus(x):
    return jnp.maximum(x, 0.0) + jnp.log(1.0 + jnp.exp(-jnp.abs(x)))


def _params(*sem):
    return pltpu.CompilerParams(dimension_semantics=sem, vmem_limit_bytes=VMEM_LIMIT)


def _inproj_body(h_ref, g_ref, wz_ref, wx_ref, wdt_ref, wq_ref, wk_ref, wv_ref,
                 z_ref, xbc_ref, dt_ref, dtt_ref, q_ref, k_ref, v_ref):
    x = h_ref[...]
    xn = x * lax.rsqrt(jnp.mean(x * x, axis=-1, keepdims=True) + EPS) * g_ref[...]
    xb = xn.astype(BF16)
    z_ref[...] = jnp.dot(xb, wz_ref[...], preferred_element_type=F32)
    xbc_ref[...] = jnp.dot(xb, wx_ref[...], preferred_element_type=F32)
    dt = jnp.dot(xb, wdt_ref[...], preferred_element_type=F32)
    dt_ref[...] = dt
    dtt_ref[...] = dt.T[0:SSD_HEADS, :]
    q_ref[...] = jnp.dot(xb, wq_ref[...], preferred_element_type=F32)
    k_ref[...] = jnp.dot(xb, wk_ref[...], preferred_element_type=F32)
    v_ref[...] = jnp.dot(xb, wv_ref[...], preferred_element_type=F32)


def _inproj(h2, g, w_in):
    T = h2.shape[0]
    c0, c1, c2 = SSD_INNER, SSD_INNER + CONV_CH, SSD_INNER + CONV_CH + SSD_HEADS
    wb = w_in.astype(BF16)
    wz, wx, wdt = wb[:, :c0], wb[:, c0:c1], wb[:, c1:c2]
    wq, wk, wv = (wb[:, c2 + i * ATT_INNER:c2 + (i + 1) * ATT_INNER] for i in range(3))
    wdt_pad = jnp.pad(wdt, ((0, 0), (0, LANES - SSD_HEADS)))
    full = lambda a: pl.BlockSpec(a.shape, lambda i: (0,) * a.ndim)
    rows = lambda n: pl.BlockSpec((IN_ROWS, n), lambda i: (i, 0))
    ins = (h2, g.reshape(1, D_MODEL), wz, wx, wdt_pad, wq, wk, wv)
    out_shape = (
        jax.ShapeDtypeStruct((T, SSD_INNER), F32), jax.ShapeDtypeStruct((T, CONV_CH), F32),
        jax.ShapeDtypeStruct((T, LANES), F32), jax.ShapeDtypeStruct((SSD_HEADS, T), F32),
        jax.ShapeDtypeStruct((T, ATT_INNER), F32), jax.ShapeDtypeStruct((T, ATT_INNER), F32),
        jax.ShapeDtypeStruct((T, ATT_INNER), F32))
    out_specs = (rows(SSD_INNER), rows(CONV_CH), rows(LANES),
                 pl.BlockSpec((SSD_HEADS, IN_ROWS), lambda i: (0, i)),
                 rows(ATT_INNER), rows(ATT_INNER), rows(ATT_INNER))
    return pl.pallas_call(
        _inproj_body, grid=(T // IN_ROWS,),
        in_specs=[rows(D_MODEL)] + [full(a) for a in ins[1:]],
        out_specs=out_specs, out_shape=out_shape,
        compiler_params=_params("arbitrary"), name="inproj")(*ins)


def _ssd_body(z_ref, xbc_ref, dt_ref, dtt_ref, cw_ref, cb_ref, dtb_ref, dtbt_ref, alog_ref,
              alogt_ref, dskip_ref, ng_ref, tril_ref, expand_ref, y_ref, ext_ref, state_ref):
    L, P, N, E = SSD_CHUNK, SSD_HEAD_DIM, SSD_STATE, SSD_HEADS_PER_GROUP
    R = SSD_STEP_CHUNKS * L
    c = pl.program_id(1)

    @pl.when(c == 0)
    def _():
        state_ref[...] = jnp.zeros_like(state_ref)
        ext_ref[0:SUBLANES, :] = jnp.zeros((SUBLANES, CONV_CH), F32)

    ext_ref[SUBLANES:SUBLANES + R, :] = xbc_ref[0]
    tril = tril_ref[...]
    expand = expand_ref[...]
    row = lax.broadcasted_iota(jnp.int32, (L, L), 0)
    col = lax.broadcasted_iota(jnp.int32, (L, L), 1)
    causal = row >= col
    for sub in range(SSD_STEP_CHUNKS):
        _ssd_chunk(sub * L, z_ref, dt_ref, dtt_ref, cw_ref, cb_ref, dtb_ref, dtbt_ref, alog_ref,
                   alogt_ref, dskip_ref, ng_ref, tril, expand, causal, y_ref, ext_ref, state_ref)
    ext_ref[0:SUBLANES, :] = ext_ref[R:R + SUBLANES, :]


def _ssd_chunk(r0, z_ref, dt_ref, dtt_ref, cw_ref, cb_ref, dtb_ref, dtbt_ref, alog_ref, alogt_ref,
               dskip_ref, ng_ref, tril, expand, causal, y_ref, ext_ref, state_ref):
    L, P, N, E = SSD_CHUNK, SSD_HEAD_DIM, SSD_STATE, SSD_HEADS_PER_GROUP
    conv = jnp.zeros((L, CONV_CH), F32) + cb_ref[...]
    for j in range(SSD_CONV):
        off = r0 + SUBLANES - (SSD_CONV - 1) + j
        conv = conv + cw_ref[j:j + 1, :] * ext_ref[off:off + L, :]
    act = _silu(conv)
    xs = act[:, :SSD_INNER]
    b_in = act[:, SSD_INNER:SSD_INNER + SSD_GROUPS * N]
    c_out = act[:, SSD_INNER + SSD_GROUPS * N:]

    dt = _softplus(dt_ref[0, r0:r0 + L, :] + dtb_ref[...])
    a_dt = dt * (-jnp.exp(alog_ref[...]))
    a_cs = jnp.dot(tril, a_dt, precision=HIGHEST, preferred_element_type=F32)
    dt_t = _softplus(dtt_ref[:, r0:r0 + L] + dtbt_ref[...])
    a_dt_t = dt_t * (-jnp.exp(alogt_ref[...]))
    a_cs_t = lax.dot_general(a_dt_t, tril, (((1,), (1,)), ((), ())), precision=HIGHEST,
                             preferred_element_type=F32)
    last = a_cs[L - 1:L, :]
    per_head = jnp.concatenate(
        [dt, jnp.exp(a_cs), jnp.exp(last - a_cs), jnp.broadcast_to(jnp.exp(last), (SUBLANES, LANES))],
        axis=0)
    hi = per_head.astype(BF16)
    lo = (per_head - hi.astype(F32)).astype(BF16)
    on_lanes = jnp.dot(jnp.concatenate([hi, lo], axis=-1), expand, preferred_element_type=F32)
    dt_x = on_lanes[0:L]
    eacs_x = on_lanes[L:2 * L]
    dec_x = on_lanes[2 * L:3 * L]
    cdec_x = on_lanes[3 * L:3 * L + 1]
    xdt = xs * dt_x
    xdtd = (xdt * dec_x).astype(BF16)
    xdt_b = xdt.astype(BF16)

    y_parts = []
    for g in range(SSD_GROUPS):
        bg = b_in[:, g * N:(g + 1) * N].astype(BF16)
        cg = c_out[:, g * N:(g + 1) * N].astype(BF16)
        cb = lax.dot_general(cg, bg, (((1,), (1,)), ((), ())), preferred_element_type=F32)
        st = state_ref[g]
        gs = slice(g * E * P, (g + 1) * E * P)
        y_off = jnp.dot(cg, st.astype(BF16), preferred_element_type=F32) * eacs_x[:, gs]
        for e in range(E):
            h = g * E + e
            diff = a_cs[:, h:h + 1] - a_cs_t[h:h + 1, :]
            l_mat = jnp.exp(jnp.where(causal, diff, -jnp.inf))
            m = (cb * l_mat).astype(BF16)
            y_diag = jnp.dot(m, xdt_b[:, h * P:(h + 1) * P], preferred_element_type=F32)
            y_parts.append(y_diag + y_off[:, e * P:(e + 1) * P])
        upd = lax.dot_general(bg, xdtd[:, gs], (((0,), (0,)), ((), ())),
                              preferred_element_type=F32)
        state_ref[g] = cdec_x[:, gs] * st + upd
    y = jnp.concatenate(y_parts, axis=-1) + dskip_ref[...] * xs
    y = y * _silu(z_ref[0, r0:r0 + L, :])
    y = y * lax.rsqrt(jnp.mean(y * y, axis=-1, keepdims=True) + EPS) * ng_ref[...]
    y_ref[0, r0:r0 + L, :] = y


def _ssd(z, xbc, dt, dtt, conv_w, conv_b, dt_bias, a_log, d_skip, norm_g):
    B, S, _ = z.shape
    L = SSD_CHUNK
    R = SSD_STEP_CHUNKS * L
    nc = S // R
    pad_row = lambda a: jnp.pad(a.reshape(1, SSD_HEADS), ((0, 0), (0, LANES - SSD_HEADS)))
    tril = jnp.asarray(np.tril(np.ones((L, L), np.float32)))
    expand_np = np.zeros((LANES, SSD_INNER), np.float32)
    for h in range(SSD_HEADS):
        expand_np[h, h * SSD_HEAD_DIM:(h + 1) * SSD_HEAD_DIM] = 1.0
    ins = (z, xbc, dt, dtt, conv_w, conv_b.reshape(1, CONV_CH), pad_row(dt_bias),
           dt_bias.reshape(SSD_HEADS, 1), pad_row(a_log), a_log.reshape(SSD_HEADS, 1),
           jnp.repeat(d_skip, SSD_HEAD_DIM).reshape(1, SSD_INNER), norm_g.reshape(1, SSD_INNER),
           tril, jnp.asarray(np.concatenate([expand_np, expand_np], axis=0), dtype=BF16))
    full = lambda a: pl.BlockSpec(a.shape, lambda b, c: (0,) * a.ndim)
    chunk = lambda n: pl.BlockSpec((1, R, n), lambda b, c: (b, c, 0))
    in_specs = [chunk(SSD_INNER), chunk(CONV_CH), chunk(LANES),
                pl.BlockSpec((SSD_HEADS, R), lambda b, c: (0, b * nc + c))]
    in_specs += [full(a) for a in ins[4:]]
    return pl.pallas_call(
        _ssd_body, grid=(B, nc), in_specs=in_specs, out_specs=chunk(SSD_INNER),
        out_shape=jax.ShapeDtypeStruct((B, S, SSD_INNER), F32),
        scratch_shapes=[pltpu.VMEM((R + SUBLANES, CONV_CH), F32),
                        pltpu.VMEM((SSD_GROUPS, SSD_STATE, SSD_HEADS_PER_GROUP * SSD_HEAD_DIM), F32)],
        compiler_params=_params("arbitrary", "arbitrary"), name="ssd")(*ins)


def _bucket_tile():
    bs = MOBA_BLOCK
    dist = np.arange(bs)[:, None] - np.arange(2 * bs)[None, :] + bs
    max_exact = REL_BUCKETS // 2
    d = np.maximum(dist, max_exact).astype(np.float32)
    large = max_exact + (np.log(d / np.float32(max_exact)) / np.float32(math.log(REL_MAX_DIST / max_exact))
                         * np.float32(REL_BUCKETS - max_exact)).astype(np.int32)
    large = np.minimum(large, REL_BUCKETS - 1)
    bucket = np.where(dist < max_exact, dist, large)
    return np.where(dist >= 0, bucket, -1).astype(np.int32)


def _bias_body(rb_ref, bucket_ref, o_ref):
    head = pl.program_id(0)
    bucket = bucket_ref[...]
    far = rb_ref[REL_BUCKETS - 1, head]
    bias = jnp.where(bucket < 0, MASK_VALUE, 0.0)
    for i in range(REL_BUCKETS - 1):
        bias = jnp.where(bucket == i, (rb_ref[i, head] - far) * LOG2E, bias)
    o_ref[0] = bias


def _bias_tiles(rel_bias):
    bs = MOBA_BLOCK
    bucket = jnp.asarray(_bucket_tile())
    return pl.pallas_call(
        _bias_body, grid=(ATT_HEADS,),
        in_specs=[pl.BlockSpec(memory_space=pltpu.SMEM), pl.BlockSpec((bs, 2 * bs), lambda h: (0, 0))],
        out_specs=pl.BlockSpec((1, bs, 2 * bs), lambda h: (h, 0, 0)),
        out_shape=jax.ShapeDtypeStruct((ATT_HEADS, bs, 2 * bs), F32),
        compiler_params=_params("arbitrary"), name="bias_tiles")(rel_bias, bucket)


def _attn_body(q_ref, k_ref, v_ref, qg_ref, kg_ref, bias_ref, o_ref,
               kaug_ref, vaug_ref, qaug_ref, sa_ref, sb_ref, m_ref, acc_ref):
    bs, hd = MOBA_BLOCK, ATT_HEAD_DIM
    nb = k_ref.shape[1] // bs
    jp = pl.program_id(2)
    nt_dims = (((1,), (1,)), ((), ()))

    @pl.when(jp == 0)
    def _():
        S = nb * bs
        pair = 2 * hd
        r = lax.broadcasted_iota(jnp.int32, (pair, pair), 0) // hd
        c = lax.broadcasted_iota(jnp.int32, (pair, pair), 1) // hd
        same_head = (r == c).astype(BF16)

        def head_norm(x, gain):
            ms = jnp.dot((x * x).astype(BF16), same_head, preferred_element_type=F32) * (1.0 / hd)
            return x * lax.rsqrt(ms + EPS) * gain

        kn = head_norm(k_ref[0], kg_ref[...])
        kmean = jnp.mean(kn.reshape(nb, bs, pair), axis=1)
        kn_b = kn.astype(BF16)
        v_b = v_ref[0].astype(BF16)
        lane = lax.broadcasted_iota(jnp.int32, (S, pair), 1)
        blk = lax.broadcasted_iota(jnp.int32, (S, pair), 0) // bs
        lane_m = lax.broadcasted_iota(jnp.int32, (nb, pair), 1)
        rowb = lax.broadcasted_iota(jnp.int32, (nb, bs), 0)
        lane_q = lax.broadcasted_iota(jnp.int32, (bs, pair), 1)
        kmean_b = []
        for h in range(2):
            own = (lane // hd) == h
            other0 = (1 - h) * hd
            kaug_ref[h] = jnp.where(own, kn_b, (lane - other0 == blk).astype(BF16))
            vaug_ref[h] = jnp.where(own, v_b, jnp.ones((S, pair), BF16))
            kmean_b.append(jnp.where((lane_m // hd) == h, kmean, 0.0).astype(BF16))
        for j in range(nb):
            rows = slice(j * bs, (j + 1) * bs)
            qn = head_norm(q_ref[0, rows, :], qg_ref[...])
            qn_b = qn.astype(BF16)
            qs = qn * (ATT_HEAD_DIM ** -0.5 * LOG2E)
            for h in range(2):
                if j <= MOBA_TOPK:
                    sel = rowb <= j
                else:
                    gate = lax.dot_general(kmean_b[h], qn_b, nt_dims,
                                           preferred_element_type=F32)
                    rank = jnp.zeros((nb, bs), jnp.int32)
                    for m in range(j):
                        gm = gate[m:m + 1, :]
                        beats = (gm > gate) | ((gm == gate) & (rowb > m))
                        rank = rank + beats.astype(jnp.int32)
                    sel = ((rowb < j) & (rank < MOBA_TOPK)) | (rowb == j)
                selmask = jnp.where(sel, 0.0, MASK_VALUE)
                pieces = [selmask, jnp.zeros((pair - nb, bs), F32)]
                if h == 0:
                    pieces = [jnp.zeros((hd, bs), F32), selmask, jnp.zeros((hd - nb, bs), F32)]
                mask_t = jnp.concatenate(pieces, axis=0).T
                qaug_ref[h, rows, :] = jnp.where((lane_q // hd) == h, qs, mask_t).astype(BF16)

    qt = 2 * bs
    q0 = pl.multiple_of(jp * qt, qt)
    all_rows = slice(0, qt)
    late_rows = slice(bs, qt)
    for h in range(2):
        m_ref[h] = jnp.full((qt, 2 * hd), -jnp.inf, F32)
        acc_ref[h] = jnp.zeros((qt, 2 * hd), F32)

    def scores(n, s_ref, rows=all_rows):
        start = pl.multiple_of(n * bs, bs)
        q_rows = pl.ds(q0 + rows.start, rows.stop - rows.start)
        for h in range(2):
            s_ref[h, rows, :] = lax.dot_general(qaug_ref[h, q_rows, :], kaug_ref[h, pl.ds(start, bs), :],
                                                nt_dims, preferred_element_type=F32)

    def update(n, s_ref, bias, rows=all_rows):
        start = pl.multiple_of(n * bs, bs)
        for r0 in range(rows.start, rows.stop, bs):
            sub = slice(r0, r0 + bs)
            for h in range(2):
                s = s_ref[h, sub, :]
                tile = None if bias is None else bias(h, r0 // bs)
                if tile is not None:
                    s = s + tile
                m_i = m_ref[h, sub, :]
                m_new = jnp.maximum(m_i, jnp.max(s, axis=-1, keepdims=True))
                alpha = jnp.exp2(m_i - m_new)
                p = jnp.exp2((s - jnp.concatenate([m_new, m_new], axis=-1)).astype(BF16))
                pv = jnp.dot(p, vaug_ref[h, pl.ds(start, bs), :], preferred_element_type=F32)
                acc_ref[h, sub, :] = alpha * acc_ref[h, sub, :] + pv
                m_ref[h, sub, :] = m_new

    def far_pair(i, _):
        n = 2 * i
        scores(n + 1, sb_ref)
        update(n, sa_ref, None)
        scores(n + 2, sa_ref)
        update(n + 1, sb_ref, None)
        return 0

    n_far = jnp.maximum(2 * jp - 1, 0)
    scores(0, sa_ref)
    lax.fori_loop(0, n_far >> 1, far_pair, 0)
    prev = lambda h: bias_ref[h, :, 0:bs]
    own = lambda h, qblk=1: bias_ref[h, :, bs:2 * bs]
    bias_p = lambda h, qblk: prev(h) if qblk == 0 else None
    bias_o = lambda h, qblk: own(h) if qblk == 0 else prev(h)
    blk_p = jnp.maximum(2 * jp - 1, 0)
    blk_o = 2 * jp
    blk_n = 2 * jp + 1

    @pl.when(jp > 0)
    def _():
        scores(blk_p, sb_ref)
        update(n_far - 1, sa_ref, None)
        scores(blk_o, sa_ref)
        update(blk_p, sb_ref, bias_p)
        scores(blk_n, sb_ref, late_rows)
        update(blk_o, sa_ref, bias_o)
        update(blk_n, sb_ref, own, late_rows)

    @pl.when(jp == 0)
    def _():
        scores(blk_n, sb_ref, late_rows)
        update(blk_o, sa_ref, bias_o)
        update(blk_n, sb_ref, own, late_rows)

    ratio = [acc_ref[h] / pltpu.roll(acc_ref[h], hd, 1) for h in range(2)]
    lane_o = lax.broadcasted_iota(jnp.int32, (qt, 2 * hd), 1)
    o_ref[0] = jnp.where(lane_o < hd, ratio[0], ratio[1])


def _attention(q, k, v, q_g, k_g, bias_tiles):
    B, S, _ = q.shape
    bs = MOBA_BLOCK
    nb = S // bs
    assert S % (2 * bs) == 0 and nb <= ATT_HEAD_DIM
    pair = 2 * ATT_HEAD_DIM
    return pl.pallas_call(
        _attn_body, grid=(B, ATT_HEADS // 2, nb // 2),
        in_specs=[
            pl.BlockSpec((1, S, pair), lambda b, hp, i: (b, 0, hp)),
            pl.BlockSpec((1, S, pair), lambda b, hp, i: (b, 0, hp)),
            pl.BlockSpec((1, S, pair), lambda b, hp, i: (b, 0, hp)),
            pl.BlockSpec((1, pair), lambda b, hp, i: (0, 0)),
            pl.BlockSpec((1, pair), lambda b, hp, i: (0, 0)),
            pl.BlockSpec((2, bs, 2 * bs), lambda b, hp, i: (hp, 0, 0)),
        ],
        out_specs=pl.BlockSpec((1, 2 * bs, pair), lambda b, hp, i: (b, i, hp)),
        out_shape=jax.ShapeDtypeStruct((B, S, ATT_INNER), F32),
        scratch_shapes=[pltpu.VMEM((2, S, pair), BF16), pltpu.VMEM((2, S, pair), BF16),
                        pltpu.VMEM((2, S, pair), BF16),
                        pltpu.VMEM((2, 2 * bs, bs), F32), pltpu.VMEM((2, 2 * bs, bs), F32),
                        pltpu.VMEM((2, 2 * bs, pair), F32),
                        pltpu.VMEM((2, 2 * bs, pair), F32)],
        compiler_params=_params("arbitrary", "arbitrary", "arbitrary"), name="moba")(
            q, k, v, jnp.tile(q_g, 2).reshape(1, pair), jnp.tile(k_g, 2).reshape(1, pair), bias_tiles)


def _outproj_body(h_ref, ys_ref, ya_ref, wa_ref, wb_ref, g_ref, wrh_ref, wrl_ref, br_ref, tri_ref,
                  hn_ref, xn_ref, route_ref, routet_ref, counts_ref, carry_ref):
    i = pl.program_id(0)

    @pl.when(i == 0)
    def _():
        carry_ref[...] = jnp.zeros_like(carry_ref)

    _outproj_rows(0, OUT_ROWS, h_ref, ys_ref, ya_ref, wa_ref, wb_ref, g_ref, wrh_ref, wrl_ref,
                  br_ref, tri_ref, hn_ref, xn_ref, route_ref, routet_ref, carry_ref)
    counts_ref[...] = carry_ref[...]


def _outproj_rows(r0, n, h_ref, ys_ref, ya_ref, wa_ref, wb_ref, g_ref, wrh_ref, wrl_ref, br_ref, tri_ref,
                  hn_ref, xn_ref, route_ref, routet_ref, carry_ref):
    rs = slice(r0, r0 + n)
    hn = (h_ref[rs, :]
          + jnp.dot(ys_ref[rs, :].astype(BF16), wa_ref[...], preferred_element_type=F32)
          + jnp.dot(ya_ref[rs, :].astype(BF16), wb_ref[...], preferred_element_type=F32))
    hn_ref[rs, :] = hn
    xn = hn * lax.rsqrt(jnp.mean(hn * hn, axis=-1, keepdims=True) + EPS) * g_ref[...]
    for c in range(SUBLANES):
        xn_ref[pl.ds(r0 * SUBLANES + c, n, stride=SUBLANES), :] = xn[:, c * LANES:(c + 1) * LANES]
    x_hi = xn.astype(BF16)
    x_lo = (xn - x_hi.astype(F32)).astype(BF16)
    hi_terms = jnp.dot(x_hi, wrl_ref[...], preferred_element_type=F32)
    logits = (hi_terms[:, :LANES] + hi_terms[:, LANES:]
              + jnp.dot(x_lo, wrh_ref[...], preferred_element_type=F32)) + br_ref[...]
    lane = lax.broadcasted_iota(jnp.int32, (n, LANES), 1)
    first = lambda hit: jnp.min(jnp.where(hit, lane, LANES), axis=-1, keepdims=True)
    l1 = jnp.where(lane < MOE_GROUPS, logits, -jnp.inf)
    mx = jnp.max(l1, axis=-1, keepdims=True)
    gval = 1.0 / jnp.sum(jnp.exp(l1 - mx), axis=-1, keepdims=True)
    gidx = first(l1 == mx)
    e_lane = lane - ROUTER_LANE0
    in_grp = (e_lane >= 0) & (e_lane < MOE_EXPERTS) & ((e_lane >> 3) == gidx)
    l2 = jnp.where(in_grp, logits, -jnp.inf)
    m1 = jnp.max(l2, axis=-1, keepdims=True)
    i1 = first(l2 == m1)
    l2b = jnp.where(lane == i1, -jnp.inf, l2)
    m2 = jnp.max(l2b, axis=-1, keepdims=True)
    i2 = first(l2b == m2)
    r = jnp.exp(m2 - m1)
    g1 = gval / (1.0 + r)
    g2 = gval * r / (1.0 + r)
    oh1 = (lane == i1)
    oh2 = (lane == i2)
    both = jnp.where(oh1 | oh2, 1.0, 0.0)
    cx = jnp.dot(tri_ref[...], both.astype(BF16), preferred_element_type=F32) + carry_ref[...]
    rank1 = jnp.sum(jnp.where(oh1, cx, 0.0), axis=-1, keepdims=True)
    rank2 = jnp.sum(jnp.where(oh2, cx, 0.0), axis=-1, keepdims=True)
    carry_ref[...] = carry_ref[...] + jnp.sum(both, axis=0, keepdims=True)
    cols = ((i1 - ROUTER_LANE0).astype(F32), (i2 - ROUTER_LANE0).astype(F32), g1, g2, rank1, rank2)
    route = jnp.zeros((n, LANES), F32)
    for j, cval in enumerate(cols):
        route = jnp.where(lane == j, cval, route)
    route_ref[rs, :] = route
    routet_ref[:, rs] = route.T[0:SUBLANES, :]


def _outproj_route(h2, y_ssd, y_att, w_out, norm_g, w_r1, b_r1, w_r2, b_r2):
    T = h2.shape[0]
    R = OUT_ROWS
    wb = w_out.astype(BF16)
    pad = LANES - MOE_GROUPS - MOE_EXPERTS
    wr = jnp.pad(jnp.concatenate([w_r1, w_r2], axis=1), ((0, 0), (0, pad)))
    wr_hi = wr.astype(BF16)
    wr_lo = (wr - wr_hi.astype(F32)).astype(BF16)
    br = jnp.pad(jnp.concatenate([b_r1, b_r2]), (0, pad)).reshape(1, LANES)
    tri = jnp.asarray(np.tril(np.ones((R, R), np.float32), -1), dtype=BF16)
    ins = (h2, y_ssd, y_att, wb[:SSD_INNER], wb[SSD_INNER:], norm_g.reshape(1, D_MODEL),
           wr_hi, jnp.concatenate([wr_hi, wr_lo], axis=1), br, tri)
    full = lambda a: pl.BlockSpec(a.shape, lambda i: (0,) * a.ndim)
    rows = lambda n: pl.BlockSpec((R, n), lambda i: (i, 0))
    return pl.pallas_call(
        _outproj_body, grid=(T // R,),
        in_specs=[rows(D_MODEL), rows(SSD_INNER), rows(ATT_INNER)] + [full(a) for a in ins[3:]],
        out_specs=(rows(D_MODEL), pl.BlockSpec((R * SUBLANES, LANES), lambda i: (i, 0)), rows(LANES),
                   pl.BlockSpec((SUBLANES, R), lambda i: (0, i)), pl.BlockSpec((1, LANES), lambda i: (0, 0))),
        out_shape=(jax.ShapeDtypeStruct((T, D_MODEL), F32),
                   jax.ShapeDtypeStruct((T * SUBLANES, LANES), F32),
                   jax.ShapeDtypeStruct((T, LANES), F32), jax.ShapeDtypeStruct((SUBLANES, T), F32),
                   jax.ShapeDtypeStruct((1, LANES), F32)),
        scratch_shapes=[pltpu.VMEM((1, LANES), F32)],
        compiler_params=_params("arbitrary"), name="outproj_route")(*ins)


def _slots_body(routet_ref, pstart_ref, dest_ref):
    cols = routet_ref.shape[1]
    expert = lax.broadcasted_iota(jnp.int32, (MOE_EXPERTS, cols), 0).astype(F32)
    pstart = pstart_ref[...]
    for j in range(2):
        hit = expert == routet_ref[j:j + 1, :]
        base = jnp.sum(jnp.where(hit, pstart, 0.0), axis=0, keepdims=True)
        dest_ref[j:j + 1, :] = (base + routet_ref[4 + j:5 + j, :]).astype(jnp.int32)


def _slots(routet, pstarts):
    T = routet.shape[1]
    cols = min(T, 4096)
    assert T % cols == 0
    return pl.pallas_call(
        _slots_body, grid=(T // cols,),
        in_specs=[pl.BlockSpec((SUBLANES, cols), lambda i: (0, i)),
                  pl.BlockSpec((MOE_EXPERTS, 1), lambda i: (0, 0))],
        out_specs=pl.BlockSpec((2, cols), lambda i: (0, i)),
        out_shape=jax.ShapeDtypeStruct((2, T), jnp.int32),
        compiler_params=_params("arbitrary"), name="slots")(
            routet, pstarts.astype(F32).reshape(MOE_EXPERTS, 1))


def _tile_copy(src_ref, src_row, dst_ref, dst_row, sem):
    tile = lambda row: pl.ds(pl.multiple_of(row * SUBLANES, SUBLANES), SUBLANES)
    return pltpu.make_async_copy(src_ref.at[tile(src_row)], dst_ref.at[tile(dst_row)], sem)


def _dispatch_body(pad_ref, d1_ref, d2_ref, xn_ref, xb_ref, zero_ref, stage_ref, sem, lsem, ssem):
    block = EXPERT_ROWS * SUBLANES
    n_blocks = xb_ref.shape[0] // block
    n_used = pad_ref[MOE_EXPERTS]

    def zero_copy(row):
        start = pl.multiple_of(row * SUBLANES, SUBLANES)
        return pltpu.make_async_copy(zero_ref, xb_ref.at[pl.ds(start, block)], sem)

    @pl.when(pl.program_id(0) == 0)
    def _():
        zero_ref[...] = jnp.zeros_like(zero_ref)
        for e in range(MOE_EXPERTS):
            zero_copy(pad_ref[e]).start()
        for e in range(MOE_EXPERTS):
            zero_copy(pad_ref[e]).wait()
        for b in range(n_blocks - MOE_EXPERTS - 1, n_blocks):
            @pl.when(b >= n_used)
            def _():
                zero_copy(b * EXPERT_ROWS).start()
        for b in range(n_blocks - MOE_EXPERTS - 1, n_blocks):
            @pl.when(b >= n_used)
            def _():
                zero_copy(b * EXPERT_ROWS).wait()

    i = pl.program_id(0)
    n = pl.num_programs(0)
    rows = MOVE_ROWS * SUBLANES

    def load(step, slot):
        src = xn_ref.at[pl.ds(pl.multiple_of(step * rows, rows), rows)]
        return pltpu.make_async_copy(src, stage_ref.at[slot], lsem.at[slot])

    def wait_scatters(slot):
        for _ in range(2):
            pltpu.make_async_copy(stage_ref.at[slot], stage_ref.at[slot], ssem.at[slot]).wait()

    @pl.when(i == 0)
    def _():
        load(0, 0).start()

    @pl.when(i >= 1)
    def _():
        wait_scatters((i - 1) & 1)

    @pl.when(i + 1 < n)
    def _():
        load(i + 1, (i + 1) & 1).start()

    slot = i & 1
    load(i, slot).wait()
    base = i * MOVE_ROWS

    def scatter(t, _):
        _tile_copy(stage_ref.at[slot], t, xb_ref, d1_ref[base + t], ssem.at[slot]).start()
        _tile_copy(stage_ref.at[slot], t, xb_ref, d2_ref[base + t], ssem.at[slot]).start(priority=1)
        return 0

    lax.fori_loop(0, MOVE_ROWS, scatter, 0)

    @pl.when(i == n - 1)
    def _():
        wait_scatters(slot)


def _dispatch(xn, dest1, dest2, pad_start, n_slots):
    T = dest1.shape[0]
    grid_spec = pltpu.PrefetchScalarGridSpec(
        num_scalar_prefetch=3, grid=(T // MOVE_ROWS,),
        in_specs=[pl.BlockSpec(memory_space=pl.ANY)],
        out_specs=pl.BlockSpec(memory_space=pl.ANY),
        scratch_shapes=[pltpu.VMEM((EXPERT_ROWS * SUBLANES, LANES), F32),
                        pltpu.VMEM((2, MOVE_ROWS * SUBLANES, LANES), F32),
                        pltpu.SemaphoreType.DMA(()), pltpu.SemaphoreType.DMA((2,)),
                        pltpu.SemaphoreType.DMA((2,))])
    return pl.pallas_call(
        _dispatch_body, grid_spec=grid_spec,
        out_shape=jax.ShapeDtypeStruct(((n_slots + EXPERT_ROWS) * SUBLANES, LANES), F32),
        compiler_params=_params("arbitrary"), name="dispatch")(pad_start, dest1, dest2, xn)


def _expert_body(be_ref, nused_ref, xb_ref, wg_ref, wu_ref, wd_ref, yb_ref,
                 x_ref, wgb_ref, wub_ref, wdb_ref):
    i = pl.program_id(0)
    used = i < nused_ref[0]
    new_expert = (i == 0) | (be_ref[i] != be_ref[jnp.maximum(i - 1, 0)])

    @pl.when(used & new_expert)
    def _():
        wgb_ref[...] = wg_ref[0, 0].astype(BF16)
        wub_ref[...] = wu_ref[0, 0].astype(BF16)
        wdb_ref[...] = wd_ref[0, 0].astype(BF16)

    @pl.when(used)
    def _():
        for c in range(SUBLANES):
            x_ref[:, c * LANES:(c + 1) * LANES] = xb_ref[pl.ds(c, EXPERT_ROWS, stride=SUBLANES), :].astype(BF16)
        x = x_ref[...]
        gate = jnp.dot(x, wgb_ref[...], preferred_element_type=F32)
        up = jnp.dot(x, wub_ref[...], preferred_element_type=F32)
        hmid = (_silu(gate) * up).astype(BF16)
        y = jnp.dot(hmid, wdb_ref[...], preferred_element_type=F32)
        for c in range(SUBLANES):
            yb_ref[pl.ds(c, EXPERT_ROWS, stride=SUBLANES), :] = y[:, c * LANES:(c + 1) * LANES]

    @pl.when(jnp.logical_not(used))
    def _():
        yb_ref[...] = jnp.zeros_like(yb_ref)


def _experts(xb, blk_exp, n_used, n_blk, layer, w_gate, w_up, w_down):
    M = EXPERT_ROWS
    tile = lambda index: pl.BlockSpec((M * SUBLANES, LANES), index)
    weight = lambda a: pl.BlockSpec((1, 1) + a.shape[2:], lambda i, be, nu: (layer, be[i], 0, 0))
    grid_spec = pltpu.PrefetchScalarGridSpec(
        num_scalar_prefetch=2, grid=(n_blk,),
        in_specs=[tile(lambda i, be, nu: (jnp.minimum(i, nu[0] - 1), 0)),
                  weight(w_gate), weight(w_up), weight(w_down)],
        out_specs=tile(lambda i, be, nu: (i, 0)),
        scratch_shapes=[pltpu.VMEM((M, D_MODEL), BF16), pltpu.VMEM((D_MODEL, MOE_HIDDEN), BF16),
                        pltpu.VMEM((D_MODEL, MOE_HIDDEN), BF16), pltpu.VMEM((MOE_HIDDEN, D_MODEL), BF16)])
    return pl.pallas_call(
        _expert_body, grid_spec=grid_spec,
        out_shape=jax.ShapeDtypeStruct((n_blk * M * SUBLANES, LANES), F32),
        compiler_params=_params("arbitrary"), name="experts")(
            blk_exp, n_used, xb, w_gate, w_up, w_down)


def _combine_body(d1_ref, d2_ref, h_ref, route_ref, yb_ref, o_ref, buf_ref, sem):
    i = pl.program_id(0)
    n = pl.num_programs(0)

    def gather(step, slot):
        base = step * MOVE_ROWS

        def body(t, _):
            _tile_copy(yb_ref, d1_ref[base + t], buf_ref.at[slot, 0], t, sem.at[slot]).start()
            _tile_copy(yb_ref, d2_ref[base + t], buf_ref.at[slot, 1], t, sem.at[slot]).start(priority=1)
            return 0

        lax.fori_loop(0, MOVE_ROWS, body, 0)

    @pl.when(i == 0)
    def _():
        gather(0, 0)

    @pl.when(i + 1 < n)
    def _():
        gather(i + 1, (i + 1) & 1)

    slot = i & 1
    pltpu.make_async_copy(buf_ref.at[slot], buf_ref.at[slot], sem.at[slot]).wait()
    route = route_ref[...]
    g1 = route[:, 2:3]
    g2 = route[:, 3:4]
    for c in range(SUBLANES):
        cs = slice(c * LANES, (c + 1) * LANES)
        rows = pl.ds(c, MOVE_ROWS, stride=SUBLANES)
        o_ref[:, cs] = h_ref[:, cs] + (buf_ref[slot, 0, rows, :] * g1 + buf_ref[slot, 1, rows, :] * g2)


def _combine(h2, route, yb, dest1, dest2):
    T = h2.shape[0]
    R = MOVE_ROWS
    grid_spec = pltpu.PrefetchScalarGridSpec(
        num_scalar_prefetch=2, grid=(T // R,),
        in_specs=[pl.BlockSpec((R, D_MODEL), lambda i, d1, d2: (i, 0)),
                  pl.BlockSpec((R, LANES), lambda i, d1, d2: (i, 0)), pl.BlockSpec(memory_space=pl.ANY)],
        out_specs=pl.BlockSpec((R, D_MODEL), lambda i, d1, d2: (i, 0)),
        scratch_shapes=[pltpu.VMEM((2, 2, R * SUBLANES, LANES), F32), pltpu.SemaphoreType.DMA((2,))])
    return pl.pallas_call(
        _combine_body, grid_spec=grid_spec, out_shape=jax.ShapeDtypeStruct((T, D_MODEL), F32),
        compiler_params=_params("arbitrary"), name="combine")(dest1, dest2, h2, route, yb)


def _moe(h2, xn, route, routet, counts, layer, w_gate, w_up, w_down):
    T = h2.shape[0]
    M = EXPERT_ROWS
    n_blk = (2 * T) // M + MOE_EXPERTS
    cnt = counts[0, ROUTER_LANE0:ROUTER_LANE0 + MOE_EXPERTS].astype(jnp.int32)
    padded = (cnt + M - 1) // M * M
    ends = jnp.cumsum(padded)
    pstarts = ends - padded
    n_used = (ends[-1] // M).astype(jnp.int32)
    blk = jnp.minimum(jnp.arange(n_blk, dtype=jnp.int32), n_used - 1) * M
    blk_exp = jnp.sum((ends[None, :] <= blk[:, None]).astype(jnp.int32), axis=1)
    blk_exp = jnp.minimum(blk_exp, MOE_EXPERTS - 1)
    dest = _slots(routet, pstarts)
    pad_info = jnp.concatenate([pstarts + cnt, n_used.reshape(1)])
    xb = _dispatch(xn, dest[0], dest[1], pad_info, n_blk * M)
    yb = _experts(xb, blk_exp, n_used.reshape(1), n_blk, layer, w_gate, w_up, w_down)
    return _combine(h2, route, yb, dest[0], dest[1])


def kernel(x, rel_bias, norm1_g, w_in, conv_w, conv_b, dt_bias, a_log, d_skip, ssd_norm_g,
           q_norm_g, k_norm_g, w_out, norm2_g, w_r1, b_r1, w_r2, b_r2, w_gate, w_up, w_down):
    B, S, D = x.shape
    T = B * S
    h2 = x.reshape(T, D)
    depth = w_in.shape[0]
    bias_tiles = _bias_tiles(rel_bias)
    for l in range(depth):
        z, xbc, dt, dtt, q, k, v = _inproj(h2, norm1_g[l], w_in[l])
        r3 = lambda a: a.reshape(B, S, a.shape[-1])
        y_ssd = _ssd(r3(z), r3(xbc), r3(dt), dtt, conv_w[l], conv_b[l], dt_bias[l], a_log[l],
                     d_skip[l], ssd_norm_g[l])
        y_att = _attention(r3(q), r3(k), r3(v), q_norm_g[l], k_norm_g[l], bias_tiles)
        h2, xn, route, routet, counts = _outproj_route(
            h2, y_ssd.reshape(T, SSD_INNER), y_att.reshape(T, ATT_INNER), w_out[l], norm2_g[l],
            w_r1[l], b_r1[l], w_r2[l], b_r2[l])
        h2 = _moe(h2, xn, route, routet, counts, l, w_gate, w_up, w_down)
    return h2.reshape(B, S, D)
```

```python
import math

import numpy as np
import jax
import jax.numpy as jnp
from jax import lax
from jax.experimental import pallas as pl
from jax.experimental.pallas import tpu as pltpu

F32 = jnp.float32
BF16 = jnp.bfloat16
HIGHEST = lax.Precision.HIGHEST

LANES = 128
SUBLANES = 8

D_MODEL = 1024
SSD_HEADS = 8
SSD_HEAD_DIM = 64
SSD_INNER = SSD_HEADS * SSD_HEAD_DIM
SSD_GROUPS = 2
SSD_HEADS_PER_GROUP = SSD_HEADS // SSD_GROUPS
SSD_STATE = 64
SSD_CONV = 4
SSD_CHUNK = 128
CONV_CH = SSD_INNER + 2 * SSD_GROUPS * SSD_STATE
ATT_HEADS = 8
ATT_HEAD_DIM = 64
ATT_INNER = ATT_HEADS * ATT_HEAD_DIM
MOBA_BLOCK = 256
MOBA_TOPK = 3
REL_BUCKETS = 32
REL_MAX_DIST = 128
MOE_GROUPS = 4
MOE_EXPERTS_PER_GROUP = 8
MOE_EXPERTS = MOE_GROUPS * MOE_EXPERTS_PER_GROUP
MOE_HIDDEN = 512
EPS = 1e-6

MASK_VALUE = -1e30
LOG2E = math.log2(math.e)
IN_ROWS = 512
SSD_STEP_CHUNKS = 2
OUT_ROWS = 512
EXPERT_ROWS = 512
MOVE_ROWS = 256
VMEM_LIMIT = 56 * 1024 * 1024
ROUTER_LANE0 = MOE_GROUPS


def _silu(x):
    return x * (1.0 / (1.0 + jnp.exp(-x)))


def _softplus(x):
    return jnp.maximum(x, 0.0) + jnp.log(1.0 + jnp.exp(-jnp.abs(x)))


def _params(*sem):
    return pltpu.CompilerParams(dimension_semantics=sem, vmem_limit_bytes=VMEM_LIMIT)


def _inproj_body(h_ref, g_ref, wz_ref, wx_ref, wdt_ref, wq_ref, wk_ref, wv_ref,
                 z_ref, xbc_ref, dt_ref, dtt_ref, q_ref, k_ref, v_ref):
    x = h_ref[...]
    xn = x * lax.rsqrt(jnp.mean(x * x, axis=-1, keepdims=True) + EPS) * g_ref[...]
    xb = xn.astype(BF16)
    z_ref[...] = jnp.dot(xb, wz_ref[...], preferred_element_type=F32)
    xbc_ref[...] = jnp.dot(xb, wx_ref[...], preferred_element_type=F32)
    dt = jnp.dot(xb, wdt_ref[...], preferred_element_type=F32)
    dt_ref[...] = dt
    dtt_ref[...] = dt.T[0:SSD_HEADS, :]
    q_ref[...] = jnp.dot(xb, wq_ref[...], preferred_element_type=F32)
    k_ref[...] = jnp.dot(xb, wk_ref[...], preferred_element_type=F32)
    v_ref[...] = jnp.dot(xb, wv_ref[...], preferred_element_type=F32)


def _inproj(h2, g, w_in):
    T = h2.shape[0]
    c0, c1, c2 = SSD_INNER, SSD_INNER + CONV_CH, SSD_INNER + CONV_CH + SSD_HEADS
    wb = w_in.astype(BF16)
    wz, wx, wdt = wb[:, :c0], wb[:, c0:c1], wb[:, c1:c2]
    wq, wk, wv = (wb[:, c2 + i * ATT_INNER:c2 + (i + 1) * ATT_INNER] for i in range(3))
    wdt_pad = jnp.pad(wdt, ((0, 0), (0, LANES - SSD_HEADS)))
    full = lambda a: pl.BlockSpec(a.shape, lambda i: (0,) * a.ndim)
    rows = lambda n: pl.BlockSpec((IN_ROWS, n), lambda i: (i, 0))
    ins = (h2, g.reshape(1, D_MODEL), wz, wx, wdt_pad, wq, wk, wv)
    out_shape = (
        jax.ShapeDtypeStruct((T, SSD_INNER), F32), jax.ShapeDtypeStruct((T, CONV_CH), F32),
        jax.ShapeDtypeStruct((T, LANES), F32), jax.ShapeDtypeStruct((SSD_HEADS, T), F32),
        jax.ShapeDtypeStruct((T, ATT_INNER), F32), jax.ShapeDtypeStruct((T, ATT_INNER), F32),
        jax.ShapeDtypeStruct((T, ATT_INNER), F32))
    out_specs = (rows(SSD_INNER), rows(CONV_CH), rows(LANES),
                 pl.BlockSpec((SSD_HEADS, IN_ROWS), lambda i: (0, i)),
                 rows(ATT_INNER), rows(ATT_INNER), rows(ATT_INNER))
    return pl.pallas_call(
        _inproj_body, grid=(T // IN_ROWS,),
        in_specs=[rows(D_MODEL)] + [full(a) for a in ins[1:]],
        out_specs=out_specs, out_shape=out_shape,
        compiler_params=_params("arbitrary"), name="inproj")(*ins)


def _ssd_body(z_ref, xbc_ref, dt_ref, dtt_ref, cw_ref, cb_ref, dtb_ref, dtbt_ref, alog_ref,
              alogt_ref, dskip_ref, ng_ref, tril_ref, expand_ref, y_ref, ext_ref, state_ref):
    L, P, N, E = SSD_CHUNK, SSD_HEAD_DIM, SSD_STATE, SSD_HEADS_PER_GROUP
    R = SSD_STEP_CHUNKS * L
    c = pl.program_id(1)

    @pl.when(c == 0)
    def _():
        state_ref[...] = jnp.zeros_like(state_ref)
        ext_ref[0:SUBLANES, :] = jnp.zeros((SUBLANES, CONV_CH), F32)

    ext_ref[SUBLANES:SUBLANES + R, :] = xbc_ref[0]
    tril = tril_ref[...]
    expand = expand_ref[...]
    row = lax.broadcasted_iota(jnp.int32, (L, L), 0)
    col = lax.broadcasted_iota(jnp.int32, (L, L), 1)
    causal = row >= col
    for sub in range(SSD_STEP_CHUNKS):
        _ssd_chunk(sub * L, z_ref, dt_ref, dtt_ref, cw_ref, cb_ref, dtb_ref, dtbt_ref, alog_ref,
                   alogt_ref, dskip_ref, ng_ref, tril, expand, causal, y_ref, ext_ref, state_ref)
    ext_ref[0:SUBLANES, :] = ext_ref[R:R + SUBLANES, :]


def _ssd_chunk(r0, z_ref, dt_ref, dtt_ref, cw_ref, cb_ref, dtb_ref, dtbt_ref, alog_ref, alogt_ref,
               dskip_ref, ng_ref, tril, expand, causal, y_ref, ext_ref, state_ref):
    L, P, N, E = SSD_CHUNK, SSD_HEAD_DIM, SSD_STATE, SSD_HEADS_PER_GROUP
    conv = jnp.zeros((L, CONV_CH), F32) + cb_ref[...]
    for j in range(SSD_CONV):
        off = r0 + SUBLANES - (SSD_CONV - 1) + j
        conv = conv + cw_ref[j:j + 1, :] * ext_ref[off:off + L, :]
    act = _silu(conv)
    xs = act[:, :SSD_INNER]
    b_in = act[:, SSD_INNER:SSD_INNER + SSD_GROUPS * N]
    c_out = act[:, SSD_INNER + SSD_GROUPS * N:]

    dt = _softplus(dt_ref[0, r0:r0 + L, :] + dtb_ref[...])
    a_dt = dt * (-jnp.exp(alog_ref[...]))
    a_cs = jnp.dot(tril, a_dt, precision=HIGHEST, preferred_element_type=F32)
    dt_t = _softplus(dtt_ref[:, r0:r0 + L] + dtbt_ref[...])
    a_dt_t = dt_t * (-jnp.exp(alogt_ref[...]))
    a_cs_t = lax.dot_general(a_dt_t, tril, (((1,), (1,)), ((), ())), precision=HIGHEST,
                             preferred_element_type=F32)
    last = a_cs[L - 1:L, :]
    per_head = jnp.concatenate(
        [dt, jnp.exp(a_cs), jnp.exp(last - a_cs), jnp.broadcast_to(jnp.exp(last), (SUBLANES, LANES))],
        axis=0)
    hi = per_head.astype(BF16)
    lo = (per_head - hi.astype(F32)).astype(BF16)
    on_lanes = jnp.dot(jnp.concatenate([hi, lo], axis=-1), expand, preferred_element_type=F32)
    dt_x = on_lanes[0:L]
    eacs_x = on_lanes[L:2 * L]
    dec_x = on_lanes[2 * L:3 * L]
    cdec_x = on_lanes[3 * L:3 * L + 1]
    xdt = xs * dt_x
    xdtd = (xdt * dec_x).astype(BF16)
    xdt_b = xdt.astype(BF16)

    y_parts = []
    for g in range(SSD_GROUPS):
        bg = b_in[:, g * N:(g + 1) * N].astype(BF16)
        cg = c_out[:, g * N:(g + 1) * N].astype(BF16)
        cb = lax.dot_general(cg, bg, (((1,), (1,)), ((), ())), preferred_element_type=F32)
        st = state_ref[g]
        gs = slice(g * E * P, (g + 1) * E * P)
        y_off = jnp.dot(cg, st.astype(BF16), preferred_element_type=F32) * eacs_x[:, gs]
        for e in range(E):
            h = g * E + e
            diff = a_cs[:, h:h + 1] - a_cs_t[h:h + 1, :]
            l_mat = jnp.exp(jnp.where(causal, diff, -jnp.inf))
            m = (cb * l_mat).astype(BF16)
            y_diag = jnp.dot(m, xdt_b[:, h * P:(h + 1) * P], preferred_element_type=F32)
            y_parts.append(y_diag + y_off[:, e * P:(e + 1) * P])
        upd = lax.dot_general(bg, xdtd[:, gs], (((0,), (0,)), ((), ())),
                              preferred_element_type=F32)
        state_ref[g] = cdec_x[:, gs] * st + upd
    y = jnp.concatenate(y_parts, axis=-1) + dskip_ref[...] * xs
    y = y * _silu(z_ref[0, r0:r0 + L, :])
    y = y * lax.rsqrt(jnp.mean(y * y, axis=-1, keepdims=True) + EPS) * ng_ref[...]
    y_ref[0, r0:r0 + L, :] = y


def _ssd(z, xbc, dt, dtt, conv_w, conv_b, dt_bias, a_log, d_skip, norm_g):
    B, S, _ = z.shape
    L = SSD_CHUNK
    R = SSD_STEP_CHUNKS * L
    nc = S // R
    pad_row = lambda a: jnp.pad(a.reshape(1, SSD_HEADS), ((0, 0), (0, LANES - SSD_HEADS)))
    tril = jnp.asarray(np.tril(np.ones((L, L), np.float32)))
    expand_np = np.zeros((LANES, SSD_INNER), np.float32)
    for h in range(SSD_HEADS):
        expand_np[h, h * SSD_HEAD_DIM:(h + 1) * SSD_HEAD_DIM] = 1.0
    ins = (z, xbc, dt, dtt, conv_w, conv_b.reshape(1, CONV_CH), pad_row(dt_bias),
           dt_bias.reshape(SSD_HEADS, 1), pad_row(a_log), a_log.reshape(SSD_HEADS, 1),
           jnp.repeat(d_skip, SSD_HEAD_DIM).reshape(1, SSD_INNER), norm_g.reshape(1, SSD_INNER),
           tril, jnp.asarray(np.concatenate([expand_np, expand_np], axis=0), dtype=BF16))
    full = lambda a: pl.BlockSpec(a.shape, lambda b, c: (0,) * a.ndim)
    chunk = lambda n: pl.BlockSpec((1, R, n), lambda b, c: (b, c, 0))
    in_specs = [chunk(SSD_INNER), chunk(CONV_CH), chunk(LANES),
                pl.BlockSpec((SSD_HEADS, R), lambda b, c: (0, b * nc + c))]
    in_specs += [full(a) for a in ins[4:]]
    return pl.pallas_call(
        _ssd_body, grid=(B, nc), in_specs=in_specs, out_specs=chunk(SSD_INNER),
        out_shape=jax.ShapeDtypeStruct((B, S, SSD_INNER), F32),
        scratch_shapes=[pltpu.VMEM((R + SUBLANES, CONV_CH), F32),
                        pltpu.VMEM((SSD_GROUPS, SSD_STATE, SSD_HEADS_PER_GROUP * SSD_HEAD_DIM), F32)],
        compiler_params=_params("arbitrary", "arbitrary"), name="ssd")(*ins)


def _bucket_tile():
    bs = MOBA_BLOCK
    dist = np.arange(bs)[:, None] - np.arange(2 * bs)[None, :] + bs
    max_exact = REL_BUCKETS // 2
    d = np.maximum(dist, max_exact).astype(np.float32)
    large = max_exact + (np.log(d / np.float32(max_exact)) / np.float32(math.log(REL_MAX_DIST / max_exact))
                         * np.float32(REL_BUCKETS - max_exact)).astype(np.int32)
    large = np.minimum(large, REL_BUCKETS - 1)
    bucket = np.where(dist < max_exact, dist, large)
    return np.where(dist >= 0, bucket, -1).astype(np.int32)


def _bias_body(rb_ref, bucket_ref, o_ref):
    head = pl.program_id(0)
    bucket = bucket_ref[...]
    far = rb_ref[REL_BUCKETS - 1, head]
    bias = jnp.where(bucket < 0, MASK_VALUE, 0.0)
    for i in range(REL_BUCKETS - 1):
        bias = jnp.where(bucket == i, (rb_ref[i, head] - far) * LOG2E, bias)
    o_ref[0] = bias


def _bias_tiles(rel_bias):
    bs = MOBA_BLOCK
    bucket = jnp.asarray(_bucket_tile())
    return pl.pallas_call(
        _bias_body, grid=(ATT_HEADS,),
        in_specs=[pl.BlockSpec(memory_space=pltpu.SMEM), pl.BlockSpec((bs, 2 * bs), lambda h: (0, 0))],
        out_specs=pl.BlockSpec((1, bs, 2 * bs), lambda h: (h, 0, 0)),
        out_shape=jax.ShapeDtypeStruct((ATT_HEADS, bs, 2 * bs), F32),
        compiler_params=_params("arbitrary"), name="bias_tiles")(rel_bias, bucket)


def _attn_body(q_ref, k_ref, v_ref, qg_ref, kg_ref, bias_ref, o_ref,
               kaug_ref, vaug_ref, qaug_ref, sa_ref, sb_ref, m_ref, acc_ref):
    bs, hd = MOBA_BLOCK, ATT_HEAD_DIM
    nb = k_ref.shape[1] // bs
    jp = pl.program_id(2)
    nt_dims = (((1,), (1,)), ((), ()))

    @pl.when(jp == 0)
    def _():
        S = nb * bs
        pair = 2 * hd
        r = lax.broadcasted_iota(jnp.int32, (pair, pair), 0) // hd
        c = lax.broadcasted_iota(jnp.int32, (pair, pair), 1) // hd
        same_head = (r == c).astype(BF16)

        def head_norm(x, gain):
            ms = jnp.dot((x * x).astype(BF16), same_head, preferred_element_type=F32) * (1.0 / hd)
            return x * lax.rsqrt(ms + EPS) * gain

        kn = head_norm(k_ref[0], kg_ref[...])
        kmean = jnp.mean(kn.reshape(nb, bs, pair), axis=1)
        kn_b = kn.astype(BF16)
        v_b = v_ref[0].astype(BF16)
        lane = lax.broadcasted_iota(jnp.int32, (S, pair), 1)
        blk = lax.broadcasted_iota(jnp.int32, (S, pair), 0) // bs
        lane_m = lax.broadcasted_iota(jnp.int32, (nb, pair), 1)
        rowb = lax.broadcasted_iota(jnp.int32, (nb, bs), 0)
        lane_q = lax.broadcasted_iota(jnp.int32, (bs, pair), 1)
        kmean_b = []
        for h in range(2):
            own = (lane // hd) == h
            other0 = (1 - h) * hd
            kaug_ref[h] = jnp.where(own, kn_b, (lane - other0 == blk).astype(BF16))
            vaug_ref[h] = jnp.where(own, v_b, jnp.ones((S, pair), BF16))
            kmean_b.append(jnp.where((lane_m // hd) == h, kmean, 0.0).astype(BF16))
        for j in range(nb):
            rows = slice(j * bs, (j + 1) * bs)
            qn = head_norm(q_ref[0, rows, :], qg_ref[...])
            qn_b = qn.astype(BF16)
            qs = qn * (ATT_HEAD_DIM ** -0.5 * LOG2E)
            for h in range(2):
                if j <= MOBA_TOPK:
                    sel = rowb <= j
                else:
                    gate = lax.dot_general(kmean_b[h], qn_b, nt_dims,
                                           preferred_element_type=F32)
                    rank = jnp.zeros((nb, bs), jnp.int32)
                    for m in range(j):
                        gm = gate[m:m + 1, :]
                        beats = (gm > gate) | ((gm == gate) & (rowb > m))
                        rank = rank + beats.astype(jnp.int32)
                    sel = ((rowb < j) & (rank < MOBA_TOPK)) | (rowb == j)
                selmask = jnp.where(sel, 0.0, MASK_VALUE)
                pieces = [selmask, jnp.zeros((pair - nb, bs), F32)]
                if h == 0:
                    pieces = [jnp.zeros((hd, bs), F32), selmask, jnp.zeros((hd - nb, bs), F32)]
                mask_t = jnp.concatenate(pieces, axis=0).T
                qaug_ref[h, rows, :] = jnp.where((lane_q // hd) == h, qs, mask_t).astype(BF16)

    qt = 2 * bs
    q0 = pl.multiple_of(jp * qt, qt)
    all_rows = slice(0, qt)
    late_rows = slice(bs, qt)
    for h in range(2):
        m_ref[h] = jnp.full((qt, 2 * hd), -jnp.inf, F32)
        acc_ref[h] = jnp.zeros((qt, 2 * hd), F32)

    def scores(n, s_ref, rows=all_rows):
        start = pl.multiple_of(n * bs, bs)
        q_rows = pl.ds(q0 + rows.start, rows.stop - rows.start)
        for h in range(2):
            s_ref[h, rows, :] = lax.dot_general(qaug_ref[h, q_rows, :], kaug_ref[h, pl.ds(start, bs), :],
                                                nt_dims, preferred_element_type=F32)

    def update(n, s_ref, bias, rows=all_rows):
        start = pl.multiple_of(n * bs, bs)
        for r0 in range(rows.start, rows.stop, bs):
            sub = slice(r0, r0 + bs)
            for h in range(2):
                s = s_ref[h, sub, :]
                tile = None if bias is None else bias(h, r0 // bs)
                if tile is not None:
                    s = s + tile
                m_i = m_ref[h, sub, :]
                m_new = jnp.maximum(m_i, jnp.max(s, axis=-1, keepdims=True))
                alpha = jnp.exp2(m_i - m_new)
                p = jnp.exp2((s - jnp.concatenate([m_new, m_new], axis=-1)).astype(BF16))
                pv = jnp.dot(p, vaug_ref[h, pl.ds(start, bs), :], preferred_element_type=F32)
                acc_ref[h, sub, :] = alpha * acc_ref[h, sub, :] + pv
                m_ref[h, sub, :] = m_new

    def far_pair(n):
        scores(n + 1, sb_ref)
        update(n, sa_ref, None)
        scores(n + 2, sa_ref)
        update(n + 1, sb_ref, None)

    def far_quad(i, _):
        far_pair(4 * i)
        far_pair(4 * i + 2)
        return 0

    n_far = jnp.maximum(2 * jp - 1, 0)
    scores(0, sa_ref)
    lax.fori_loop(0, n_far >> 2, far_quad, 0)

    @pl.when((n_far & 2) != 0)
    def _():
        far_pair(n_far & ~3)

    prev = lambda h: bias_ref[h, :, 0:bs]
    own = lambda h, qblk=1: bias_ref[h, :, bs:2 * bs]
    bias_p = lambda h, qblk: prev(h) if qblk == 0 else None
    bias_o = lambda h, qblk: own(h) if qblk == 0 else prev(h)
    blk_p = jnp.maximum(2 * jp - 1, 0)
    blk_o = 2 * jp
    blk_n = 2 * jp + 1

    @pl.when(jp > 0)
    def _():
        scores(blk_p, sb_ref)
        update(n_far - 1, sa_ref, None)
        scores(blk_o, sa_ref)
        update(blk_p, sb_ref, bias_p)
        scores(blk_n, sb_ref, late_rows)
        update(blk_o, sa_ref, bias_o)
        update(blk_n, sb_ref, own, late_rows)

    @pl.when(jp == 0)
    def _():
        scores(blk_n, sb_ref, late_rows)
        update(blk_o, sa_ref, bias_o)
        update(blk_n, sb_ref, own, late_rows)

    ratio = [acc_ref[h] / pltpu.roll(acc_ref[h], hd, 1) for h in range(2)]
    lane_o = lax.broadcasted_iota(jnp.int32, (qt, 2 * hd), 1)
    o_ref[0] = jnp.where(lane_o < hd, ratio[0], ratio[1])


def _attention(q, k, v, q_g, k_g, bias_tiles):
    B, S, _ = q.shape
    bs = MOBA_BLOCK
    nb = S // bs
    assert S % (2 * bs) == 0 and nb <= ATT_HEAD_DIM
    pair = 2 * ATT_HEAD_DIM
    return pl.pallas_call(
        _attn_body, grid=(B, ATT_HEADS // 2, nb // 2),
        in_specs=[
            pl.BlockSpec((1, S, pair), lambda b, hp, i: (b, 0, hp)),
            pl.BlockSpec((1, S, pair), lambda b, hp, i: (b, 0, hp)),
            pl.BlockSpec((1, S, pair), lambda b, hp, i: (b, 0, hp)),
            pl.BlockSpec((1, pair), lambda b, hp, i: (0, 0)),
            pl.BlockSpec((1, pair), lambda b, hp, i: (0, 0)),
            pl.BlockSpec((2, bs, 2 * bs), lambda b, hp, i: (hp, 0, 0)),
        ],
        out_specs=pl.BlockSpec((1, 2 * bs, pair), lambda b, hp, i: (b, i, hp)),
        out_shape=jax.ShapeDtypeStruct((B, S, ATT_INNER), F32),
        scratch_shapes=[pltpu.VMEM((2, S, pair), BF16), pltpu.VMEM((2, S, pair), BF16),
                        pltpu.VMEM((2, S, pair), BF16),
                        pltpu.VMEM((2, 2 * bs, bs), F32), pltpu.VMEM((2, 2 * bs, bs), F32),
                        pltpu.VMEM((2, 2 * bs, pair), F32),
                        pltpu.VMEM((2, 2 * bs, pair), F32)],
        compiler_params=_params("arbitrary", "arbitrary", "arbitrary"), name="moba")(
            q, k, v, jnp.tile(q_g, 2).reshape(1, pair), jnp.tile(k_g, 2).reshape(1, pair), bias_tiles)


def _outproj_body(h_ref, ys_ref, ya_ref, wa_ref, wb_ref, g_ref, wrh_ref, wrl_ref, br_ref, tri_ref,
                  hn_ref, xn_ref, route_ref, routet_ref, counts_ref, carry_ref):
    i = pl.program_id(0)

    @pl.when(i == 0)
    def _():
        carry_ref[...] = jnp.zeros_like(carry_ref)

    _outproj_rows(0, OUT_ROWS, h_ref, ys_ref, ya_ref, wa_ref, wb_ref, g_ref, wrh_ref, wrl_ref,
                  br_ref, tri_ref, hn_ref, xn_ref, route_ref, routet_ref, carry_ref)
    counts_ref[...] = carry_ref[...]


def _outproj_rows(r0, n, h_ref, ys_ref, ya_ref, wa_ref, wb_ref, g_ref, wrh_ref, wrl_ref, br_ref, tri_ref,
                  hn_ref, xn_ref, route_ref, routet_ref, carry_ref):
    rs = slice(r0, r0 + n)
    hn = (h_ref[rs, :]
          + jnp.dot(ys_ref[rs, :].astype(BF16), wa_ref[...], preferred_element_type=F32)
          + jnp.dot(ya_ref[rs, :].astype(BF16), wb_ref[...], preferred_element_type=F32))
    hn_ref[rs, :] = hn
    xn = hn * lax.rsqrt(jnp.mean(hn * hn, axis=-1, keepdims=True) + EPS) * g_ref[...]
    for c in range(SUBLANES):
        xn_ref[pl.ds(r0 * SUBLANES + c, n, stride=SUBLANES), :] = xn[:, c * LANES:(c + 1) * LANES]
    x_hi = xn.astype(BF16)
    x_lo = (xn - x_hi.astype(F32)).astype(BF16)
    hi_terms = jnp.dot(x_hi, wrl_ref[...], preferred_element_type=F32)
    logits = (hi_terms[:, :LANES] + hi_terms[:, LANES:]
              + jnp.dot(x_lo, wrh_ref[...], preferred_element_type=F32)) + br_ref[...]
    lane = lax.broadcasted_iota(jnp.int32, (n, LANES), 1)
    first = lambda hit: jnp.min(jnp.where(hit, lane, LANES), axis=-1, keepdims=True)
    l1 = jnp.where(lane < MOE_GROUPS, logits, -jnp.inf)
    mx = jnp.max(l1, axis=-1, keepdims=True)
    gval = 1.0 / jnp.sum(jnp.exp(l1 - mx), axis=-1, keepdims=True)
    gidx = first(l1 == mx)
    e_lane = lane - ROUTER_LANE0
    in_grp = (e_lane >= 0) & (e_lane < MOE_EXPERTS) & ((e_lane >> 3) == gidx)
    l2 = jnp.where(in_grp, logits, -jnp.inf)
    m1 = jnp.max(l2, axis=-1, keepdims=True)
    i1 = first(l2 == m1)
    l2b = jnp.where(lane == i1, -jnp.inf, l2)
    m2 = jnp.max(l2b, axis=-1, keepdims=True)
    i2 = first(l2b == m2)
    r = jnp.exp(m2 - m1)
    g1 = gval / (1.0 + r)
    g2 = gval * r / (1.0 + r)
    oh1 = (lane == i1)
    oh2 = (lane == i2)
    both = jnp.where(oh1 | oh2, 1.0, 0.0)
    cx = jnp.dot(tri_ref[...], both.astype(BF16), preferred_element_type=F32) + carry_ref[...]
    rank1 = jnp.sum(jnp.where(oh1, cx, 0.0), axis=-1, keepdims=True)
    rank2 = jnp.sum(jnp.where(oh2, cx, 0.0), axis=-1, keepdims=True)
    carry_ref[...] = carry_ref[...] + jnp.sum(both, axis=0, keepdims=True)
    cols = ((i1 - ROUTER_LANE0).astype(F32), (i2 - ROUTER_LANE0).astype(F32), g1, g2, rank1, rank2)
    route = jnp.zeros((n, LANES), F32)
    for j, cval in enumerate(cols):
        route = jnp.where(lane == j, cval, route)
    route_ref[rs, :] = route
    routet_ref[:, rs] = route.T[0:SUBLANES, :]


def _outproj_route(h2, y_ssd, y_att, w_out, norm_g, w_r1, b_r1, w_r2, b_r2):
    T = h2.shape[0]
    R = OUT_ROWS
    wb = w_out.astype(BF16)
    pad = LANES - MOE_GROUPS - MOE_EXPERTS
    wr = jnp.pad(jnp.concatenate([w_r1, w_r2], axis=1), ((0, 0), (0, pad)))
    wr_hi = wr.astype(BF16)
    wr_lo = (wr - wr_hi.astype(F32)).astype(BF16)
    br = jnp.pad(jnp.concatenate([b_r1, b_r2]), (0, pad)).reshape(1, LANES)
    tri = jnp.asarray(np.tril(np.ones((R, R), np.float32), -1), dtype=BF16)
    ins = (h2, y_ssd, y_att, wb[:SSD_INNER], wb[SSD_INNER:], norm_g.reshape(1, D_MODEL),
           wr_hi, jnp.concatenate([wr_hi, wr_lo], axis=1), br, tri)
    full = lambda a: pl.BlockSpec(a.shape, lambda i: (0,) * a.ndim)
    rows = lambda n: pl.BlockSpec((R, n), lambda i: (i, 0))
    return pl.pallas_call(
        _outproj_body, grid=(T // R,),
        in_specs=[rows(D_MODEL), rows(SSD_INNER), rows(ATT_INNER)] + [full(a) for a in ins[3:]],
        out_specs=(rows(D_MODEL), pl.BlockSpec((R * SUBLANES, LANES), lambda i: (i, 0)), rows(LANES),
                   pl.BlockSpec((SUBLANES, R), lambda i: (0, i)), pl.BlockSpec((1, LANES), lambda i: (0, 0))),
        out_shape=(jax.ShapeDtypeStruct((T, D_MODEL), F32),
                   jax.ShapeDtypeStruct((T * SUBLANES, LANES), F32),
                   jax.ShapeDtypeStruct((T, LANES), F32), jax.ShapeDtypeStruct((SUBLANES, T), F32),
                   jax.ShapeDtypeStruct((1, LANES), F32)),
        scratch_shapes=[pltpu.VMEM((1, LANES), F32)],
        compiler_params=_params("arbitrary"), name="outproj_route")(*ins)


def _slots_body(routet_ref, pstart_ref, dest_ref):
    cols = routet_ref.shape[1]
    expert = lax.broadcasted_iota(jnp.int32, (MOE_EXPERTS, cols), 0).astype(F32)
    pstart = pstart_ref[...]
    for j in range(2):
        hit = expert == routet_ref[j:j + 1, :]
        base = jnp.sum(jnp.where(hit, pstart, 0.0), axis=0, keepdims=True)
        dest_ref[j:j + 1, :] = (base + routet_ref[4 + j:5 + j, :]).astype(jnp.int32)


def _slots(routet, pstarts):
    T = routet.shape[1]
    cols = min(T, 4096)
    assert T % cols == 0
    return pl.pallas_call(
        _slots_body, grid=(T // cols,),
        in_specs=[pl.BlockSpec((SUBLANES, cols), lambda i: (0, i)),
                  pl.BlockSpec((MOE_EXPERTS, 1), lambda i: (0, 0))],
        out_specs=pl.BlockSpec((2, cols), lambda i: (0, i)),
        out_shape=jax.ShapeDtypeStruct((2, T), jnp.int32),
        compiler_params=_params("arbitrary"), name="slots")(
            routet, pstarts.astype(F32).reshape(MOE_EXPERTS, 1))


def _tile_copy(src_ref, src_row, dst_ref, dst_row, sem):
    tile = lambda row: pl.ds(pl.multiple_of(row * SUBLANES, SUBLANES), SUBLANES)
    return pltpu.make_async_copy(src_ref.at[tile(src_row)], dst_ref.at[tile(dst_row)], sem)


def _dispatch_body(pad_ref, d1_ref, d2_ref, xn_ref, xb_ref, zero_ref, stage_ref, sem, lsem, ssem):
    block = EXPERT_ROWS * SUBLANES
    n_blocks = xb_ref.shape[0] // block
    n_used = pad_ref[MOE_EXPERTS]

    def zero_copy(row):
        start = pl.multiple_of(row * SUBLANES, SUBLANES)
        return pltpu.make_async_copy(zero_ref, xb_ref.at[pl.ds(start, block)], sem)

    @pl.when(pl.program_id(0) == 0)
    def _():
        zero_ref[...] = jnp.zeros_like(zero_ref)
        for e in range(MOE_EXPERTS):
            zero_copy(pad_ref[e]).start()
        for e in range(MOE_EXPERTS):
            zero_copy(pad_ref[e]).wait()
        for b in range(n_blocks - MOE_EXPERTS - 1, n_blocks):
            @pl.when(b >= n_used)
            def _():
                zero_copy(b * EXPERT_ROWS).start()
        for b in range(n_blocks - MOE_EXPERTS - 1, n_blocks):
            @pl.when(b >= n_used)
            def _():
                zero_copy(b * EXPERT_ROWS).wait()

    i = pl.program_id(0)
    n = pl.num_programs(0)
    rows = MOVE_ROWS * SUBLANES

    def load(step, slot):
        src = xn_ref.at[pl.ds(pl.multiple_of(step * rows, rows), rows)]
        return pltpu.make_async_copy(src, stage_ref.at[slot], lsem.at[slot])

    def wait_scatters(slot):
        for _ in range(2):
            pltpu.make_async_copy(stage_ref.at[slot], stage_ref.at[slot], ssem.at[slot]).wait()

    @pl.when(i == 0)
    def _():
        load(0, 0).start()

    @pl.when(i >= 1)
    def _():
        wait_scatters((i - 1) & 1)

    @pl.when(i + 1 < n)
    def _():
        load(i + 1, (i + 1) & 1).start()

    slot = i & 1
    load(i, slot).wait()
    base = i * MOVE_ROWS

    def scatter(t, _):
        _tile_copy(stage_ref.at[slot], t, xb_ref, d1_ref[base + t], ssem.at[slot]).start()
        _tile_copy(stage_ref.at[slot], t, xb_ref, d2_ref[base + t], ssem.at[slot]).start(priority=1)
        return 0

    lax.fori_loop(0, MOVE_ROWS, scatter, 0)

    @pl.when(i == n - 1)
    def _():
        wait_scatters(slot)


def _dispatch(xn, dest1, dest2, pad_start, n_slots):
    T = dest1.shape[0]
    grid_spec = pltpu.PrefetchScalarGridSpec(
        num_scalar_prefetch=3, grid=(T // MOVE_ROWS,),
        in_specs=[pl.BlockSpec(memory_space=pl.ANY)],
        out_specs=pl.BlockSpec(memory_space=pl.ANY),
        scratch_shapes=[pltpu.VMEM((EXPERT_ROWS * SUBLANES, LANES), F32),
                        pltpu.VMEM((2, MOVE_ROWS * SUBLANES, LANES), F32),
                        pltpu.SemaphoreType.DMA(()), pltpu.SemaphoreType.DMA((2,)),
                        pltpu.SemaphoreType.DMA((2,))])
    return pl.pallas_call(
        _dispatch_body, grid_spec=grid_spec,
        out_shape=jax.ShapeDtypeStruct(((n_slots + EXPERT_ROWS) * SUBLANES, LANES), F32),
        compiler_params=_params("arbitrary"), name="dispatch")(pad_start, dest1, dest2, xn)


def _expert_body(be_ref, nused_ref, xb_ref, wg_ref, wu_ref, wd_ref, yb_ref,
                 x_ref, wgb_ref, wub_ref, wdb_ref):
    i = pl.program_id(0)
    used = i < nused_ref[0]
    new_expert = (i == 0) | (be_ref[i] != be_ref[jnp.maximum(i - 1, 0)])

    @pl.when(used & new_expert)
    def _():
        wgb_ref[...] = wg_ref[0, 0].astype(BF16)
        wub_ref[...] = wu_ref[0, 0].astype(BF16)
        wdb_ref[...] = wd_ref[0, 0].astype(BF16)

    @pl.when(used)
    def _():
        for c in range(SUBLANES):
            x_ref[:, c * LANES:(c + 1) * LANES] = xb_ref[pl.ds(c, EXPERT_ROWS, stride=SUBLANES), :].astype(BF16)
        x = x_ref[...]
        gate = jnp.dot(x, wgb_ref[...], preferred_element_type=F32)
        up = jnp.dot(x, wub_ref[...], preferred_element_type=F32)
        hmid = (_silu(gate) * up).astype(BF16)
        y = jnp.dot(hmid, wdb_ref[...], preferred_element_type=F32)
        for c in range(SUBLANES):
            yb_ref[pl.ds(c, EXPERT_ROWS, stride=SUBLANES), :] = y[:, c * LANES:(c + 1) * LANES]

    @pl.when(jnp.logical_not(used))
    def _():
        yb_ref[...] = jnp.zeros_like(yb_ref)


def _experts(xb, blk_exp, n_used, n_blk, layer, w_gate, w_up, w_down):
    M = EXPERT_ROWS
    tile = lambda index: pl.BlockSpec((M * SUBLANES, LANES), index)
    weight = lambda a: pl.BlockSpec((1, 1) + a.shape[2:], lambda i, be, nu: (layer, be[i], 0, 0))
    grid_spec = pltpu.PrefetchScalarGridSpec(
        num_scalar_prefetch=2, grid=(n_blk,),
        in_specs=[tile(lambda i, be, nu: (jnp.minimum(i, nu[0] - 1), 0)),
                  weight(w_gate), weight(w_up), weight(w_down)],
        out_specs=tile(lambda i, be, nu: (i, 0)),
        scratch_shapes=[pltpu.VMEM((M, D_MODEL), BF16), pltpu.VMEM((D_MODEL, MOE_HIDDEN), BF16),
                        pltpu.VMEM((D_MODEL, MOE_HIDDEN), BF16), pltpu.VMEM((MOE_HIDDEN, D_MODEL), BF16)])
    return pl.pallas_call(
        _expert_body, grid_spec=grid_spec,
        out_shape=jax.ShapeDtypeStruct((n_blk * M * SUBLANES, LANES), F32),
        compiler_params=_params("arbitrary"), name="experts")(
            blk_exp, n_used, xb, w_gate, w_up, w_down)


def _combine_body(d1_ref, d2_ref, h_ref, route_ref, yb_ref, o_ref, buf_ref, sem):
    i = pl.program_id(0)
    n = pl.num_programs(0)

    def gather(step, slot):
        base = step * MOVE_ROWS

        def body(t, _):
            _tile_copy(yb_ref, d1_ref[base + t], buf_ref.at[slot, 0], t, sem.at[slot]).start()
            _tile_copy(yb_ref, d2_ref[base + t], buf_ref.at[slot, 1], t, sem.at[slot]).start(priority=1)
            return 0

        lax.fori_loop(0, MOVE_ROWS, body, 0)

    @pl.when(i == 0)
    def _():
        gather(0, 0)

    @pl.when(i + 1 < n)
    def _():
        gather(i + 1, (i + 1) & 1)

    slot = i & 1
    pltpu.make_async_copy(buf_ref.at[slot], buf_ref.at[slot], sem.at[slot]).wait()
    route = route_ref[...]
    g1 = route[:, 2:3]
    g2 = route[:, 3:4]
    for c in range(SUBLANES):
        cs = slice(c * LANES, (c + 1) * LANES)
        rows = pl.ds(c, MOVE_ROWS, stride=SUBLANES)
        o_ref[:, cs] = h_ref[:, cs] + (buf_ref[slot, 0, rows, :] * g1 + buf_ref[slot, 1, rows, :] * g2)


def _combine(h2, route, yb, dest1, dest2):
    T = h2.shape[0]
    R = MOVE_ROWS
    grid_spec = pltpu.PrefetchScalarGridSpec(
        num_scalar_prefetch=2, grid=(T // R,),
        in_specs=[pl.BlockSpec((R, D_MODEL), lambda i, d1, d2: (i, 0)),
                  pl.BlockSpec((R, LANES), lambda i, d1, d2: (i, 0)), pl.BlockSpec(memory_space=pl.ANY)],
        out_specs=pl.BlockSpec((R, D_MODEL), lambda i, d1, d2: (i, 0)),
        scratch_shapes=[pltpu.VMEM((2, 2, R * SUBLANES, LANES), F32), pltpu.SemaphoreType.DMA((2,))])
    return pl.pallas_call(
        _combine_body, grid_spec=grid_spec, out_shape=jax.ShapeDtypeStruct((T, D_MODEL), F32),
        compiler_params=_params("arbitrary"), name="combine")(dest1, dest2, h2, route, yb)


def _moe(h2, xn, route, routet, counts, layer, w_gate, w_up, w_down):
    T = h2.shape[0]
    M = EXPERT_ROWS
    n_blk = (2 * T) // M + MOE_EXPERTS
    cnt = counts[0, ROUTER_LANE0:ROUTER_LANE0 + MOE_EXPERTS].astype(jnp.int32)
    padded = (cnt + M - 1) // M * M
    ends = jnp.cumsum(padded)
    pstarts = ends - padded
    n_used = (ends[-1] // M).astype(jnp.int32)
    blk = jnp.minimum(jnp.arange(n_blk, dtype=jnp.int32), n_used - 1) * M
    blk_exp = jnp.sum((ends[None, :] <= blk[:, None]).astype(jnp.int32), axis=1)
    blk_exp = jnp.minimum(blk_exp, MOE_EXPERTS - 1)
    dest = _slots(routet, pstarts)
    pad_info = jnp.concatenate([pstarts + cnt, n_used.reshape(1)])
    xb = _dispatch(xn, dest[0], dest[1], pad_info, n_blk * M)
    yb = _experts(xb, blk_exp, n_used.reshape(1), n_blk, layer, w_gate, w_up, w_down)
    return _combine(h2, route, yb, dest[0], dest[1])


def kernel(x, rel_bias, norm1_g, w_in, conv_w, conv_b, dt_bias, a_log, d_skip, ssd_norm_g,
           q_norm_g, k_norm_g, w_out, norm2_g, w_r1, b_r1, w_r2, b_r2, w_gate, w_up, w_down):
    B, S, D = x.shape
    T = B * S
    h2 = x.reshape(T, D)
    depth = w_in.shape[0]
    bias_tiles = _bias_tiles(rel_bias)
    for l in range(depth):
        z, xbc, dt, dtt, q, k, v = _inproj(h2, norm1_g[l], w_in[l])
        r3 = lambda a: a.reshape(B, S, a.shape[-1])
        y_ssd = _ssd(r3(z), r3(xbc), r3(dt), dtt, conv_w[l], conv_b[l], dt_bias[l], a_log[l],
                     d_skip[l], ssd_norm_g[l])
        y_att = _attention(r3(q), r3(k), r3(v), q_norm_g[l], k_norm_g[l], bias_tiles)
        h2, xn, route, routet, counts = _outproj_route(
            h2, y_ssd.reshape(T, SSD_INNER), y_att.reshape(T, ATT_INNER), w_out[l], norm2_g[l],
            w_r1[l], b_r1[l], w_r2[l], b_r2[l])
        h2 = _moe(h2, xn, route, routet, counts, l, w_gate, w_up, w_down)
    return h2.reshape(B, S, D)
```

```python
import math

import numpy as np
import jax
import jax.numpy as jnp
from jax import lax
from jax.experimental import pallas as pl
from jax.experimental.pallas import tpu as pltpu

F32 = jnp.float32
BF16 = jnp.bfloat16
HIGHEST = lax.Precision.HIGHEST

LANES = 128
SUBLANES = 8

D_MODEL = 1024
SSD_HEADS = 8
SSD_HEAD_DIM = 64
SSD_INNER = SSD_HEADS * SSD_HEAD_DIM
SSD_GROUPS = 2
SSD_HEADS_PER_GROUP = SSD_HEADS // SSD_GROUPS
SSD_STATE = 64
SSD_CONV = 4
SSD_CHUNK = 128
CONV_CH = SSD_INNER + 2 * SSD_GROUPS * SSD_STATE
ATT_HEADS = 8
ATT_HEAD_DIM = 64
ATT_INNER = ATT_HEADS * ATT_HEAD_DIM
MOBA_BLOCK = 256
MOBA_TOPK = 3
REL_BUCKETS = 32
REL_MAX_DIST = 128
MOE_GROUPS = 4
MOE_EXPERTS_PER_GROUP = 8
MOE_EXPERTS = MOE_GROUPS * MOE_EXPERTS_PER_GROUP
MOE_HIDDEN = 512
EPS = 1e-6

MASK_VALUE = -1e30
LOG2E = math.log2(math.e)
IN_ROWS = 1024
SSD_STEP_CHUNKS = 8
OUT_ROWS = 512
EXPERT_ROWS = 512
MOVE_ROWS = 256
VMEM_LIMIT = 56 * 1024 * 1024
ROUTER_LANE0 = MOE_GROUPS


def _silu(x):
    return x * (1.0 / (1.0 + jnp.exp(-x)))


def _softplus(x):
    return jnp.maximum(x, 0.0) + jnp.log(1.0 + jnp.exp(-jnp.abs(x)))


def _params(*sem):
    return pltpu.CompilerParams(dimension_semantics=sem, vmem_limit_bytes=VMEM_LIMIT)


def _inproj_body(h_ref, g_ref, wz_ref, wx_ref, wdt_ref, wq_ref, wk_ref, wv_ref,
                 z_ref, xbc_ref, dt_ref, dtt_ref, q_ref, k_ref, v_ref):
    x = h_ref[...]
    xn = x * lax.rsqrt(jnp.mean(x * x, axis=-1, keepdims=True) + EPS) * g_ref[...]
    xb = xn.astype(BF16)
    z_ref[...] = jnp.dot(xb, wz_ref[...], preferred_element_type=F32)
    xbc_ref[...] = jnp.dot(xb, wx_ref[...], preferred_element_type=F32)
    dt = jnp.dot(xb, wdt_ref[...], preferred_element_type=F32)
    dt_ref[...] = dt
    dtt_ref[...] = dt.T[0:SSD_HEADS, :]
    q_ref[...] = jnp.dot(xb, wq_ref[...], preferred_element_type=F32)
    k_ref[...] = jnp.dot(xb, wk_ref[...], preferred_element_type=F32)
    v_ref[...] = jnp.dot(xb, wv_ref[...], preferred_element_type=F32)


def _inproj(h2, g, w_in):
    T = h2.shape[0]
    c0, c1, c2 = SSD_INNER, SSD_INNER + CONV_CH, SSD_INNER + CONV_CH + SSD_HEADS
    wb = w_in.astype(BF16)
    wz, wx, wdt = wb[:, :c0], wb[:, c0:c1], wb[:, c1:c2]
    wq, wk, wv = (wb[:, c2 + i * ATT_INNER:c2 + (i + 1) * ATT_INNER] for i in range(3))
    wdt_pad = jnp.pad(wdt, ((0, 0), (0, LANES - SSD_HEADS)))
    full = lambda a: pl.BlockSpec(a.shape, lambda i: (0,) * a.ndim)
    rows = lambda n: pl.BlockSpec((IN_ROWS, n), lambda i: (i, 0))
    ins = (h2, g.reshape(1, D_MODEL), wz, wx, wdt_pad, wq, wk, wv)
    out_shape = (
        jax.ShapeDtypeStruct((T, SSD_INNER), F32), jax.ShapeDtypeStruct((T, CONV_CH), F32),
        jax.ShapeDtypeStruct((T, LANES), F32), jax.ShapeDtypeStruct((SSD_HEADS, T), F32),
        jax.ShapeDtypeStruct((T, ATT_INNER), F32), jax.ShapeDtypeStruct((T, ATT_INNER), F32),
        jax.ShapeDtypeStruct((T, ATT_INNER), F32))
    out_specs = (rows(SSD_INNER), rows(CONV_CH), rows(LANES),
                 pl.BlockSpec((SSD_HEADS, IN_ROWS), lambda i: (0, i)),
                 rows(ATT_INNER), rows(ATT_INNER), rows(ATT_INNER))
    return pl.pallas_call(
        _inproj_body, grid=(T // IN_ROWS,),
        in_specs=[rows(D_MODEL)] + [full(a) for a in ins[1:]],
        out_specs=out_specs, out_shape=out_shape,
        compiler_params=_params("arbitrary"), name="inproj")(*ins)


def _ssd_body(z_ref, xbc_ref, dt_ref, dtt_ref, cw_ref, cb_ref, dtb_ref, dtbt_ref, alog_ref,
              alogt_ref, dskip_ref, ng_ref, tril_ref, expand_ref, y_ref, ext_ref, state_ref):
    L, P, N, E = SSD_CHUNK, SSD_HEAD_DIM, SSD_STATE, SSD_HEADS_PER_GROUP
    R = SSD_STEP_CHUNKS * L
    c = pl.program_id(1)

    @pl.when(c == 0)
    def _():
        state_ref[...] = jnp.zeros_like(state_ref)
        ext_ref[0:SUBLANES, :] = jnp.zeros((SUBLANES, CONV_CH), F32)

    ext_ref[SUBLANES:SUBLANES + R, :] = xbc_ref[0]
    tril = tril_ref[...]
    expand = expand_ref[...]
    row = lax.broadcasted_iota(jnp.int32, (L, L), 0)
    col = lax.broadcasted_iota(jnp.int32, (L, L), 1)
    causal = row >= col
    for sub in range(SSD_STEP_CHUNKS):
        _ssd_chunk(sub * L, z_ref, dt_ref, dtt_ref, cw_ref, cb_ref, dtb_ref, dtbt_ref, alog_ref,
                   alogt_ref, dskip_ref, ng_ref, tril, expand, causal, y_ref, ext_ref, state_ref)
    ext_ref[0:SUBLANES, :] = ext_ref[R:R + SUBLANES, :]


def _ssd_chunk(r0, z_ref, dt_ref, dtt_ref, cw_ref, cb_ref, dtb_ref, dtbt_ref, alog_ref, alogt_ref,
               dskip_ref, ng_ref, tril, expand, causal, y_ref, ext_ref, state_ref):
    L, P, N, E = SSD_CHUNK, SSD_HEAD_DIM, SSD_STATE, SSD_HEADS_PER_GROUP
    conv = cb_ref[...]
    for j in range(SSD_CONV):
        off = r0 + SUBLANES - (SSD_CONV - 1) + j
        conv = conv + cw_ref[j:j + 1, :] * ext_ref[off:off + L, :]
    act = _silu(conv)
    xs = act[:, :SSD_INNER]
    b_in = act[:, SSD_INNER:SSD_INNER + SSD_GROUPS * N]
    c_out = act[:, SSD_INNER + SSD_GROUPS * N:]

    dt = _softplus(dt_ref[0, r0:r0 + L, :] + dtb_ref[...])
    a_dt = dt * (-jnp.exp(alog_ref[...]))
    a_cs = jnp.dot(tril, a_dt, precision=HIGHEST, preferred_element_type=F32)
    dt_t = _softplus(dtt_ref[:, r0:r0 + L] + dtbt_ref[...])
    a_dt_t = dt_t * (-jnp.exp(alogt_ref[...]))
    a_cs_t = lax.dot_general(a_dt_t, tril, (((1,), (1,)), ((), ())), precision=HIGHEST,
                             preferred_element_type=F32)
    last = a_cs[L - 1:L, :]
    per_head = jnp.concatenate(
        [dt, jnp.exp(a_cs), jnp.exp(last - a_cs), jnp.broadcast_to(jnp.exp(last), (SUBLANES, LANES))],
        axis=0)
    hi = per_head.astype(BF16)
    lo = (per_head - hi.astype(F32)).astype(BF16)
    on_lanes = jnp.dot(jnp.concatenate([hi, lo], axis=-1), expand, preferred_element_type=F32)
    dt_x = on_lanes[0:L]
    eacs_x = on_lanes[L:2 * L]
    dec_x = on_lanes[2 * L:3 * L]
    cdec_x = on_lanes[3 * L:3 * L + 1]
    xdt = xs * dt_x
    xdtd = (xdt * dec_x).astype(BF16)
    xdt_b = xdt.astype(BF16)

    y_parts = []
    for g in range(SSD_GROUPS):
        bg = b_in[:, g * N:(g + 1) * N].astype(BF16)
        cg = c_out[:, g * N:(g + 1) * N].astype(BF16)
        cb = lax.dot_general(cg, bg, (((1,), (1,)), ((), ())), preferred_element_type=F32)
        st = state_ref[g]
        gs = slice(g * E * P, (g + 1) * E * P)
        y_off = jnp.dot(cg, st.astype(BF16), preferred_element_type=F32) * eacs_x[:, gs]
        for e in range(E):
            h = g * E + e
            diff = a_cs[:, h:h + 1] - a_cs_t[h:h + 1, :]
            l_mat = jnp.exp(jnp.where(causal, diff, -jnp.inf))
            m = (cb * l_mat).astype(BF16)
            y_diag = jnp.dot(m, xdt_b[:, h * P:(h + 1) * P], preferred_element_type=F32)
            y_parts.append(y_diag + y_off[:, e * P:(e + 1) * P])
        upd = lax.dot_general(bg, xdtd[:, gs], (((0,), (0,)), ((), ())),
                              preferred_element_type=F32)
        state_ref[g] = cdec_x[:, gs] * st + upd
    y = jnp.concatenate(y_parts, axis=-1) + dskip_ref[...] * xs
    y = y * _silu(z_ref[0, r0:r0 + L, :])
    y = y * lax.rsqrt(jnp.mean(y * y, axis=-1, keepdims=True) + EPS) * ng_ref[...]
    y_ref[0, r0:r0 + L, :] = y


def _ssd(z, xbc, dt, dtt, conv_w, conv_b, dt_bias, a_log, d_skip, norm_g):
    B, S, _ = z.shape
    L = SSD_CHUNK
    R = SSD_STEP_CHUNKS * L
    nc = S // R
    pad_row = lambda a: jnp.pad(a.reshape(1, SSD_HEADS), ((0, 0), (0, LANES - SSD_HEADS)))
    tril = jnp.asarray(np.tril(np.ones((L, L), np.float32)))
    expand_np = np.zeros((LANES, SSD_INNER), np.float32)
    for h in range(SSD_HEADS):
        expand_np[h, h * SSD_HEAD_DIM:(h + 1) * SSD_HEAD_DIM] = 1.0
    ins = (z, xbc, dt, dtt, conv_w, conv_b.reshape(1, CONV_CH), pad_row(dt_bias),
           dt_bias.reshape(SSD_HEADS, 1), pad_row(a_log), a_log.reshape(SSD_HEADS, 1),
           jnp.repeat(d_skip, SSD_HEAD_DIM).reshape(1, SSD_INNER), norm_g.reshape(1, SSD_INNER),
           tril, jnp.asarray(np.concatenate([expand_np, expand_np], axis=0), dtype=BF16))
    full = lambda a: pl.BlockSpec(a.shape, lambda b, c: (0,) * a.ndim)
    chunk = lambda n: pl.BlockSpec((1, R, n), lambda b, c: (b, c, 0))
    in_specs = [chunk(SSD_INNER), chunk(CONV_CH), chunk(LANES),
                pl.BlockSpec((SSD_HEADS, R), lambda b, c: (0, b * nc + c))]
    in_specs += [full(a) for a in ins[4:]]
    return pl.pallas_call(
        _ssd_body, grid=(B, nc), in_specs=in_specs, out_specs=chunk(SSD_INNER),
        out_shape=jax.ShapeDtypeStruct((B, S, SSD_INNER), F32),
        scratch_shapes=[pltpu.VMEM((R + SUBLANES, CONV_CH), F32),
                        pltpu.VMEM((SSD_GROUPS, SSD_STATE, SSD_HEADS_PER_GROUP * SSD_HEAD_DIM), F32)],
        compiler_params=_params("arbitrary", "arbitrary"), name="ssd")(*ins)


def _bucket_tile():
    bs = MOBA_BLOCK
    dist = np.arange(bs)[:, None] - np.arange(2 * bs)[None, :] + bs
    max_exact = REL_BUCKETS // 2
    d = np.maximum(dist, max_exact).astype(np.float32)
    large = max_exact + (np.log(d / np.float32(max_exact)) / np.float32(math.log(REL_MAX_DIST / max_exact))
                         * np.float32(REL_BUCKETS - max_exact)).astype(np.int32)
    large = np.minimum(large, REL_BUCKETS - 1)
    bucket = np.where(dist < max_exact, dist, large)
    return np.where(dist >= 0, bucket, -1).astype(np.int32)


def _bias_body(rb_ref, bucket_ref, o_ref):
    head = pl.program_id(0)
    bucket = bucket_ref[...]
    far = rb_ref[REL_BUCKETS - 1, head]
    bias = jnp.where(bucket < 0, MASK_VALUE, 0.0)
    for i in range(REL_BUCKETS - 1):
        bias = jnp.where(bucket == i, (rb_ref[i, head] - far) * LOG2E, bias)
    o_ref[0] = bias


def _bias_tiles(rel_bias):
    bs = MOBA_BLOCK
    bucket = jnp.asarray(_bucket_tile())
    return pl.pallas_call(
        _bias_body, grid=(ATT_HEADS,),
        in_specs=[pl.BlockSpec(memory_space=pltpu.SMEM), pl.BlockSpec((bs, 2 * bs), lambda h: (0, 0))],
        out_specs=pl.BlockSpec((1, bs, 2 * bs), lambda h: (h, 0, 0)),
        out_shape=jax.ShapeDtypeStruct((ATT_HEADS, bs, 2 * bs), F32),
        compiler_params=_params("arbitrary"), name="bias_tiles")(rel_bias, bucket)


def _attn_body(q_ref, k_ref, v_ref, qg_ref, kg_ref, bias_ref, o_ref,
               kaug_ref, vaug_ref, qaug_ref, sa_ref, sb_ref, m_ref, acc_ref):
    bs, hd = MOBA_BLOCK, ATT_HEAD_DIM
    nb = k_ref.shape[1] // bs
    jp = pl.program_id(2)
    nt_dims = (((1,), (1,)), ((), ()))

    @pl.when(jp == 0)
    def _():
        S = nb * bs
        pair = 2 * hd
        r = lax.broadcasted_iota(jnp.int32, (pair, pair), 0) // hd
        c = lax.broadcasted_iota(jnp.int32, (pair, pair), 1) // hd
        head_mean = jnp.where(r == c, 1.0 / hd, 0.0).astype(BF16)

        def head_norm(x, gain):
            ms = jnp.dot((x * x).astype(BF16), head_mean, preferred_element_type=F32)
            return x * lax.rsqrt(ms + EPS) * gain

        kn = head_norm(k_ref[0], kg_ref[...])
        kmean = jnp.mean(kn.reshape(nb, bs, pair), axis=1)
        kn_b = kn.astype(BF16)
        v_b = v_ref[0].astype(BF16)
        lane = lax.broadcasted_iota(jnp.int32, (S, pair), 1)
        blk = lax.broadcasted_iota(jnp.int32, (S, pair), 0) // bs
        lane_m = lax.broadcasted_iota(jnp.int32, (nb, pair), 1)
        rowb = lax.broadcasted_iota(jnp.int32, (nb, bs), 0)
        lane_q = lax.broadcasted_iota(jnp.int32, (bs, pair), 1)
        kmean_b = []
        for h in range(2):
            own = (lane // hd) == h
            other0 = (1 - h) * hd
            kaug_ref[h] = jnp.where(own, kn_b, (lane - other0 == blk).astype(BF16))
            vaug_ref[h] = jnp.where(own, v_b, jnp.ones((S, pair), BF16))
            kmean_b.append(jnp.where((lane_m // hd) == h, kmean, 0.0).astype(BF16))
        for j in range(nb):
            rows = slice(j * bs, (j + 1) * bs)
            qn = head_norm(q_ref[0, rows, :], qg_ref[...])
            qn_b = qn.astype(BF16)
            qs = qn * (ATT_HEAD_DIM ** -0.5 * LOG2E)
            for h in range(2):
                if j <= MOBA_TOPK:
                    sel = rowb <= j
                else:
                    gate = lax.dot_general(kmean_b[h], qn_b, nt_dims,
                                           preferred_element_type=F32)
                    rank = jnp.zeros((nb, bs), jnp.int32)
                    for m in range(j):
                        gm = gate[m:m + 1, :]
                        beats = (gm > gate) | ((gm == gate) & (rowb > m))
                        rank = rank + beats.astype(jnp.int32)
                    sel = ((rowb < j) & (rank < MOBA_TOPK)) | (rowb == j)
                selmask = jnp.where(sel, 0.0, MASK_VALUE)
                pieces = [selmask, jnp.zeros((pair - nb, bs), F32)]
                if h == 0:
                    pieces = [jnp.zeros((hd, bs), F32), selmask, jnp.zeros((hd - nb, bs), F32)]
                mask_t = jnp.concatenate(pieces, axis=0).T
                qaug_ref[h, rows, :] = jnp.where((lane_q // hd) == h, qs, mask_t).astype(BF16)

    qt = 2 * bs
    q0 = pl.multiple_of(jp * qt, qt)
    all_rows = slice(0, qt)
    late_rows = slice(bs, qt)
    for h in range(2):
        m_ref[h] = jnp.full((qt, 2 * hd), -jnp.inf, F32)
        acc_ref[h] = jnp.zeros((qt, 2 * hd), F32)

    def scores(n, s_ref, rows=all_rows):
        start = pl.multiple_of(n * bs, bs)
        q_rows = pl.ds(q0 + rows.start, rows.stop - rows.start)
        for h in range(2):
            s_ref[h, rows, :] = lax.dot_general(qaug_ref[h, q_rows, :], kaug_ref[h, pl.ds(start, bs), :],
                                                nt_dims, preferred_element_type=F32)

    def update(n, s_ref, bias, rows=all_rows):
        start = pl.multiple_of(n * bs, bs)
        for r0 in range(rows.start, rows.stop, bs):
            sub = slice(r0, r0 + bs)
            for h in range(2):
                s = s_ref[h, sub, :]
                tile = None if bias is None else bias(h, r0 // bs)
                if tile is not None:
                    s = s + tile
                m_i = m_ref[h, sub, :]
                m_new = jnp.maximum(m_i, jnp.max(s, axis=-1, keepdims=True))
                alpha = jnp.exp2(m_i - m_new)
                p = jnp.exp2((s - jnp.concatenate([m_new, m_new], axis=-1)).astype(BF16))
                pv = jnp.dot(p, vaug_ref[h, pl.ds(start, bs), :], preferred_element_type=F32)
                acc_ref[h, sub, :] = alpha * acc_ref[h, sub, :] + pv
                m_ref[h, sub, :] = m_new

    def far_pair(n):
        scores(n + 1, sb_ref)
        update(n, sa_ref, None)
        scores(n + 2, sa_ref)
        update(n + 1, sb_ref, None)

    def far_quad(i, _):
        far_pair(4 * i)
        far_pair(4 * i + 2)
        return 0

    n_far = jnp.maximum(2 * jp - 1, 0)
    scores(0, sa_ref)
    lax.fori_loop(0, n_far >> 2, far_quad, 0)

    @pl.when((n_far & 2) != 0)
    def _():
        far_pair(n_far & ~3)

    prev = lambda h: bias_ref[h, :, 0:bs]
    own = lambda h, qblk=1: bias_ref[h, :, bs:2 * bs]
    bias_p = lambda h, qblk: prev(h) if qblk == 0 else None
    bias_o = lambda h, qblk: own(h) if qblk == 0 else prev(h)
    blk_p = jnp.maximum(2 * jp - 1, 0)
    blk_o = 2 * jp
    blk_n = 2 * jp + 1

    @pl.when(jp > 0)
    def _():
        scores(blk_p, sb_ref)
        update(n_far - 1, sa_ref, None)
        scores(blk_o, sa_ref)
        update(blk_p, sb_ref, bias_p)
        scores(blk_n, sb_ref, late_rows)
        update(blk_o, sa_ref, bias_o)
        update(blk_n, sb_ref, own, late_rows)

    @pl.when(jp == 0)
    def _():
        scores(blk_n, sb_ref, late_rows)
        update(blk_o, sa_ref, bias_o)
        update(blk_n, sb_ref, own, late_rows)

    ratio = [acc_ref[h] / pltpu.roll(acc_ref[h], hd, 1) for h in range(2)]
    lane_o = lax.broadcasted_iota(jnp.int32, (qt, 2 * hd), 1)
    o_ref[0] = jnp.where(lane_o < hd, ratio[0], ratio[1])


def _attention(q, k, v, q_g, k_g, bias_tiles):
    B, S, _ = q.shape
    bs = MOBA_BLOCK
    nb = S // bs
    assert S % (2 * bs) == 0 and nb <= ATT_HEAD_DIM
    pair = 2 * ATT_HEAD_DIM
    return pl.pallas_call(
        _attn_body, grid=(B, ATT_HEADS // 2, nb // 2),
        in_specs=[
            pl.BlockSpec((1, S, pair), lambda b, hp, i: (b, 0, hp)),
            pl.BlockSpec((1, S, pair), lambda b, hp, i: (b, 0, hp)),
            pl.BlockSpec((1, S, pair), lambda b, hp, i: (b, 0, hp)),
            pl.BlockSpec((1, pair), lambda b, hp, i: (0, 0)),
            pl.BlockSpec((1, pair), lambda b, hp, i: (0, 0)),
            pl.BlockSpec((2, bs, 2 * bs), lambda b, hp, i: (hp, 0, 0)),
        ],
        out_specs=pl.BlockSpec((1, 2 * bs, pair), lambda b, hp, i: (b, i, hp)),
        out_shape=jax.ShapeDtypeStruct((B, S, ATT_INNER), F32),
        scratch_shapes=[pltpu.VMEM((2, S, pair), BF16), pltpu.VMEM((2, S, pair), BF16),
                        pltpu.VMEM((2, S, pair), BF16),
                        pltpu.VMEM((2, 2 * bs, bs), F32), pltpu.VMEM((2, 2 * bs, bs), F32),
                        pltpu.VMEM((2, 2 * bs, pair), F32),
                        pltpu.VMEM((2, 2 * bs, pair), F32)],
        compiler_params=_params("arbitrary", "arbitrary", "arbitrary"), name="moba")(
            q, k, v, jnp.tile(q_g, 2).reshape(1, pair), jnp.tile(k_g, 2).reshape(1, pair), bias_tiles)


def _outproj_body(h_ref, ys_ref, ya_ref, wa_ref, wb_ref, g_ref, wrh_ref, wrl_ref, br_ref, tri_ref,
                  hn_ref, xn_ref, route_ref, routet_ref, counts_ref, carry_ref):
    i = pl.program_id(0)

    @pl.when(i == 0)
    def _():
        carry_ref[...] = jnp.zeros_like(carry_ref)

    _outproj_rows(0, OUT_ROWS, h_ref, ys_ref, ya_ref, wa_ref, wb_ref, g_ref, wrh_ref, wrl_ref,
                  br_ref, tri_ref, hn_ref, xn_ref, route_ref, routet_ref, carry_ref)
    counts_ref[...] = carry_ref[...]


def _outproj_rows(r0, n, h_ref, ys_ref, ya_ref, wa_ref, wb_ref, g_ref, wrh_ref, wrl_ref, br_ref, tri_ref,
                  hn_ref, xn_ref, route_ref, routet_ref, carry_ref):
    rs = slice(r0, r0 + n)
    hn = (h_ref[rs, :]
          + jnp.dot(ys_ref[rs, :].astype(BF16), wa_ref[...], preferred_element_type=F32)
          + jnp.dot(ya_ref[rs, :].astype(BF16), wb_ref[...], preferred_element_type=F32))
    hn_ref[rs, :] = hn
    xn = hn * lax.rsqrt(jnp.mean(hn * hn, axis=-1, keepdims=True) + EPS) * g_ref[...]
    for c in range(SUBLANES):
        xn_ref[pl.ds(r0 * SUBLANES + c, n, stride=SUBLANES), :] = xn[:, c * LANES:(c + 1) * LANES]
    x_hi = xn.astype(BF16)
    x_lo = (xn - x_hi.astype(F32)).astype(BF16)
    hi_terms = jnp.dot(x_hi, wrl_ref[...], preferred_element_type=F32)
    logits = (hi_terms[:, :LANES] + hi_terms[:, LANES:]
              + jnp.dot(x_lo, wrh_ref[...], preferred_element_type=F32)) + br_ref[...]
    lane = lax.broadcasted_iota(jnp.int32, (n, LANES), 1)
    first = lambda hit: jnp.min(jnp.where(hit, lane, LANES), axis=-1, keepdims=True)
    l1 = jnp.where(lane < MOE_GROUPS, logits, -jnp.inf)
    mx = jnp.max(l1, axis=-1, keepdims=True)
    gval = 1.0 / jnp.sum(jnp.exp(l1 - mx), axis=-1, keepdims=True)
    gidx = first(l1 == mx)
    e_lane = lane - ROUTER_LANE0
    in_grp = (e_lane >= 0) & (e_lane < MOE_EXPERTS) & ((e_lane >> 3) == gidx)
    l2 = jnp.where(in_grp, logits, -jnp.inf)
    m1 = jnp.max(l2, axis=-1, keepdims=True)
    i1 = first(l2 == m1)
    l2b = jnp.where(lane == i1, -jnp.inf, l2)
    m2 = jnp.max(l2b, axis=-1, keepdims=True)
    i2 = first(l2b == m2)
    r = jnp.exp(m2 - m1)
    g1 = gval / (1.0 + r)
    g2 = gval * r / (1.0 + r)
    oh1 = (lane == i1)
    oh2 = (lane == i2)
    both = jnp.where(oh1 | oh2, 1.0, 0.0)
    cx = jnp.dot(tri_ref[...], both.astype(BF16), preferred_element_type=F32) + carry_ref[...]
    rank1 = jnp.sum(jnp.where(oh1, cx, 0.0), axis=-1, keepdims=True)
    rank2 = jnp.sum(jnp.where(oh2, cx, 0.0), axis=-1, keepdims=True)
    carry_ref[...] = carry_ref[...] + jnp.sum(both, axis=0, keepdims=True)
    cols = ((i1 - ROUTER_LANE0).astype(F32), (i2 - ROUTER_LANE0).astype(F32), g1, g2, rank1, rank2)
    route = jnp.zeros((n, LANES), F32)
    for j, cval in enumerate(cols):
        route = jnp.where(lane == j, cval, route)
    route_ref[rs, :] = route
    routet_ref[:, rs] = route.T[0:SUBLANES, :]


def _outproj_route(h2, y_ssd, y_att, w_out, norm_g, w_r1, b_r1, w_r2, b_r2):
    T = h2.shape[0]
    R = OUT_ROWS
    wb = w_out.astype(BF16)
    pad = LANES - MOE_GROUPS - MOE_EXPERTS
    wr = jnp.pad(jnp.concatenate([w_r1, w_r2], axis=1), ((0, 0), (0, pad)))
    wr_hi = wr.astype(BF16)
    wr_lo = (wr - wr_hi.astype(F32)).astype(BF16)
    br = jnp.pad(jnp.concatenate([b_r1, b_r2]), (0, pad)).reshape(1, LANES)
    tri = jnp.asarray(np.tril(np.ones((R, R), np.float32), -1), dtype=BF16)
    ins = (h2, y_ssd, y_att, wb[:SSD_INNER], wb[SSD_INNER:], norm_g.reshape(1, D_MODEL),
           wr_hi, jnp.concatenate([wr_hi, wr_lo], axis=1), br, tri)
    full = lambda a: pl.BlockSpec(a.shape, lambda i: (0,) * a.ndim)
    rows = lambda n: pl.BlockSpec((R, n), lambda i: (i, 0))
    return pl.pallas_call(
        _outproj_body, grid=(T // R,),
        in_specs=[rows(D_MODEL), rows(SSD_INNER), rows(ATT_INNER)] + [full(a) for a in ins[3:]],
        out_specs=(rows(D_MODEL), pl.BlockSpec((R * SUBLANES, LANES), lambda i: (i, 0)), rows(LANES),
                   pl.BlockSpec((SUBLANES, R), lambda i: (0, i)), pl.BlockSpec((1, LANES), lambda i: (0, 0))),
        out_shape=(jax.ShapeDtypeStruct((T, D_MODEL), F32),
                   jax.ShapeDtypeStruct((T * SUBLANES, LANES), F32),
                   jax.ShapeDtypeStruct((T, LANES), F32), jax.ShapeDtypeStruct((SUBLANES, T), F32),
                   jax.ShapeDtypeStruct((1, LANES), F32)),
        scratch_shapes=[pltpu.VMEM((1, LANES), F32)],
        compiler_params=_params("arbitrary"), name="outproj_route")(*ins)


def _slots_body(routet_ref, pstart_ref, dest_ref):
    cols = routet_ref.shape[1]
    expert = lax.broadcasted_iota(jnp.int32, (MOE_EXPERTS, cols), 0).astype(F32)
    pstart = pstart_ref[...]
    for j in range(2):
        hit = expert == routet_ref[j:j + 1, :]
        base = jnp.sum(jnp.where(hit, pstart, 0.0), axis=0, keepdims=True)
        dest_ref[j:j + 1, :] = (base + routet_ref[4 + j:5 + j, :]).astype(jnp.int32)


def _slots(routet, pstarts):
    T = routet.shape[1]
    cols = min(T, 4096)
    assert T % cols == 0
    return pl.pallas_call(
        _slots_body, grid=(T // cols,),
        in_specs=[pl.BlockSpec((SUBLANES, cols), lambda i: (0, i)),
                  pl.BlockSpec((MOE_EXPERTS, 1), lambda i: (0, 0))],
        out_specs=pl.BlockSpec((2, cols), lambda i: (0, i)),
        out_shape=jax.ShapeDtypeStruct((2, T), jnp.int32),
        compiler_params=_params("arbitrary"), name="slots")(
            routet, pstarts.astype(F32).reshape(MOE_EXPERTS, 1))


def _tile_copy(src_ref, src_row, dst_ref, dst_row, sem):
    tile = lambda row: pl.ds(pl.multiple_of(row * SUBLANES, SUBLANES), SUBLANES)
    return pltpu.make_async_copy(src_ref.at[tile(src_row)], dst_ref.at[tile(dst_row)], sem)


def _dispatch_body(pad_ref, d1_ref, d2_ref, xn_ref, xb_ref, zero_ref, stage_ref, sem, lsem, ssem):
    block = EXPERT_ROWS * SUBLANES
    n_blocks = xb_ref.shape[0] // block
    n_used = pad_ref[MOE_EXPERTS]

    def zero_copy(row):
        start = pl.multiple_of(row * SUBLANES, SUBLANES)
        return pltpu.make_async_copy(zero_ref, xb_ref.at[pl.ds(start, block)], sem)

    @pl.when(pl.program_id(0) == 0)
    def _():
        zero_ref[...] = jnp.zeros_like(zero_ref)
        for e in range(MOE_EXPERTS):
            zero_copy(pad_ref[e]).start()
        for e in range(MOE_EXPERTS):
            zero_copy(pad_ref[e]).wait()
        for b in range(n_blocks - MOE_EXPERTS - 1, n_blocks):
            @pl.when(b >= n_used)
            def _():
                zero_copy(b * EXPERT_ROWS).start()
        for b in range(n_blocks - MOE_EXPERTS - 1, n_blocks):
            @pl.when(b >= n_used)
            def _():
                zero_copy(b * EXPERT_ROWS).wait()

    i = pl.program_id(0)
    n = pl.num_programs(0)
    rows = MOVE_ROWS * SUBLANES

    def load(step, slot):
        src = xn_ref.at[pl.ds(pl.multiple_of(step * rows, rows), rows)]
        return pltpu.make_async_copy(src, stage_ref.at[slot], lsem.at[slot])

    def wait_scatters(slot):
        for _ in range(2):
            pltpu.make_async_copy(stage_ref.at[slot], stage_ref.at[slot], ssem.at[slot]).wait()

    @pl.when(i == 0)
    def _():
        load(0, 0).start()

    @pl.when(i >= 1)
    def _():
        wait_scatters((i - 1) & 1)

    @pl.when(i + 1 < n)
    def _():
        load(i + 1, (i + 1) & 1).start()

    slot = i & 1
    load(i, slot).wait()
    base = i * MOVE_ROWS

    def scatter(t, _):
        _tile_copy(stage_ref.at[slot], t, xb_ref, d1_ref[base + t], ssem.at[slot]).start()
        _tile_copy(stage_ref.at[slot], t, xb_ref, d2_ref[base + t], ssem.at[slot]).start(priority=1)
        return 0

    lax.fori_loop(0, MOVE_ROWS, scatter, 0)

    @pl.when(i == n - 1)
    def _():
        wait_scatters(slot)


def _dispatch(xn, dest1, dest2, pad_start, n_slots):
    T = dest1.shape[0]
    grid_spec = pltpu.PrefetchScalarGridSpec(
        num_scalar_prefetch=3, grid=(T // MOVE_ROWS,),
        in_specs=[pl.BlockSpec(memory_space=pl.ANY)],
        out_specs=pl.BlockSpec(memory_space=pl.ANY),
        scratch_shapes=[pltpu.VMEM((EXPERT_ROWS * SUBLANES, LANES), F32),
                        pltpu.VMEM((2, MOVE_ROWS * SUBLANES, LANES), F32),
                        pltpu.SemaphoreType.DMA(()), pltpu.SemaphoreType.DMA((2,)),
                        pltpu.SemaphoreType.DMA((2,))])
    return pl.pallas_call(
        _dispatch_body, grid_spec=grid_spec,
        out_shape=jax.ShapeDtypeStruct(((n_slots + EXPERT_ROWS) * SUBLANES, LANES), F32),
        compiler_params=_params("arbitrary"), name="dispatch")(pad_start, dest1, dest2, xn)


def _expert_body(be_ref, nused_ref, xb_ref, wg_ref, wu_ref, wd_ref, yb_ref,
                 x_ref, wgb_ref, wub_ref, wdb_ref):
    i = pl.program_id(0)
    used = i < nused_ref[0]
    new_expert = (i == 0) | (be_ref[i] != be_ref[jnp.maximum(i - 1, 0)])

    @pl.when(used & new_expert)
    def _():
        wgb_ref[...] = wg_ref[0, 0].astype(BF16)
        wub_ref[...] = wu_ref[0, 0].astype(BF16)
        wdb_ref[...] = wd_ref[0, 0].astype(BF16)

    @pl.when(used)
    def _():
        for c in range(SUBLANES):
            x_ref[:, c * LANES:(c + 1) * LANES] = xb_ref[pl.ds(c, EXPERT_ROWS, stride=SUBLANES), :].astype(BF16)
        x = x_ref[...]
        gate = jnp.dot(x, wgb_ref[...], preferred_element_type=F32)
        up = jnp.dot(x, wub_ref[...], preferred_element_type=F32)
        hmid = (_silu(gate) * up).astype(BF16)
        y = jnp.dot(hmid, wdb_ref[...], preferred_element_type=F32)
        for c in range(SUBLANES):
            yb_ref[pl.ds(c, EXPERT_ROWS, stride=SUBLANES), :] = y[:, c * LANES:(c + 1) * LANES]

    @pl.when(jnp.logical_not(used))
    def _():
        yb_ref[...] = jnp.zeros_like(yb_ref)


def _experts(xb, blk_exp, n_used, n_blk, layer, w_gate, w_up, w_down):
    M = EXPERT_ROWS
    tile = lambda index: pl.BlockSpec((M * SUBLANES, LANES), index)
    weight = lambda a: pl.BlockSpec((1, 1) + a.shape[2:], lambda i, be, nu: (layer, be[i], 0, 0))
    grid_spec = pltpu.PrefetchScalarGridSpec(
        num_scalar_prefetch=2, grid=(n_blk,),
        in_specs=[tile(lambda i, be, nu: (jnp.minimum(i, nu[0] - 1), 0)),
                  weight(w_gate), weight(w_up), weight(w_down)],
        out_specs=tile(lambda i, be, nu: (i, 0)),
        scratch_shapes=[pltpu.VMEM((M, D_MODEL), BF16), pltpu.VMEM((D_MODEL, MOE_HIDDEN), BF16),
                        pltpu.VMEM((D_MODEL, MOE_HIDDEN), BF16), pltpu.VMEM((MOE_HIDDEN, D_MODEL), BF16)])
    return pl.pallas_call(
        _expert_body, grid_spec=grid_spec,
        out_shape=jax.ShapeDtypeStruct((n_blk * M * SUBLANES, LANES), F32),
        compiler_params=_params("arbitrary"), name="experts")(
            blk_exp, n_used, xb, w_gate, w_up, w_down)


def _combine_body(d1_ref, d2_ref, h_ref, route_ref, yb_ref, o_ref, buf_ref, sem):
    i = pl.program_id(0)
    n = pl.num_programs(0)

    def gather(step, slot):
        base = step * MOVE_ROWS

        def body(t, _):
            _tile_copy(yb_ref, d1_ref[base + t], buf_ref.at[slot, 0], t, sem.at[slot]).start()
            _tile_copy(yb_ref, d2_ref[base + t], buf_ref.at[slot, 1], t, sem.at[slot]).start(priority=1)
            return 0

        lax.fori_loop(0, MOVE_ROWS, body, 0)

    @pl.when(i == 0)
    def _():
        gather(0, 0)

    @pl.when(i + 1 < n)
    def _():
        gather(i + 1, (i + 1) & 1)

    slot = i & 1
    pltpu.make_async_copy(buf_ref.at[slot], buf_ref.at[slot], sem.at[slot]).wait()
    route = route_ref[...]
    g1 = route[:, 2:3]
    g2 = route[:, 3:4]
    for c in range(SUBLANES):
        cs = slice(c * LANES, (c + 1) * LANES)
        rows = pl.ds(c, MOVE_ROWS, stride=SUBLANES)
        o_ref[:, cs] = h_ref[:, cs] + (buf_ref[slot, 0, rows, :] * g1 + buf_ref[slot, 1, rows, :] * g2)


def _combine(h2, route, yb, dest1, dest2):
    T = h2.shape[0]
    R = MOVE_ROWS
    grid_spec = pltpu.PrefetchScalarGridSpec(
        num_scalar_prefetch=2, grid=(T // R,),
        in_specs=[pl.BlockSpec((R, D_MODEL), lambda i, d1, d2: (i, 0)),
                  pl.BlockSpec((R, LANES), lambda i, d1, d2: (i, 0)), pl.BlockSpec(memory_space=pl.ANY)],
        out_specs=pl.BlockSpec((R, D_MODEL), lambda i, d1, d2: (i, 0)),
        scratch_shapes=[pltpu.VMEM((2, 2, R * SUBLANES, LANES), F32), pltpu.SemaphoreType.DMA((2,))])
    return pl.pallas_call(
        _combine_body, grid_spec=grid_spec, out_shape=jax.ShapeDtypeStruct((T, D_MODEL), F32),
        compiler_params=_params("arbitrary"), name="combine")(dest1, dest2, h2, route, yb)


def _moe(h2, xn, route, routet, counts, layer, w_gate, w_up, w_down):
    T = h2.shape[0]
    M = EXPERT_ROWS
    n_blk = (2 * T) // M + MOE_EXPERTS
    cnt = counts[0, ROUTER_LANE0:ROUTER_LANE0 + MOE_EXPERTS].astype(jnp.int32)
    padded = (cnt + M - 1) // M * M
    ends = jnp.cumsum(padded)
    pstarts = ends - padded
    n_used = (ends[-1] // M).astype(jnp.int32)
    blk = jnp.minimum(jnp.arange(n_blk, dtype=jnp.int32), n_used - 1) * M
    blk_exp = jnp.sum((ends[None, :] <= blk[:, None]).astype(jnp.int32), axis=1)
    blk_exp = jnp.minimum(blk_exp, MOE_EXPERTS - 1)
    dest = _slots(routet, pstarts)
    pad_info = jnp.concatenate([pstarts + cnt, n_used.reshape(1)])
    xb = _dispatch(xn, dest[0], dest[1], pad_info, n_blk * M)
    yb = _experts(xb, blk_exp, n_used.reshape(1), n_blk, layer, w_gate, w_up, w_down)
    return _combine(h2, route, yb, dest[0], dest[1])


def kernel(x, rel_bias, norm1_g, w_in, conv_w, conv_b, dt_bias, a_log, d_skip, ssd_norm_g,
           q_norm_g, k_norm_g, w_out, norm2_g, w_r1, b_r1, w_r2, b_r2, w_gate, w_up, w_down):
    B, S, D = x.shape
    T = B * S
    h2 = x.reshape(T, D)
    depth = w_in.shape[0]
    bias_tiles = _bias_tiles(rel_bias)
    for l in range(depth):
        z, xbc, dt, dtt, q, k, v = _inproj(h2, norm1_g[l], w_in[l])
        r3 = lambda a: a.reshape(B, S, a.shape[-1])
        y_ssd = _ssd(r3(z), r3(xbc), r3(dt), dtt, conv_w[l], conv_b[l], dt_bias[l], a_log[l],
                     d_skip[l], ssd_norm_g[l])
        y_att = _attention(r3(q), r3(k), r3(v), q_norm_g[l], k_norm_g[l], bias_tiles)
        h2, xn, route, routet, counts = _outproj_route(
            h2, y_ssd.reshape(T, SSD_INNER), y_att.reshape(T, ATT_INNER), w_out[l], norm2_g[l],
            w_r1[l], b_r1[l], w_r2[l], b_r2[l])
        h2 = _moe(h2, xn, route, routet, counts, l, w_gate, w_up, w_down)
    return h2.reshape(B, S, D)
```

```python
import math

import numpy as np
import jax
import jax.numpy as jnp
from jax import lax
from jax.experimental import pallas as pl
from jax.experimental.pallas import tpu as pltpu

F32 = jnp.float32
BF16 = jnp.bfloat16
HIGHEST = lax.Precision.HIGHEST

LANES = 128
SUBLANES = 8

D_MODEL = 1024
SSD_HEADS = 8
SSD_HEAD_DIM = 64
SSD_INNER = SSD_HEADS * SSD_HEAD_DIM
SSD_GROUPS = 2
SSD_HEADS_PER_GROUP = SSD_HEADS // SSD_GROUPS
SSD_STATE = 64
SSD_CONV = 4
SSD_CHUNK = 128
CONV_CH = SSD_INNER + 2 * SSD_GROUPS * SSD_STATE
ATT_HEADS = 8
ATT_HEAD_DIM = 64
ATT_INNER = ATT_HEADS * ATT_HEAD_DIM
MOBA_BLOCK = 256
MOBA_TOPK = 3
REL_BUCKETS = 32
REL_MAX_DIST = 128
MOE_GROUPS = 4
MOE_EXPERTS_PER_GROUP = 8
MOE_EXPERTS = MOE_GROUPS * MOE_EXPERTS_PER_GROUP
MOE_HIDDEN = 512
EPS = 1e-6

MASK_VALUE = -1e30
LOG2E = math.log2(math.e)
IN_ROWS = 1024
SSD_STEP_CHUNKS = 8
OUT_ROWS = 512
EXPERT_ROWS = 512
MOVE_ROWS = 256
VMEM_LIMIT = 56 * 1024 * 1024
ROUTER_LANE0 = MOE_GROUPS
ROUTER_ROWS = 48


def _silu(x):
    return x * (1.0 / (1.0 + jnp.exp(-x)))


def _softplus(x):
    return jnp.maximum(x, 0.0) + jnp.log(1.0 + jnp.exp(-jnp.abs(x)))


def _params(*sem):
    return pltpu.CompilerParams(dimension_semantics=sem, vmem_limit_bytes=VMEM_LIMIT)


def _inproj_body(h_ref, g_ref, wz_ref, wx_ref, wdt_ref, wq_ref, wk_ref, wv_ref,
                 z_ref, xbc_ref, dt_ref, dtt_ref, q_ref, k_ref, v_ref):
    x = h_ref[...]
    xn = x * lax.rsqrt(jnp.mean(x * x, axis=-1, keepdims=True) + EPS) * g_ref[...]
    xb = xn.astype(BF16)
    z_ref[...] = jnp.dot(xb, wz_ref[...], preferred_element_type=F32)
    xbc_ref[...] = jnp.dot(xb, wx_ref[...], preferred_element_type=F32)
    dt = jnp.dot(xb, wdt_ref[...], preferred_element_type=F32)
    dt_ref[...] = dt
    dtt_ref[...] = dt.T[0:SSD_HEADS, :]
    q_ref[...] = jnp.dot(xb, wq_ref[...], preferred_element_type=F32)
    k_ref[...] = jnp.dot(xb, wk_ref[...], preferred_element_type=F32)
    v_ref[...] = jnp.dot(xb, wv_ref[...], preferred_element_type=F32)


def _inproj(h2, g, w_in):
    T = h2.shape[0]
    c0, c1, c2 = SSD_INNER, SSD_INNER + CONV_CH, SSD_INNER + CONV_CH + SSD_HEADS
    wb = w_in.astype(BF16)
    wz, wx, wdt = wb[:, :c0], wb[:, c0:c1], wb[:, c1:c2]
    wq, wk, wv = (wb[:, c2 + i * ATT_INNER:c2 + (i + 1) * ATT_INNER] for i in range(3))
    wdt_pad = jnp.pad(wdt, ((0, 0), (0, LANES - SSD_HEADS)))
    full = lambda a: pl.BlockSpec(a.shape, lambda i: (0,) * a.ndim)
    rows = lambda n: pl.BlockSpec((IN_ROWS, n), lambda i: (i, 0))
    ins = (h2, g.reshape(1, D_MODEL), wz, wx, wdt_pad, wq, wk, wv)
    out_shape = (
        jax.ShapeDtypeStruct((T, SSD_INNER), F32), jax.ShapeDtypeStruct((T, CONV_CH), F32),
        jax.ShapeDtypeStruct((T, LANES), F32), jax.ShapeDtypeStruct((SSD_HEADS, T), F32),
        jax.ShapeDtypeStruct((T, ATT_INNER), F32), jax.ShapeDtypeStruct((T, ATT_INNER), F32),
        jax.ShapeDtypeStruct((T, ATT_INNER), F32))
    out_specs = (rows(SSD_INNER), rows(CONV_CH), rows(LANES),
                 pl.BlockSpec((SSD_HEADS, IN_ROWS), lambda i: (0, i)),
                 rows(ATT_INNER), rows(ATT_INNER), rows(ATT_INNER))
    return pl.pallas_call(
        _inproj_body, grid=(T // IN_ROWS,),
        in_specs=[rows(D_MODEL)] + [full(a) for a in ins[1:]],
        out_specs=out_specs, out_shape=out_shape,
        compiler_params=_params("arbitrary"), name="inproj")(*ins)


def _ssd_body(z_ref, xbc_ref, dt_ref, dtt_ref, cw_ref, cb_ref, dtb_ref, dtbt_ref, alog_ref,
              alogt_ref, dskip_ref, ng_ref, tril_ref, expand_ref, y_ref, ext_ref, state_ref):
    L, P, N, E = SSD_CHUNK, SSD_HEAD_DIM, SSD_STATE, SSD_HEADS_PER_GROUP
    R = SSD_STEP_CHUNKS * L
    c = pl.program_id(1)

    @pl.when(c == 0)
    def _():
        state_ref[...] = jnp.zeros_like(state_ref)
        ext_ref[0:SUBLANES, :] = jnp.zeros((SUBLANES, CONV_CH), F32)

    ext_ref[SUBLANES:SUBLANES + R, :] = xbc_ref[0]
    tril = tril_ref[...]
    expand = expand_ref[...]
    row = lax.broadcasted_iota(jnp.int32, (L, L), 0)
    col = lax.broadcasted_iota(jnp.int32, (L, L), 1)
    causal = row >= col
    for sub in range(SSD_STEP_CHUNKS):
        _ssd_chunk(sub * L, z_ref, dt_ref, dtt_ref, cw_ref, cb_ref, dtb_ref, dtbt_ref, alog_ref,
                   alogt_ref, dskip_ref, ng_ref, tril, expand, causal, y_ref, ext_ref, state_ref)
    ext_ref[0:SUBLANES, :] = ext_ref[R:R + SUBLANES, :]


def _ssd_chunk(r0, z_ref, dt_ref, dtt_ref, cw_ref, cb_ref, dtb_ref, dtbt_ref, alog_ref, alogt_ref,
               dskip_ref, ng_ref, tril, expand, causal, y_ref, ext_ref, state_ref):
    L, P, N, E = SSD_CHUNK, SSD_HEAD_DIM, SSD_STATE, SSD_HEADS_PER_GROUP
    conv = cb_ref[...]
    for j in range(SSD_CONV):
        off = r0 + SUBLANES - (SSD_CONV - 1) + j
        conv = conv + cw_ref[j:j + 1, :] * ext_ref[off:off + L, :]
    act = _silu(conv)
    xs = act[:, :SSD_INNER]
    b_in = act[:, SSD_INNER:SSD_INNER + SSD_GROUPS * N]
    c_out = act[:, SSD_INNER + SSD_GROUPS * N:]

    dt = _softplus(dt_ref[0, r0:r0 + L, :] + dtb_ref[...])
    a_dt = dt * (-jnp.exp(alog_ref[...]))
    a_cs = jnp.dot(tril, a_dt, precision=HIGHEST, preferred_element_type=F32)
    dt_t = _softplus(dtt_ref[:, r0:r0 + L] + dtbt_ref[...])
    a_dt_t = dt_t * (-jnp.exp(alogt_ref[...]))
    a_cs_t = lax.dot_general(a_dt_t, tril, (((1,), (1,)), ((), ())), precision=HIGHEST,
                             preferred_element_type=F32)
    last = a_cs[L - 1:L, :]
    per_head = jnp.concatenate(
        [dt, jnp.exp(a_cs), jnp.exp(last - a_cs), jnp.broadcast_to(jnp.exp(last), (SUBLANES, LANES))],
        axis=0)
    hi = per_head.astype(BF16)
    lo = (per_head - hi.astype(F32)).astype(BF16)
    on_lanes = jnp.dot(jnp.concatenate([hi, lo], axis=-1), expand, preferred_element_type=F32)
    dt_x = on_lanes[0:L]
    eacs_x = on_lanes[L:2 * L]
    dec_x = on_lanes[2 * L:3 * L]
    cdec_x = on_lanes[3 * L:3 * L + 1]
    xdt = xs * dt_x
    xdtd = (xdt * dec_x).astype(BF16)
    xdt_b = xdt.astype(BF16)

    y_parts = []
    for g in range(SSD_GROUPS):
        bg = b_in[:, g * N:(g + 1) * N].astype(BF16)
        cg = c_out[:, g * N:(g + 1) * N].astype(BF16)
        cb = lax.dot_general(cg, bg, (((1,), (1,)), ((), ())), preferred_element_type=F32)
        st = state_ref[g]
        gs = slice(g * E * P, (g + 1) * E * P)
        y_off = jnp.dot(cg, st.astype(BF16), preferred_element_type=F32) * eacs_x[:, gs]
        for e in range(E):
            h = g * E + e
            diff = a_cs[:, h:h + 1] - a_cs_t[h:h + 1, :]
            l_mat = jnp.exp(jnp.where(causal, diff, -jnp.inf))
            m = (cb * l_mat).astype(BF16)
            y_diag = jnp.dot(m, xdt_b[:, h * P:(h + 1) * P], preferred_element_type=F32)
            y_parts.append(y_diag + y_off[:, e * P:(e + 1) * P])
        upd = lax.dot_general(bg, xdtd[:, gs], (((0,), (0,)), ((), ())),
                              preferred_element_type=F32)
        state_ref[g] = cdec_x[:, gs] * st + upd
    y = jnp.concatenate(y_parts, axis=-1) + dskip_ref[...] * xs
    y = y * _silu(z_ref[0, r0:r0 + L, :])
    y = y * lax.rsqrt(jnp.mean(y * y, axis=-1, keepdims=True) + EPS) * ng_ref[...]
    y_ref[0, r0:r0 + L, :] = y


def _ssd(z, xbc, dt, dtt, conv_w, conv_b, dt_bias, a_log, d_skip, norm_g):
    B, S, _ = z.shape
    L = SSD_CHUNK
    R = SSD_STEP_CHUNKS * L
    nc = S // R
    pad_row = lambda a: jnp.pad(a.reshape(1, SSD_HEADS), ((0, 0), (0, LANES - SSD_HEADS)))
    tril = jnp.asarray(np.tril(np.ones((L, L), np.float32)))
    expand_np = np.zeros((LANES, SSD_INNER), np.float32)
    for h in range(SSD_HEADS):
        expand_np[h, h * SSD_HEAD_DIM:(h + 1) * SSD_HEAD_DIM] = 1.0
    ins = (z, xbc, dt, dtt, conv_w, conv_b.reshape(1, CONV_CH), pad_row(dt_bias),
           dt_bias.reshape(SSD_HEADS, 1), pad_row(a_log), a_log.reshape(SSD_HEADS, 1),
           jnp.repeat(d_skip, SSD_HEAD_DIM).reshape(1, SSD_INNER), norm_g.reshape(1, SSD_INNER),
           tril, jnp.asarray(np.concatenate([expand_np, expand_np], axis=0), dtype=BF16))
    full = lambda a: pl.BlockSpec(a.shape, lambda b, c: (0,) * a.ndim)
    chunk = lambda n: pl.BlockSpec((1, R, n), lambda b, c: (b, c, 0))
    in_specs = [chunk(SSD_INNER), chunk(CONV_CH), chunk(LANES),
                pl.BlockSpec((SSD_HEADS, R), lambda b, c: (0, b * nc + c))]
    in_specs += [full(a) for a in ins[4:]]
    return pl.pallas_call(
        _ssd_body, grid=(B, nc), in_specs=in_specs, out_specs=chunk(SSD_INNER),
        out_shape=jax.ShapeDtypeStruct((B, S, SSD_INNER), F32),
        scratch_shapes=[pltpu.VMEM((R + SUBLANES, CONV_CH), F32),
                        pltpu.VMEM((SSD_GROUPS, SSD_STATE, SSD_HEADS_PER_GROUP * SSD_HEAD_DIM), F32)],
        compiler_params=_params("arbitrary", "arbitrary"), name="ssd")(*ins)


def _bucket_tile():
    bs = MOBA_BLOCK
    dist = np.arange(bs)[:, None] - np.arange(2 * bs)[None, :] + bs
    max_exact = REL_BUCKETS // 2
    d = np.maximum(dist, max_exact).astype(np.float32)
    large = max_exact + (np.log(d / np.float32(max_exact)) / np.float32(math.log(REL_MAX_DIST / max_exact))
                         * np.float32(REL_BUCKETS - max_exact)).astype(np.int32)
    large = np.minimum(large, REL_BUCKETS - 1)
    bucket = np.where(dist < max_exact, dist, large)
    return np.where(dist >= 0, bucket, -1).astype(np.int32)


def _bias_body(rb_ref, bucket_ref, o_ref):
    head = pl.program_id(0)
    bucket = bucket_ref[...]
    far = rb_ref[REL_BUCKETS - 1, head]
    bias = jnp.where(bucket < 0, MASK_VALUE, 0.0)
    for i in range(REL_BUCKETS - 1):
        bias = jnp.where(bucket == i, (rb_ref[i, head] - far) * LOG2E, bias)
    o_ref[0] = bias


def _bias_tiles(rel_bias):
    bs = MOBA_BLOCK
    bucket = jnp.asarray(_bucket_tile())
    return pl.pallas_call(
        _bias_body, grid=(ATT_HEADS,),
        in_specs=[pl.BlockSpec(memory_space=pltpu.SMEM), pl.BlockSpec((bs, 2 * bs), lambda h: (0, 0))],
        out_specs=pl.BlockSpec((1, bs, 2 * bs), lambda h: (h, 0, 0)),
        out_shape=jax.ShapeDtypeStruct((ATT_HEADS, bs, 2 * bs), F32),
        compiler_params=_params("arbitrary"), name="bias_tiles")(rel_bias, bucket)


def _attn_body(q_ref, k_ref, v_ref, qg_ref, kg_ref, bias_ref, o_ref,
               kaug_ref, vaug_ref, qaug_ref, sa_ref, sb_ref, m_ref, acc_ref):
    bs, hd = MOBA_BLOCK, ATT_HEAD_DIM
    nb = k_ref.shape[1] // bs
    jp = pl.program_id(2)
    nt_dims = (((1,), (1,)), ((), ()))

    @pl.when(jp == 0)
    def _():
        S = nb * bs
        pair = 2 * hd
        r = lax.broadcasted_iota(jnp.int32, (pair, pair), 0) // hd
        c = lax.broadcasted_iota(jnp.int32, (pair, pair), 1) // hd
        head_mean = jnp.where(r == c, 1.0 / hd, 0.0).astype(BF16)

        def head_norm(x, gain):
            ms = jnp.dot((x * x).astype(BF16), head_mean, preferred_element_type=F32)
            return x * lax.rsqrt(ms + EPS) * gain

        kn = head_norm(k_ref[0], kg_ref[...])
        kmean = jnp.mean(kn.reshape(nb, bs, pair), axis=1)
        kn_b = kn.astype(BF16)
        v_b = v_ref[0].astype(BF16)
        lane = lax.broadcasted_iota(jnp.int32, (S, pair), 1)
        blk = lax.broadcasted_iota(jnp.int32, (S, pair), 0) // bs
        lane_m = lax.broadcasted_iota(jnp.int32, (nb, pair), 1)
        rowb = lax.broadcasted_iota(jnp.int32, (nb, bs), 0)
        lane_q = lax.broadcasted_iota(jnp.int32, (bs, pair), 1)
        kmean_b = []
        for h in range(2):
            own = (lane // hd) == h
            other0 = (1 - h) * hd
            kaug_ref[h] = jnp.where(own, kn_b, (lane - other0 == blk).astype(BF16))
            vaug_ref[h] = jnp.where(own, v_b, jnp.ones((S, pair), BF16))
            kmean_b.append(jnp.where((lane_m // hd) == h, kmean, 0.0).astype(BF16))
        for j in range(nb):
            rows = slice(j * bs, (j + 1) * bs)
            qn = head_norm(q_ref[0, rows, :], qg_ref[...])
            qn_b = qn.astype(BF16)
            qs = qn * (ATT_HEAD_DIM ** -0.5 * LOG2E)
            for h in range(2):
                if j <= MOBA_TOPK:
                    sel = rowb <= j
                else:
                    gate = lax.dot_general(kmean_b[h], qn_b, nt_dims,
                                           preferred_element_type=F32)
                    rank = jnp.zeros((nb, bs), jnp.int32)
                    for m in range(j):
                        gm = gate[m:m + 1, :]
                        beats = (gm > gate) | ((gm == gate) & (rowb > m))
                        rank = rank + beats.astype(jnp.int32)
                    sel = ((rowb < j) & (rank < MOBA_TOPK)) | (rowb == j)
                selmask = jnp.where(sel, 0.0, MASK_VALUE)
                pieces = [selmask, jnp.zeros((pair - nb, bs), F32)]
                if h == 0:
                    pieces = [jnp.zeros((hd, bs), F32), selmask, jnp.zeros((hd - nb, bs), F32)]
                mask_t = jnp.concatenate(pieces, axis=0).T
                qaug_ref[h, rows, :] = jnp.where((lane_q // hd) == h, qs, mask_t).astype(BF16)

    qt = 2 * bs
    q0 = pl.multiple_of(jp * qt, qt)
    all_rows = slice(0, qt)
    late_rows = slice(bs, qt)
    for h in range(2):
        m_ref[h] = jnp.full((qt, 2 * hd), -jnp.inf, F32)
        acc_ref[h] = jnp.zeros((qt, 2 * hd), F32)

    def scores(n, s_ref, rows=all_rows):
        start = pl.multiple_of(n * bs, bs)
        q_rows = pl.ds(q0 + rows.start, rows.stop - rows.start)
        for h in range(2):
            s_ref[h, rows, :] = lax.dot_general(qaug_ref[h, q_rows, :], kaug_ref[h, pl.ds(start, bs), :],
                                                nt_dims, preferred_element_type=F32)

    def update(n, s_ref, bias, rows=all_rows):
        start = pl.multiple_of(n * bs, bs)
        for r0 in range(rows.start, rows.stop, bs):
            sub = slice(r0, r0 + bs)
            for h in range(2):
                s = s_ref[h, sub, :]
                tile = None if bias is None else bias(h, r0 // bs)
                if tile is not None:
                    s = s + tile
                m_i = m_ref[h, sub, :]
                m_new = jnp.maximum(m_i, jnp.max(s, axis=-1, keepdims=True))
                alpha = jnp.exp2(m_i - m_new)
                p = jnp.exp2((s - jnp.concatenate([m_new, m_new], axis=-1)).astype(BF16))
                pv = jnp.dot(p, vaug_ref[h, pl.ds(start, bs), :], preferred_element_type=F32)
                acc_ref[h, sub, :] = alpha * acc_ref[h, sub, :] + pv
                m_ref[h, sub, :] = m_new

    def far_pair(n):
        scores(n + 1, sb_ref)
        update(n, sa_ref, None)
        scores(n + 2, sa_ref)
        update(n + 1, sb_ref, None)

    def far_quad(i, _):
        far_pair(4 * i)
        far_pair(4 * i + 2)
        return 0

    n_far = jnp.maximum(2 * jp - 1, 0)
    scores(0, sa_ref)
    lax.fori_loop(0, n_far >> 2, far_quad, 0)

    @pl.when((n_far & 2) != 0)
    def _():
        far_pair(n_far & ~3)

    prev = lambda h: bias_ref[h, :, 0:bs]
    own = lambda h, qblk=1: bias_ref[h, :, bs:2 * bs]
    bias_p = lambda h, qblk: prev(h) if qblk == 0 else None
    bias_o = lambda h, qblk: own(h) if qblk == 0 else prev(h)
    blk_p = jnp.maximum(2 * jp - 1, 0)
    blk_o = 2 * jp
    blk_n = 2 * jp + 1

    @pl.when(jp > 0)
    def _():
        scores(blk_p, sb_ref)
        update(n_far - 1, sa_ref, None)
        scores(blk_o, sa_ref)
        update(blk_p, sb_ref, bias_p)
        scores(blk_n, sb_ref, late_rows)
        update(blk_o, sa_ref, bias_o)
        update(blk_n, sb_ref, own, late_rows)

    @pl.when(jp == 0)
    def _():
        scores(blk_n, sb_ref, late_rows)
        update(blk_o, sa_ref, bias_o)
        update(blk_n, sb_ref, own, late_rows)

    ratio = [acc_ref[h] / pltpu.roll(acc_ref[h], hd, 1) for h in range(2)]
    lane_o = lax.broadcasted_iota(jnp.int32, (qt, 2 * hd), 1)
    o_ref[0] = jnp.where(lane_o < hd, ratio[0], ratio[1])


def _attention(q, k, v, q_g, k_g, bias_tiles):
    B, S, _ = q.shape
    bs = MOBA_BLOCK
    nb = S // bs
    assert S % (2 * bs) == 0 and nb <= ATT_HEAD_DIM
    pair = 2 * ATT_HEAD_DIM
    return pl.pallas_call(
        _attn_body, grid=(B, ATT_HEADS // 2, nb // 2),
        in_specs=[
            pl.BlockSpec((1, S, pair), lambda b, hp, i: (b, 0, hp)),
            pl.BlockSpec((1, S, pair), lambda b, hp, i: (b, 0, hp)),
            pl.BlockSpec((1, S, pair), lambda b, hp, i: (b, 0, hp)),
            pl.BlockSpec((1, pair), lambda b, hp, i: (0, 0)),
            pl.BlockSpec((1, pair), lambda b, hp, i: (0, 0)),
            pl.BlockSpec((2, bs, 2 * bs), lambda b, hp, i: (hp, 0, 0)),
        ],
        out_specs=pl.BlockSpec((1, 2 * bs, pair), lambda b, hp, i: (b, i, hp)),
        out_shape=jax.ShapeDtypeStruct((B, S, ATT_INNER), F32),
        scratch_shapes=[pltpu.VMEM((2, S, pair), BF16), pltpu.VMEM((2, S, pair), BF16),
                        pltpu.VMEM((2, S, pair), BF16),
                        pltpu.VMEM((2, 2 * bs, bs), F32), pltpu.VMEM((2, 2 * bs, bs), F32),
                        pltpu.VMEM((2, 2 * bs, pair), F32),
                        pltpu.VMEM((2, 2 * bs, pair), F32)],
        compiler_params=_params("arbitrary", "arbitrary", "arbitrary"), name="moba")(
            q, k, v, jnp.tile(q_g, 2).reshape(1, pair), jnp.tile(k_g, 2).reshape(1, pair), bias_tiles)


def _outproj_body(h_ref, ys_ref, ya_ref, wa_ref, wb_ref, g_ref, wrh_ref, wrl_ref, br_ref, tri_ref,
                  hn_ref, xn_ref, route_ref, routet_ref, counts_ref, carry_ref):
    i = pl.program_id(0)

    @pl.when(i == 0)
    def _():
        carry_ref[...] = jnp.zeros_like(carry_ref)

    _outproj_rows(0, OUT_ROWS, h_ref, ys_ref, ya_ref, wa_ref, wb_ref, g_ref, wrh_ref, wrl_ref,
                  br_ref, tri_ref, hn_ref, xn_ref, route_ref, routet_ref, carry_ref)
    counts_ref[...] = carry_ref[...]


def _outproj_rows(r0, n, h_ref, ys_ref, ya_ref, wa_ref, wb_ref, g_ref, wrh_ref, wrl_ref, br_ref, tri_ref,
                  hn_ref, xn_ref, route_ref, routet_ref, carry_ref):
    rs = slice(r0, r0 + n)
    hn = (h_ref[rs, :]
          + jnp.dot(ys_ref[rs, :].astype(BF16), wa_ref[...], preferred_element_type=F32)
          + jnp.dot(ya_ref[rs, :].astype(BF16), wb_ref[...], preferred_element_type=F32))
    hn_ref[rs, :] = hn
    xn = hn * lax.rsqrt(jnp.mean(hn * hn, axis=-1, keepdims=True) + EPS) * g_ref[...]
    for c in range(SUBLANES):
        xn_ref[pl.ds(r0 * SUBLANES + c, n, stride=SUBLANES), :] = xn[:, c * LANES:(c + 1) * LANES]
    x_hi = xn.astype(BF16)
    x_lo = (xn - x_hi.astype(F32)).astype(BF16)
    hi_terms = jnp.dot(x_hi, wrl_ref[...], preferred_element_type=F32)
    logits = (hi_terms[:, :LANES] + hi_terms[:, LANES:]
              + jnp.dot(x_lo, wrh_ref[...], preferred_element_type=F32)) + br_ref[...]
    lt = logits.T[0:ROUTER_ROWS, :]
    row = lax.broadcasted_iota(jnp.int32, (ROUTER_ROWS, n), 0)
    first = lambda hit: jnp.min(jnp.where(hit, row, LANES), axis=0, keepdims=True)
    l1 = jnp.where(row < MOE_GROUPS, lt, -jnp.inf)
    mx = jnp.max(l1, axis=0, keepdims=True)
    gval = 1.0 / jnp.sum(jnp.exp(l1 - mx), axis=0, keepdims=True)
    gidx = first(l1 == mx)
    e_row = row - ROUTER_LANE0
    in_grp = (e_row >= 0) & (e_row < MOE_EXPERTS) & ((e_row >> 3) == gidx)
    l2 = jnp.where(in_grp, lt, -jnp.inf)
    m1 = jnp.max(l2, axis=0, keepdims=True)
    i1 = first(l2 == m1)
    l2b = jnp.where(row == i1, -jnp.inf, l2)
    m2 = jnp.max(l2b, axis=0, keepdims=True)
    i2 = first(l2b == m2)
    r = jnp.exp(m2 - m1)
    g1 = gval / (1.0 + r)
    g2 = gval * r / (1.0 + r)
    oh1 = (row == i1)
    oh2 = (row == i2)
    both = jnp.where(oh1 | oh2, 1.0, 0.0)
    cx = jnp.dot(both.astype(BF16), tri_ref[...], preferred_element_type=F32) + carry_ref[...]
    rank1 = jnp.sum(jnp.where(oh1, cx, 0.0), axis=0, keepdims=True)
    rank2 = jnp.sum(jnp.where(oh2, cx, 0.0), axis=0, keepdims=True)
    carry_ref[...] = carry_ref[...] + jnp.sum(both, axis=1, keepdims=True)
    routet = jnp.concatenate(
        [(i1 - ROUTER_LANE0).astype(F32), (i2 - ROUTER_LANE0).astype(F32), g1, g2, rank1, rank2,
         jnp.zeros((SUBLANES - 6, n), F32)], axis=0)
    routet_ref[:, rs] = routet
    route_ref[rs, :] = jnp.concatenate([routet, jnp.zeros((LANES - SUBLANES, n), F32)], axis=0).T


def _outproj_route(h2, y_ssd, y_att, w_out, norm_g, w_r1, b_r1, w_r2, b_r2):
    T = h2.shape[0]
    R = OUT_ROWS
    wb = w_out.astype(BF16)
    pad = LANES - MOE_GROUPS - MOE_EXPERTS
    wr = jnp.pad(jnp.concatenate([w_r1, w_r2], axis=1), ((0, 0), (0, pad)))
    wr_hi = wr.astype(BF16)
    wr_lo = (wr - wr_hi.astype(F32)).astype(BF16)
    br = jnp.pad(jnp.concatenate([b_r1, b_r2]), (0, pad)).reshape(1, LANES)
    tri = jnp.asarray(np.triu(np.ones((R, R), np.float32), 1), dtype=BF16)
    ins = (h2, y_ssd, y_att, wb[:SSD_INNER], wb[SSD_INNER:], norm_g.reshape(1, D_MODEL),
           wr_hi, jnp.concatenate([wr_hi, wr_lo], axis=1), br, tri)
    full = lambda a: pl.BlockSpec(a.shape, lambda i: (0,) * a.ndim)
    rows = lambda n: pl.BlockSpec((R, n), lambda i: (i, 0))
    return pl.pallas_call(
        _outproj_body, grid=(T // R,),
        in_specs=[rows(D_MODEL), rows(SSD_INNER), rows(ATT_INNER)] + [full(a) for a in ins[3:]],
        out_specs=(rows(D_MODEL), pl.BlockSpec((R * SUBLANES, LANES), lambda i: (i, 0)), rows(LANES),
                   pl.BlockSpec((SUBLANES, R), lambda i: (0, i)),
                   pl.BlockSpec((ROUTER_ROWS, 1), lambda i: (0, 0))),
        out_shape=(jax.ShapeDtypeStruct((T, D_MODEL), F32),
                   jax.ShapeDtypeStruct((T * SUBLANES, LANES), F32),
                   jax.ShapeDtypeStruct((T, LANES), F32), jax.ShapeDtypeStruct((SUBLANES, T), F32),
                   jax.ShapeDtypeStruct((ROUTER_ROWS, 1), F32)),
        scratch_shapes=[pltpu.VMEM((ROUTER_ROWS, 1), F32)],
        compiler_params=_params("arbitrary"), name="outproj_route")(*ins)


def _slots_body(routet_ref, pstart_ref, dest_ref):
    cols = routet_ref.shape[1]
    expert = lax.broadcasted_iota(jnp.int32, (MOE_EXPERTS, cols), 0).astype(F32)
    pstart = pstart_ref[...]
    for j in range(2):
        hit = expert == routet_ref[j:j + 1, :]
        base = jnp.sum(jnp.where(hit, pstart, 0.0), axis=0, keepdims=True)
        dest_ref[j:j + 1, :] = (base + routet_ref[4 + j:5 + j, :]).astype(jnp.int32)


def _slots(routet, pstarts):
    T = routet.shape[1]
    cols = min(T, 4096)
    assert T % cols == 0
    return pl.pallas_call(
        _slots_body, grid=(T // cols,),
        in_specs=[pl.BlockSpec((SUBLANES, cols), lambda i: (0, i)),
                  pl.BlockSpec((MOE_EXPERTS, 1), lambda i: (0, 0))],
        out_specs=pl.BlockSpec((2, cols), lambda i: (0, i)),
        out_shape=jax.ShapeDtypeStruct((2, T), jnp.int32),
        compiler_params=_params("arbitrary"), name="slots")(
            routet, pstarts.astype(F32).reshape(MOE_EXPERTS, 1))


def _tile_copy(src_ref, src_row, dst_ref, dst_row, sem):
    tile = lambda row: pl.ds(pl.multiple_of(row * SUBLANES, SUBLANES), SUBLANES)
    return pltpu.make_async_copy(src_ref.at[tile(src_row)], dst_ref.at[tile(dst_row)], sem)


def _dispatch_body(pad_ref, d1_ref, d2_ref, xn_ref, xb_ref, zero_ref, stage_ref, sem, lsem, ssem):
    block = EXPERT_ROWS * SUBLANES
    n_blocks = xb_ref.shape[0] // block
    n_used = pad_ref[MOE_EXPERTS]

    def zero_copy(row):
        start = pl.multiple_of(row * SUBLANES, SUBLANES)
        return pltpu.make_async_copy(zero_ref, xb_ref.at[pl.ds(start, block)], sem)

    @pl.when(pl.program_id(0) == 0)
    def _():
        zero_ref[...] = jnp.zeros_like(zero_ref)
        for e in range(MOE_EXPERTS):
            zero_copy(pad_ref[e]).start()
        for e in range(MOE_EXPERTS):
            zero_copy(pad_ref[e]).wait()
        for b in range(n_blocks - MOE_EXPERTS - 1, n_blocks):
            @pl.when(b >= n_used)
            def _():
                zero_copy(b * EXPERT_ROWS).start()
        for b in range(n_blocks - MOE_EXPERTS - 1, n_blocks):
            @pl.when(b >= n_used)
            def _():
                zero_copy(b * EXPERT_ROWS).wait()

    i = pl.program_id(0)
    n = pl.num_programs(0)
    rows = MOVE_ROWS * SUBLANES

    def load(step, slot):
        src = xn_ref.at[pl.ds(pl.multiple_of(step * rows, rows), rows)]
        return pltpu.make_async_copy(src, stage_ref.at[slot], lsem.at[slot])

    def wait_scatters(slot):
        for _ in range(2):
            pltpu.make_async_copy(stage_ref.at[slot], stage_ref.at[slot], ssem.at[slot]).wait()

    @pl.when(i == 0)
    def _():
        load(0, 0).start()

    @pl.when(i >= 1)
    def _():
        wait_scatters((i - 1) & 1)

    @pl.when(i + 1 < n)
    def _():
        load(i + 1, (i + 1) & 1).start()

    slot = i & 1
    load(i, slot).wait()
    base = i * MOVE_ROWS

    def scatter(t, _):
        _tile_copy(stage_ref.at[slot], t, xb_ref, d1_ref[base + t], ssem.at[slot]).start()
        _tile_copy(stage_ref.at[slot], t, xb_ref, d2_ref[base + t], ssem.at[slot]).start(priority=1)
        return 0

    lax.fori_loop(0, MOVE_ROWS, scatter, 0)

    @pl.when(i == n - 1)
    def _():
        wait_scatters(slot)


def _dispatch(xn, dest1, dest2, pad_start, n_slots):
    T = dest1.shape[0]
    grid_spec = pltpu.PrefetchScalarGridSpec(
        num_scalar_prefetch=3, grid=(T // MOVE_ROWS,),
        in_specs=[pl.BlockSpec(memory_space=pl.ANY)],
        out_specs=pl.BlockSpec(memory_space=pl.ANY),
        scratch_shapes=[pltpu.VMEM((EXPERT_ROWS * SUBLANES, LANES), F32),
                        pltpu.VMEM((2, MOVE_ROWS * SUBLANES, LANES), F32),
                        pltpu.SemaphoreType.DMA(()), pltpu.SemaphoreType.DMA((2,)),
                        pltpu.SemaphoreType.DMA((2,))])
    return pl.pallas_call(
        _dispatch_body, grid_spec=grid_spec,
        out_shape=jax.ShapeDtypeStruct(((n_slots + EXPERT_ROWS) * SUBLANES, LANES), F32),
        compiler_params=_params("arbitrary"), name="dispatch")(pad_start, dest1, dest2, xn)


def _expert_body(be_ref, nused_ref, xb_ref, wg_ref, wu_ref, wd_ref, yb_ref,
                 x_ref, wgb_ref, wub_ref, wdb_ref):
    i = pl.program_id(0)
    used = i < nused_ref[0]
    new_expert = (i == 0) | (be_ref[i] != be_ref[jnp.maximum(i - 1, 0)])

    @pl.when(used & new_expert)
    def _():
        wgb_ref[...] = wg_ref[0, 0].astype(BF16)
        wub_ref[...] = wu_ref[0, 0].astype(BF16)
        wdb_ref[...] = wd_ref[0, 0].astype(BF16)

    @pl.when(used)
    def _():
        for c in range(SUBLANES):
            x_ref[:, c * LANES:(c + 1) * LANES] = xb_ref[pl.ds(c, EXPERT_ROWS, stride=SUBLANES), :].astype(BF16)
        x = x_ref[...]
        gate = jnp.dot(x, wgb_ref[...], preferred_element_type=F32)
        up = jnp.dot(x, wub_ref[...], preferred_element_type=F32)
        hmid = (_silu(gate) * up).astype(BF16)
        y = jnp.dot(hmid, wdb_ref[...], preferred_element_type=F32)
        for c in range(SUBLANES):
            yb_ref[pl.ds(c, EXPERT_ROWS, stride=SUBLANES), :] = y[:, c * LANES:(c + 1) * LANES]

    @pl.when(jnp.logical_not(used))
    def _():
        yb_ref[...] = jnp.zeros_like(yb_ref)


def _experts(xb, blk_exp, n_used, n_blk, layer, w_gate, w_up, w_down):
    M = EXPERT_ROWS
    tile = lambda index: pl.BlockSpec((M * SUBLANES, LANES), index)
    weight = lambda a: pl.BlockSpec((1, 1) + a.shape[2:], lambda i, be, nu: (layer, be[i], 0, 0))
    grid_spec = pltpu.PrefetchScalarGridSpec(
        num_scalar_prefetch=2, grid=(n_blk,),
        in_specs=[tile(lambda i, be, nu: (jnp.minimum(i, nu[0] - 1), 0)),
                  weight(w_gate), weight(w_up), weight(w_down)],
        out_specs=tile(lambda i, be, nu: (i, 0)),
        scratch_shapes=[pltpu.VMEM((M, D_MODEL), BF16), pltpu.VMEM((D_MODEL, MOE_HIDDEN), BF16),
                        pltpu.VMEM((D_MODEL, MOE_HIDDEN), BF16), pltpu.VMEM((MOE_HIDDEN, D_MODEL), BF16)])
    return pl.pallas_call(
        _expert_body, grid_spec=grid_spec,
        out_shape=jax.ShapeDtypeStruct((n_blk * M * SUBLANES, LANES), F32),
        compiler_params=_params("arbitrary"), name="experts")(
            blk_exp, n_used, xb, w_gate, w_up, w_down)


def _combine_body(d1_ref, d2_ref, h_ref, route_ref, yb_ref, o_ref, buf_ref, sem):
    i = pl.program_id(0)
    n = pl.num_programs(0)

    def gather(step, slot):
        base = step * MOVE_ROWS

        def body(t, _):
            _tile_copy(yb_ref, d1_ref[base + t], buf_ref.at[slot, 0], t, sem.at[slot]).start()
            _tile_copy(yb_ref, d2_ref[base + t], buf_ref.at[slot, 1], t, sem.at[slot]).start(priority=1)
            return 0

        lax.fori_loop(0, MOVE_ROWS, body, 0)

    @pl.when(i == 0)
    def _():
        gather(0, 0)

    @pl.when(i + 1 < n)
    def _():
        gather(i + 1, (i + 1) & 1)

    slot = i & 1
    pltpu.make_async_copy(buf_ref.at[slot], buf_ref.at[slot], sem.at[slot]).wait()
    route = route_ref[...]
    g1 = route[:, 2:3]
    g2 = route[:, 3:4]
    for c in range(SUBLANES):
        cs = slice(c * LANES, (c + 1) * LANES)
        rows = pl.ds(c, MOVE_ROWS, stride=SUBLANES)
        o_ref[:, cs] = h_ref[:, cs] + (buf_ref[slot, 0, rows, :] * g1 + buf_ref[slot, 1, rows, :] * g2)


def _combine(h2, route, yb, dest1, dest2):
    T = h2.shape[0]
    R = MOVE_ROWS
    grid_spec = pltpu.PrefetchScalarGridSpec(
        num_scalar_prefetch=2, grid=(T // R,),
        in_specs=[pl.BlockSpec((R, D_MODEL), lambda i, d1, d2: (i, 0)),
                  pl.BlockSpec((R, LANES), lambda i, d1, d2: (i, 0)), pl.BlockSpec(memory_space=pl.ANY)],
        out_specs=pl.BlockSpec((R, D_MODEL), lambda i, d1, d2: (i, 0)),
        scratch_shapes=[pltpu.VMEM((2, 2, R * SUBLANES, LANES), F32), pltpu.SemaphoreType.DMA((2,))])
    return pl.pallas_call(
        _combine_body, grid_spec=grid_spec, out_shape=jax.ShapeDtypeStruct((T, D_MODEL), F32),
        compiler_params=_params("arbitrary"), name="combine")(dest1, dest2, h2, route, yb)


def _moe(h2, xn, route, routet, counts, layer, w_gate, w_up, w_down):
    T = h2.shape[0]
    M = EXPERT_ROWS
    n_blk = (2 * T) // M + MOE_EXPERTS
    cnt = counts[ROUTER_LANE0:ROUTER_LANE0 + MOE_EXPERTS, 0].astype(jnp.int32)
    padded = (cnt + M - 1) // M * M
    ends = jnp.cumsum(padded)
    pstarts = ends - padded
    n_used = (ends[-1] // M).astype(jnp.int32)
    blk = jnp.minimum(jnp.arange(n_blk, dtype=jnp.int32), n_used - 1) * M
    blk_exp = jnp.sum((ends[None, :] <= blk[:, None]).astype(jnp.int32), axis=1)
    blk_exp = jnp.minimum(blk_exp, MOE_EXPERTS - 1)
    dest = _slots(routet, pstarts)
    pad_info = jnp.concatenate([pstarts + cnt, n_used.reshape(1)])
    xb = _dispatch(xn, dest[0], dest[1], pad_info, n_blk * M)
    yb = _experts(xb, blk_exp, n_used.reshape(1), n_blk, layer, w_gate, w_up, w_down)
    return _combine(h2, route, yb, dest[0], dest[1])


def kernel(x, rel_bias, norm1_g, w_in, conv_w, conv_b, dt_bias, a_log, d_skip, ssd_norm_g,
           q_norm_g, k_norm_g, w_out, norm2_g, w_r1, b_r1, w_r2, b_r2, w_gate, w_up, w_down):
    B, S, D = x.shape
    T = B * S
    h2 = x.reshape(T, D)
    depth = w_in.shape[0]
    bias_tiles = _bias_tiles(rel_bias)
    for l in range(depth):
        z, xbc, dt, dtt, q, k, v = _inproj(h2, norm1_g[l], w_in[l])
        r3 = lambda a: a.reshape(B, S, a.shape[-1])
        y_ssd = _ssd(r3(z), r3(xbc), r3(dt), dtt, conv_w[l], conv_b[l], dt_bias[l], a_log[l],
                     d_skip[l], ssd_norm_g[l])
        y_att = _attention(r3(q), r3(k), r3(v), q_norm_g[l], k_norm_g[l], bias_tiles)
        h2, xn, route, routet, counts = _outproj_route(
            h2, y_ssd.reshape(T, SSD_INNER), y_att.reshape(T, ATT_INNER), w_out[l], norm2_g[l],
            w_r1[l], b_r1[l], w_r2[l], b_r2[l])
        h2 = _moe(h2, xn, route, routet, counts, l, w_gate, w_up, w_down)
    return h2.reshape(B, S, D)
```

```python
import math

import numpy as np
import jax
import jax.numpy as jnp
from jax import lax
from jax.experimental import pallas as pl
from jax.experimental.pallas import tpu as pltpu

F32 = jnp.float32
BF16 = jnp.bfloat16
HIGHEST = lax.Precision.HIGHEST

LANES = 128
SUBLANES = 8

D_MODEL = 1024
SSD_HEADS = 8
SSD_HEAD_DIM = 64
SSD_INNER = SSD_HEADS * SSD_HEAD_DIM
SSD_GROUPS = 2
SSD_HEADS_PER_GROUP = SSD_HEADS // SSD_GROUPS
SSD_STATE = 64
SSD_CONV = 4
SSD_CHUNK = 128
CONV_CH = SSD_INNER + 2 * SSD_GROUPS * SSD_STATE
ATT_HEADS = 8
ATT_HEAD_DIM = 64
ATT_INNER = ATT_HEADS * ATT_HEAD_DIM
MOBA_BLOCK = 256
MOBA_TOPK = 3
REL_BUCKETS = 32
REL_MAX_DIST = 128
MOE_GROUPS = 4
MOE_EXPERTS_PER_GROUP = 8
MOE_EXPERTS = MOE_GROUPS * MOE_EXPERTS_PER_GROUP
MOE_HIDDEN = 512
EPS = 1e-6

MASK_VALUE = -1e30
LOG2E = math.log2(math.e)
IN_ROWS = 1024
SSD_STEP_CHUNKS = 8
OUT_ROWS = 512
EXPERT_ROWS = 512
MOVE_ROWS = 256
VMEM_LIMIT = 56 * 1024 * 1024
ROUTER_LANE0 = MOE_GROUPS
ROUTER_ROWS = 48


def _silu(x):
    return x * (1.0 / (1.0 + jnp.exp(-x)))


def _softplus(x):
    return jnp.maximum(x, 0.0) + jnp.log(1.0 + jnp.exp(-jnp.abs(x)))


def _params(*sem):
    return pltpu.CompilerParams(dimension_semantics=sem, vmem_limit_bytes=VMEM_LIMIT)


def _inproj_body(h_ref, g_ref, wz_ref, wx_ref, wdt_ref, wq_ref, wk_ref, wv_ref,
                 z_ref, xbc_ref, dt_ref, dtt_ref, q_ref, k_ref, v_ref):
    x = h_ref[...]
    xn = x * lax.rsqrt(jnp.mean(x * x, axis=-1, keepdims=True) + EPS) * g_ref[...]
    xb = xn.astype(BF16)
    z_ref[...] = jnp.dot(xb, wz_ref[...], preferred_element_type=F32)
    xbc_ref[...] = jnp.dot(xb, wx_ref[...], preferred_element_type=F32)
    dt = jnp.dot(xb, wdt_ref[...], preferred_element_type=F32)
    dt_ref[...] = dt
    dtt_ref[...] = dt.T[0:SSD_HEADS, :]
    q_ref[...] = jnp.dot(xb, wq_ref[...], preferred_element_type=F32)
    k_ref[...] = jnp.dot(xb, wk_ref[...], preferred_element_type=F32)
    v_ref[...] = jnp.dot(xb, wv_ref[...], preferred_element_type=F32)


def _inproj(h2, g, w_in):
    T = h2.shape[0]
    c0, c1, c2 = SSD_INNER, SSD_INNER + CONV_CH, SSD_INNER + CONV_CH + SSD_HEADS
    wb = w_in.astype(BF16)
    wz, wx, wdt = wb[:, :c0], wb[:, c0:c1], wb[:, c1:c2]
    wq, wk, wv = (wb[:, c2 + i * ATT_INNER:c2 + (i + 1) * ATT_INNER] for i in range(3))
    wdt_pad = jnp.pad(wdt, ((0, 0), (0, LANES - SSD_HEADS)))
    full = lambda a: pl.BlockSpec(a.shape, lambda i: (0,) * a.ndim)
    rows = lambda n: pl.BlockSpec((IN_ROWS, n), lambda i: (i, 0))
    ins = (h2, g.reshape(1, D_MODEL), wz, wx, wdt_pad, wq, wk, wv)
    out_shape = (
        jax.ShapeDtypeStruct((T, SSD_INNER), F32), jax.ShapeDtypeStruct((T, CONV_CH), F32),
        jax.ShapeDtypeStruct((T, LANES), F32), jax.ShapeDtypeStruct((SSD_HEADS, T), F32),
        jax.ShapeDtypeStruct((T, ATT_INNER), F32), jax.ShapeDtypeStruct((T, ATT_INNER), F32),
        jax.ShapeDtypeStruct((T, ATT_INNER), F32))
    out_specs = (rows(SSD_INNER), rows(CONV_CH), rows(LANES),
                 pl.BlockSpec((SSD_HEADS, IN_ROWS), lambda i: (0, i)),
                 rows(ATT_INNER), rows(ATT_INNER), rows(ATT_INNER))
    return pl.pallas_call(
        _inproj_body, grid=(T // IN_ROWS,),
        in_specs=[rows(D_MODEL)] + [full(a) for a in ins[1:]],
        out_specs=out_specs, out_shape=out_shape,
        compiler_params=_params("arbitrary"), name="inproj")(*ins)


def _ssd_body(z_ref, xbc_ref, dt_ref, dtt_ref, cw_ref, cb_ref, dtb_ref, dtbt_ref, alog_ref,
              alogt_ref, dskip_ref, ng_ref, tril_ref, expand_ref, y_ref, ext_ref, state_ref):
    L, P, N, E = SSD_CHUNK, SSD_HEAD_DIM, SSD_STATE, SSD_HEADS_PER_GROUP
    R = SSD_STEP_CHUNKS * L
    c = pl.program_id(1)

    @pl.when(c == 0)
    def _():
        state_ref[...] = jnp.zeros_like(state_ref)
        ext_ref[0:SUBLANES, :] = jnp.zeros((SUBLANES, CONV_CH), F32)

    ext_ref[SUBLANES:SUBLANES + R, :] = xbc_ref[0]
    tril = tril_ref[...]
    expand = expand_ref[...]
    row = lax.broadcasted_iota(jnp.int32, (L, L), 0)
    col = lax.broadcasted_iota(jnp.int32, (L, L), 1)
    causal = row >= col
    for sub in range(SSD_STEP_CHUNKS):
        _ssd_chunk(sub * L, z_ref, dt_ref, dtt_ref, cw_ref, cb_ref, dtb_ref, dtbt_ref, alog_ref,
                   alogt_ref, dskip_ref, ng_ref, tril, expand, causal, y_ref, ext_ref, state_ref)
    ext_ref[0:SUBLANES, :] = ext_ref[R:R + SUBLANES, :]


def _ssd_chunk(r0, z_ref, dt_ref, dtt_ref, cw_ref, cb_ref, dtb_ref, dtbt_ref, alog_ref, alogt_ref,
               dskip_ref, ng_ref, tril, expand, causal, y_ref, ext_ref, state_ref):
    L, P, N, E = SSD_CHUNK, SSD_HEAD_DIM, SSD_STATE, SSD_HEADS_PER_GROUP
    conv = cb_ref[...]
    for j in range(SSD_CONV):
        off = r0 + SUBLANES - (SSD_CONV - 1) + j
        conv = conv + cw_ref[j:j + 1, :] * ext_ref[off:off + L, :]
    act = _silu(conv)
    xs = act[:, :SSD_INNER]
    b_in = act[:, SSD_INNER:SSD_INNER + SSD_GROUPS * N]
    c_out = act[:, SSD_INNER + SSD_GROUPS * N:]

    dt = _softplus(dt_ref[0, r0:r0 + L, :] + dtb_ref[...])
    a_dt = dt * (-jnp.exp(alog_ref[...]))
    a_cs = jnp.dot(tril, a_dt, precision=HIGHEST, preferred_element_type=F32)
    dt_t = _softplus(dtt_ref[:, r0:r0 + L] + dtbt_ref[...])
    a_dt_t = dt_t * (-jnp.exp(alogt_ref[...]))
    a_cs_t = lax.dot_general(a_dt_t, tril, (((1,), (1,)), ((), ())), precision=HIGHEST,
                             preferred_element_type=F32)
    last = a_cs[L - 1:L, :]
    per_head = jnp.concatenate(
        [dt, jnp.exp(a_cs), jnp.exp(last - a_cs), jnp.broadcast_to(jnp.exp(last), (SUBLANES, LANES))],
        axis=0)
    hi = per_head.astype(BF16)
    lo = (per_head - hi.astype(F32)).astype(BF16)
    on_lanes = jnp.dot(jnp.concatenate([hi, lo], axis=-1), expand, preferred_element_type=F32)
    dt_x = on_lanes[0:L]
    eacs_x = on_lanes[L:2 * L]
    dec_x = on_lanes[2 * L:3 * L]
    cdec_x = on_lanes[3 * L:3 * L + 1]
    xdt = xs * dt_x
    xdtd = (xdt * dec_x).astype(BF16)
    xdt_b = xdt.astype(BF16)

    y_parts = []
    for g in range(SSD_GROUPS):
        bg = b_in[:, g * N:(g + 1) * N].astype(BF16)
        cg = c_out[:, g * N:(g + 1) * N].astype(BF16)
        cb = lax.dot_general(cg, bg, (((1,), (1,)), ((), ())), preferred_element_type=F32)
        st = state_ref[g]
        gs = slice(g * E * P, (g + 1) * E * P)
        y_off = jnp.dot(cg, st.astype(BF16), preferred_element_type=F32) * eacs_x[:, gs]
        for e in range(E):
            h = g * E + e
            diff = a_cs[:, h:h + 1] - a_cs_t[h:h + 1, :]
            l_mat = jnp.exp(jnp.where(causal, diff, -jnp.inf))
            m = (cb * l_mat).astype(BF16)
            y_diag = jnp.dot(m, xdt_b[:, h * P:(h + 1) * P], preferred_element_type=F32)
            y_parts.append(y_diag + y_off[:, e * P:(e + 1) * P])
        upd = lax.dot_general(bg, xdtd[:, gs], (((0,), (0,)), ((), ())),
                              preferred_element_type=F32)
        state_ref[g] = cdec_x[:, gs] * st + upd
    y = jnp.concatenate(y_parts, axis=-1) + dskip_ref[...] * xs
    y = y * _silu(z_ref[0, r0:r0 + L, :])
    y = y * lax.rsqrt(jnp.mean(y * y, axis=-1, keepdims=True) + EPS) * ng_ref[...]
    y_ref[0, r0:r0 + L, :] = y


def _ssd(z, xbc, dt, dtt, conv_w, conv_b, dt_bias, a_log, d_skip, norm_g):
    B, S, _ = z.shape
    L = SSD_CHUNK
    R = SSD_STEP_CHUNKS * L
    nc = S // R
    pad_row = lambda a: jnp.pad(a.reshape(1, SSD_HEADS), ((0, 0), (0, LANES - SSD_HEADS)))
    tril = jnp.asarray(np.tril(np.ones((L, L), np.float32)))
    expand_np = np.zeros((LANES, SSD_INNER), np.float32)
    for h in range(SSD_HEADS):
        expand_np[h, h * SSD_HEAD_DIM:(h + 1) * SSD_HEAD_DIM] = 1.0
    ins = (z, xbc, dt, dtt, conv_w, conv_b.reshape(1, CONV_CH), pad_row(dt_bias),
           dt_bias.reshape(SSD_HEADS, 1), pad_row(a_log), a_log.reshape(SSD_HEADS, 1),
           jnp.repeat(d_skip, SSD_HEAD_DIM).reshape(1, SSD_INNER), norm_g.reshape(1, SSD_INNER),
           tril, jnp.asarray(np.concatenate([expand_np, expand_np], axis=0), dtype=BF16))
    full = lambda a: pl.BlockSpec(a.shape, lambda b, c: (0,) * a.ndim)
    chunk = lambda n: pl.BlockSpec((1, R, n), lambda b, c: (b, c, 0))
    in_specs = [chunk(SSD_INNER), chunk(CONV_CH), chunk(LANES),
                pl.BlockSpec((SSD_HEADS, R), lambda b, c: (0, b * nc + c))]
    in_specs += [full(a) for a in ins[4:]]
    return pl.pallas_call(
        _ssd_body, grid=(B, nc), in_specs=in_specs, out_specs=chunk(SSD_INNER),
        out_shape=jax.ShapeDtypeStruct((B, S, SSD_INNER), F32),
        scratch_shapes=[pltpu.VMEM((R + SUBLANES, CONV_CH), F32),
                        pltpu.VMEM((SSD_GROUPS, SSD_STATE, SSD_HEADS_PER_GROUP * SSD_HEAD_DIM), F32)],
        compiler_params=_params("arbitrary", "arbitrary"), name="ssd")(*ins)


def _bucket_tile():
    bs = MOBA_BLOCK
    dist = np.arange(bs)[:, None] - np.arange(2 * bs)[None, :] + bs
    max_exact = REL_BUCKETS // 2
    d = np.maximum(dist, max_exact).astype(np.float32)
    large = max_exact + (np.log(d / np.float32(max_exact)) / np.float32(math.log(REL_MAX_DIST / max_exact))
                         * np.float32(REL_BUCKETS - max_exact)).astype(np.int32)
    large = np.minimum(large, REL_BUCKETS - 1)
    bucket = np.where(dist < max_exact, dist, large)
    return np.where(dist >= 0, bucket, -1).astype(np.int32)


def _bias_body(rb_ref, bucket_ref, o_ref):
    head = pl.program_id(0)
    bucket = bucket_ref[...]
    far = rb_ref[REL_BUCKETS - 1, head]
    bias = jnp.where(bucket < 0, MASK_VALUE, 0.0)
    for i in range(REL_BUCKETS - 1):
        bias = jnp.where(bucket == i, (rb_ref[i, head] - far) * LOG2E, bias)
    o_ref[0] = bias


def _bias_tiles(rel_bias):
    bs = MOBA_BLOCK
    bucket = jnp.asarray(_bucket_tile())
    return pl.pallas_call(
        _bias_body, grid=(ATT_HEADS,),
        in_specs=[pl.BlockSpec(memory_space=pltpu.SMEM), pl.BlockSpec((bs, 2 * bs), lambda h: (0, 0))],
        out_specs=pl.BlockSpec((1, bs, 2 * bs), lambda h: (h, 0, 0)),
        out_shape=jax.ShapeDtypeStruct((ATT_HEADS, bs, 2 * bs), F32),
        compiler_params=_params("arbitrary"), name="bias_tiles")(rel_bias, bucket)


def _attn_body(q_ref, k_ref, v_ref, qg_ref, kg_ref, bias_ref, o_ref,
               kaug_ref, vaug_ref, qaug_ref, sa_ref, sb_ref, m_ref, acc_ref):
    bs, hd = MOBA_BLOCK, ATT_HEAD_DIM
    nb = k_ref.shape[1] // bs
    jp = pl.program_id(2)
    nt_dims = (((1,), (1,)), ((), ()))

    @pl.when(jp == 0)
    def _():
        S = nb * bs
        pair = 2 * hd
        r = lax.broadcasted_iota(jnp.int32, (pair, pair), 0) // hd
        c = lax.broadcasted_iota(jnp.int32, (pair, pair), 1) // hd
        head_mean = jnp.where(r == c, 1.0 / hd, 0.0).astype(BF16)

        def head_norm(x, gain):
            ms = jnp.dot((x * x).astype(BF16), head_mean, preferred_element_type=F32)
            return x * lax.rsqrt(ms + EPS) * gain

        kn = head_norm(k_ref[0], kg_ref[...])
        kmean = jnp.mean(kn.reshape(nb, bs, pair), axis=1)
        kn_b = kn.astype(BF16)
        v_b = v_ref[0].astype(BF16)
        lane = lax.broadcasted_iota(jnp.int32, (S, pair), 1)
        blk = lax.broadcasted_iota(jnp.int32, (S, pair), 0) // bs
        lane_m = lax.broadcasted_iota(jnp.int32, (nb, pair), 1)
        rowb = lax.broadcasted_iota(jnp.int32, (nb, bs), 0)
        lane_q = lax.broadcasted_iota(jnp.int32, (bs, pair), 1)
        kmean_b = []
        for h in range(2):
            own = (lane // hd) == h
            other0 = (1 - h) * hd
            kaug_ref[h] = jnp.where(own, kn_b, (lane - other0 == blk).astype(BF16))
            vaug_ref[h] = jnp.where(own, v_b, jnp.ones((S, pair), BF16))
            kmean_b.append(jnp.where((lane_m // hd) == h, kmean, 0.0).astype(BF16))
        blocks = [slice(j * bs, (j + 1) * bs) for j in range(nb)]
        qn_all = [head_norm(q_ref[0, rows, :], qg_ref[...]) for rows in blocks]
        masks_all = []
        for j in range(nb):
            qn_b = qn_all[j].astype(BF16)
            masks = []
            for h in range(2):
                if j <= MOBA_TOPK:
                    sel = rowb <= j
                else:
                    gate = lax.dot_general(kmean_b[h], qn_b, nt_dims,
                                           preferred_element_type=F32)
                    beats = []
                    for m in range(j):
                        gm = gate[m:m + 1, :]
                        beats.append(((gm > gate) | ((gm == gate) & (rowb > m))).astype(jnp.int32))
                    while len(beats) > 1:
                        beats = [a + b for a, b in zip(beats[::2], beats[1::2])] + beats[len(beats) & ~1:]
                    sel = ((rowb < j) & (beats[0] < MOBA_TOPK)) | (rowb == j)
                masks.append(jnp.where(sel, 0.0, MASK_VALUE))
            masks_all.append(masks)
        gap = jnp.zeros((hd - nb, bs), F32)
        for j, rows in enumerate(blocks):
            mask_t = jnp.concatenate([masks_all[j][1], gap, masks_all[j][0], gap], axis=0).T
            qs = qn_all[j] * (ATT_HEAD_DIM ** -0.5 * LOG2E)
            for h in range(2):
                qaug_ref[h, rows, :] = jnp.where((lane_q // hd) == h, qs, mask_t).astype(BF16)

    qt = 2 * bs
    q0 = pl.multiple_of(jp * qt, qt)
    all_rows = slice(0, qt)
    late_rows = slice(bs, qt)
    for h in range(2):
        m_ref[h] = jnp.full((qt, 2 * hd), -jnp.inf, F32)
        acc_ref[h] = jnp.zeros((qt, 2 * hd), F32)

    def scores(n, s_ref, rows=all_rows):
        start = pl.multiple_of(n * bs, bs)
        q_rows = pl.ds(q0 + rows.start, rows.stop - rows.start)
        for h in range(2):
            s_ref[h, rows, :] = lax.dot_general(qaug_ref[h, q_rows, :], kaug_ref[h, pl.ds(start, bs), :],
                                                nt_dims, preferred_element_type=F32)

    def update(n, s_ref, bias, rows=all_rows):
        start = pl.multiple_of(n * bs, bs)
        for r0 in range(rows.start, rows.stop, bs):
            sub = slice(r0, r0 + bs)
            for h in range(2):
                s = s_ref[h, sub, :]
                tile = None if bias is None else bias(h, r0 // bs)
                if tile is not None:
                    s = s + tile
                m_i = m_ref[h, sub, :]
                m_new = jnp.maximum(m_i, jnp.max(s, axis=-1, keepdims=True))
                alpha = jnp.exp2(m_i - m_new)
                p = jnp.exp2((s - jnp.concatenate([m_new, m_new], axis=-1)).astype(BF16))
                pv = jnp.dot(p, vaug_ref[h, pl.ds(start, bs), :], preferred_element_type=F32)
                acc_ref[h, sub, :] = alpha * acc_ref[h, sub, :] + pv
                m_ref[h, sub, :] = m_new

    def far_pair(n):
        scores(n + 1, sb_ref)
        update(n, sa_ref, None)
        scores(n + 2, sa_ref)
        update(n + 1, sb_ref, None)

    def far_quad(i, _):
        far_pair(4 * i)
        far_pair(4 * i + 2)
        return 0

    n_far = jnp.maximum(2 * jp - 1, 0)
    scores(0, sa_ref)
    lax.fori_loop(0, n_far >> 2, far_quad, 0)

    @pl.when((n_far & 2) != 0)
    def _():
        far_pair(n_far & ~3)

    prev = lambda h: bias_ref[h, :, 0:bs]
    own = lambda h, qblk=1: bias_ref[h, :, bs:2 * bs]
    bias_p = lambda h, qblk: prev(h) if qblk == 0 else None
    bias_o = lambda h, qblk: own(h) if qblk == 0 else prev(h)
    blk_p = jnp.maximum(2 * jp - 1, 0)
    blk_o = 2 * jp
    blk_n = 2 * jp + 1

    @pl.when(jp > 0)
    def _():
        scores(blk_p, sb_ref)
        update(n_far - 1, sa_ref, None)
        scores(blk_o, sa_ref)
        update(blk_p, sb_ref, bias_p)
        scores(blk_n, sb_ref, late_rows)
        update(blk_o, sa_ref, bias_o)
        update(blk_n, sb_ref, own, late_rows)

    @pl.when(jp == 0)
    def _():
        scores(blk_n, sb_ref, late_rows)
        update(blk_o, sa_ref, bias_o)
        update(blk_n, sb_ref, own, late_rows)

    ratio = [acc_ref[h] / pltpu.roll(acc_ref[h], hd, 1) for h in range(2)]
    lane_o = lax.broadcasted_iota(jnp.int32, (qt, 2 * hd), 1)
    o_ref[0] = jnp.where(lane_o < hd, ratio[0], ratio[1])


def _attention(q, k, v, q_g, k_g, bias_tiles):
    B, S, _ = q.shape
    bs = MOBA_BLOCK
    nb = S // bs
    assert S % (2 * bs) == 0 and nb <= ATT_HEAD_DIM
    pair = 2 * ATT_HEAD_DIM
    return pl.pallas_call(
        _attn_body, grid=(B, ATT_HEADS // 2, nb // 2),
        in_specs=[
            pl.BlockSpec((1, S, pair), lambda b, hp, i: (b, 0, hp)),
            pl.BlockSpec((1, S, pair), lambda b, hp, i: (b, 0, hp)),
            pl.BlockSpec((1, S, pair), lambda b, hp, i: (b, 0, hp)),
            pl.BlockSpec((1, pair), lambda b, hp, i: (0, 0)),
            pl.BlockSpec((1, pair), lambda b, hp, i: (0, 0)),
            pl.BlockSpec((2, bs, 2 * bs), lambda b, hp, i: (hp, 0, 0)),
        ],
        out_specs=pl.BlockSpec((1, 2 * bs, pair), lambda b, hp, i: (b, i, hp)),
        out_shape=jax.ShapeDtypeStruct((B, S, ATT_INNER), F32),
        scratch_shapes=[pltpu.VMEM((2, S, pair), BF16), pltpu.VMEM((2, S, pair), BF16),
                        pltpu.VMEM((2, S, pair), BF16),
                        pltpu.VMEM((2, 2 * bs, bs), F32), pltpu.VMEM((2, 2 * bs, bs), F32),
                        pltpu.VMEM((2, 2 * bs, pair), F32),
                        pltpu.VMEM((2, 2 * bs, pair), F32)],
        compiler_params=_params("arbitrary", "arbitrary", "arbitrary"), name="moba")(
            q, k, v, jnp.tile(q_g, 2).reshape(1, pair), jnp.tile(k_g, 2).reshape(1, pair), bias_tiles)


def _outproj_body(h_ref, ys_ref, ya_ref, wa_ref, wb_ref, g_ref, wrh_ref, wrl_ref, br_ref, tri_ref,
                  hn_ref, xn_ref, route_ref, routet_ref, counts_ref, carry_ref):
    i = pl.program_id(0)

    @pl.when(i == 0)
    def _():
        carry_ref[...] = jnp.zeros_like(carry_ref)

    _outproj_rows(0, OUT_ROWS, h_ref, ys_ref, ya_ref, wa_ref, wb_ref, g_ref, wrh_ref, wrl_ref,
                  br_ref, tri_ref, hn_ref, xn_ref, route_ref, routet_ref, carry_ref)
    counts_ref[...] = carry_ref[...]


def _outproj_rows(r0, n, h_ref, ys_ref, ya_ref, wa_ref, wb_ref, g_ref, wrh_ref, wrl_ref, br_ref, tri_ref,
                  hn_ref, xn_ref, route_ref, routet_ref, carry_ref):
    rs = slice(r0, r0 + n)
    hn = (h_ref[rs, :]
          + jnp.dot(ys_ref[rs, :].astype(BF16), wa_ref[...], preferred_element_type=F32)
          + jnp.dot(ya_ref[rs, :].astype(BF16), wb_ref[...], preferred_element_type=F32))
    hn_ref[rs, :] = hn
    xn = hn * lax.rsqrt(jnp.mean(hn * hn, axis=-1, keepdims=True) + EPS) * g_ref[...]
    for c in range(SUBLANES):
        xn_ref[pl.ds(r0 * SUBLANES + c, n, stride=SUBLANES), :] = xn[:, c * LANES:(c + 1) * LANES]
    x_hi = xn.astype(BF16)
    x_lo = (xn - x_hi.astype(F32)).astype(BF16)
    hi_terms = jnp.dot(x_hi, wrl_ref[...], preferred_element_type=F32)
    logits = (hi_terms[:, :LANES] + hi_terms[:, LANES:]
              + jnp.dot(x_lo, wrh_ref[...], preferred_element_type=F32)) + br_ref[...]
    lt = logits.T[0:ROUTER_ROWS, :]
    row = lax.broadcasted_iota(jnp.int32, (ROUTER_ROWS, n), 0)
    first = lambda hit: jnp.min(jnp.where(hit, row, LANES), axis=0, keepdims=True)
    l1 = jnp.where(row < MOE_GROUPS, lt, -jnp.inf)
    mx = jnp.max(l1, axis=0, keepdims=True)
    gval = 1.0 / jnp.sum(jnp.exp(l1 - mx), axis=0, keepdims=True)
    gidx = first(l1 == mx)
    e_row = row - ROUTER_LANE0
    in_grp = (e_row >= 0) & (e_row < MOE_EXPERTS) & ((e_row >> 3) == gidx)
    l2 = jnp.where(in_grp, lt, -jnp.inf)
    m1 = jnp.max(l2, axis=0, keepdims=True)
    i1 = first(l2 == m1)
    l2b = jnp.where(row == i1, -jnp.inf, l2)
    m2 = jnp.max(l2b, axis=0, keepdims=True)
    i2 = first(l2b == m2)
    r = jnp.exp(m2 - m1)
    g1 = gval / (1.0 + r)
    g2 = gval * r / (1.0 + r)
    oh1 = (row == i1)
    oh2 = (row == i2)
    both = jnp.where(oh1 | oh2, 1.0, 0.0)
    cx = jnp.dot(both.astype(BF16), tri_ref[...], preferred_element_type=F32) + carry_ref[...]
    rank1 = jnp.sum(jnp.where(oh1, cx, 0.0), axis=0, keepdims=True)
    rank2 = jnp.sum(jnp.where(oh2, cx, 0.0), axis=0, keepdims=True)
    carry_ref[...] = carry_ref[...] + jnp.sum(both, axis=1, keepdims=True)
    routet = jnp.concatenate(
        [(i1 - ROUTER_LANE0).astype(F32), (i2 - ROUTER_LANE0).astype(F32), g1, g2, rank1, rank2,
         jnp.zeros((SUBLANES - 6, n), F32)], axis=0)
    routet_ref[:, rs] = routet
    route_ref[rs, :] = jnp.concatenate([routet, jnp.zeros((LANES - SUBLANES, n), F32)], axis=0).T


def _outproj_route(h2, y_ssd, y_att, w_out, norm_g, w_r1, b_r1, w_r2, b_r2):
    T = h2.shape[0]
    R = OUT_ROWS
    wb = w_out.astype(BF16)
    pad = LANES - MOE_GROUPS - MOE_EXPERTS
    wr = jnp.pad(jnp.concatenate([w_r1, w_r2], axis=1), ((0, 0), (0, pad)))
    wr_hi = wr.astype(BF16)
    wr_lo = (wr - wr_hi.astype(F32)).astype(BF16)
    br = jnp.pad(jnp.concatenate([b_r1, b_r2]), (0, pad)).reshape(1, LANES)
    tri = jnp.asarray(np.triu(np.ones((R, R), np.float32), 1), dtype=BF16)
    ins = (h2, y_ssd, y_att, wb[:SSD_INNER], wb[SSD_INNER:], norm_g.reshape(1, D_MODEL),
           wr_hi, jnp.concatenate([wr_hi, wr_lo], axis=1), br, tri)
    full = lambda a: pl.BlockSpec(a.shape, lambda i: (0,) * a.ndim)
    rows = lambda n: pl.BlockSpec((R, n), lambda i: (i, 0))
    return pl.pallas_call(
        _outproj_body, grid=(T // R,),
        in_specs=[rows(D_MODEL), rows(SSD_INNER), rows(ATT_INNER)] + [full(a) for a in ins[3:]],
        out_specs=(rows(D_MODEL), pl.BlockSpec((R * SUBLANES, LANES), lambda i: (i, 0)), rows(LANES),
                   pl.BlockSpec((SUBLANES, R), lambda i: (0, i)),
                   pl.BlockSpec((ROUTER_ROWS, 1), lambda i: (0, 0))),
        out_shape=(jax.ShapeDtypeStruct((T, D_MODEL), F32),
                   jax.ShapeDtypeStruct((T * SUBLANES, LANES), F32),
                   jax.ShapeDtypeStruct((T, LANES), F32), jax.ShapeDtypeStruct((SUBLANES, T), F32),
                   jax.ShapeDtypeStruct((ROUTER_ROWS, 1), F32)),
        scratch_shapes=[pltpu.VMEM((ROUTER_ROWS, 1), F32)],
        compiler_params=_params("arbitrary"), name="outproj_route")(*ins)


def _slots_body(routet_ref, pstart_ref, dest_ref):
    cols = routet_ref.shape[1]
    expert = lax.broadcasted_iota(jnp.int32, (MOE_EXPERTS, cols), 0).astype(F32)
    pstart = pstart_ref[...]
    for j in range(2):
        hit = expert == routet_ref[j:j + 1, :]
        base = jnp.sum(jnp.where(hit, pstart, 0.0), axis=0, keepdims=True)
        dest_ref[j:j + 1, :] = (base + routet_ref[4 + j:5 + j, :]).astype(jnp.int32)


def _slots(routet, pstarts):
    T = routet.shape[1]
    cols = min(T, 4096)
    assert T % cols == 0
    return pl.pallas_call(
        _slots_body, grid=(T // cols,),
        in_specs=[pl.BlockSpec((SUBLANES, cols), lambda i: (0, i)),
                  pl.BlockSpec((MOE_EXPERTS, 1), lambda i: (0, 0))],
        out_specs=pl.BlockSpec((2, cols), lambda i: (0, i)),
        out_shape=jax.ShapeDtypeStruct((2, T), jnp.int32),
        compiler_params=_params("arbitrary"), name="slots")(
            routet, pstarts.astype(F32).reshape(MOE_EXPERTS, 1))


def _tile_copy(src_ref, src_row, dst_ref, dst_row, sem):
    tile = lambda row: pl.ds(pl.multiple_of(row * SUBLANES, SUBLANES), SUBLANES)
    return pltpu.make_async_copy(src_ref.at[tile(src_row)], dst_ref.at[tile(dst_row)], sem)


def _dispatch_body(pad_ref, d1_ref, d2_ref, xn_ref, xb_ref, zero_ref, stage_ref, sem, lsem, ssem):
    block = EXPERT_ROWS * SUBLANES
    n_blocks = xb_ref.shape[0] // block
    n_used = pad_ref[MOE_EXPERTS]

    def zero_copy(row):
        start = pl.multiple_of(row * SUBLANES, SUBLANES)
        return pltpu.make_async_copy(zero_ref, xb_ref.at[pl.ds(start, block)], sem)

    @pl.when(pl.program_id(0) == 0)
    def _():
        zero_ref[...] = jnp.zeros_like(zero_ref)
        for e in range(MOE_EXPERTS):
            zero_copy(pad_ref[e]).start()
        for e in range(MOE_EXPERTS):
            zero_copy(pad_ref[e]).wait()
        for b in range(n_blocks - MOE_EXPERTS - 1, n_blocks):
            @pl.when(b >= n_used)
            def _():
                zero_copy(b * EXPERT_ROWS).start()
        for b in range(n_blocks - MOE_EXPERTS - 1, n_blocks):
            @pl.when(b >= n_used)
            def _():
                zero_copy(b * EXPERT_ROWS).wait()

    i = pl.program_id(0)
    n = pl.num_programs(0)
    rows = MOVE_ROWS * SUBLANES

    def load(step, slot):
        src = xn_ref.at[pl.ds(pl.multiple_of(step * rows, rows), rows)]
        return pltpu.make_async_copy(src, stage_ref.at[slot], lsem.at[slot])

    def wait_scatters(slot):
        for _ in range(2):
            pltpu.make_async_copy(stage_ref.at[slot], stage_ref.at[slot], ssem.at[slot]).wait()

    @pl.when(i == 0)
    def _():
        load(0, 0).start()

    @pl.when(i >= 1)
    def _():
        wait_scatters((i - 1) & 1)

    @pl.when(i + 1 < n)
    def _():
        load(i + 1, (i + 1) & 1).start()

    slot = i & 1
    load(i, slot).wait()
    base = i * MOVE_ROWS

    def scatter(t, _):
        _tile_copy(stage_ref.at[slot], t, xb_ref, d1_ref[base + t], ssem.at[slot]).start()
        _tile_copy(stage_ref.at[slot], t, xb_ref, d2_ref[base + t], ssem.at[slot]).start(priority=1)
        return 0

    lax.fori_loop(0, MOVE_ROWS, scatter, 0)

    @pl.when(i == n - 1)
    def _():
        wait_scatters(slot)


def _dispatch(xn, dest1, dest2, pad_start, n_slots):
    T = dest1.shape[0]
    grid_spec = pltpu.PrefetchScalarGridSpec(
        num_scalar_prefetch=3, grid=(T // MOVE_ROWS,),
        in_specs=[pl.BlockSpec(memory_space=pl.ANY)],
        out_specs=pl.BlockSpec(memory_space=pl.ANY),
        scratch_shapes=[pltpu.VMEM((EXPERT_ROWS * SUBLANES, LANES), F32),
                        pltpu.VMEM((2, MOVE_ROWS * SUBLANES, LANES), F32),
                        pltpu.SemaphoreType.DMA(()), pltpu.SemaphoreType.DMA((2,)),
                        pltpu.SemaphoreType.DMA((2,))])
    return pl.pallas_call(
        _dispatch_body, grid_spec=grid_spec,
        out_shape=jax.ShapeDtypeStruct(((n_slots + EXPERT_ROWS) * SUBLANES, LANES), F32),
        compiler_params=_params("arbitrary"), name="dispatch")(pad_start, dest1, dest2, xn)


def _expert_body(be_ref, nused_ref, xb_ref, wg_ref, wu_ref, wd_ref, yb_ref,
                 x_ref, wgb_ref, wub_ref, wdb_ref):
    i = pl.program_id(0)
    used = i < nused_ref[0]
    new_expert = (i == 0) | (be_ref[i] != be_ref[jnp.maximum(i - 1, 0)])

    @pl.when(used & new_expert)
    def _():
        wgb_ref[...] = wg_ref[0, 0].astype(BF16)
        wub_ref[...] = wu_ref[0, 0].astype(BF16)
        wdb_ref[...] = wd_ref[0, 0].astype(BF16)

    @pl.when(used)
    def _():
        for c in range(SUBLANES):
            x_ref[:, c * LANES:(c + 1) * LANES] = xb_ref[pl.ds(c, EXPERT_ROWS, stride=SUBLANES), :].astype(BF16)
        x = x_ref[...]
        gate = jnp.dot(x, wgb_ref[...], preferred_element_type=F32)
        up = jnp.dot(x, wub_ref[...], preferred_element_type=F32)
        hmid = (_silu(gate) * up).astype(BF16)
        y = jnp.dot(hmid, wdb_ref[...], preferred_element_type=F32)
        for c in range(SUBLANES):
            yb_ref[pl.ds(c, EXPERT_ROWS, stride=SUBLANES), :] = y[:, c * LANES:(c + 1) * LANES]

    @pl.when(jnp.logical_not(used))
    def _():
        yb_ref[...] = jnp.zeros_like(yb_ref)


def _experts(xb, blk_exp, n_used, n_blk, layer, w_gate, w_up, w_down):
    M = EXPERT_ROWS
    tile = lambda index: pl.BlockSpec((M * SUBLANES, LANES), index)
    weight = lambda a: pl.BlockSpec((1, 1) + a.shape[2:], lambda i, be, nu: (layer, be[i], 0, 0))
    grid_spec = pltpu.PrefetchScalarGridSpec(
        num_scalar_prefetch=2, grid=(n_blk,),
        in_specs=[tile(lambda i, be, nu: (jnp.minimum(i, nu[0] - 1), 0)),
                  weight(w_gate), weight(w_up), weight(w_down)],
        out_specs=tile(lambda i, be, nu: (i, 0)),
        scratch_shapes=[pltpu.VMEM((M, D_MODEL), BF16), pltpu.VMEM((D_MODEL, MOE_HIDDEN), BF16),
                        pltpu.VMEM((D_MODEL, MOE_HIDDEN), BF16), pltpu.VMEM((MOE_HIDDEN, D_MODEL), BF16)])
    return pl.pallas_call(
        _expert_body, grid_spec=grid_spec,
        out_shape=jax.ShapeDtypeStruct((n_blk * M * SUBLANES, LANES), F32),
        compiler_params=_params("arbitrary"), name="experts")(
            blk_exp, n_used, xb, w_gate, w_up, w_down)


def _combine_body(d1_ref, d2_ref, h_ref, route_ref, yb_ref, o_ref, buf_ref, sem):
    i = pl.program_id(0)
    n = pl.num_programs(0)

    def gather(step, slot):
        base = step * MOVE_ROWS

        def body(t, _):
            _tile_copy(yb_ref, d1_ref[base + t], buf_ref.at[slot, 0], t, sem.at[slot]).start()
            _tile_copy(yb_ref, d2_ref[base + t], buf_ref.at[slot, 1], t, sem.at[slot]).start(priority=1)
            return 0

        lax.fori_loop(0, MOVE_ROWS, body, 0)

    @pl.when(i == 0)
    def _():
        gather(0, 0)

    @pl.when(i + 1 < n)
    def _():
        gather(i + 1, (i + 1) & 1)

    slot = i & 1
    pltpu.make_async_copy(buf_ref.at[slot], buf_ref.at[slot], sem.at[slot]).wait()
    route = route_ref[...]
    g1 = route[:, 2:3]
    g2 = route[:, 3:4]
    for c in range(SUBLANES):
        cs = slice(c * LANES, (c + 1) * LANES)
        rows = pl.ds(c, MOVE_ROWS, stride=SUBLANES)
        o_ref[:, cs] = h_ref[:, cs] + (buf_ref[slot, 0, rows, :] * g1 + buf_ref[slot, 1, rows, :] * g2)


def _combine(h2, route, yb, dest1, dest2):
    T = h2.shape[0]
    R = MOVE_ROWS
    grid_spec = pltpu.PrefetchScalarGridSpec(
        num_scalar_prefetch=2, grid=(T // R,),
        in_specs=[pl.BlockSpec((R, D_MODEL), lambda i, d1, d2: (i, 0)),
                  pl.BlockSpec((R, LANES), lambda i, d1, d2: (i, 0)), pl.BlockSpec(memory_space=pl.ANY)],
        out_specs=pl.BlockSpec((R, D_MODEL), lambda i, d1, d2: (i, 0)),
        scratch_shapes=[pltpu.VMEM((2, 2, R * SUBLANES, LANES), F32), pltpu.SemaphoreType.DMA((2,))])
    return pl.pallas_call(
        _combine_body, grid_spec=grid_spec, out_shape=jax.ShapeDtypeStruct((T, D_MODEL), F32),
        compiler_params=_params("arbitrary"), name="combine")(dest1, dest2, h2, route, yb)


def _moe(h2, xn, route, routet, counts, layer, w_gate, w_up, w_down):
    T = h2.shape[0]
    M = EXPERT_ROWS
    n_blk = (2 * T) // M + MOE_EXPERTS
    cnt = counts[ROUTER_LANE0:ROUTER_LANE0 + MOE_EXPERTS, 0].astype(jnp.int32)
    padded = (cnt + M - 1) // M * M
    ends = jnp.cumsum(padded)
    pstarts = ends - padded
    n_used = (ends[-1] // M).astype(jnp.int32)
    blk = jnp.minimum(jnp.arange(n_blk, dtype=jnp.int32), n_used - 1) * M
    blk_exp = jnp.sum((ends[None, :] <= blk[:, None]).astype(jnp.int32), axis=1)
    blk_exp = jnp.minimum(blk_exp, MOE_EXPERTS - 1)
    dest = _slots(routet, pstarts)
    pad_info = jnp.concatenate([pstarts + cnt, n_used.reshape(1)])
    xb = _dispatch(xn, dest[0], dest[1], pad_info, n_blk * M)
    yb = _experts(xb, blk_exp, n_used.reshape(1), n_blk, layer, w_gate, w_up, w_down)
    return _combine(h2, route, yb, dest[0], dest[1])


def kernel(x, rel_bias, norm1_g, w_in, conv_w, conv_b, dt_bias, a_log, d_skip, ssd_norm_g,
           q_norm_g, k_norm_g, w_out, norm2_g, w_r1, b_r1, w_r2, b_r2, w_gate, w_up, w_down):
    B, S, D = x.shape
    T = B * S
    h2 = x.reshape(T, D)
    depth = w_in.shape[0]
    bias_tiles = _bias_tiles(rel_bias)
    for l in range(depth):
        z, xbc, dt, dtt, q, k, v = _inproj(h2, norm1_g[l], w_in[l])
        r3 = lambda a: a.reshape(B, S, a.shape[-1])
        y_ssd = _ssd(r3(z), r3(xbc), r3(dt), dtt, conv_w[l], conv_b[l], dt_bias[l], a_log[l],
                     d_skip[l], ssd_norm_g[l])
        y_att = _attention(r3(q), r3(k), r3(v), q_norm_g[l], k_norm_g[l], bias_tiles)
        h2, xn, route, routet, counts = _outproj_route(
            h2, y_ssd.reshape(T, SSD_INNER), y_att.reshape(T, ATT_INNER), w_out[l], norm2_g[l],
            w_r1[l], b_r1[l], w_r2[l], b_r2[l])
        h2 = _moe(h2, xn, route, routet, counts, l, w_gate, w_up, w_down)
    return h2.reshape(B, S, D)
```

```python
import math

import numpy as np
import jax
import jax.numpy as jnp
from jax import lax
from jax.experimental import pallas as pl
from jax.experimental.pallas import tpu as pltpu

F32 = jnp.float32
BF16 = jnp.bfloat16
HIGHEST = lax.Precision.HIGHEST

LANES = 128
SUBLANES = 8

D_MODEL = 1024
SSD_HEADS = 8
SSD_HEAD_DIM = 64
SSD_INNER = SSD_HEADS * SSD_HEAD_DIM
SSD_GROUPS = 2
SSD_HEADS_PER_GROUP = SSD_HEADS // SSD_GROUPS
SSD_STATE = 64
SSD_CONV = 4
SSD_CHUNK = 128
CONV_CH = SSD_INNER + 2 * SSD_GROUPS * SSD_STATE
ATT_HEADS = 8
ATT_HEAD_DIM = 64
ATT_INNER = ATT_HEADS * ATT_HEAD_DIM
MOBA_BLOCK = 256
MOBA_TOPK = 3
REL_BUCKETS = 32
REL_MAX_DIST = 128
MOE_GROUPS = 4
MOE_EXPERTS_PER_GROUP = 8
MOE_EXPERTS = MOE_GROUPS * MOE_EXPERTS_PER_GROUP
MOE_HIDDEN = 512
EPS = 1e-6

MASK_VALUE = -1e30
LOG2E = math.log2(math.e)
IN_ROWS = 1024
SSD_STEP_CHUNKS = 8
OUT_ROWS = 512
EXPERT_ROWS = 512
MOVE_ROWS = 256
VMEM_LIMIT = 56 * 1024 * 1024
ROUTER_LANE0 = MOE_GROUPS
ROUTER_ROWS = 48


def _silu(x):
    return x * (1.0 / (1.0 + jnp.exp(-x)))


def _softplus(x):
    return jnp.maximum(x, 0.0) + jnp.log(1.0 + jnp.exp(-jnp.abs(x)))


def _params(*sem):
    return pltpu.CompilerParams(dimension_semantics=sem, vmem_limit_bytes=VMEM_LIMIT)


def _inproj_body(h_ref, g_ref, wz_ref, wx_ref, wdt_ref, wq_ref, wk_ref, wv_ref,
                 z_ref, xbc_ref, dt_ref, dtt_ref, q_ref, k_ref, v_ref):
    x = h_ref[...]
    xn = x * lax.rsqrt(jnp.mean(x * x, axis=-1, keepdims=True) + EPS) * g_ref[...]
    xb = xn.astype(BF16)
    z_ref[...] = jnp.dot(xb, wz_ref[...], preferred_element_type=F32)
    xbc_ref[...] = jnp.dot(xb, wx_ref[...], preferred_element_type=F32)
    dt = jnp.dot(xb, wdt_ref[...], preferred_element_type=F32)
    dt_ref[...] = dt
    dtt_ref[...] = dt.T[0:SSD_HEADS, :]
    q_ref[...] = jnp.dot(xb, wq_ref[...], preferred_element_type=F32)
    k_ref[...] = jnp.dot(xb, wk_ref[...], preferred_element_type=F32)
    v_ref[...] = jnp.dot(xb, wv_ref[...], preferred_element_type=F32)


def _inproj(h2, g, w_in):
    T = h2.shape[0]
    c0, c1, c2 = SSD_INNER, SSD_INNER + CONV_CH, SSD_INNER + CONV_CH + SSD_HEADS
    wb = w_in.astype(BF16)
    wz, wx, wdt = wb[:, :c0], wb[:, c0:c1], wb[:, c1:c2]
    wq, wk, wv = (wb[:, c2 + i * ATT_INNER:c2 + (i + 1) * ATT_INNER] for i in range(3))
    wdt_pad = jnp.pad(wdt, ((0, 0), (0, LANES - SSD_HEADS)))
    full = lambda a: pl.BlockSpec(a.shape, lambda i: (0,) * a.ndim)
    rows = lambda n: pl.BlockSpec((IN_ROWS, n), lambda i: (i, 0))
    ins = (h2, g.reshape(1, D_MODEL), wz, wx, wdt_pad, wq, wk, wv)
    out_shape = (
        jax.ShapeDtypeStruct((T, SSD_INNER), F32), jax.ShapeDtypeStruct((T, CONV_CH), F32),
        jax.ShapeDtypeStruct((T, LANES), F32), jax.ShapeDtypeStruct((SSD_HEADS, T), F32),
        jax.ShapeDtypeStruct((T, ATT_INNER), F32), jax.ShapeDtypeStruct((T, ATT_INNER), F32),
        jax.ShapeDtypeStruct((T, ATT_INNER), F32))
    out_specs = (rows(SSD_INNER), rows(CONV_CH), rows(LANES),
                 pl.BlockSpec((SSD_HEADS, IN_ROWS), lambda i: (0, i)),
                 rows(ATT_INNER), rows(ATT_INNER), rows(ATT_INNER))
    return pl.pallas_call(
        _inproj_body, grid=(T // IN_ROWS,),
        in_specs=[rows(D_MODEL)] + [full(a) for a in ins[1:]],
        out_specs=out_specs, out_shape=out_shape,
        compiler_params=_params("arbitrary"), name="inproj")(*ins)


def _ssd_body(z_ref, xbc_ref, dt_ref, dtt_ref, cw_ref, cb_ref, dtb_ref, dtbt_ref, alog_ref,
              alogt_ref, dskip_ref, ng_ref, tril_ref, expand_ref, y_ref, ext_ref, state_ref):
    L, P, N, E = SSD_CHUNK, SSD_HEAD_DIM, SSD_STATE, SSD_HEADS_PER_GROUP
    R = SSD_STEP_CHUNKS * L
    c = pl.program_id(1)

    @pl.when(c == 0)
    def _():
        state_ref[...] = jnp.zeros_like(state_ref)
        ext_ref[0:SUBLANES, :] = jnp.zeros((SUBLANES, CONV_CH), F32)

    ext_ref[SUBLANES:SUBLANES + R, :] = xbc_ref[0]
    tril = tril_ref[...]
    expand = expand_ref[...]
    row = lax.broadcasted_iota(jnp.int32, (L, L), 0)
    col = lax.broadcasted_iota(jnp.int32, (L, L), 1)
    causal = row >= col
    for sub in range(SSD_STEP_CHUNKS):
        _ssd_chunk(sub * L, z_ref, dt_ref, dtt_ref, cw_ref, cb_ref, dtb_ref, dtbt_ref, alog_ref,
                   alogt_ref, dskip_ref, ng_ref, tril, expand, causal, y_ref, ext_ref, state_ref)
    ext_ref[0:SUBLANES, :] = ext_ref[R:R + SUBLANES, :]


def _ssd_chunk(r0, z_ref, dt_ref, dtt_ref, cw_ref, cb_ref, dtb_ref, dtbt_ref, alog_ref, alogt_ref,
               dskip_ref, ng_ref, tril, expand, causal, y_ref, ext_ref, state_ref):
    L, P, N, E = SSD_CHUNK, SSD_HEAD_DIM, SSD_STATE, SSD_HEADS_PER_GROUP
    conv = cb_ref[...]
    for j in range(SSD_CONV):
        off = r0 + SUBLANES - (SSD_CONV - 1) + j
        conv = conv + cw_ref[j:j + 1, :] * ext_ref[off:off + L, :]
    act = _silu(conv)
    xs = act[:, :SSD_INNER]
    b_in = act[:, SSD_INNER:SSD_INNER + SSD_GROUPS * N]
    c_out = act[:, SSD_INNER + SSD_GROUPS * N:]

    dt = _softplus(dt_ref[0, r0:r0 + L, :] + dtb_ref[...])
    a_dt = dt * (-jnp.exp(alog_ref[...]))
    a_cs = jnp.dot(tril, a_dt, precision=HIGHEST, preferred_element_type=F32)
    dt_t = _softplus(dtt_ref[:, r0:r0 + L] + dtbt_ref[...])
    a_dt_t = dt_t * (-jnp.exp(alogt_ref[...]))
    a_cs_t = lax.dot_general(a_dt_t, tril, (((1,), (1,)), ((), ())), precision=HIGHEST,
                             preferred_element_type=F32)
    last = a_cs[L - 1:L, :]
    per_head = jnp.concatenate(
        [dt, jnp.exp(a_cs), jnp.exp(last - a_cs), jnp.broadcast_to(jnp.exp(last), (SUBLANES, LANES))],
        axis=0)
    hi = per_head.astype(BF16)
    lo = (per_head - hi.astype(F32)).astype(BF16)
    on_lanes = jnp.dot(jnp.concatenate([hi, lo], axis=-1), expand, preferred_element_type=F32)
    dt_x = on_lanes[0:L]
    eacs_x = on_lanes[L:2 * L]
    dec_x = on_lanes[2 * L:3 * L]
    cdec_x = on_lanes[3 * L:3 * L + 1]
    xdt = xs * dt_x
    xdtd = (xdt * dec_x).astype(BF16)
    xdt_b = xdt.astype(BF16)

    y_parts = []
    for g in range(SSD_GROUPS):
        bg = b_in[:, g * N:(g + 1) * N].astype(BF16)
        cg = c_out[:, g * N:(g + 1) * N].astype(BF16)
        cb = lax.dot_general(cg, bg, (((1,), (1,)), ((), ())), preferred_element_type=F32)
        st = state_ref[g]
        gs = slice(g * E * P, (g + 1) * E * P)
        y_off = jnp.dot(cg, st.astype(BF16), preferred_element_type=F32) * eacs_x[:, gs]
        for e in range(E):
            h = g * E + e
            diff = a_cs[:, h:h + 1] - a_cs_t[h:h + 1, :]
            l_mat = jnp.exp(jnp.where(causal, diff, -jnp.inf))
            m = (cb * l_mat).astype(BF16)
            y_diag = jnp.dot(m, xdt_b[:, h * P:(h + 1) * P], preferred_element_type=F32)
            y_parts.append(y_diag + y_off[:, e * P:(e + 1) * P])
        upd = lax.dot_general(bg, xdtd[:, gs], (((0,), (0,)), ((), ())),
                              preferred_element_type=F32)
        state_ref[g] = cdec_x[:, gs] * st + upd
    y = jnp.concatenate(y_parts, axis=-1) + dskip_ref[...] * xs
    y = y * _silu(z_ref[0, r0:r0 + L, :])
    y = y * lax.rsqrt(jnp.mean(y * y, axis=-1, keepdims=True) + EPS) * ng_ref[...]
    y_ref[0, r0:r0 + L, :] = y


def _ssd(z, xbc, dt, dtt, conv_w, conv_b, dt_bias, a_log, d_skip, norm_g):
    B, S, _ = z.shape
    L = SSD_CHUNK
    R = SSD_STEP_CHUNKS * L
    nc = S // R
    pad_row = lambda a: jnp.pad(a.reshape(1, SSD_HEADS), ((0, 0), (0, LANES - SSD_HEADS)))
    tril = jnp.asarray(np.tril(np.ones((L, L), np.float32)))
    expand_np = np.zeros((LANES, SSD_INNER), np.float32)
    for h in range(SSD_HEADS):
        expand_np[h, h * SSD_HEAD_DIM:(h + 1) * SSD_HEAD_DIM] = 1.0
    ins = (z, xbc, dt, dtt, conv_w, conv_b.reshape(1, CONV_CH), pad_row(dt_bias),
           dt_bias.reshape(SSD_HEADS, 1), pad_row(a_log), a_log.reshape(SSD_HEADS, 1),
           jnp.repeat(d_skip, SSD_HEAD_DIM).reshape(1, SSD_INNER), norm_g.reshape(1, SSD_INNER),
           tril, jnp.asarray(np.concatenate([expand_np, expand_np], axis=0), dtype=BF16))
    full = lambda a: pl.BlockSpec(a.shape, lambda b, c: (0,) * a.ndim)
    chunk = lambda n: pl.BlockSpec((1, R, n), lambda b, c: (b, c, 0))
    in_specs = [chunk(SSD_INNER), chunk(CONV_CH), chunk(LANES),
                pl.BlockSpec((SSD_HEADS, R), lambda b, c: (0, b * nc + c))]
    in_specs += [full(a) for a in ins[4:]]
    return pl.pallas_call(
        _ssd_body, grid=(B, nc), in_specs=in_specs, out_specs=chunk(SSD_INNER),
        out_shape=jax.ShapeDtypeStruct((B, S, SSD_INNER), F32),
        scratch_shapes=[pltpu.VMEM((R + SUBLANES, CONV_CH), F32),
                        pltpu.VMEM((SSD_GROUPS, SSD_STATE, SSD_HEADS_PER_GROUP * SSD_HEAD_DIM), F32)],
        compiler_params=_params("arbitrary", "arbitrary"), name="ssd")(*ins)


def _bucket_tile():
    bs = MOBA_BLOCK
    dist = np.arange(bs)[:, None] - np.arange(2 * bs)[None, :] + bs
    max_exact = REL_BUCKETS // 2
    d = np.maximum(dist, max_exact).astype(np.float32)
    large = max_exact + (np.log(d / np.float32(max_exact)) / np.float32(math.log(REL_MAX_DIST / max_exact))
                         * np.float32(REL_BUCKETS - max_exact)).astype(np.int32)
    large = np.minimum(large, REL_BUCKETS - 1)
    bucket = np.where(dist < max_exact, dist, large)
    return np.where(dist >= 0, bucket, -1).astype(np.int32)


def _bias_body(rb_ref, bucket_ref, o_ref):
    head = pl.program_id(0)
    bucket = bucket_ref[...]
    far = rb_ref[REL_BUCKETS - 1, head]
    bias = jnp.where(bucket < 0, MASK_VALUE, 0.0)
    for i in range(REL_BUCKETS - 1):
        bias = jnp.where(bucket == i, (rb_ref[i, head] - far) * LOG2E, bias)
    o_ref[0] = bias


def _bias_tiles(rel_bias):
    bs = MOBA_BLOCK
    bucket = jnp.asarray(_bucket_tile())
    return pl.pallas_call(
        _bias_body, grid=(ATT_HEADS,),
        in_specs=[pl.BlockSpec(memory_space=pltpu.SMEM), pl.BlockSpec((bs, 2 * bs), lambda h: (0, 0))],
        out_specs=pl.BlockSpec((1, bs, 2 * bs), lambda h: (h, 0, 0)),
        out_shape=jax.ShapeDtypeStruct((ATT_HEADS, bs, 2 * bs), F32),
        compiler_params=_params("arbitrary"), name="bias_tiles")(rel_bias, bucket)


def _attn_body(q_ref, k_ref, v_ref, qg_ref, kg_ref, bias_ref, o_ref,
               kaug_ref, vaug_ref, qaug_ref, sa_ref, sb_ref, m_ref, acc_ref):
    bs, hd = MOBA_BLOCK, ATT_HEAD_DIM
    nb = k_ref.shape[1] // bs
    jp = pl.program_id(2)
    nt_dims = (((1,), (1,)), ((), ()))

    @pl.when(jp == 0)
    def _():
        S = nb * bs
        pair = 2 * hd
        r = lax.broadcasted_iota(jnp.int32, (pair, pair), 0) // hd
        c = lax.broadcasted_iota(jnp.int32, (pair, pair), 1) // hd
        head_mean = jnp.where(r == c, 1.0 / hd, 0.0).astype(BF16)

        def head_norm(x, gain):
            ms = jnp.dot((x * x).astype(BF16), head_mean, preferred_element_type=F32)
            return x * lax.rsqrt(ms + EPS) * gain

        kn = head_norm(k_ref[0], kg_ref[...])
        kmean = jnp.mean(kn.reshape(nb, bs, pair), axis=1)
        kn_b = kn.astype(BF16)
        v_b = v_ref[0].astype(BF16)
        lane = lax.broadcasted_iota(jnp.int32, (S, pair), 1)
        blk = lax.broadcasted_iota(jnp.int32, (S, pair), 0) // bs
        lane_m = lax.broadcasted_iota(jnp.int32, (nb, pair), 1)
        rowb = lax.broadcasted_iota(jnp.int32, (nb, bs), 0)
        lane_q = lax.broadcasted_iota(jnp.int32, (bs, pair), 1)
        kmean_b = []
        for h in range(2):
            own = (lane // hd) == h
            other0 = (1 - h) * hd
            kaug_ref[h] = jnp.where(own, kn_b, (lane - other0 == blk).astype(BF16))
            vaug_ref[h] = jnp.where(own, v_b, jnp.ones((S, pair), BF16))
            kmean_b.append(jnp.where((lane_m // hd) == h, kmean, 0.0).astype(BF16))
        blocks = [slice(j * bs, (j + 1) * bs) for j in range(nb)]
        qn_all = [head_norm(q_ref[0, rows, :], qg_ref[...]) for rows in blocks]
        masks_all = []
        for j in range(nb):
            qn_b = qn_all[j].astype(BF16)
            masks = []
            for h in range(2):
                if j <= MOBA_TOPK:
                    sel = rowb <= j
                else:
                    gate = lax.dot_general(kmean_b[h], qn_b, nt_dims,
                                           preferred_element_type=F32)
                    beats = []
                    for m in range(j):
                        gm = gate[m:m + 1, :]
                        beats.append(((gm > gate) | ((gm == gate) & (rowb > m))).astype(jnp.int32))
                    while len(beats) > 1:
                        beats = [a + b for a, b in zip(beats[::2], beats[1::2])] + beats[len(beats) & ~1:]
                    sel = ((rowb < j) & (beats[0] < MOBA_TOPK)) | (rowb == j)
                masks.append(jnp.where(sel, 0.0, MASK_VALUE))
            masks_all.append(masks)
        gap = jnp.zeros((hd - nb, bs), F32)
        for j, rows in enumerate(blocks):
            mask_t = jnp.concatenate([masks_all[j][1], gap, masks_all[j][0], gap], axis=0).T
            qs = qn_all[j] * (ATT_HEAD_DIM ** -0.5 * LOG2E)
            for h in range(2):
                qaug_ref[h, rows, :] = jnp.where((lane_q // hd) == h, qs, mask_t).astype(BF16)

    qt = 2 * bs
    q0 = pl.multiple_of(jp * qt, qt)
    all_rows = slice(0, qt)
    late_rows = slice(bs, qt)
    for h in range(2):
        m_ref[h] = jnp.full((qt, 2 * hd), -jnp.inf, F32)
        acc_ref[h] = jnp.zeros((qt, 2 * hd), F32)

    def scores(n, s_ref, rows=all_rows):
        start = pl.multiple_of(n * bs, bs)
        q_rows = pl.ds(q0 + rows.start, rows.stop - rows.start)
        for h in range(2):
            s_ref[h, rows, :] = lax.dot_general(qaug_ref[h, q_rows, :], kaug_ref[h, pl.ds(start, bs), :],
                                                nt_dims, preferred_element_type=F32)

    def update(n, s_ref, bias, rows=all_rows):
        start = pl.multiple_of(n * bs, bs)
        for r0 in range(rows.start, rows.stop, bs):
            sub = slice(r0, r0 + bs)
            for h in range(2):
                s = s_ref[h, sub, :]
                tile = None if bias is None else bias(h, r0 // bs)
                if tile is not None:
                    s = s + tile
                m_i = m_ref[h, sub, :]
                m_new = jnp.maximum(m_i, jnp.max(s, axis=-1, keepdims=True))
                alpha = jnp.exp2(m_i - m_new)
                p = jnp.exp2((s - jnp.concatenate([m_new, m_new], axis=-1)).astype(BF16))
                pv = jnp.dot(p, vaug_ref[h, pl.ds(start, bs), :], preferred_element_type=F32)
                acc_ref[h, sub, :] = alpha * acc_ref[h, sub, :] + pv
                m_ref[h, sub, :] = m_new

    def far_pair(n):
        scores(n + 1, sb_ref)
        update(n, sa_ref, None)
        scores(n + 2, sa_ref)
        update(n + 1, sb_ref, None)

    def far_quad(i, _):
        far_pair(4 * i)
        far_pair(4 * i + 2)
        return 0

    n_far = jnp.maximum(2 * jp - 1, 0)
    scores(0, sa_ref)
    lax.fori_loop(0, n_far >> 2, far_quad, 0)

    @pl.when((n_far & 2) != 0)
    def _():
        far_pair(n_far & ~3)

    prev = lambda h: bias_ref[h, :, 0:bs]
    own = lambda h, qblk=1: bias_ref[h, :, bs:2 * bs]
    bias_p = lambda h, qblk: prev(h) if qblk == 0 else None
    bias_o = lambda h, qblk: own(h) if qblk == 0 else prev(h)
    blk_p = jnp.maximum(2 * jp - 1, 0)
    blk_o = 2 * jp
    blk_n = 2 * jp + 1

    @pl.when(jp > 0)
    def _():
        scores(blk_p, sb_ref)
        update(n_far - 1, sa_ref, None)
        scores(blk_o, sa_ref)
        update(blk_p, sb_ref, bias_p)
        scores(blk_n, sb_ref, late_rows)
        update(blk_o, sa_ref, bias_o)
        update(blk_n, sb_ref, own, late_rows)

    @pl.when(jp == 0)
    def _():
        scores(blk_n, sb_ref, late_rows)
        update(blk_o, sa_ref, bias_o)
        update(blk_n, sb_ref, own, late_rows)

    first_half = lax.broadcasted_iota(jnp.int32, (qt, 2 * hd), 1) < hd
    numer = jnp.where(first_half, acc_ref[0], acc_ref[1])
    denom = pltpu.roll(jnp.where(first_half, acc_ref[1], acc_ref[0]), hd, 1)
    o_ref[0] = numer / denom


def _attention(q, k, v, q_g, k_g, bias_tiles):
    B, S, _ = q.shape
    bs = MOBA_BLOCK
    nb = S // bs
    assert S % (2 * bs) == 0 and nb <= ATT_HEAD_DIM
    pair = 2 * ATT_HEAD_DIM
    return pl.pallas_call(
        _attn_body, grid=(B, ATT_HEADS // 2, nb // 2),
        in_specs=[
            pl.BlockSpec((1, S, pair), lambda b, hp, i: (b, 0, hp)),
            pl.BlockSpec((1, S, pair), lambda b, hp, i: (b, 0, hp)),
            pl.BlockSpec((1, S, pair), lambda b, hp, i: (b, 0, hp)),
            pl.BlockSpec((1, pair), lambda b, hp, i: (0, 0)),
            pl.BlockSpec((1, pair), lambda b, hp, i: (0, 0)),
            pl.BlockSpec((2, bs, 2 * bs), lambda b, hp, i: (hp, 0, 0)),
        ],
        out_specs=pl.BlockSpec((1, 2 * bs, pair), lambda b, hp, i: (b, i, hp)),
        out_shape=jax.ShapeDtypeStruct((B, S, ATT_INNER), F32),
        scratch_shapes=[pltpu.VMEM((2, S, pair), BF16), pltpu.VMEM((2, S, pair), BF16),
                        pltpu.VMEM((2, S, pair), BF16),
                        pltpu.VMEM((2, 2 * bs, bs), F32), pltpu.VMEM((2, 2 * bs, bs), F32),
                        pltpu.VMEM((2, 2 * bs, pair), F32),
                        pltpu.VMEM((2, 2 * bs, pair), F32)],
        compiler_params=_params("arbitrary", "arbitrary", "arbitrary"), name="moba")(
            q, k, v, jnp.tile(q_g, 2).reshape(1, pair), jnp.tile(k_g, 2).reshape(1, pair), bias_tiles)


def _outproj_body(h_ref, ys_ref, ya_ref, wa_ref, wb_ref, g_ref, wrh_ref, wrl_ref, br_ref, tri_ref,
                  hn_ref, xn_ref, route_ref, routet_ref, counts_ref, carry_ref):
    i = pl.program_id(0)

    @pl.when(i == 0)
    def _():
        carry_ref[...] = jnp.zeros_like(carry_ref)

    _outproj_rows(0, OUT_ROWS, h_ref, ys_ref, ya_ref, wa_ref, wb_ref, g_ref, wrh_ref, wrl_ref,
                  br_ref, tri_ref, hn_ref, xn_ref, route_ref, routet_ref, carry_ref)
    counts_ref[...] = carry_ref[...]


def _outproj_rows(r0, n, h_ref, ys_ref, ya_ref, wa_ref, wb_ref, g_ref, wrh_ref, wrl_ref, br_ref, tri_ref,
                  hn_ref, xn_ref, route_ref, routet_ref, carry_ref):
    rs = slice(r0, r0 + n)
    hn = (h_ref[rs, :]
          + jnp.dot(ys_ref[rs, :].astype(BF16), wa_ref[...], preferred_element_type=F32)
          + jnp.dot(ya_ref[rs, :].astype(BF16), wb_ref[...], preferred_element_type=F32))
    hn_ref[rs, :] = hn
    xn = hn * lax.rsqrt(jnp.mean(hn * hn, axis=-1, keepdims=True) + EPS) * g_ref[...]
    for c in range(SUBLANES):
        xn_ref[pl.ds(r0 * SUBLANES + c, n, stride=SUBLANES), :] = xn[:, c * LANES:(c + 1) * LANES]
    x_hi = xn.astype(BF16)
    x_lo = (xn - x_hi.astype(F32)).astype(BF16)
    hi_terms = jnp.dot(x_hi, wrl_ref[...], preferred_element_type=F32)
    logits = (hi_terms[:, :LANES] + hi_terms[:, LANES:]
              + jnp.dot(x_lo, wrh_ref[...], preferred_element_type=F32)) + br_ref[...]
    lt = logits.T[0:ROUTER_ROWS, :]
    row = lax.broadcasted_iota(jnp.int32, (ROUTER_ROWS, n), 0)
    first = lambda hit: jnp.min(jnp.where(hit, row, LANES), axis=0, keepdims=True)
    l1 = jnp.where(row < MOE_GROUPS, lt, -jnp.inf)
    mx = jnp.max(l1, axis=0, keepdims=True)
    gval = 1.0 / jnp.sum(jnp.exp(l1 - mx), axis=0, keepdims=True)
    gidx = first(l1 == mx)
    e_row = row - ROUTER_LANE0
    in_grp = (e_row >= 0) & (e_row < MOE_EXPERTS) & ((e_row >> 3) == gidx)
    l2 = jnp.where(in_grp, lt, -jnp.inf)
    m1 = jnp.max(l2, axis=0, keepdims=True)
    i1 = first(l2 == m1)
    l2b = jnp.where(row == i1, -jnp.inf, l2)
    m2 = jnp.max(l2b, axis=0, keepdims=True)
    i2 = first(l2b == m2)
    r = jnp.exp(m2 - m1)
    g1 = gval / (1.0 + r)
    g2 = gval * r / (1.0 + r)
    oh1 = (row == i1)
    oh2 = (row == i2)
    both = jnp.where(oh1 | oh2, 1.0, 0.0)
    cx = jnp.dot(both.astype(BF16), tri_ref[...], preferred_element_type=F32) + carry_ref[...]
    rank1 = jnp.sum(jnp.where(oh1, cx, 0.0), axis=0, keepdims=True)
    rank2 = jnp.sum(jnp.where(oh2, cx, 0.0), axis=0, keepdims=True)
    carry_ref[...] = carry_ref[...] + jnp.sum(both, axis=1, keepdims=True)
    routet = jnp.concatenate(
        [(i1 - ROUTER_LANE0).astype(F32), (i2 - ROUTER_LANE0).astype(F32), g1, g2, rank1, rank2,
         jnp.zeros((SUBLANES - 6, n), F32)], axis=0)
    routet_ref[:, rs] = routet
    route_ref[rs, :] = jnp.concatenate([routet, jnp.zeros((LANES - SUBLANES, n), F32)], axis=0).T


def _outproj_route(h2, y_ssd, y_att, w_out, norm_g, w_r1, b_r1, w_r2, b_r2):
    T = h2.shape[0]
    R = OUT_ROWS
    wb = w_out.astype(BF16)
    pad = LANES - MOE_GROUPS - MOE_EXPERTS
    wr = jnp.pad(jnp.concatenate([w_r1, w_r2], axis=1), ((0, 0), (0, pad)))
    wr_hi = wr.astype(BF16)
    wr_lo = (wr - wr_hi.astype(F32)).astype(BF16)
    br = jnp.pad(jnp.concatenate([b_r1, b_r2]), (0, pad)).reshape(1, LANES)
    tri = jnp.asarray(np.triu(np.ones((R, R), np.float32), 1), dtype=BF16)
    ins = (h2, y_ssd, y_att, wb[:SSD_INNER], wb[SSD_INNER:], norm_g.reshape(1, D_MODEL),
           wr_hi, jnp.concatenate([wr_hi, wr_lo], axis=1), br, tri)
    full = lambda a: pl.BlockSpec(a.shape, lambda i: (0,) * a.ndim)
    rows = lambda n: pl.BlockSpec((R, n), lambda i: (i, 0))
    return pl.pallas_call(
        _outproj_body, grid=(T // R,),
        in_specs=[rows(D_MODEL), rows(SSD_INNER), rows(ATT_INNER)] + [full(a) for a in ins[3:]],
        out_specs=(rows(D_MODEL), pl.BlockSpec((R * SUBLANES, LANES), lambda i: (i, 0)), rows(LANES),
                   pl.BlockSpec((SUBLANES, R), lambda i: (0, i)),
                   pl.BlockSpec((ROUTER_ROWS, 1), lambda i: (0, 0))),
        out_shape=(jax.ShapeDtypeStruct((T, D_MODEL), F32),
                   jax.ShapeDtypeStruct((T * SUBLANES, LANES), F32),
                   jax.ShapeDtypeStruct((T, LANES), F32), jax.ShapeDtypeStruct((SUBLANES, T), F32),
                   jax.ShapeDtypeStruct((ROUTER_ROWS, 1), F32)),
        scratch_shapes=[pltpu.VMEM((ROUTER_ROWS, 1), F32)],
        compiler_params=_params("arbitrary"), name="outproj_route")(*ins)


def _slots_body(routet_ref, pstart_ref, dest_ref):
    cols = routet_ref.shape[1]
    expert = lax.broadcasted_iota(jnp.int32, (MOE_EXPERTS, cols), 0).astype(F32)
    pstart = pstart_ref[...]
    for j in range(2):
        hit = expert == routet_ref[j:j + 1, :]
        base = jnp.sum(jnp.where(hit, pstart, 0.0), axis=0, keepdims=True)
        dest_ref[j:j + 1, :] = (base + routet_ref[4 + j:5 + j, :]).astype(jnp.int32)


def _slots(routet, pstarts):
    T = routet.shape[1]
    cols = min(T, 4096)
    assert T % cols == 0
    return pl.pallas_call(
        _slots_body, grid=(T // cols,),
        in_specs=[pl.BlockSpec((SUBLANES, cols), lambda i: (0, i)),
                  pl.BlockSpec((MOE_EXPERTS, 1), lambda i: (0, 0))],
        out_specs=pl.BlockSpec((2, cols), lambda i: (0, i)),
        out_shape=jax.ShapeDtypeStruct((2, T), jnp.int32),
        compiler_params=_params("arbitrary"), name="slots")(
            routet, pstarts.astype(F32).reshape(MOE_EXPERTS, 1))


def _tile_copy(src_ref, src_row, dst_ref, dst_row, sem):
    tile = lambda row: pl.ds(pl.multiple_of(row * SUBLANES, SUBLANES), SUBLANES)
    return pltpu.make_async_copy(src_ref.at[tile(src_row)], dst_ref.at[tile(dst_row)], sem)


def _dispatch_body(pad_ref, d1_ref, d2_ref, xn_ref, xb_ref, zero_ref, stage_ref, sem, lsem, ssem):
    block = EXPERT_ROWS * SUBLANES
    n_blocks = xb_ref.shape[0] // block
    n_used = pad_ref[MOE_EXPERTS]

    def zero_copy(row):
        start = pl.multiple_of(row * SUBLANES, SUBLANES)
        return pltpu.make_async_copy(zero_ref, xb_ref.at[pl.ds(start, block)], sem)

    @pl.when(pl.program_id(0) == 0)
    def _():
        zero_ref[...] = jnp.zeros_like(zero_ref)
        for e in range(MOE_EXPERTS):
            zero_copy(pad_ref[e]).start()
        for e in range(MOE_EXPERTS):
            zero_copy(pad_ref[e]).wait()
        for b in range(n_blocks - MOE_EXPERTS - 1, n_blocks):
            @pl.when(b >= n_used)
            def _():
                zero_copy(b * EXPERT_ROWS).start()
        for b in range(n_blocks - MOE_EXPERTS - 1, n_blocks):
            @pl.when(b >= n_used)
            def _():
                zero_copy(b * EXPERT_ROWS).wait()

    i = pl.program_id(0)
    n = pl.num_programs(0)
    rows = MOVE_ROWS * SUBLANES

    def load(step, slot):
        src = xn_ref.at[pl.ds(pl.multiple_of(step * rows, rows), rows)]
        return pltpu.make_async_copy(src, stage_ref.at[slot], lsem.at[slot])

    def wait_scatters(slot):
        for _ in range(2):
            pltpu.make_async_copy(stage_ref.at[slot], stage_ref.at[slot], ssem.at[slot]).wait()

    @pl.when(i == 0)
    def _():
        load(0, 0).start()

    @pl.when(i >= 1)
    def _():
        wait_scatters((i - 1) & 1)

    @pl.when(i + 1 < n)
    def _():
        load(i + 1, (i + 1) & 1).start()

    slot = i & 1
    load(i, slot).wait()
    base = i * MOVE_ROWS

    def scatter(t, _):
        _tile_copy(stage_ref.at[slot], t, xb_ref, d1_ref[base + t], ssem.at[slot]).start()
        _tile_copy(stage_ref.at[slot], t, xb_ref, d2_ref[base + t], ssem.at[slot]).start(priority=1)
        return 0

    lax.fori_loop(0, MOVE_ROWS, scatter, 0)

    @pl.when(i == n - 1)
    def _():
        wait_scatters(slot)


def _dispatch(xn, dest1, dest2, pad_start, n_slots):
    T = dest1.shape[0]
    grid_spec = pltpu.PrefetchScalarGridSpec(
        num_scalar_prefetch=3, grid=(T // MOVE_ROWS,),
        in_specs=[pl.BlockSpec(memory_space=pl.ANY)],
        out_specs=pl.BlockSpec(memory_space=pl.ANY),
        scratch_shapes=[pltpu.VMEM((EXPERT_ROWS * SUBLANES, LANES), F32),
                        pltpu.VMEM((2, MOVE_ROWS * SUBLANES, LANES), F32),
                        pltpu.SemaphoreType.DMA(()), pltpu.SemaphoreType.DMA((2,)),
                        pltpu.SemaphoreType.DMA((2,))])
    return pl.pallas_call(
        _dispatch_body, grid_spec=grid_spec,
        out_shape=jax.ShapeDtypeStruct(((n_slots + EXPERT_ROWS) * SUBLANES, LANES), F32),
        compiler_params=_params("arbitrary"), name="dispatch")(pad_start, dest1, dest2, xn)


def _expert_body(be_ref, nused_ref, xb_ref, wg_ref, wu_ref, wd_ref, yb_ref,
                 x_ref, wgb_ref, wub_ref, wdb_ref):
    i = pl.program_id(0)
    used = i < nused_ref[0]
    new_expert = (i == 0) | (be_ref[i] != be_ref[jnp.maximum(i - 1, 0)])

    @pl.when(used & new_expert)
    def _():
        wgb_ref[...] = wg_ref[0, 0].astype(BF16)
        wub_ref[...] = wu_ref[0, 0].astype(BF16)
        wdb_ref[...] = wd_ref[0, 0].astype(BF16)

    @pl.when(used)
    def _():
        for c in range(SUBLANES):
            x_ref[:, c * LANES:(c + 1) * LANES] = xb_ref[pl.ds(c, EXPERT_ROWS, stride=SUBLANES), :].astype(BF16)
        x = x_ref[...]
        gate = jnp.dot(x, wgb_ref[...], preferred_element_type=F32)
        up = jnp.dot(x, wub_ref[...], preferred_element_type=F32)
        hmid = (_silu(gate) * up).astype(BF16)
        y = jnp.dot(hmid, wdb_ref[...], preferred_element_type=F32)
        for c in range(SUBLANES):
            yb_ref[pl.ds(c, EXPERT_ROWS, stride=SUBLANES), :] = y[:, c * LANES:(c + 1) * LANES]

    @pl.when(jnp.logical_not(used))
    def _():
        yb_ref[...] = jnp.zeros_like(yb_ref)


def _experts(xb, blk_exp, n_used, n_blk, layer, w_gate, w_up, w_down):
    M = EXPERT_ROWS
    tile = lambda index: pl.BlockSpec((M * SUBLANES, LANES), index)
    weight = lambda a: pl.BlockSpec((1, 1) + a.shape[2:], lambda i, be, nu: (layer, be[i], 0, 0))
    grid_spec = pltpu.PrefetchScalarGridSpec(
        num_scalar_prefetch=2, grid=(n_blk,),
        in_specs=[tile(lambda i, be, nu: (jnp.minimum(i, nu[0] - 1), 0)),
                  weight(w_gate), weight(w_up), weight(w_down)],
        out_specs=tile(lambda i, be, nu: (i, 0)),
        scratch_shapes=[pltpu.VMEM((M, D_MODEL), BF16), pltpu.VMEM((D_MODEL, MOE_HIDDEN), BF16),
                        pltpu.VMEM((D_MODEL, MOE_HIDDEN), BF16), pltpu.VMEM((MOE_HIDDEN, D_MODEL), BF16)])
    return pl.pallas_call(
        _expert_body, grid_spec=grid_spec,
        out_shape=jax.ShapeDtypeStruct((n_blk * M * SUBLANES, LANES), F32),
        compiler_params=_params("arbitrary"), name="experts")(
            blk_exp, n_used, xb, w_gate, w_up, w_down)


def _combine_body(d1_ref, d2_ref, h_ref, route_ref, yb_ref, o_ref, buf_ref, sem):
    i = pl.program_id(0)
    n = pl.num_programs(0)

    def gather(step, slot):
        base = step * MOVE_ROWS

        def body(t, _):
            _tile_copy(yb_ref, d1_ref[base + t], buf_ref.at[slot, 0], t, sem.at[slot]).start()
            _tile_copy(yb_ref, d2_ref[base + t], buf_ref.at[slot, 1], t, sem.at[slot]).start(priority=1)
            return 0

        lax.fori_loop(0, MOVE_ROWS, body, 0)

    @pl.when(i == 0)
    def _():
        gather(0, 0)

    @pl.when(i + 1 < n)
    def _():
        gather(i + 1, (i + 1) & 1)

    slot = i & 1
    pltpu.make_async_copy(buf_ref.at[slot], buf_ref.at[slot], sem.at[slot]).wait()
    route = route_ref[...]
    g1 = route[:, 2:3]
    g2 = route[:, 3:4]
    for c in range(SUBLANES):
        cs = slice(c * LANES, (c + 1) * LANES)
        rows = pl.ds(c, MOVE_ROWS, stride=SUBLANES)
        o_ref[:, cs] = h_ref[:, cs] + (buf_ref[slot, 0, rows, :] * g1 + buf_ref[slot, 1, rows, :] * g2)


def _combine(h2, route, yb, dest1, dest2):
    T = h2.shape[0]
    R = MOVE_ROWS
    grid_spec = pltpu.PrefetchScalarGridSpec(
        num_scalar_prefetch=2, grid=(T // R,),
        in_specs=[pl.BlockSpec((R, D_MODEL), lambda i, d1, d2: (i, 0)),
                  pl.BlockSpec((R, LANES), lambda i, d1, d2: (i, 0)), pl.BlockSpec(memory_space=pl.ANY)],
        out_specs=pl.BlockSpec((R, D_MODEL), lambda i, d1, d2: (i, 0)),
        scratch_shapes=[pltpu.VMEM((2, 2, R * SUBLANES, LANES), F32), pltpu.SemaphoreType.DMA((2,))])
    return pl.pallas_call(
        _combine_body, grid_spec=grid_spec, out_shape=jax.ShapeDtypeStruct((T, D_MODEL), F32),
        compiler_params=_params("arbitrary"), name="combine")(dest1, dest2, h2, route, yb)


def _moe(h2, xn, route, routet, counts, layer, w_gate, w_up, w_down):
    T = h2.shape[0]
    M = EXPERT_ROWS
    n_blk = (2 * T) // M + MOE_EXPERTS
    cnt = counts[ROUTER_LANE0:ROUTER_LANE0 + MOE_EXPERTS, 0].astype(jnp.int32)
    padded = (cnt + M - 1) // M * M
    ends = jnp.cumsum(padded)
    pstarts = ends - padded
    n_used = (ends[-1] // M).astype(jnp.int32)
    blk = jnp.minimum(jnp.arange(n_blk, dtype=jnp.int32), n_used - 1) * M
    blk_exp = jnp.sum((ends[None, :] <= blk[:, None]).astype(jnp.int32), axis=1)
    blk_exp = jnp.minimum(blk_exp, MOE_EXPERTS - 1)
    dest = _slots(routet, pstarts)
    pad_info = jnp.concatenate([pstarts + cnt, n_used.reshape(1)])
    xb = _dispatch(xn, dest[0], dest[1], pad_info, n_blk * M)
    yb = _experts(xb, blk_exp, n_used.reshape(1), n_blk, layer, w_gate, w_up, w_down)
    return _combine(h2, route, yb, dest[0], dest[1])


def kernel(x, rel_bias, norm1_g, w_in, conv_w, conv_b, dt_bias, a_log, d_skip, ssd_norm_g,
           q_norm_g, k_norm_g, w_out, norm2_g, w_r1, b_r1, w_r2, b_r2, w_gate, w_up, w_down):
    B, S, D = x.shape
    T = B * S
    h2 = x.reshape(T, D)
    depth = w_in.shape[0]
    bias_tiles = _bias_tiles(rel_bias)
    for l in range(depth):
        z, xbc, dt, dtt, q, k, v = _inproj(h2, norm1_g[l], w_in[l])
        r3 = lambda a: a.reshape(B, S, a.shape[-1])
        y_ssd = _ssd(r3(z), r3(xbc), r3(dt), dtt, conv_w[l], conv_b[l], dt_bias[l], a_log[l],
                     d_skip[l], ssd_norm_g[l])
        y_att = _attention(r3(q), r3(k), r3(v), q_norm_g[l], k_norm_g[l], bias_tiles)
        h2, xn, route, routet, counts = _outproj_route(
            h2, y_ssd.reshape(T, SSD_INNER), y_att.reshape(T, ATT_INNER), w_out[l], norm2_g[l],
            w_r1[l], b_r1[l], w_r2[l], b_r2[l])
        h2 = _moe(h2, xn, route, routet, counts, l, w_gate, w_up, w_down)
    return h2.reshape(B, S, D)
```

```python
import math

import numpy as np
import jax
import jax.numpy as jnp
from jax import lax
from jax.experimental import pallas as pl
from jax.experimental.pallas import tpu as pltpu

F32 = jnp.float32
BF16 = jnp.bfloat16
HIGHEST = lax.Precision.HIGHEST

LANES = 128
SUBLANES = 8

D_MODEL = 1024
SSD_HEADS = 8
SSD_HEAD_DIM = 64
SSD_INNER = SSD_HEADS * SSD_HEAD_DIM
SSD_GROUPS = 2
SSD_HEADS_PER_GROUP = SSD_HEADS // SSD_GROUPS
SSD_STATE = 64
SSD_CONV = 4
SSD_CHUNK = 128
CONV_CH = SSD_INNER + 2 * SSD_GROUPS * SSD_STATE
ATT_HEADS = 8
ATT_HEAD_DIM = 64
ATT_INNER = ATT_HEADS * ATT_HEAD_DIM
MOBA_BLOCK = 256
MOBA_TOPK = 3
REL_BUCKETS = 32
REL_MAX_DIST = 128
MOE_GROUPS = 4
MOE_EXPERTS_PER_GROUP = 8
MOE_EXPERTS = MOE_GROUPS * MOE_EXPERTS_PER_GROUP
MOE_HIDDEN = 512
EPS = 1e-6

MASK_VALUE = -1e30
LOG2E = math.log2(math.e)
IN_ROWS = 1024
SSD_STEP_CHUNKS = 8
OUT_ROWS = 512
EXPERT_ROWS = 512
MOVE_ROWS = 256
VMEM_LIMIT = 56 * 1024 * 1024
ROUTER_LANE0 = MOE_GROUPS
ROUTER_ROWS = 48


def _silu(x):
    return x * (1.0 / (1.0 + jnp.exp(-x)))


def _softplus(x):
    return jnp.maximum(x, 0.0) + jnp.log(1.0 + jnp.exp(-jnp.abs(x)))


def _params(*sem):
    return pltpu.CompilerParams(dimension_semantics=sem, vmem_limit_bytes=VMEM_LIMIT)


def _inproj_body(h_ref, g_ref, wz_ref, wx_ref, wdt_ref, wq_ref, wk_ref, wv_ref,
                 z_ref, xbc_ref, dt_ref, dtt_ref, q_ref, k_ref, v_ref):
    x = h_ref[...]
    xn = x * lax.rsqrt(jnp.mean(x * x, axis=-1, keepdims=True) + EPS) * g_ref[...]
    xb = xn.astype(BF16)
    z_ref[...] = jnp.dot(xb, wz_ref[...], preferred_element_type=F32)
    xbc_ref[...] = jnp.dot(xb, wx_ref[...], preferred_element_type=F32)
    dt = jnp.dot(xb, wdt_ref[...], preferred_element_type=F32)
    dt_ref[...] = dt
    dtt_ref[...] = dt.T[0:SSD_HEADS, :]
    q_ref[...] = jnp.dot(xb, wq_ref[...], preferred_element_type=F32)
    k_ref[...] = jnp.dot(xb, wk_ref[...], preferred_element_type=F32)
    v_ref[...] = jnp.dot(xb, wv_ref[...], preferred_element_type=F32)


def _inproj(h2, g, w_in):
    T = h2.shape[0]
    c0, c1, c2 = SSD_INNER, SSD_INNER + CONV_CH, SSD_INNER + CONV_CH + SSD_HEADS
    wb = w_in.astype(BF16)
    wz, wx, wdt = wb[:, :c0], wb[:, c0:c1], wb[:, c1:c2]
    wq, wk, wv = (wb[:, c2 + i * ATT_INNER:c2 + (i + 1) * ATT_INNER] for i in range(3))
    wdt_pad = jnp.pad(wdt, ((0, 0), (0, LANES - SSD_HEADS)))
    full = lambda a: pl.BlockSpec(a.shape, lambda i: (0,) * a.ndim)
    rows = lambda n: pl.BlockSpec((IN_ROWS, n), lambda i: (i, 0))
    ins = (h2, g.reshape(1, D_MODEL), wz, wx, wdt_pad, wq, wk, wv)
    out_shape = (
        jax.ShapeDtypeStruct((T, SSD_INNER), F32), jax.ShapeDtypeStruct((T, CONV_CH), F32),
        jax.ShapeDtypeStruct((T, LANES), F32), jax.ShapeDtypeStruct((SSD_HEADS, T), F32),
        jax.ShapeDtypeStruct((T, ATT_INNER), F32), jax.ShapeDtypeStruct((T, ATT_INNER), F32),
        jax.ShapeDtypeStruct((T, ATT_INNER), F32))
    out_specs = (rows(SSD_INNER), rows(CONV_CH), rows(LANES),
                 pl.BlockSpec((SSD_HEADS, IN_ROWS), lambda i: (0, i)),
                 rows(ATT_INNER), rows(ATT_INNER), rows(ATT_INNER))
    return pl.pallas_call(
        _inproj_body, grid=(T // IN_ROWS,),
        in_specs=[rows(D_MODEL)] + [full(a) for a in ins[1:]],
        out_specs=out_specs, out_shape=out_shape,
        compiler_params=_params("arbitrary"), name="inproj")(*ins)


def _ssd_body(z_ref, xbc_ref, dt_ref, dtt_ref, cw_ref, cb_ref, dtb_ref, dtbt_ref, alog_ref,
              alogt_ref, dskip_ref, ng_ref, tril_ref, expand_ref, y_ref, ext_ref, state_ref):
    L, P, N, E = SSD_CHUNK, SSD_HEAD_DIM, SSD_STATE, SSD_HEADS_PER_GROUP
    R = SSD_STEP_CHUNKS * L
    c = pl.program_id(1)

    @pl.when(c == 0)
    def _():
        state_ref[...] = jnp.zeros_like(state_ref)
        ext_ref[0:SUBLANES, :] = jnp.zeros((SUBLANES, CONV_CH), F32)

    ext_ref[SUBLANES:SUBLANES + R, :] = xbc_ref[0]
    tril = tril_ref[...]
    expand = expand_ref[...]
    row = lax.broadcasted_iota(jnp.int32, (L, L), 0)
    col = lax.broadcasted_iota(jnp.int32, (L, L), 1)
    causal = row >= col
    for sub in range(SSD_STEP_CHUNKS):
        _ssd_chunk(sub * L, z_ref, dt_ref, dtt_ref, cw_ref, cb_ref, dtb_ref, dtbt_ref, alog_ref,
                   alogt_ref, dskip_ref, ng_ref, tril, expand, causal, y_ref, ext_ref, state_ref)
    ext_ref[0:SUBLANES, :] = ext_ref[R:R + SUBLANES, :]


def _ssd_chunk(r0, z_ref, dt_ref, dtt_ref, cw_ref, cb_ref, dtb_ref, dtbt_ref, alog_ref, alogt_ref,
               dskip_ref, ng_ref, tril, expand, causal, y_ref, ext_ref, state_ref):
    L, P, N, E = SSD_CHUNK, SSD_HEAD_DIM, SSD_STATE, SSD_HEADS_PER_GROUP
    conv = cb_ref[...]
    for j in range(SSD_CONV):
        off = r0 + SUBLANES - (SSD_CONV - 1) + j
        conv = conv + cw_ref[j:j + 1, :] * ext_ref[off:off + L, :]
    act = _silu(conv)
    xs = act[:, :SSD_INNER]
    b_in = act[:, SSD_INNER:SSD_INNER + SSD_GROUPS * N]
    c_out = act[:, SSD_INNER + SSD_GROUPS * N:]

    dt = _softplus(dt_ref[0, r0:r0 + L, :] + dtb_ref[...])
    a_dt = dt * (-jnp.exp(alog_ref[...]))
    a_cs = jnp.dot(tril, a_dt, precision=HIGHEST, preferred_element_type=F32)
    dt_t = _softplus(dtt_ref[:, r0:r0 + L] + dtbt_ref[...])
    a_dt_t = dt_t * (-jnp.exp(alogt_ref[...]))
    a_cs_t = lax.dot_general(a_dt_t, tril, (((1,), (1,)), ((), ())), precision=HIGHEST,
                             preferred_element_type=F32)
    last = a_cs[L - 1:L, :]
    per_head = jnp.concatenate(
        [dt, jnp.exp(a_cs), jnp.exp(last - a_cs), jnp.broadcast_to(jnp.exp(last), (SUBLANES, LANES))],
        axis=0)
    hi = per_head.astype(BF16)
    lo = (per_head - hi.astype(F32)).astype(BF16)
    on_lanes = jnp.dot(jnp.concatenate([hi, lo], axis=-1), expand, preferred_element_type=F32)
    dt_x = on_lanes[0:L]
    eacs_x = on_lanes[L:2 * L]
    dec_x = on_lanes[2 * L:3 * L]
    cdec_x = on_lanes[3 * L:3 * L + 1]
    xdt = xs * dt_x
    xdtd = (xdt * dec_x).astype(BF16)
    xdt_b = xdt.astype(BF16)

    y_parts = []
    for g in range(SSD_GROUPS):
        bg = b_in[:, g * N:(g + 1) * N].astype(BF16)
        cg = c_out[:, g * N:(g + 1) * N].astype(BF16)
        cb = lax.dot_general(cg, bg, (((1,), (1,)), ((), ())), preferred_element_type=F32)
        st = state_ref[g]
        gs = slice(g * E * P, (g + 1) * E * P)
        y_off = jnp.dot(cg, st.astype(BF16), preferred_element_type=F32) * eacs_x[:, gs]
        for e in range(E):
            h = g * E + e
            diff = a_cs[:, h:h + 1] - a_cs_t[h:h + 1, :]
            l_mat = jnp.exp(jnp.where(causal, diff, -jnp.inf))
            m = (cb * l_mat).astype(BF16)
            y_diag = jnp.dot(m, xdt_b[:, h * P:(h + 1) * P], preferred_element_type=F32)
            y_parts.append(y_diag + y_off[:, e * P:(e + 1) * P])
        upd = lax.dot_general(bg, xdtd[:, gs], (((0,), (0,)), ((), ())),
                              preferred_element_type=F32)
        state_ref[g] = cdec_x[:, gs] * st + upd
    y = jnp.concatenate(y_parts, axis=-1) + dskip_ref[...] * xs
    y = y * _silu(z_ref[0, r0:r0 + L, :])
    y = y * lax.rsqrt(jnp.mean(y * y, axis=-1, keepdims=True) + EPS) * ng_ref[...]
    y_ref[0, r0:r0 + L, :] = y


def _ssd(z, xbc, dt, dtt, conv_w, conv_b, dt_bias, a_log, d_skip, norm_g):
    B, S, _ = z.shape
    L = SSD_CHUNK
    R = SSD_STEP_CHUNKS * L
    nc = S // R
    pad_row = lambda a: jnp.pad(a.reshape(1, SSD_HEADS), ((0, 0), (0, LANES - SSD_HEADS)))
    tril = jnp.asarray(np.tril(np.ones((L, L), np.float32)))
    expand_np = np.zeros((LANES, SSD_INNER), np.float32)
    for h in range(SSD_HEADS):
        expand_np[h, h * SSD_HEAD_DIM:(h + 1) * SSD_HEAD_DIM] = 1.0
    ins = (z, xbc, dt, dtt, conv_w, conv_b.reshape(1, CONV_CH), pad_row(dt_bias),
           dt_bias.reshape(SSD_HEADS, 1), pad_row(a_log), a_log.reshape(SSD_HEADS, 1),
           jnp.repeat(d_skip, SSD_HEAD_DIM).reshape(1, SSD_INNER), norm_g.reshape(1, SSD_INNER),
           tril, jnp.asarray(np.concatenate([expand_np, expand_np], axis=0), dtype=BF16))
    full = lambda a: pl.BlockSpec(a.shape, lambda b, c: (0,) * a.ndim)
    chunk = lambda n: pl.BlockSpec((1, R, n), lambda b, c: (b, c, 0))
    in_specs = [chunk(SSD_INNER), chunk(CONV_CH), chunk(LANES),
                pl.BlockSpec((SSD_HEADS, R), lambda b, c: (0, b * nc + c))]
    in_specs += [full(a) for a in ins[4:]]
    return pl.pallas_call(
        _ssd_body, grid=(B, nc), in_specs=in_specs, out_specs=chunk(SSD_INNER),
        out_shape=jax.ShapeDtypeStruct((B, S, SSD_INNER), F32),
        scratch_shapes=[pltpu.VMEM((R + SUBLANES, CONV_CH), F32),
                        pltpu.VMEM((SSD_GROUPS, SSD_STATE, SSD_HEADS_PER_GROUP * SSD_HEAD_DIM), F32)],
        compiler_params=_params("arbitrary", "arbitrary"), name="ssd")(*ins)


def _bucket_tile():
    bs = MOBA_BLOCK
    dist = np.arange(bs)[:, None] - np.arange(2 * bs)[None, :] + bs
    max_exact = REL_BUCKETS // 2
    d = np.maximum(dist, max_exact).astype(np.float32)
    large = max_exact + (np.log(d / np.float32(max_exact)) / np.float32(math.log(REL_MAX_DIST / max_exact))
                         * np.float32(REL_BUCKETS - max_exact)).astype(np.int32)
    large = np.minimum(large, REL_BUCKETS - 1)
    bucket = np.where(dist < max_exact, dist, large)
    return np.where(dist >= 0, bucket, -1).astype(np.int32)


def _bias_body(rb_ref, bucket_ref, o_ref):
    head = pl.program_id(0)
    bucket = bucket_ref[...]
    far = rb_ref[REL_BUCKETS - 1, head]
    bias = jnp.where(bucket < 0, MASK_VALUE, 0.0)
    for i in range(REL_BUCKETS - 1):
        bias = jnp.where(bucket == i, (rb_ref[i, head] - far) * LOG2E, bias)
    o_ref[0] = bias


def _bias_tiles(rel_bias):
    bs = MOBA_BLOCK
    bucket = jnp.asarray(_bucket_tile())
    return pl.pallas_call(
        _bias_body, grid=(ATT_HEADS,),
        in_specs=[pl.BlockSpec(memory_space=pltpu.SMEM), pl.BlockSpec((bs, 2 * bs), lambda h: (0, 0))],
        out_specs=pl.BlockSpec((1, bs, 2 * bs), lambda h: (h, 0, 0)),
        out_shape=jax.ShapeDtypeStruct((ATT_HEADS, bs, 2 * bs), F32),
        compiler_params=_params("arbitrary"), name="bias_tiles")(rel_bias, bucket)


def _attn_body(q_ref, k_ref, v_ref, qg_ref, kg_ref, bias_ref, o_ref,
               kaug_ref, vaug_ref, qaug_ref, sa_ref, sb_ref, m_ref, acc_ref):
    bs, hd = MOBA_BLOCK, ATT_HEAD_DIM
    nb = k_ref.shape[1] // bs
    jp = pl.program_id(2)
    nt_dims = (((1,), (1,)), ((), ()))

    @pl.when(jp == 0)
    def _():
        S = nb * bs
        pair = 2 * hd
        r = lax.broadcasted_iota(jnp.int32, (pair, pair), 0) // hd
        c = lax.broadcasted_iota(jnp.int32, (pair, pair), 1) // hd
        head_mean = jnp.where(r == c, 1.0 / hd, 0.0).astype(BF16)

        def head_norm(x, gain):
            ms = jnp.dot((x * x).astype(BF16), head_mean, preferred_element_type=F32)
            return x * lax.rsqrt(ms + EPS) * gain

        kn = head_norm(k_ref[0], kg_ref[...])
        kmean = jnp.mean(kn.reshape(nb, bs, pair), axis=1)
        kn_b = kn.astype(BF16)
        v_b = v_ref[0].astype(BF16)
        lane = lax.broadcasted_iota(jnp.int32, (S, pair), 1)
        blk = lax.broadcasted_iota(jnp.int32, (S, pair), 0) // bs
        lane_m = lax.broadcasted_iota(jnp.int32, (nb, pair), 1)
        rowb = lax.broadcasted_iota(jnp.int32, (nb, bs), 0)
        lane_q = lax.broadcasted_iota(jnp.int32, (bs, pair), 1)
        kmean_b = []
        for h in range(2):
            own = (lane // hd) == h
            other0 = (1 - h) * hd
            kaug_ref[h] = jnp.where(own, kn_b, (lane - other0 == blk).astype(BF16))
            vaug_ref[h] = jnp.where(own, v_b, jnp.ones((S, pair), BF16))
            kmean_b.append(jnp.where((lane_m // hd) == h, kmean, 0.0).astype(BF16))
        blocks = [slice(j * bs, (j + 1) * bs) for j in range(nb)]
        qn_all = [head_norm(q_ref[0, rows, :], qg_ref[...]) for rows in blocks]
        masks_all = []
        for j in range(nb):
            qn_b = qn_all[j].astype(BF16)
            masks = []
            for h in range(2):
                if j <= MOBA_TOPK:
                    sel = rowb <= j
                else:
                    gate = lax.dot_general(kmean_b[h], qn_b, nt_dims,
                                           preferred_element_type=F32)
                    beats = []
                    for m in range(j):
                        gm = gate[m:m + 1, :]
                        beats.append(((gm > gate) | ((gm == gate) & (rowb > m))).astype(jnp.int32))
                    while len(beats) > 1:
                        beats = [a + b for a, b in zip(beats[::2], beats[1::2])] + beats[len(beats) & ~1:]
                    sel = ((rowb < j) & (beats[0] < MOBA_TOPK)) | (rowb == j)
                masks.append(jnp.where(sel, 0.0, MASK_VALUE))
            masks_all.append(masks)
        gap = jnp.zeros((hd - nb, bs), F32)
        for j, rows in enumerate(blocks):
            mask_t = jnp.concatenate([masks_all[j][1], gap, masks_all[j][0], gap], axis=0).T
            qs = qn_all[j] * (ATT_HEAD_DIM ** -0.5 * LOG2E)
            for h in range(2):
                qaug_ref[h, rows, :] = jnp.where((lane_q // hd) == h, qs, mask_t).astype(BF16)

    qt = 2 * bs
    q0 = pl.multiple_of(jp * qt, qt)
    all_rows = slice(0, qt)
    late_rows = slice(bs, qt)
    for h in range(2):
        m_ref[h] = jnp.full((qt, 2 * hd), -jnp.inf, F32)
        acc_ref[h] = jnp.zeros((qt, 2 * hd), F32)

    def scores(n, s_ref, rows=all_rows):
        start = pl.multiple_of(n * bs, bs)
        q_rows = pl.ds(q0 + rows.start, rows.stop - rows.start)
        for h in range(2):
            s_ref[h, rows, :] = lax.dot_general(qaug_ref[h, q_rows, :], kaug_ref[h, pl.ds(start, bs), :],
                                                nt_dims, preferred_element_type=F32)

    def update(n, s_ref, bias, rows=all_rows):
        start = pl.multiple_of(n * bs, bs)
        for r0 in range(rows.start, rows.stop, bs):
            sub = slice(r0, r0 + bs)
            for h in range(2):
                s = s_ref[h, sub, :]
                tile = None if bias is None else bias(h, r0 // bs)
                if tile is not None:
                    s = s + tile
                m_i = m_ref[h, sub, :]
                m_new = jnp.maximum(m_i, jnp.max(s, axis=-1, keepdims=True))
                alpha = jnp.exp2(m_i - m_new)
                p = jnp.exp2((s - jnp.concatenate([m_new, m_new], axis=-1)).astype(BF16))
                pv = jnp.dot(p, vaug_ref[h, pl.ds(start, bs), :], preferred_element_type=F32)
                acc_ref[h, sub, :] = alpha * acc_ref[h, sub, :] + pv
                m_ref[h, sub, :] = m_new

    def far_pair(n):
        scores(n + 1, sb_ref)
        update(n, sa_ref, None)
        scores(n + 2, sa_ref)
        update(n + 1, sb_ref, None)

    def far_quad(i, _):
        far_pair(4 * i)
        far_pair(4 * i + 2)
        return 0

    n_far = jnp.maximum(2 * jp - 1, 0)
    scores(0, sa_ref)
    lax.fori_loop(0, n_far >> 2, far_quad, 0)

    @pl.when((n_far & 2) != 0)
    def _():
        far_pair(n_far & ~3)

    prev = lambda h: bias_ref[h, :, 0:bs]
    own = lambda h, qblk=1: bias_ref[h, :, bs:2 * bs]
    bias_p = lambda h, qblk: prev(h) if qblk == 0 else None
    bias_o = lambda h, qblk: own(h) if qblk == 0 else prev(h)
    blk_p = jnp.maximum(2 * jp - 1, 0)
    blk_o = 2 * jp
    blk_n = 2 * jp + 1

    @pl.when(jp > 0)
    def _():
        scores(blk_p, sb_ref)
        update(n_far - 1, sa_ref, None)
        scores(blk_o, sa_ref)
        update(blk_p, sb_ref, bias_p)
        scores(blk_n, sb_ref, late_rows)
        update(blk_o, sa_ref, bias_o)
        update(blk_n, sb_ref, own, late_rows)

    @pl.when(jp == 0)
    def _():
        scores(blk_n, sb_ref, late_rows)
        update(blk_o, sa_ref, bias_o)
        update(blk_n, sb_ref, own, late_rows)

    first_half = lax.broadcasted_iota(jnp.int32, (qt, 2 * hd), 1) < hd
    numer = jnp.where(first_half, acc_ref[0], acc_ref[1])
    denom = pltpu.roll(jnp.where(first_half, acc_ref[1], acc_ref[0]), hd, 1)
    o_ref[0] = numer / denom


def _attention(q, k, v, q_g, k_g, bias_tiles):
    B, S, _ = q.shape
    bs = MOBA_BLOCK
    nb = S // bs
    assert S % (2 * bs) == 0 and nb <= ATT_HEAD_DIM
    pair = 2 * ATT_HEAD_DIM
    return pl.pallas_call(
        _attn_body, grid=(B, ATT_HEADS // 2, nb // 2),
        in_specs=[
            pl.BlockSpec((1, S, pair), lambda b, hp, i: (b, 0, hp)),
            pl.BlockSpec((1, S, pair), lambda b, hp, i: (b, 0, hp)),
            pl.BlockSpec((1, S, pair), lambda b, hp, i: (b, 0, hp)),
            pl.BlockSpec((1, pair), lambda b, hp, i: (0, 0)),
            pl.BlockSpec((1, pair), lambda b, hp, i: (0, 0)),
            pl.BlockSpec((2, bs, 2 * bs), lambda b, hp, i: (hp, 0, 0)),
        ],
        out_specs=pl.BlockSpec((1, 2 * bs, pair), lambda b, hp, i: (b, i, hp)),
        out_shape=jax.ShapeDtypeStruct((B, S, ATT_INNER), F32),
        scratch_shapes=[pltpu.VMEM((2, S, pair), BF16), pltpu.VMEM((2, S, pair), BF16),
                        pltpu.VMEM((2, S, pair), BF16),
                        pltpu.VMEM((2, 2 * bs, bs), F32), pltpu.VMEM((2, 2 * bs, bs), F32),
                        pltpu.VMEM((2, 2 * bs, pair), F32),
                        pltpu.VMEM((2, 2 * bs, pair), F32)],
        compiler_params=_params("arbitrary", "arbitrary", "arbitrary"), name="moba")(
            q, k, v, jnp.tile(q_g, 2).reshape(1, pair), jnp.tile(k_g, 2).reshape(1, pair), bias_tiles)


def _outproj_body(h_ref, ys_ref, ya_ref, wa_ref, wb_ref, g_ref, wrh_ref, wrl_ref, br_ref, tri_ref,
                  hn_ref, xn_ref, route_ref, routet_ref, counts_ref, carry_ref):
    i = pl.program_id(0)

    @pl.when(i == 0)
    def _():
        carry_ref[...] = jnp.zeros_like(carry_ref)

    _outproj_rows(0, OUT_ROWS, h_ref, ys_ref, ya_ref, wa_ref, wb_ref, g_ref, wrh_ref, wrl_ref,
                  br_ref, tri_ref, hn_ref, xn_ref, route_ref, routet_ref, carry_ref)
    counts_ref[...] = carry_ref[...]


def _outproj_rows(r0, n, h_ref, ys_ref, ya_ref, wa_ref, wb_ref, g_ref, wrh_ref, wrl_ref, br_ref, tri_ref,
                  hn_ref, xn_ref, route_ref, routet_ref, carry_ref):
    rs = slice(r0, r0 + n)
    hn = (h_ref[rs, :]
          + jnp.dot(ys_ref[rs, :].astype(BF16), wa_ref[...], preferred_element_type=F32)
          + jnp.dot(ya_ref[rs, :].astype(BF16), wb_ref[...], preferred_element_type=F32))
    hn_ref[rs, :] = hn
    xn = hn * lax.rsqrt(jnp.mean(hn * hn, axis=-1, keepdims=True) + EPS) * g_ref[...]
    for c in range(SUBLANES):
        xn_ref[pl.ds(r0 * SUBLANES + c, n, stride=SUBLANES), :] = xn[:, c * LANES:(c + 1) * LANES]
    x_hi = xn.astype(BF16)
    x_lo = (xn - x_hi.astype(F32)).astype(BF16)
    hi_terms = jnp.dot(x_hi, wrl_ref[...], preferred_element_type=F32)
    logits = (hi_terms[:, :LANES] + hi_terms[:, LANES:]
              + jnp.dot(x_lo, wrh_ref[...], preferred_element_type=F32)) + br_ref[...]
    lt = logits.T[0:ROUTER_ROWS, :]
    row = lax.broadcasted_iota(jnp.int32, (ROUTER_ROWS, n), 0)
    first = lambda hit: jnp.min(jnp.where(hit, row, LANES), axis=0, keepdims=True)
    l1 = jnp.where(row < MOE_GROUPS, lt, -jnp.inf)
    mx = jnp.max(l1, axis=0, keepdims=True)
    gval = 1.0 / jnp.sum(jnp.exp(l1 - mx), axis=0, keepdims=True)
    gidx = first(l1 == mx)
    e_row = row - ROUTER_LANE0
    in_grp = (e_row >= 0) & (e_row < MOE_EXPERTS) & ((e_row >> 3) == gidx)
    l2 = jnp.where(in_grp, lt, -jnp.inf)
    m1 = jnp.max(l2, axis=0, keepdims=True)
    i1 = first(l2 == m1)
    l2b = jnp.where(row == i1, -jnp.inf, l2)
    m2 = jnp.max(l2b, axis=0, keepdims=True)
    i2 = first(l2b == m2)
    r = jnp.exp(m2 - m1)
    g1 = gval / (1.0 + r)
    g2 = gval * r / (1.0 + r)
    oh1 = (row == i1)
    oh2 = (row == i2)
    both = jnp.where(oh1 | oh2, 1.0, 0.0)
    cx = jnp.dot(both.astype(BF16), tri_ref[...], preferred_element_type=F32) + carry_ref[...]
    rank1 = jnp.sum(jnp.where(oh1, cx, 0.0), axis=0, keepdims=True)
    rank2 = jnp.sum(jnp.where(oh2, cx, 0.0), axis=0, keepdims=True)
    carry_ref[...] = carry_ref[...] + jnp.sum(both, axis=1, keepdims=True)
    routet = jnp.concatenate(
        [(i1 - ROUTER_LANE0).astype(F32), (i2 - ROUTER_LANE0).astype(F32), g1, g2, rank1, rank2,
         jnp.zeros((SUBLANES - 6, n), F32)], axis=0)
    routet_ref[:, rs] = routet
    route_ref[rs, :] = jnp.concatenate([routet, jnp.zeros((LANES - SUBLANES, n), F32)], axis=0).T


def _outproj_route(h2, y_ssd, y_att, w_out, norm_g, w_r1, b_r1, w_r2, b_r2):
    T = h2.shape[0]
    R = OUT_ROWS
    wb = w_out.astype(BF16)
    pad = LANES - MOE_GROUPS - MOE_EXPERTS
    wr = jnp.pad(jnp.concatenate([w_r1, w_r2], axis=1), ((0, 0), (0, pad)))
    wr_hi = wr.astype(BF16)
    wr_lo = (wr - wr_hi.astype(F32)).astype(BF16)
    br = jnp.pad(jnp.concatenate([b_r1, b_r2]), (0, pad)).reshape(1, LANES)
    tri = jnp.asarray(np.triu(np.ones((R, R), np.float32), 1), dtype=BF16)
    ins = (h2, y_ssd, y_att, wb[:SSD_INNER], wb[SSD_INNER:], norm_g.reshape(1, D_MODEL),
           wr_hi, jnp.concatenate([wr_hi, wr_lo], axis=1), br, tri)
    full = lambda a: pl.BlockSpec(a.shape, lambda i: (0,) * a.ndim)
    rows = lambda n: pl.BlockSpec((R, n), lambda i: (i, 0))
    return pl.pallas_call(
        _outproj_body, grid=(T // R,),
        in_specs=[rows(D_MODEL), rows(SSD_INNER), rows(ATT_INNER)] + [full(a) for a in ins[3:]],
        out_specs=(rows(D_MODEL), pl.BlockSpec((R * SUBLANES, LANES), lambda i: (i, 0)), rows(LANES),
                   pl.BlockSpec((SUBLANES, R), lambda i: (0, i)),
                   pl.BlockSpec((ROUTER_ROWS, 1), lambda i: (0, 0))),
        out_shape=(jax.ShapeDtypeStruct((T, D_MODEL), F32),
                   jax.ShapeDtypeStruct((T * SUBLANES, LANES), F32),
                   jax.ShapeDtypeStruct((T, LANES), F32), jax.ShapeDtypeStruct((SUBLANES, T), F32),
                   jax.ShapeDtypeStruct((ROUTER_ROWS, 1), F32)),
        scratch_shapes=[pltpu.VMEM((ROUTER_ROWS, 1), F32)],
        compiler_params=_params("arbitrary"), name="outproj_route")(*ins)


def _slots_body(routet_ref, pstart_ref, dest_ref):
    cols = routet_ref.shape[1]
    expert = lax.broadcasted_iota(jnp.int32, (MOE_EXPERTS, cols), 0).astype(F32)
    pstart = pstart_ref[...]
    for j in range(2):
        hit = expert == routet_ref[j:j + 1, :]
        base = jnp.sum(jnp.where(hit, pstart, 0.0), axis=0, keepdims=True)
        dest_ref[j:j + 1, :] = (base + routet_ref[4 + j:5 + j, :]).astype(jnp.int32)


def _slots(routet, pstarts):
    T = routet.shape[1]
    cols = min(T, 4096)
    assert T % cols == 0
    return pl.pallas_call(
        _slots_body, grid=(T // cols,),
        in_specs=[pl.BlockSpec((SUBLANES, cols), lambda i: (0, i)),
                  pl.BlockSpec((MOE_EXPERTS, 1), lambda i: (0, 0))],
        out_specs=pl.BlockSpec((2, cols), lambda i: (0, i)),
        out_shape=jax.ShapeDtypeStruct((2, T), jnp.int32),
        compiler_params=_params("arbitrary"), name="slots")(
            routet, pstarts.astype(F32).reshape(MOE_EXPERTS, 1))


def _tile_copy(src_ref, src_row, dst_ref, dst_row, sem):
    def tile(ref, row):
        if len(ref.shape) == 3:
            return ref.at[row]
        return ref.at[pl.ds(pl.multiple_of(row * SUBLANES, SUBLANES), SUBLANES)]
    return pltpu.make_async_copy(tile(src_ref, src_row), tile(dst_ref, dst_row), sem)


def _dispatch_body(pad_ref, d1_ref, d2_ref, xn_ref, xb_ref, zero_ref, stage_ref, sem, lsem, ssem):
    n_blocks = xb_ref.shape[0] // EXPERT_ROWS
    n_used = pad_ref[MOE_EXPERTS]

    def zero_copy(row):
        return pltpu.make_async_copy(zero_ref, xb_ref.at[pl.ds(row, EXPERT_ROWS)], sem)

    @pl.when(pl.program_id(0) == 0)
    def _():
        zero_ref[...] = jnp.zeros_like(zero_ref)
        for e in range(MOE_EXPERTS):
            zero_copy(pad_ref[e]).start()
        for e in range(MOE_EXPERTS):
            zero_copy(pad_ref[e]).wait()
        for b in range(n_blocks - MOE_EXPERTS - 1, n_blocks):
            @pl.when(b >= n_used)
            def _():
                zero_copy(b * EXPERT_ROWS).start()
        for b in range(n_blocks - MOE_EXPERTS - 1, n_blocks):
            @pl.when(b >= n_used)
            def _():
                zero_copy(b * EXPERT_ROWS).wait()

    i = pl.program_id(0)
    n = pl.num_programs(0)
    rows = MOVE_ROWS * SUBLANES

    def load(step, slot):
        src = xn_ref.at[pl.ds(pl.multiple_of(step * rows, rows), rows)]
        return pltpu.make_async_copy(src, stage_ref.at[slot], lsem.at[slot])

    def wait_scatters(slot):
        for _ in range(2):
            pltpu.make_async_copy(stage_ref.at[slot], stage_ref.at[slot], ssem.at[slot]).wait()

    @pl.when(i == 0)
    def _():
        load(0, 0).start()

    @pl.when(i >= 1)
    def _():
        wait_scatters((i - 1) & 1)

    @pl.when(i + 1 < n)
    def _():
        load(i + 1, (i + 1) & 1).start()

    slot = i & 1
    load(i, slot).wait()
    base = i * MOVE_ROWS

    def scatter(t, _):
        _tile_copy(stage_ref.at[slot], t, xb_ref, d1_ref[base + t], ssem.at[slot]).start()
        _tile_copy(stage_ref.at[slot], t, xb_ref, d2_ref[base + t], ssem.at[slot]).start(priority=1)
        return 0

    lax.fori_loop(0, MOVE_ROWS, scatter, 0)

    @pl.when(i == n - 1)
    def _():
        wait_scatters(slot)


def _dispatch(xn, dest1, dest2, pad_start, n_slots):
    T = dest1.shape[0]
    grid_spec = pltpu.PrefetchScalarGridSpec(
        num_scalar_prefetch=3, grid=(T // MOVE_ROWS,),
        in_specs=[pl.BlockSpec(memory_space=pl.ANY)],
        out_specs=pl.BlockSpec(memory_space=pl.ANY),
        scratch_shapes=[pltpu.VMEM((EXPERT_ROWS, SUBLANES, LANES), F32),
                        pltpu.VMEM((2, MOVE_ROWS * SUBLANES, LANES), F32),
                        pltpu.SemaphoreType.DMA(()), pltpu.SemaphoreType.DMA((2,)),
                        pltpu.SemaphoreType.DMA((2,))])
    xb = pl.pallas_call(
        _dispatch_body, grid_spec=grid_spec,
        out_shape=jax.ShapeDtypeStruct((n_slots + EXPERT_ROWS, SUBLANES, LANES), F32),
        compiler_params=_params("arbitrary"), name="dispatch")(pad_start, dest1, dest2, xn)
    return xb.reshape(-1, LANES)


def _expert_body(be_ref, nused_ref, xb_ref, wg_ref, wu_ref, wd_ref, yb_ref,
                 x_ref, wgb_ref, wub_ref, wdb_ref):
    i = pl.program_id(0)
    used = i < nused_ref[0]
    new_expert = (i == 0) | (be_ref[i] != be_ref[jnp.maximum(i - 1, 0)])

    @pl.when(used & new_expert)
    def _():
        wgb_ref[...] = wg_ref[0, 0].astype(BF16)
        wub_ref[...] = wu_ref[0, 0].astype(BF16)
        wdb_ref[...] = wd_ref[0, 0].astype(BF16)

    @pl.when(used)
    def _():
        for c in range(SUBLANES):
            x_ref[:, c * LANES:(c + 1) * LANES] = xb_ref[pl.ds(c, EXPERT_ROWS, stride=SUBLANES), :].astype(BF16)
        x = x_ref[...]
        gate = jnp.dot(x, wgb_ref[...], preferred_element_type=F32)
        up = jnp.dot(x, wub_ref[...], preferred_element_type=F32)
        hmid = (_silu(gate) * up).astype(BF16)
        y = jnp.dot(hmid, wdb_ref[...], preferred_element_type=F32)
        for c in range(SUBLANES):
            yb_ref[pl.ds(c, EXPERT_ROWS, stride=SUBLANES), :] = y[:, c * LANES:(c + 1) * LANES]

    @pl.when(jnp.logical_not(used))
    def _():
        yb_ref[...] = jnp.zeros_like(yb_ref)


def _experts(xb, blk_exp, n_used, n_blk, layer, w_gate, w_up, w_down):
    M = EXPERT_ROWS
    tile = lambda index: pl.BlockSpec((M * SUBLANES, LANES), index)
    weight = lambda a: pl.BlockSpec((1, 1) + a.shape[2:], lambda i, be, nu: (layer, be[i], 0, 0))
    grid_spec = pltpu.PrefetchScalarGridSpec(
        num_scalar_prefetch=2, grid=(n_blk,),
        in_specs=[tile(lambda i, be, nu: (jnp.minimum(i, nu[0] - 1), 0)),
                  weight(w_gate), weight(w_up), weight(w_down)],
        out_specs=tile(lambda i, be, nu: (i, 0)),
        scratch_shapes=[pltpu.VMEM((M, D_MODEL), BF16), pltpu.VMEM((D_MODEL, MOE_HIDDEN), BF16),
                        pltpu.VMEM((D_MODEL, MOE_HIDDEN), BF16), pltpu.VMEM((MOE_HIDDEN, D_MODEL), BF16)])
    return pl.pallas_call(
        _expert_body, grid_spec=grid_spec,
        out_shape=jax.ShapeDtypeStruct((n_blk * M * SUBLANES, LANES), F32),
        compiler_params=_params("arbitrary"), name="experts")(
            blk_exp, n_used, xb, w_gate, w_up, w_down)


def _combine_body(d1_ref, d2_ref, h_ref, route_ref, yb_ref, o_ref, buf_ref, sem):
    i = pl.program_id(0)
    n = pl.num_programs(0)

    def gather(step, slot):
        base = step * MOVE_ROWS

        def body(t, _):
            _tile_copy(yb_ref, d1_ref[base + t], buf_ref.at[slot, 0], t, sem.at[slot]).start()
            _tile_copy(yb_ref, d2_ref[base + t], buf_ref.at[slot, 1], t, sem.at[slot]).start(priority=1)
            return 0

        lax.fori_loop(0, MOVE_ROWS, body, 0)

    @pl.when(i == 0)
    def _():
        gather(0, 0)

    @pl.when(i + 1 < n)
    def _():
        gather(i + 1, (i + 1) & 1)

    slot = i & 1
    pltpu.make_async_copy(buf_ref.at[slot], buf_ref.at[slot], sem.at[slot]).wait()
    route = route_ref[...]
    g1 = route[:, 2:3]
    g2 = route[:, 3:4]
    for c in range(SUBLANES):
        cs = slice(c * LANES, (c + 1) * LANES)
        rows = pl.ds(c, MOVE_ROWS, stride=SUBLANES)
        o_ref[:, cs] = h_ref[:, cs] + (buf_ref[slot, 0, rows, :] * g1 + buf_ref[slot, 1, rows, :] * g2)


def _combine(h2, route, yb, dest1, dest2):
    T = h2.shape[0]
    R = MOVE_ROWS
    grid_spec = pltpu.PrefetchScalarGridSpec(
        num_scalar_prefetch=2, grid=(T // R,),
        in_specs=[pl.BlockSpec((R, D_MODEL), lambda i, d1, d2: (i, 0)),
                  pl.BlockSpec((R, LANES), lambda i, d1, d2: (i, 0)), pl.BlockSpec(memory_space=pl.ANY)],
        out_specs=pl.BlockSpec((R, D_MODEL), lambda i, d1, d2: (i, 0)),
        scratch_shapes=[pltpu.VMEM((2, 2, R * SUBLANES, LANES), F32), pltpu.SemaphoreType.DMA((2,))])
    return pl.pallas_call(
        _combine_body, grid_spec=grid_spec, out_shape=jax.ShapeDtypeStruct((T, D_MODEL), F32),
        compiler_params=_params("arbitrary"), name="combine")(dest1, dest2, h2, route, yb)


def _moe(h2, xn, route, routet, counts, layer, w_gate, w_up, w_down):
    T = h2.shape[0]
    M = EXPERT_ROWS
    n_blk = (2 * T) // M + MOE_EXPERTS
    cnt = counts[ROUTER_LANE0:ROUTER_LANE0 + MOE_EXPERTS, 0].astype(jnp.int32)
    padded = (cnt + M - 1) // M * M
    ends = jnp.cumsum(padded)
    pstarts = ends - padded
    n_used = (ends[-1] // M).astype(jnp.int32)
    blk = jnp.minimum(jnp.arange(n_blk, dtype=jnp.int32), n_used - 1) * M
    blk_exp = jnp.sum((ends[None, :] <= blk[:, None]).astype(jnp.int32), axis=1)
    blk_exp = jnp.minimum(blk_exp, MOE_EXPERTS - 1)
    dest = _slots(routet, pstarts)
    pad_info = jnp.concatenate([pstarts + cnt, n_used.reshape(1)])
    xb = _dispatch(xn, dest[0], dest[1], pad_info, n_blk * M)
    yb = _experts(xb, blk_exp, n_used.reshape(1), n_blk, layer, w_gate, w_up, w_down)
    return _combine(h2, route, yb.reshape(-1, SUBLANES, LANES), dest[0], dest[1])


def kernel(x, rel_bias, norm1_g, w_in, conv_w, conv_b, dt_bias, a_log, d_skip, ssd_norm_g,
           q_norm_g, k_norm_g, w_out, norm2_g, w_r1, b_r1, w_r2, b_r2, w_gate, w_up, w_down):
    B, S, D = x.shape
    T = B * S
    h2 = x.reshape(T, D)
    depth = w_in.shape[0]
    bias_tiles = _bias_tiles(rel_bias)
    for l in range(depth):
        z, xbc, dt, dtt, q, k, v = _inproj(h2, norm1_g[l], w_in[l])
        r3 = lambda a: a.reshape(B, S, a.shape[-1])
        y_ssd = _ssd(r3(z), r3(xbc), r3(dt), dtt, conv_w[l], conv_b[l], dt_bias[l], a_log[l],
                     d_skip[l], ssd_norm_g[l])
        y_att = _attention(r3(q), r3(k), r3(v), q_norm_g[l], k_norm_g[l], bias_tiles)
        h2, xn, route, routet, counts = _outproj_route(
            h2, y_ssd.reshape(T, SSD_INNER), y_att.reshape(T, ATT_INNER), w_out[l], norm2_g[l],
            w_r1[l], b_r1[l], w_r2[l], b_r2[l])
        h2 = _moe(h2, xn, route, routet, counts, l, w_gate, w_up, w_down)
    return h2.reshape(B, S, D)
```

```python
import math

import numpy as np
import jax
import jax.numpy as jnp
from jax import lax
from jax.experimental import pallas as pl
from jax.experimental.pallas import tpu as pltpu

F32 = jnp.float32
BF16 = jnp.bfloat16
HIGHEST = lax.Precision.HIGHEST

LANES = 128
SUBLANES = 8

D_MODEL = 1024
SSD_HEADS = 8
SSD_HEAD_DIM = 64
SSD_INNER = SSD_HEADS * SSD_HEAD_DIM
SSD_GROUPS = 2
SSD_HEADS_PER_GROUP = SSD_HEADS // SSD_GROUPS
SSD_STATE = 64
SSD_CONV = 4
SSD_CHUNK = 128
CONV_CH = SSD_INNER + 2 * SSD_GROUPS * SSD_STATE
ATT_HEADS = 8
ATT_HEAD_DIM = 64
ATT_INNER = ATT_HEADS * ATT_HEAD_DIM
MOBA_BLOCK = 256
MOBA_TOPK = 3
REL_BUCKETS = 32
REL_MAX_DIST = 128
MOE_GROUPS = 4
MOE_EXPERTS_PER_GROUP = 8
MOE_EXPERTS = MOE_GROUPS * MOE_EXPERTS_PER_GROUP
MOE_HIDDEN = 512
EPS = 1e-6

MASK_VALUE = -1e30
LOG2E = math.log2(math.e)
IN_ROWS = 1024
SSD_STEP_CHUNKS = 8
OUT_ROWS = 1024
EXPERT_ROWS = 512
MOVE_ROWS = 256
VMEM_LIMIT = 56 * 1024 * 1024
ROUTER_LANE0 = MOE_GROUPS
ROUTER_ROWS = 48


def _silu(x):
    return x * (1.0 / (1.0 + jnp.exp(-x)))


def _softplus(x):
    return jnp.maximum(x, 0.0) + jnp.log(1.0 + jnp.exp(-jnp.abs(x)))


def _params(*sem):
    return pltpu.CompilerParams(dimension_semantics=sem, vmem_limit_bytes=VMEM_LIMIT)


def _inproj_body(h_ref, g_ref, wz_ref, wx_ref, wdt_ref, wq_ref, wk_ref, wv_ref,
                 z_ref, xbc_ref, dt_ref, dtt_ref, q_ref, k_ref, v_ref):
    x = h_ref[...]
    xn = x * lax.rsqrt(jnp.mean(x * x, axis=-1, keepdims=True) + EPS) * g_ref[...]
    xb = xn.astype(BF16)
    z_ref[...] = jnp.dot(xb, wz_ref[...], preferred_element_type=F32)
    xbc_ref[...] = jnp.dot(xb, wx_ref[...], preferred_element_type=F32)
    dt = jnp.dot(xb, wdt_ref[...], preferred_element_type=F32)
    dt_ref[...] = dt
    dtt_ref[...] = dt.T[0:SSD_HEADS, :]
    q_ref[...] = jnp.dot(xb, wq_ref[...], preferred_element_type=F32)
    k_ref[...] = jnp.dot(xb, wk_ref[...], preferred_element_type=F32)
    v_ref[...] = jnp.dot(xb, wv_ref[...], preferred_element_type=F32)


def _inproj(h2, g, w_in):
    T = h2.shape[0]
    c0, c1, c2 = SSD_INNER, SSD_INNER + CONV_CH, SSD_INNER + CONV_CH + SSD_HEADS
    wb = w_in.astype(BF16)
    wz, wx, wdt = wb[:, :c0], wb[:, c0:c1], wb[:, c1:c2]
    wq, wk, wv = (wb[:, c2 + i * ATT_INNER:c2 + (i + 1) * ATT_INNER] for i in range(3))
    wdt_pad = jnp.pad(wdt, ((0, 0), (0, LANES - SSD_HEADS)))
    full = lambda a: pl.BlockSpec(a.shape, lambda i: (0,) * a.ndim)
    rows = lambda n: pl.BlockSpec((IN_ROWS, n), lambda i: (i, 0))
    ins = (h2, g.reshape(1, D_MODEL), wz, wx, wdt_pad, wq, wk, wv)
    out_shape = (
        jax.ShapeDtypeStruct((T, SSD_INNER), F32), jax.ShapeDtypeStruct((T, CONV_CH), F32),
        jax.ShapeDtypeStruct((T, LANES), F32), jax.ShapeDtypeStruct((SSD_HEADS, T), F32),
        jax.ShapeDtypeStruct((T, ATT_INNER), F32), jax.ShapeDtypeStruct((T, ATT_INNER), F32),
        jax.ShapeDtypeStruct((T, ATT_INNER), F32))
    out_specs = (rows(SSD_INNER), rows(CONV_CH), rows(LANES),
                 pl.BlockSpec((SSD_HEADS, IN_ROWS), lambda i: (0, i)),
                 rows(ATT_INNER), rows(ATT_INNER), rows(ATT_INNER))
    return pl.pallas_call(
        _inproj_body, grid=(T // IN_ROWS,),
        in_specs=[rows(D_MODEL)] + [full(a) for a in ins[1:]],
        out_specs=out_specs, out_shape=out_shape,
        compiler_params=_params("arbitrary"), name="inproj")(*ins)


def _ssd_body(z_ref, xbc_ref, dt_ref, dtt_ref, cw_ref, cb_ref, dtb_ref, dtbt_ref, alog_ref,
              alogt_ref, dskip_ref, ng_ref, tril_ref, expand_ref, y_ref, ext_ref, state_ref):
    L, P, N, E = SSD_CHUNK, SSD_HEAD_DIM, SSD_STATE, SSD_HEADS_PER_GROUP
    R = SSD_STEP_CHUNKS * L
    c = pl.program_id(1)

    @pl.when(c == 0)
    def _():
        state_ref[...] = jnp.zeros_like(state_ref)
        ext_ref[0:SUBLANES, :] = jnp.zeros((SUBLANES, CONV_CH), F32)

    ext_ref[SUBLANES:SUBLANES + R, :] = xbc_ref[0]
    tril = tril_ref[...]
    expand = expand_ref[...]
    row = lax.broadcasted_iota(jnp.int32, (L, L), 0)
    col = lax.broadcasted_iota(jnp.int32, (L, L), 1)
    causal = row >= col
    for sub in range(SSD_STEP_CHUNKS):
        _ssd_chunk(sub * L, z_ref, dt_ref, dtt_ref, cw_ref, cb_ref, dtb_ref, dtbt_ref, alog_ref,
                   alogt_ref, dskip_ref, ng_ref, tril, expand, causal, y_ref, ext_ref, state_ref)
    ext_ref[0:SUBLANES, :] = ext_ref[R:R + SUBLANES, :]


def _ssd_chunk(r0, z_ref, dt_ref, dtt_ref, cw_ref, cb_ref, dtb_ref, dtbt_ref, alog_ref, alogt_ref,
               dskip_ref, ng_ref, tril, expand, causal, y_ref, ext_ref, state_ref):
    L, P, N, E = SSD_CHUNK, SSD_HEAD_DIM, SSD_STATE, SSD_HEADS_PER_GROUP
    conv = cb_ref[...]
    for j in range(SSD_CONV):
        off = r0 + SUBLANES - (SSD_CONV - 1) + j
        conv = conv + cw_ref[j:j + 1, :] * ext_ref[off:off + L, :]
    act = _silu(conv)
    xs = act[:, :SSD_INNER]
    b_in = act[:, SSD_INNER:SSD_INNER + SSD_GROUPS * N]
    c_out = act[:, SSD_INNER + SSD_GROUPS * N:]

    dt = _softplus(dt_ref[0, r0:r0 + L, :] + dtb_ref[...])
    a_dt = dt * (-jnp.exp(alog_ref[...]))
    a_cs = jnp.dot(tril, a_dt, precision=HIGHEST, preferred_element_type=F32)
    dt_t = _softplus(dtt_ref[:, r0:r0 + L] + dtbt_ref[...])
    a_dt_t = dt_t * (-jnp.exp(alogt_ref[...]))
    a_cs_t = lax.dot_general(a_dt_t, tril, (((1,), (1,)), ((), ())), precision=HIGHEST,
                             preferred_element_type=F32)
    last = a_cs[L - 1:L, :]
    per_head = jnp.concatenate(
        [dt, jnp.exp(a_cs), jnp.exp(last - a_cs), jnp.broadcast_to(jnp.exp(last), (SUBLANES, LANES))],
        axis=0)
    hi = per_head.astype(BF16)
    lo = (per_head - hi.astype(F32)).astype(BF16)
    on_lanes = jnp.dot(jnp.concatenate([hi, lo], axis=-1), expand, preferred_element_type=F32)
    dt_x = on_lanes[0:L]
    eacs_x = on_lanes[L:2 * L]
    dec_x = on_lanes[2 * L:3 * L]
    cdec_x = on_lanes[3 * L:3 * L + 1]
    xdt = xs * dt_x
    xdtd = (xdt * dec_x).astype(BF16)
    xdt_b = xdt.astype(BF16)

    y_parts = []
    groups = []
    for g in range(SSD_GROUPS):
        bg = b_in[:, g * N:(g + 1) * N].astype(BF16)
        cg = c_out[:, g * N:(g + 1) * N].astype(BF16)
        cb = lax.dot_general(cg, bg, (((1,), (1,)), ((), ())), preferred_element_type=F32)
        gs = slice(g * E * P, (g + 1) * E * P)
        upd = lax.dot_general(bg, xdtd[:, gs], (((0,), (0,)), ((), ())),
                              preferred_element_type=F32)
        groups.append((cg, gs, upd))
        for e in range(E):
            h = g * E + e
            diff = a_cs[:, h:h + 1] - a_cs_t[h:h + 1, :]
            l_mat = jnp.exp(jnp.where(causal, diff, -jnp.inf))
            m = (cb * l_mat).astype(BF16)
            y_parts.append(jnp.dot(m, xdt_b[:, h * P:(h + 1) * P], preferred_element_type=F32))
    y_off = []
    for g, (cg, gs, upd) in enumerate(groups):
        st = state_ref[g]
        y_off.append(jnp.dot(cg, st.astype(BF16), preferred_element_type=F32))
        state_ref[g] = cdec_x[:, gs] * st + upd
    y = (jnp.concatenate(y_parts, axis=-1) + jnp.concatenate(y_off, axis=-1) * eacs_x
         + dskip_ref[...] * xs)
    y = y * _silu(z_ref[0, r0:r0 + L, :])
    y = y * lax.rsqrt(jnp.mean(y * y, axis=-1, keepdims=True) + EPS) * ng_ref[...]
    y_ref[0, r0:r0 + L, :] = y


def _ssd(z, xbc, dt, dtt, conv_w, conv_b, dt_bias, a_log, d_skip, norm_g):
    B, S, _ = z.shape
    L = SSD_CHUNK
    R = SSD_STEP_CHUNKS * L
    nc = S // R
    pad_row = lambda a: jnp.pad(a.reshape(1, SSD_HEADS), ((0, 0), (0, LANES - SSD_HEADS)))
    tril = jnp.asarray(np.tril(np.ones((L, L), np.float32)))
    expand_np = np.zeros((LANES, SSD_INNER), np.float32)
    for h in range(SSD_HEADS):
        expand_np[h, h * SSD_HEAD_DIM:(h + 1) * SSD_HEAD_DIM] = 1.0
    ins = (z, xbc, dt, dtt, conv_w, conv_b.reshape(1, CONV_CH), pad_row(dt_bias),
           dt_bias.reshape(SSD_HEADS, 1), pad_row(a_log), a_log.reshape(SSD_HEADS, 1),
           jnp.repeat(d_skip, SSD_HEAD_DIM).reshape(1, SSD_INNER), norm_g.reshape(1, SSD_INNER),
           tril, jnp.asarray(np.concatenate([expand_np, expand_np], axis=0), dtype=BF16))
    full = lambda a: pl.BlockSpec(a.shape, lambda b, c: (0,) * a.ndim)
    chunk = lambda n: pl.BlockSpec((1, R, n), lambda b, c: (b, c, 0))
    in_specs = [chunk(SSD_INNER), chunk(CONV_CH), chunk(LANES),
                pl.BlockSpec((SSD_HEADS, R), lambda b, c: (0, b * nc + c))]
    in_specs += [full(a) for a in ins[4:]]
    return pl.pallas_call(
        _ssd_body, grid=(B, nc), in_specs=in_specs, out_specs=chunk(SSD_INNER),
        out_shape=jax.ShapeDtypeStruct((B, S, SSD_INNER), F32),
        scratch_shapes=[pltpu.VMEM((R + SUBLANES, CONV_CH), F32),
                        pltpu.VMEM((SSD_GROUPS, SSD_STATE, SSD_HEADS_PER_GROUP * SSD_HEAD_DIM), F32)],
        compiler_params=_params("arbitrary", "arbitrary"), name="ssd")(*ins)


def _bucket_tile():
    bs = MOBA_BLOCK
    dist = np.arange(bs)[:, None] - np.arange(2 * bs)[None, :] + bs
    max_exact = REL_BUCKETS // 2
    d = np.maximum(dist, max_exact).astype(np.float32)
    large = max_exact + (np.log(d / np.float32(max_exact)) / np.float32(math.log(REL_MAX_DIST / max_exact))
                         * np.float32(REL_BUCKETS - max_exact)).astype(np.int32)
    large = np.minimum(large, REL_BUCKETS - 1)
    bucket = np.where(dist < max_exact, dist, large)
    return np.where(dist >= 0, bucket, -1).astype(np.int32)


def _bias_body(rb_ref, bucket_ref, o_ref):
    head = pl.program_id(0)
    bucket = bucket_ref[...]
    far = rb_ref[REL_BUCKETS - 1, head]
    bias = jnp.where(bucket < 0, MASK_VALUE, 0.0)
    for i in range(REL_BUCKETS - 1):
        bias = jnp.where(bucket == i, (rb_ref[i, head] - far) * LOG2E, bias)
    o_ref[0] = bias


def _bias_tiles(rel_bias):
    bs = MOBA_BLOCK
    bucket = jnp.asarray(_bucket_tile())
    return pl.pallas_call(
        _bias_body, grid=(ATT_HEADS,),
        in_specs=[pl.BlockSpec(memory_space=pltpu.SMEM), pl.BlockSpec((bs, 2 * bs), lambda h: (0, 0))],
        out_specs=pl.BlockSpec((1, bs, 2 * bs), lambda h: (h, 0, 0)),
        out_shape=jax.ShapeDtypeStruct((ATT_HEADS, bs, 2 * bs), F32),
        compiler_params=_params("arbitrary"), name="bias_tiles")(rel_bias, bucket)


def _attn_body(q_ref, k_ref, v_ref, qg_ref, kg_ref, bias_ref, o_ref,
               kaug_ref, vaug_ref, qaug_ref, sa_ref, sb_ref, m_ref, acc_ref):
    bs, hd = MOBA_BLOCK, ATT_HEAD_DIM
    nb = k_ref.shape[1] // bs
    jp = pl.program_id(2)
    nt_dims = (((1,), (1,)), ((), ()))

    @pl.when(jp == 0)
    def _():
        S = nb * bs
        pair = 2 * hd
        r = lax.broadcasted_iota(jnp.int32, (pair, pair), 0) // hd
        c = lax.broadcasted_iota(jnp.int32, (pair, pair), 1) // hd
        head_mean = jnp.where(r == c, 1.0 / hd, 0.0).astype(BF16)

        def head_norm(x, gain):
            ms = jnp.dot((x * x).astype(BF16), head_mean, preferred_element_type=F32)
            return x * lax.rsqrt(ms + EPS) * gain

        kn = head_norm(k_ref[0], kg_ref[...])
        kmean = jnp.mean(kn.reshape(nb, bs, pair), axis=1)
        kn_b = kn.astype(BF16)
        v_b = v_ref[0].astype(BF16)
        lane = lax.broadcasted_iota(jnp.int32, (S, pair), 1)
        blk = lax.broadcasted_iota(jnp.int32, (S, pair), 0) // bs
        lane_m = lax.broadcasted_iota(jnp.int32, (nb, pair), 1)
        rowb = lax.broadcasted_iota(jnp.int32, (nb, bs), 0)
        lane_q = lax.broadcasted_iota(jnp.int32, (bs, pair), 1)
        kmean_b = []
        for h in range(2):
            own = (lane // hd) == h
            other0 = (1 - h) * hd
            kaug_ref[h] = jnp.where(own, kn_b, (lane - other0 == blk).astype(BF16))
            vaug_ref[h] = jnp.where(own, v_b, jnp.ones((S, pair), BF16))
            kmean_b.append(jnp.where((lane_m // hd) == h, kmean, 0.0).astype(BF16))
        blocks = [slice(j * bs, (j + 1) * bs) for j in range(nb)]
        qn_all = [head_norm(q_ref[0, rows, :], qg_ref[...]) for rows in blocks]
        masks_all = []
        for j in range(nb):
            qn_b = qn_all[j].astype(BF16)
            masks = []
            for h in range(2):
                if j <= MOBA_TOPK:
                    sel = rowb <= j
                else:
                    gate = lax.dot_general(kmean_b[h], qn_b, nt_dims,
                                           preferred_element_type=F32)
                    beats = []
                    for m in range(j):
                        gm = gate[m:m + 1, :]
                        beats.append(((gm > gate) | ((gm == gate) & (rowb > m))).astype(jnp.int32))
                    while len(beats) > 1:
                        beats = [a + b for a, b in zip(beats[::2], beats[1::2])] + beats[len(beats) & ~1:]
                    sel = ((rowb < j) & (beats[0] < MOBA_TOPK)) | (rowb == j)
                masks.append(jnp.where(sel, 0.0, MASK_VALUE))
            masks_all.append(masks)
        gap = jnp.zeros((hd - nb, bs), F32)
        for j, rows in enumerate(blocks):
            mask_t = jnp.concatenate([masks_all[j][1], gap, masks_all[j][0], gap], axis=0).T
            qs = qn_all[j] * (ATT_HEAD_DIM ** -0.5 * LOG2E)
            for h in range(2):
                qaug_ref[h, rows, :] = jnp.where((lane_q // hd) == h, qs, mask_t).astype(BF16)

    qt = 2 * bs
    q0 = pl.multiple_of(jp * qt, qt)
    all_rows = slice(0, qt)
    late_rows = slice(bs, qt)
    for h in range(2):
        m_ref[h] = jnp.full((qt, 2 * hd), -jnp.inf, F32)
        acc_ref[h] = jnp.zeros((qt, 2 * hd), F32)

    def scores(n, s_ref, rows=all_rows):
        start = pl.multiple_of(n * bs, bs)
        q_rows = pl.ds(q0 + rows.start, rows.stop - rows.start)
        for h in range(2):
            s_ref[h, rows, :] = lax.dot_general(qaug_ref[h, q_rows, :], kaug_ref[h, pl.ds(start, bs), :],
                                                nt_dims, preferred_element_type=F32)

    def update(n, s_ref, bias, rows=all_rows):
        start = pl.multiple_of(n * bs, bs)
        for r0 in range(rows.start, rows.stop, bs):
            sub = slice(r0, r0 + bs)
            for h in range(2):
                s = s_ref[h, sub, :]
                tile = None if bias is None else bias(h, r0 // bs)
                if tile is not None:
                    s = s + tile
                m_i = m_ref[h, sub, :]
                m_new = jnp.maximum(m_i, jnp.max(s, axis=-1, keepdims=True))
                alpha = jnp.exp2(m_i - m_new)
                p = jnp.exp2((s - jnp.concatenate([m_new, m_new], axis=-1)).astype(BF16))
                pv = jnp.dot(p, vaug_ref[h, pl.ds(start, bs), :], preferred_element_type=F32)
                acc_ref[h, sub, :] = alpha * acc_ref[h, sub, :] + pv
                m_ref[h, sub, :] = m_new

    def far_pair(n):
        scores(n + 1, sb_ref)
        update(n, sa_ref, None)
        scores(n + 2, sa_ref)
        update(n + 1, sb_ref, None)

    def far_quad(i, _):
        far_pair(4 * i)
        far_pair(4 * i + 2)
        return 0

    n_far = jnp.maximum(2 * jp - 1, 0)
    scores(0, sa_ref)
    lax.fori_loop(0, n_far >> 2, far_quad, 0)

    @pl.when((n_far & 2) != 0)
    def _():
        far_pair(n_far & ~3)

    prev = lambda h: bias_ref[h, :, 0:bs]
    own = lambda h, qblk=1: bias_ref[h, :, bs:2 * bs]
    bias_p = lambda h, qblk: prev(h) if qblk == 0 else None
    bias_o = lambda h, qblk: own(h) if qblk == 0 else prev(h)
    blk_p = jnp.maximum(2 * jp - 1, 0)
    blk_o = 2 * jp
    blk_n = 2 * jp + 1

    @pl.when(jp > 0)
    def _():
        scores(blk_p, sb_ref)
        update(n_far - 1, sa_ref, None)
        scores(blk_o, sa_ref)
        update(blk_p, sb_ref, bias_p)
        scores(blk_n, sb_ref, late_rows)
        update(blk_o, sa_ref, bias_o)
        update(blk_n, sb_ref, own, late_rows)

    @pl.when(jp == 0)
    def _():
        scores(blk_n, sb_ref, late_rows)
        update(blk_o, sa_ref, bias_o)
        update(blk_n, sb_ref, own, late_rows)

    first_half = lax.broadcasted_iota(jnp.int32, (qt, 2 * hd), 1) < hd
    numer = jnp.where(first_half, acc_ref[0], acc_ref[1])
    denom = pltpu.roll(jnp.where(first_half, acc_ref[1], acc_ref[0]), hd, 1)
    o_ref[0] = numer / denom


def _attention(q, k, v, q_g, k_g, bias_tiles):
    B, S, _ = q.shape
    bs = MOBA_BLOCK
    nb = S // bs
    assert S % (2 * bs) == 0 and nb <= ATT_HEAD_DIM
    pair = 2 * ATT_HEAD_DIM
    return pl.pallas_call(
        _attn_body, grid=(B, ATT_HEADS // 2, nb // 2),
        in_specs=[
            pl.BlockSpec((1, S, pair), lambda b, hp, i: (b, 0, hp)),
            pl.BlockSpec((1, S, pair), lambda b, hp, i: (b, 0, hp)),
            pl.BlockSpec((1, S, pair), lambda b, hp, i: (b, 0, hp)),
            pl.BlockSpec((1, pair), lambda b, hp, i: (0, 0)),
            pl.BlockSpec((1, pair), lambda b, hp, i: (0, 0)),
            pl.BlockSpec((2, bs, 2 * bs), lambda b, hp, i: (hp, 0, 0)),
        ],
        out_specs=pl.BlockSpec((1, 2 * bs, pair), lambda b, hp, i: (b, i, hp)),
        out_shape=jax.ShapeDtypeStruct((B, S, ATT_INNER), F32),
        scratch_shapes=[pltpu.VMEM((2, S, pair), BF16), pltpu.VMEM((2, S, pair), BF16),
                        pltpu.VMEM((2, S, pair), BF16),
                        pltpu.VMEM((2, 2 * bs, bs), F32), pltpu.VMEM((2, 2 * bs, bs), F32),
                        pltpu.VMEM((2, 2 * bs, pair), F32),
                        pltpu.VMEM((2, 2 * bs, pair), F32)],
        compiler_params=_params("arbitrary", "arbitrary", "arbitrary"), name="moba")(
            q, k, v, jnp.tile(q_g, 2).reshape(1, pair), jnp.tile(k_g, 2).reshape(1, pair), bias_tiles)


def _outproj_body(h_ref, ys_ref, ya_ref, wa_ref, wb_ref, g_ref, wrh_ref, wrl_ref, br_ref, tri_ref,
                  hn_ref, xn_ref, route_ref, routet_ref, counts_ref, carry_ref):
    i = pl.program_id(0)

    @pl.when(i == 0)
    def _():
        carry_ref[...] = jnp.zeros_like(carry_ref)

    _outproj_rows(0, OUT_ROWS, h_ref, ys_ref, ya_ref, wa_ref, wb_ref, g_ref, wrh_ref, wrl_ref,
                  br_ref, tri_ref, hn_ref, xn_ref, route_ref, routet_ref, carry_ref)
    counts_ref[...] = carry_ref[...]


def _outproj_rows(r0, n, h_ref, ys_ref, ya_ref, wa_ref, wb_ref, g_ref, wrh_ref, wrl_ref, br_ref, tri_ref,
                  hn_ref, xn_ref, route_ref, routet_ref, carry_ref):
    rs = slice(r0, r0 + n)
    hn = (h_ref[rs, :]
          + jnp.dot(ys_ref[rs, :].astype(BF16), wa_ref[...], preferred_element_type=F32)
          + jnp.dot(ya_ref[rs, :].astype(BF16), wb_ref[...], preferred_element_type=F32))
    hn_ref[rs, :] = hn
    xn = hn * lax.rsqrt(jnp.mean(hn * hn, axis=-1, keepdims=True) + EPS) * g_ref[...]
    for c in range(SUBLANES):
        xn_ref[pl.ds(r0 * SUBLANES + c, n, stride=SUBLANES), :] = xn[:, c * LANES:(c + 1) * LANES]
    x_hi = xn.astype(BF16)
    x_lo = (xn - x_hi.astype(F32)).astype(BF16)
    hi_terms = jnp.dot(x_hi, wrl_ref[...], preferred_element_type=F32)
    logits = (hi_terms[:, :LANES] + hi_terms[:, LANES:]
              + jnp.dot(x_lo, wrh_ref[...], preferred_element_type=F32)) + br_ref[...]
    lt = logits.T[0:ROUTER_ROWS, :]
    row = lax.broadcasted_iota(jnp.int32, (ROUTER_ROWS, n), 0)
    first = lambda hit: jnp.min(jnp.where(hit, row, LANES), axis=0, keepdims=True)
    l1 = jnp.where(row < MOE_GROUPS, lt, -jnp.inf)
    mx = jnp.max(l1, axis=0, keepdims=True)
    gval = 1.0 / jnp.sum(jnp.exp(l1 - mx), axis=0, keepdims=True)
    gidx = first(l1 == mx)
    e_row = row - ROUTER_LANE0
    in_grp = (e_row >= 0) & (e_row < MOE_EXPERTS) & ((e_row >> 3) == gidx)
    l2 = jnp.where(in_grp, lt, -jnp.inf)
    m1 = jnp.max(l2, axis=0, keepdims=True)
    i1 = first(l2 == m1)
    l2b = jnp.where(row == i1, -jnp.inf, l2)
    m2 = jnp.max(l2b, axis=0, keepdims=True)
    i2 = first(l2b == m2)
    r = jnp.exp(m2 - m1)
    g1 = gval / (1.0 + r)
    g2 = gval * r / (1.0 + r)
    oh1 = (row == i1)
    oh2 = (row == i2)
    both = jnp.where(oh1 | oh2, 1.0, 0.0)
    cx = jnp.dot(both.astype(BF16), tri_ref[...], preferred_element_type=F32) + carry_ref[...]
    rank1 = jnp.sum(jnp.where(oh1, cx, 0.0), axis=0, keepdims=True)
    rank2 = jnp.sum(jnp.where(oh2, cx, 0.0), axis=0, keepdims=True)
    carry_ref[...] = carry_ref[...] + jnp.sum(both, axis=1, keepdims=True)
    routet = jnp.concatenate(
        [(i1 - ROUTER_LANE0).astype(F32), (i2 - ROUTER_LANE0).astype(F32), g1, g2, rank1, rank2,
         jnp.zeros((SUBLANES - 6, n), F32)], axis=0)
    routet_ref[:, rs] = routet
    route_ref[rs, :] = jnp.concatenate([routet, jnp.zeros((LANES - SUBLANES, n), F32)], axis=0).T


def _outproj_route(h2, y_ssd, y_att, w_out, norm_g, w_r1, b_r1, w_r2, b_r2):
    T = h2.shape[0]
    R = OUT_ROWS
    wb = w_out.astype(BF16)
    pad = LANES - MOE_GROUPS - MOE_EXPERTS
    wr = jnp.pad(jnp.concatenate([w_r1, w_r2], axis=1), ((0, 0), (0, pad)))
    wr_hi = wr.astype(BF16)
    wr_lo = (wr - wr_hi.astype(F32)).astype(BF16)
    br = jnp.pad(jnp.concatenate([b_r1, b_r2]), (0, pad)).reshape(1, LANES)
    tri = jnp.asarray(np.triu(np.ones((R, R), np.float32), 1), dtype=BF16)
    ins = (h2, y_ssd, y_att, wb[:SSD_INNER], wb[SSD_INNER:], norm_g.reshape(1, D_MODEL),
           wr_hi, jnp.concatenate([wr_hi, wr_lo], axis=1), br, tri)
    full = lambda a: pl.BlockSpec(a.shape, lambda i: (0,) * a.ndim)
    rows = lambda n: pl.BlockSpec((R, n), lambda i: (i, 0))
    return pl.pallas_call(
        _outproj_body, grid=(T // R,),
        in_specs=[rows(D_MODEL), rows(SSD_INNER), rows(ATT_INNER)] + [full(a) for a in ins[3:]],
        out_specs=(rows(D_MODEL), pl.BlockSpec((R * SUBLANES, LANES), lambda i: (i, 0)), rows(LANES),
                   pl.BlockSpec((SUBLANES, R), lambda i: (0, i)),
                   pl.BlockSpec((ROUTER_ROWS, 1), lambda i: (0, 0))),
        out_shape=(jax.ShapeDtypeStruct((T, D_MODEL), F32),
                   jax.ShapeDtypeStruct((T * SUBLANES, LANES), F32),
                   jax.ShapeDtypeStruct((T, LANES), F32), jax.ShapeDtypeStruct((SUBLANES, T), F32),
                   jax.ShapeDtypeStruct((ROUTER_ROWS, 1), F32)),
        scratch_shapes=[pltpu.VMEM((ROUTER_ROWS, 1), F32)],
        compiler_params=_params("arbitrary"), name="outproj_route")(*ins)


def _slots_body(routet_ref, pstart_ref, dest_ref):
    cols = routet_ref.shape[1]
    expert = lax.broadcasted_iota(jnp.int32, (MOE_EXPERTS, cols), 0).astype(F32)
    pstart = pstart_ref[...]
    for j in range(2):
        hit = expert == routet_ref[j:j + 1, :]
        base = jnp.sum(jnp.where(hit, pstart, 0.0), axis=0, keepdims=True)
        dest_ref[j:j + 1, :] = (base + routet_ref[4 + j:5 + j, :]).astype(jnp.int32)


def _slots(routet, pstarts):
    T = routet.shape[1]
    cols = min(T, 4096)
    assert T % cols == 0
    return pl.pallas_call(
        _slots_body, grid=(T // cols,),
        in_specs=[pl.BlockSpec((SUBLANES, cols), lambda i: (0, i)),
                  pl.BlockSpec((MOE_EXPERTS, 1), lambda i: (0, 0))],
        out_specs=pl.BlockSpec((2, cols), lambda i: (0, i)),
        out_shape=jax.ShapeDtypeStruct((2, T), jnp.int32),
        compiler_params=_params("arbitrary"), name="slots")(
            routet, pstarts.astype(F32).reshape(MOE_EXPERTS, 1))


def _tile_copy(src_ref, src_row, dst_ref, dst_row, sem):
    def tile(ref, row):
        if len(ref.shape) == 3:
            return ref.at[row]
        return ref.at[pl.ds(pl.multiple_of(row * SUBLANES, SUBLANES), SUBLANES)]
    return pltpu.make_async_copy(tile(src_ref, src_row), tile(dst_ref, dst_row), sem)


def _dispatch_body(pad_ref, d1_ref, d2_ref, xn_ref, xb_ref, zero_ref, stage_ref, sem, lsem, ssem):
    n_blocks = xb_ref.shape[0] // EXPERT_ROWS
    n_used = pad_ref[MOE_EXPERTS]

    def zero_copy(row):
        return pltpu.make_async_copy(zero_ref, xb_ref.at[pl.ds(row, EXPERT_ROWS)], sem)

    @pl.when(pl.program_id(0) == 0)
    def _():
        zero_ref[...] = jnp.zeros_like(zero_ref)
        for e in range(MOE_EXPERTS):
            zero_copy(pad_ref[e]).start()
        for e in range(MOE_EXPERTS):
            zero_copy(pad_ref[e]).wait()
        for b in range(n_blocks - MOE_EXPERTS - 1, n_blocks):
            @pl.when(b >= n_used)
            def _():
                zero_copy(b * EXPERT_ROWS).start()
        for b in range(n_blocks - MOE_EXPERTS - 1, n_blocks):
            @pl.when(b >= n_used)
            def _():
                zero_copy(b * EXPERT_ROWS).wait()

    i = pl.program_id(0)
    n = pl.num_programs(0)
    rows = MOVE_ROWS * SUBLANES

    def load(step, slot):
        src = xn_ref.at[pl.ds(pl.multiple_of(step * rows, rows), rows)]
        return pltpu.make_async_copy(src, stage_ref.at[slot], lsem.at[slot])

    def wait_scatters(slot):
        for _ in range(2):
            pltpu.make_async_copy(stage_ref.at[slot], stage_ref.at[slot], ssem.at[slot]).wait()

    @pl.when(i == 0)
    def _():
        load(0, 0).start()

    @pl.when(i >= 1)
    def _():
        wait_scatters((i - 1) & 1)

    @pl.when(i + 1 < n)
    def _():
        load(i + 1, (i + 1) & 1).start()

    slot = i & 1
    load(i, slot).wait()
    base = i * MOVE_ROWS

    def scatter(t, _):
        _tile_copy(stage_ref.at[slot], t, xb_ref, d1_ref[base + t], ssem.at[slot]).start()
        _tile_copy(stage_ref.at[slot], t, xb_ref, d2_ref[base + t], ssem.at[slot]).start(priority=1)
        return 0

    lax.fori_loop(0, MOVE_ROWS, scatter, 0)

    @pl.when(i == n - 1)
    def _():
        wait_scatters(slot)


def _dispatch(xn, dest1, dest2, pad_start, n_slots):
    T = dest1.shape[0]
    grid_spec = pltpu.PrefetchScalarGridSpec(
        num_scalar_prefetch=3, grid=(T // MOVE_ROWS,),
        in_specs=[pl.BlockSpec(memory_space=pl.ANY)],
        out_specs=pl.BlockSpec(memory_space=pl.ANY),
        scratch_shapes=[pltpu.VMEM((EXPERT_ROWS, SUBLANES, LANES), F32),
                        pltpu.VMEM((2, MOVE_ROWS * SUBLANES, LANES), F32),
                        pltpu.SemaphoreType.DMA(()), pltpu.SemaphoreType.DMA((2,)),
                        pltpu.SemaphoreType.DMA((2,))])
    xb = pl.pallas_call(
        _dispatch_body, grid_spec=grid_spec,
        out_shape=jax.ShapeDtypeStruct((n_slots + EXPERT_ROWS, SUBLANES, LANES), F32),
        compiler_params=_params("arbitrary"), name="dispatch")(pad_start, dest1, dest2, xn)
    return xb.reshape(-1, LANES)


def _expert_body(be_ref, nused_ref, xb_ref, wg_ref, wu_ref, wd_ref, yb_ref,
                 x_ref, wgb_ref, wub_ref, wdb_ref):
    i = pl.program_id(0)
    used = i < nused_ref[0]
    new_expert = (i == 0) | (be_ref[i] != be_ref[jnp.maximum(i - 1, 0)])

    @pl.when(used & new_expert)
    def _():
        wgb_ref[...] = wg_ref[0, 0].astype(BF16)
        wub_ref[...] = wu_ref[0, 0].astype(BF16)
        wdb_ref[...] = wd_ref[0, 0].astype(BF16)

    @pl.when(used)
    def _():
        for c in range(SUBLANES):
            x_ref[:, c * LANES:(c + 1) * LANES] = xb_ref[pl.ds(c, EXPERT_ROWS, stride=SUBLANES), :].astype(BF16)
        x = x_ref[...]
        gate = jnp.dot(x, wgb_ref[...], preferred_element_type=F32)
        up = jnp.dot(x, wub_ref[...], preferred_element_type=F32)
        hmid = (_silu(gate) * up).astype(BF16)
        y = jnp.dot(hmid, wdb_ref[...], preferred_element_type=F32)
        for c in range(SUBLANES):
            yb_ref[pl.ds(c, EXPERT_ROWS, stride=SUBLANES), :] = y[:, c * LANES:(c + 1) * LANES]

    @pl.when(jnp.logical_not(used))
    def _():
        yb_ref[...] = jnp.zeros_like(yb_ref)


def _experts(xb, blk_exp, n_used, n_blk, layer, w_gate, w_up, w_down):
    M = EXPERT_ROWS
    tile = lambda index: pl.BlockSpec((M * SUBLANES, LANES), index)
    weight = lambda a: pl.BlockSpec((1, 1) + a.shape[2:], lambda i, be, nu: (layer, be[i], 0, 0))
    grid_spec = pltpu.PrefetchScalarGridSpec(
        num_scalar_prefetch=2, grid=(n_blk,),
        in_specs=[tile(lambda i, be, nu: (jnp.minimum(i, nu[0] - 1), 0)),
                  weight(w_gate), weight(w_up), weight(w_down)],
        out_specs=tile(lambda i, be, nu: (i, 0)),
        scratch_shapes=[pltpu.VMEM((M, D_MODEL), BF16), pltpu.VMEM((D_MODEL, MOE_HIDDEN), BF16),
                        pltpu.VMEM((D_MODEL, MOE_HIDDEN), BF16), pltpu.VMEM((MOE_HIDDEN, D_MODEL), BF16)])
    return pl.pallas_call(
        _expert_body, grid_spec=grid_spec,
        out_shape=jax.ShapeDtypeStruct((n_blk * M * SUBLANES, LANES), F32),
        compiler_params=_params("arbitrary"), name="experts")(
            blk_exp, n_used, xb, w_gate, w_up, w_down)


def _combine_body(d1_ref, d2_ref, h_ref, route_ref, yb_ref, o_ref, buf_ref, sem):
    i = pl.program_id(0)
    n = pl.num_programs(0)

    def gather(step, slot):
        base = step * MOVE_ROWS

        def body(t, _):
            _tile_copy(yb_ref, d1_ref[base + t], buf_ref.at[slot, 0], t, sem.at[slot]).start()
            _tile_copy(yb_ref, d2_ref[base + t], buf_ref.at[slot, 1], t, sem.at[slot]).start(priority=1)
            return 0

        lax.fori_loop(0, MOVE_ROWS, body, 0)

    @pl.when(i == 0)
    def _():
        gather(0, 0)

    @pl.when(i + 1 < n)
    def _():
        gather(i + 1, (i + 1) & 1)

    slot = i & 1
    pltpu.make_async_copy(buf_ref.at[slot], buf_ref.at[slot], sem.at[slot]).wait()
    route = route_ref[...]
    g1 = route[:, 2:3]
    g2 = route[:, 3:4]
    for c in range(SUBLANES):
        cs = slice(c * LANES, (c + 1) * LANES)
        rows = pl.ds(c, MOVE_ROWS, stride=SUBLANES)
        o_ref[:, cs] = h_ref[:, cs] + (buf_ref[slot, 0, rows, :] * g1 + buf_ref[slot, 1, rows, :] * g2)


def _combine(h2, route, yb, dest1, dest2):
    T = h2.shape[0]
    R = MOVE_ROWS
    grid_spec = pltpu.PrefetchScalarGridSpec(
        num_scalar_prefetch=2, grid=(T // R,),
        in_specs=[pl.BlockSpec((R, D_MODEL), lambda i, d1, d2: (i, 0)),
                  pl.BlockSpec((R, LANES), lambda i, d1, d2: (i, 0)), pl.BlockSpec(memory_space=pl.ANY)],
        out_specs=pl.BlockSpec((R, D_MODEL), lambda i, d1, d2: (i, 0)),
        scratch_shapes=[pltpu.VMEM((2, 2, R * SUBLANES, LANES), F32), pltpu.SemaphoreType.DMA((2,))])
    return pl.pallas_call(
        _combine_body, grid_spec=grid_spec, out_shape=jax.ShapeDtypeStruct((T, D_MODEL), F32),
        compiler_params=_params("arbitrary"), name="combine")(dest1, dest2, h2, route, yb)


def _moe(h2, xn, route, routet, counts, layer, w_gate, w_up, w_down):
    T = h2.shape[0]
    M = EXPERT_ROWS
    n_blk = (2 * T) // M + MOE_EXPERTS
    cnt = counts[ROUTER_LANE0:ROUTER_LANE0 + MOE_EXPERTS, 0].astype(jnp.int32)
    padded = (cnt + M - 1) // M * M
    ends = jnp.cumsum(padded)
    pstarts = ends - padded
    n_used = (ends[-1] // M).astype(jnp.int32)
    blk = jnp.minimum(jnp.arange(n_blk, dtype=jnp.int32), n_used - 1) * M
    blk_exp = jnp.sum((ends[None, :] <= blk[:, None]).astype(jnp.int32), axis=1)
    blk_exp = jnp.minimum(blk_exp, MOE_EXPERTS - 1)
    dest = _slots(routet, pstarts)
    pad_info = jnp.concatenate([pstarts + cnt, n_used.reshape(1)])
    xb = _dispatch(xn, dest[0], dest[1], pad_info, n_blk * M)
    yb = _experts(xb, blk_exp, n_used.reshape(1), n_blk, layer, w_gate, w_up, w_down)
    return _combine(h2, route, yb.reshape(-1, SUBLANES, LANES), dest[0], dest[1])


def kernel(x, rel_bias, norm1_g, w_in, conv_w, conv_b, dt_bias, a_log, d_skip, ssd_norm_g,
           q_norm_g, k_norm_g, w_out, norm2_g, w_r1, b_r1, w_r2, b_r2, w_gate, w_up, w_down):
    B, S, D = x.shape
    T = B * S
    h2 = x.reshape(T, D)
    depth = w_in.shape[0]
    bias_tiles = _bias_tiles(rel_bias)
    for l in range(depth):
        z, xbc, dt, dtt, q, k, v = _inproj(h2, norm1_g[l], w_in[l])
        r3 = lambda a: a.reshape(B, S, a.shape[-1])
        y_ssd = _ssd(r3(z), r3(xbc), r3(dt), dtt, conv_w[l], conv_b[l], dt_bias[l], a_log[l],
                     d_skip[l], ssd_norm_g[l])
        y_att = _attention(r3(q), r3(k), r3(v), q_norm_g[l], k_norm_g[l], bias_tiles)
        h2, xn, route, routet, counts = _outproj_route(
            h2, y_ssd.reshape(T, SSD_INNER), y_att.reshape(T, ATT_INNER), w_out[l], norm2_g[l],
            w_r1[l], b_r1[l], w_r2[l], b_r2[l])
        h2 = _moe(h2, xn, route, routet, counts, l, w_gate, w_up, w_down)
    return h2.reshape(B, S, D)
```

```python
import math

import numpy as np
import jax
import jax.numpy as jnp
from jax import lax
from jax.experimental import pallas as pl
from jax.experimental.pallas import tpu as pltpu

F32 = jnp.float32
BF16 = jnp.bfloat16
HIGHEST = lax.Precision.HIGHEST

LANES = 128
SUBLANES = 8

D_MODEL = 1024
SSD_HEADS = 8
SSD_HEAD_DIM = 64
SSD_INNER = SSD_HEADS * SSD_HEAD_DIM
SSD_GROUPS = 2
SSD_HEADS_PER_GROUP = SSD_HEADS // SSD_GROUPS
SSD_STATE = 64
SSD_CONV = 4
SSD_CHUNK = 128
CONV_CH = SSD_INNER + 2 * SSD_GROUPS * SSD_STATE
ATT_HEADS = 8
ATT_HEAD_DIM = 64
ATT_INNER = ATT_HEADS * ATT_HEAD_DIM
MOBA_BLOCK = 256
MOBA_TOPK = 3
REL_BUCKETS = 32
REL_MAX_DIST = 128
MOE_GROUPS = 4
MOE_EXPERTS_PER_GROUP = 8
MOE_EXPERTS = MOE_GROUPS * MOE_EXPERTS_PER_GROUP
MOE_HIDDEN = 512
EPS = 1e-6

MASK_VALUE = -1e30
LOG2E = math.log2(math.e)
IN_ROWS = 1024
SSD_STEP_CHUNKS = 16
OUT_ROWS = 1024
EXPERT_ROWS = 512
MOVE_ROWS = 256
VMEM_LIMIT = 56 * 1024 * 1024
ROUTER_LANE0 = MOE_GROUPS
ROUTER_ROWS = 48


def _silu(x):
    return x * (1.0 / (1.0 + jnp.exp(-x)))


def _softplus(x):
    return jnp.maximum(x, 0.0) + jnp.log(1.0 + jnp.exp(-jnp.abs(x)))


def _params(*sem):
    return pltpu.CompilerParams(dimension_semantics=sem, vmem_limit_bytes=VMEM_LIMIT)


def _inproj_body(h_ref, g_ref, wz_ref, wx_ref, wdt_ref, wq_ref, wk_ref, wv_ref,
                 z_ref, xbc_ref, dt_ref, dtt_ref, q_ref, k_ref, v_ref):
    x = h_ref[...]
    xn = x * lax.rsqrt(jnp.mean(x * x, axis=-1, keepdims=True) + EPS) * g_ref[...]
    xb = xn.astype(BF16)
    z_ref[...] = jnp.dot(xb, wz_ref[...], preferred_element_type=F32)
    xbc_ref[...] = jnp.dot(xb, wx_ref[...], preferred_element_type=F32)
    dt = jnp.dot(xb, wdt_ref[...], preferred_element_type=F32)
    dt_ref[...] = dt
    dtt_ref[...] = dt.T[0:SSD_HEADS, :]
    q_ref[...] = jnp.dot(xb, wq_ref[...], preferred_element_type=F32)
    k_ref[...] = jnp.dot(xb, wk_ref[...], preferred_element_type=F32)
    v_ref[...] = jnp.dot(xb, wv_ref[...], preferred_element_type=F32)


def _inproj(h2, g, w_in):
    T = h2.shape[0]
    c0, c1, c2 = SSD_INNER, SSD_INNER + CONV_CH, SSD_INNER + CONV_CH + SSD_HEADS
    wb = w_in.astype(BF16)
    wz, wx, wdt = wb[:, :c0], wb[:, c0:c1], wb[:, c1:c2]
    wq, wk, wv = (wb[:, c2 + i * ATT_INNER:c2 + (i + 1) * ATT_INNER] for i in range(3))
    wdt_pad = jnp.pad(wdt, ((0, 0), (0, LANES - SSD_HEADS)))
    full = lambda a: pl.BlockSpec(a.shape, lambda i: (0,) * a.ndim)
    rows = lambda n: pl.BlockSpec((IN_ROWS, n), lambda i: (i, 0))
    ins = (h2, g.reshape(1, D_MODEL), wz, wx, wdt_pad, wq, wk, wv)
    out_shape = (
        jax.ShapeDtypeStruct((T, SSD_INNER), F32), jax.ShapeDtypeStruct((T, CONV_CH), F32),
        jax.ShapeDtypeStruct((T, LANES), F32), jax.ShapeDtypeStruct((SSD_HEADS, T), F32),
        jax.ShapeDtypeStruct((T, ATT_INNER), F32), jax.ShapeDtypeStruct((T, ATT_INNER), F32),
        jax.ShapeDtypeStruct((T, ATT_INNER), F32))
    out_specs = (rows(SSD_INNER), rows(CONV_CH), rows(LANES),
                 pl.BlockSpec((SSD_HEADS, IN_ROWS), lambda i: (0, i)),
                 rows(ATT_INNER), rows(ATT_INNER), rows(ATT_INNER))
    return pl.pallas_call(
        _inproj_body, grid=(T // IN_ROWS,),
        in_specs=[rows(D_MODEL)] + [full(a) for a in ins[1:]],
        out_specs=out_specs, out_shape=out_shape,
        compiler_params=_params("arbitrary"), name="inproj")(*ins)


def _ssd_body(z_ref, xbc_ref, dt_ref, dtt_ref, cw_ref, cb_ref, dtb_ref, dtbt_ref, alog_ref,
              alogt_ref, dskip_ref, ng_ref, tril_ref, expand_ref, y_ref, ext_ref, state_ref):
    L, P, N, E = SSD_CHUNK, SSD_HEAD_DIM, SSD_STATE, SSD_HEADS_PER_GROUP
    R = SSD_STEP_CHUNKS * L
    c = pl.program_id(1)

    @pl.when(c == 0)
    def _():
        state_ref[...] = jnp.zeros_like(state_ref)
        ext_ref[0:SUBLANES, :] = jnp.zeros((SUBLANES, CONV_CH), F32)

    ext_ref[SUBLANES:SUBLANES + R, :] = xbc_ref[0]
    tril = tril_ref[...]
    expand = expand_ref[...]
    row = lax.broadcasted_iota(jnp.int32, (L, L), 0)
    col = lax.broadcasted_iota(jnp.int32, (L, L), 1)
    causal = row >= col
    for sub in range(SSD_STEP_CHUNKS):
        _ssd_chunk(sub * L, z_ref, dt_ref, dtt_ref, cw_ref, cb_ref, dtb_ref, dtbt_ref, alog_ref,
                   alogt_ref, dskip_ref, ng_ref, tril, expand, causal, y_ref, ext_ref, state_ref)
    ext_ref[0:SUBLANES, :] = ext_ref[R:R + SUBLANES, :]


def _ssd_chunk(r0, z_ref, dt_ref, dtt_ref, cw_ref, cb_ref, dtb_ref, dtbt_ref, alog_ref, alogt_ref,
               dskip_ref, ng_ref, tril, expand, causal, y_ref, ext_ref, state_ref):
    L, P, N, E = SSD_CHUNK, SSD_HEAD_DIM, SSD_STATE, SSD_HEADS_PER_GROUP
    conv = cb_ref[...]
    for j in range(SSD_CONV):
        off = r0 + SUBLANES - (SSD_CONV - 1) + j
        conv = conv + cw_ref[j:j + 1, :] * ext_ref[off:off + L, :]
    act = _silu(conv)
    xs = act[:, :SSD_INNER]
    b_in = act[:, SSD_INNER:SSD_INNER + SSD_GROUPS * N]
    c_out = act[:, SSD_INNER + SSD_GROUPS * N:]

    dt = _softplus(dt_ref[0, r0:r0 + L, :] + dtb_ref[...])
    a_dt = dt * (-jnp.exp(alog_ref[...]))
    a_cs = jnp.dot(tril, a_dt, precision=HIGHEST, preferred_element_type=F32)
    dt_t = _softplus(dtt_ref[:, r0:r0 + L] + dtbt_ref[...])
    a_dt_t = dt_t * (-jnp.exp(alogt_ref[...]))
    a_cs_t = lax.dot_general(a_dt_t, tril, (((1,), (1,)), ((), ())), precision=HIGHEST,
                             preferred_element_type=F32)
    last = a_cs[L - 1:L, :]
    per_head = jnp.concatenate(
        [dt, jnp.exp(a_cs), jnp.exp(last - a_cs), jnp.broadcast_to(jnp.exp(last), (SUBLANES, LANES))],
        axis=0)
    hi = per_head.astype(BF16)
    lo = (per_head - hi.astype(F32)).astype(BF16)
    on_lanes = jnp.dot(jnp.concatenate([hi, lo], axis=-1), expand, preferred_element_type=F32)
    dt_x = on_lanes[0:L]
    eacs_x = on_lanes[L:2 * L]
    dec_x = on_lanes[2 * L:3 * L]
    cdec_x = on_lanes[3 * L:3 * L + 1]
    xdt = xs * dt_x
    xdtd = (xdt * dec_x).astype(BF16)
    xdt_b = xdt.astype(BF16)

    y_parts = []
    groups = []
    for g in range(SSD_GROUPS):
        bg = b_in[:, g * N:(g + 1) * N].astype(BF16)
        cg = c_out[:, g * N:(g + 1) * N].astype(BF16)
        cb = lax.dot_general(cg, bg, (((1,), (1,)), ((), ())), preferred_element_type=F32)
        gs = slice(g * E * P, (g + 1) * E * P)
        upd = lax.dot_general(bg, xdtd[:, gs], (((0,), (0,)), ((), ())),
                              preferred_element_type=F32)
        groups.append((cg, gs, upd))
        for e in range(E):
            h = g * E + e
            diff = a_cs[:, h:h + 1] - a_cs_t[h:h + 1, :]
            l_mat = jnp.exp(jnp.where(causal, diff, -jnp.inf))
            m = (cb * l_mat).astype(BF16)
            y_parts.append(jnp.dot(m, xdt_b[:, h * P:(h + 1) * P], preferred_element_type=F32))
    y_off = []
    for g, (cg, gs, upd) in enumerate(groups):
        st = state_ref[g]
        y_off.append(jnp.dot(cg, st.astype(BF16), preferred_element_type=F32))
        state_ref[g] = cdec_x[:, gs] * st + upd
    y = (jnp.concatenate(y_parts, axis=-1) + jnp.concatenate(y_off, axis=-1) * eacs_x
         + dskip_ref[...] * xs)
    y = y * _silu(z_ref[0, r0:r0 + L, :])
    y = y * lax.rsqrt(jnp.mean(y * y, axis=-1, keepdims=True) + EPS) * ng_ref[...]
    y_ref[0, r0:r0 + L, :] = y


def _ssd(z, xbc, dt, dtt, conv_w, conv_b, dt_bias, a_log, d_skip, norm_g):
    B, S, _ = z.shape
    L = SSD_CHUNK
    R = SSD_STEP_CHUNKS * L
    nc = S // R
    pad_row = lambda a: jnp.pad(a.reshape(1, SSD_HEADS), ((0, 0), (0, LANES - SSD_HEADS)))
    tril = jnp.asarray(np.tril(np.ones((L, L), np.float32)))
    expand_np = np.zeros((LANES, SSD_INNER), np.float32)
    for h in range(SSD_HEADS):
        expand_np[h, h * SSD_HEAD_DIM:(h + 1) * SSD_HEAD_DIM] = 1.0
    ins = (z, xbc, dt, dtt, conv_w, conv_b.reshape(1, CONV_CH), pad_row(dt_bias),
           dt_bias.reshape(SSD_HEADS, 1), pad_row(a_log), a_log.reshape(SSD_HEADS, 1),
           jnp.repeat(d_skip, SSD_HEAD_DIM).reshape(1, SSD_INNER), norm_g.reshape(1, SSD_INNER),
           tril, jnp.asarray(np.concatenate([expand_np, expand_np], axis=0), dtype=BF16))
    full = lambda a: pl.BlockSpec(a.shape, lambda b, c: (0,) * a.ndim)
    chunk = lambda n: pl.BlockSpec((1, R, n), lambda b, c: (b, c, 0))
    in_specs = [chunk(SSD_INNER), chunk(CONV_CH), chunk(LANES),
                pl.BlockSpec((SSD_HEADS, R), lambda b, c: (0, b * nc + c))]
    in_specs += [full(a) for a in ins[4:]]
    return pl.pallas_call(
        _ssd_body, grid=(B, nc), in_specs=in_specs, out_specs=chunk(SSD_INNER),
        out_shape=jax.ShapeDtypeStruct((B, S, SSD_INNER), F32),
        scratch_shapes=[pltpu.VMEM((R + SUBLANES, CONV_CH), F32),
                        pltpu.VMEM((SSD_GROUPS, SSD_STATE, SSD_HEADS_PER_GROUP * SSD_HEAD_DIM), F32)],
        compiler_params=_params("arbitrary", "arbitrary"), name="ssd")(*ins)


def _bucket_tile():
    bs = MOBA_BLOCK
    dist = np.arange(bs)[:, None] - np.arange(2 * bs)[None, :] + bs
    max_exact = REL_BUCKETS // 2
    d = np.maximum(dist, max_exact).astype(np.float32)
    large = max_exact + (np.log(d / np.float32(max_exact)) / np.float32(math.log(REL_MAX_DIST / max_exact))
                         * np.float32(REL_BUCKETS - max_exact)).astype(np.int32)
    large = np.minimum(large, REL_BUCKETS - 1)
    bucket = np.where(dist < max_exact, dist, large)
    return np.where(dist >= 0, bucket, -1).astype(np.int32)


def _bias_body(rb_ref, bucket_ref, o_ref):
    head = pl.program_id(0)
    bucket = bucket_ref[...]
    far = rb_ref[REL_BUCKETS - 1, head]
    bias = jnp.where(bucket < 0, MASK_VALUE, 0.0)
    for i in range(REL_BUCKETS - 1):
        bias = jnp.where(bucket == i, (rb_ref[i, head] - far) * LOG2E, bias)
    o_ref[0] = bias


def _bias_tiles(rel_bias):
    bs = MOBA_BLOCK
    bucket = jnp.asarray(_bucket_tile())
    return pl.pallas_call(
        _bias_body, grid=(ATT_HEADS,),
        in_specs=[pl.BlockSpec(memory_space=pltpu.SMEM), pl.BlockSpec((bs, 2 * bs), lambda h: (0, 0))],
        out_specs=pl.BlockSpec((1, bs, 2 * bs), lambda h: (h, 0, 0)),
        out_shape=jax.ShapeDtypeStruct((ATT_HEADS, bs, 2 * bs), F32),
        compiler_params=_params("arbitrary"), name="bias_tiles")(rel_bias, bucket)


def _attn_body(q_ref, k_ref, v_ref, qg_ref, kg_ref, bias_ref, o_ref,
               kaug_ref, vaug_ref, qaug_ref, sa_ref, sb_ref, m_ref, acc_ref):
    bs, hd = MOBA_BLOCK, ATT_HEAD_DIM
    nb = k_ref.shape[1] // bs
    jp = pl.program_id(2)
    nt_dims = (((1,), (1,)), ((), ()))

    @pl.when(jp == 0)
    def _():
        S = nb * bs
        pair = 2 * hd
        r = lax.broadcasted_iota(jnp.int32, (pair, pair), 0) // hd
        c = lax.broadcasted_iota(jnp.int32, (pair, pair), 1) // hd
        head_mean = jnp.where(r == c, 1.0 / hd, 0.0).astype(BF16)

        def head_norm(x, gain):
            ms = jnp.dot((x * x).astype(BF16), head_mean, preferred_element_type=F32)
            return x * lax.rsqrt(ms + EPS) * gain

        kn = head_norm(k_ref[0], kg_ref[...])
        kmean = jnp.mean(kn.reshape(nb, bs, pair), axis=1)
        kn_b = kn.astype(BF16)
        v_b = v_ref[0].astype(BF16)
        lane = lax.broadcasted_iota(jnp.int32, (S, pair), 1)
        blk = lax.broadcasted_iota(jnp.int32, (S, pair), 0) // bs
        lane_m = lax.broadcasted_iota(jnp.int32, (nb, pair), 1)
        rowb = lax.broadcasted_iota(jnp.int32, (nb, bs), 0)
        lane_q = lax.broadcasted_iota(jnp.int32, (bs, pair), 1)
        kmean_b = []
        for h in range(2):
            own = (lane // hd) == h
            other0 = (1 - h) * hd
            kaug_ref[h] = jnp.where(own, kn_b, (lane - other0 == blk).astype(BF16))
            vaug_ref[h] = jnp.where(own, v_b, jnp.ones((S, pair), BF16))
            kmean_b.append(jnp.where((lane_m // hd) == h, kmean, 0.0).astype(BF16))
        blocks = [slice(j * bs, (j + 1) * bs) for j in range(nb)]
        qn_all = [head_norm(q_ref[0, rows, :], qg_ref[...]) for rows in blocks]
        masks_all = []
        for j in range(nb):
            qn_b = qn_all[j].astype(BF16)
            masks = []
            for h in range(2):
                if j <= MOBA_TOPK:
                    sel = rowb <= j
                else:
                    gate = lax.dot_general(kmean_b[h], qn_b, nt_dims,
                                           preferred_element_type=F32)
                    beats = []
                    for m in range(j):
                        gm = gate[m:m + 1, :]
                        beats.append(((gm > gate) | ((gm == gate) & (rowb > m))).astype(jnp.int32))
                    while len(beats) > 1:
                        beats = [a + b for a, b in zip(beats[::2], beats[1::2])] + beats[len(beats) & ~1:]
                    sel = ((rowb < j) & (beats[0] < MOBA_TOPK)) | (rowb == j)
                masks.append(jnp.where(sel, 0.0, MASK_VALUE))
            masks_all.append(masks)
        gap = jnp.zeros((hd - nb, bs), F32)
        for j, rows in enumerate(blocks):
            mask_t = jnp.concatenate([masks_all[j][1], gap, masks_all[j][0], gap], axis=0).T
            qs = qn_all[j] * (ATT_HEAD_DIM ** -0.5 * LOG2E)
            for h in range(2):
                qaug_ref[h, rows, :] = jnp.where((lane_q // hd) == h, qs, mask_t).astype(BF16)

    qt = 2 * bs
    q0 = pl.multiple_of(jp * qt, qt)
    all_rows = slice(0, qt)
    late_rows = slice(bs, qt)
    for h in range(2):
        m_ref[h] = jnp.full((qt, 2 * hd), -jnp.inf, F32)
        acc_ref[h] = jnp.zeros((qt, 2 * hd), F32)

    def scores(n, s_ref, rows=all_rows):
        start = pl.multiple_of(n * bs, bs)
        q_rows = pl.ds(q0 + rows.start, rows.stop - rows.start)
        for h in range(2):
            s_ref[h, rows, :] = lax.dot_general(qaug_ref[h, q_rows, :], kaug_ref[h, pl.ds(start, bs), :],
                                                nt_dims, preferred_element_type=F32)

    def update(n, s_ref, bias, rows=all_rows):
        start = pl.multiple_of(n * bs, bs)
        for r0 in range(rows.start, rows.stop, bs):
            sub = slice(r0, r0 + bs)
            for h in range(2):
                s = s_ref[h, sub, :]
                tile = None if bias is None else bias(h, r0 // bs)
                if tile is not None:
                    s = s + tile
                m_i = m_ref[h, sub, :]
                m_new = jnp.maximum(m_i, jnp.max(s, axis=-1, keepdims=True))
                alpha = jnp.exp2(m_i - m_new)
                p = jnp.exp2((s - jnp.concatenate([m_new, m_new], axis=-1)).astype(BF16))
                pv = jnp.dot(p, vaug_ref[h, pl.ds(start, bs), :], preferred_element_type=F32)
                acc_ref[h, sub, :] = alpha * acc_ref[h, sub, :] + pv
                m_ref[h, sub, :] = m_new

    def far_pair(n):
        scores(n + 1, sb_ref)
        update(n, sa_ref, None)
        scores(n + 2, sa_ref)
        update(n + 1, sb_ref, None)

    def far_quad(i, _):
        far_pair(4 * i)
        far_pair(4 * i + 2)
        return 0

    n_far = jnp.maximum(2 * jp - 1, 0)
    n_loop = jnp.maximum(n_far - 3, 0)
    scores(0, sa_ref)
    lax.fori_loop(0, n_loop >> 2, far_quad, 0)

    @pl.when((n_loop & 2) != 0)
    def _():
        far_pair(n_loop & ~3)

    prev = lambda h: bias_ref[h, :, 0:bs]
    own = lambda h, qblk=1: bias_ref[h, :, bs:2 * bs]
    bias_p = lambda h, qblk: prev(h) if qblk == 0 else None
    bias_o = lambda h, qblk: own(h) if qblk == 0 else prev(h)
    blk_p = jnp.maximum(2 * jp - 1, 0)
    blk_o = 2 * jp
    blk_n = 2 * jp + 1

    def tail():
        scores(blk_p, sb_ref)
        update(n_far - 1, sa_ref, None)
        scores(blk_o, sa_ref)
        update(blk_p, sb_ref, bias_p)
        scores(blk_n, sb_ref, late_rows)
        update(blk_o, sa_ref, bias_o)
        update(blk_n, sb_ref, own, late_rows)

    @pl.when(jp >= 2)
    def _():
        far_pair(n_far - 3)
        tail()

    @pl.when(jp == 1)
    def _():
        tail()

    @pl.when(jp == 0)
    def _():
        scores(blk_n, sb_ref, late_rows)
        update(blk_o, sa_ref, bias_o)
        update(blk_n, sb_ref, own, late_rows)

    first_half = lax.broadcasted_iota(jnp.int32, (qt, 2 * hd), 1) < hd
    numer = jnp.where(first_half, acc_ref[0], acc_ref[1])
    denom = pltpu.roll(jnp.where(first_half, acc_ref[1], acc_ref[0]), hd, 1)
    o_ref[0] = numer / denom


def _attention(q, k, v, q_g, k_g, bias_tiles):
    B, S, _ = q.shape
    bs = MOBA_BLOCK
    nb = S // bs
    assert S % (2 * bs) == 0 and nb <= ATT_HEAD_DIM
    pair = 2 * ATT_HEAD_DIM
    return pl.pallas_call(
        _attn_body, grid=(B, ATT_HEADS // 2, nb // 2),
        in_specs=[
            pl.BlockSpec((1, S, pair), lambda b, hp, i: (b, 0, hp)),
            pl.BlockSpec((1, S, pair), lambda b, hp, i: (b, 0, hp)),
            pl.BlockSpec((1, S, pair), lambda b, hp, i: (b, 0, hp)),
            pl.BlockSpec((1, pair), lambda b, hp, i: (0, 0)),
            pl.BlockSpec((1, pair), lambda b, hp, i: (0, 0)),
            pl.BlockSpec((2, bs, 2 * bs), lambda b, hp, i: (hp, 0, 0)),
        ],
        out_specs=pl.BlockSpec((1, 2 * bs, pair), lambda b, hp, i: (b, i, hp)),
        out_shape=jax.ShapeDtypeStruct((B, S, ATT_INNER), F32),
        scratch_shapes=[pltpu.VMEM((2, S, pair), BF16), pltpu.VMEM((2, S, pair), BF16),
                        pltpu.VMEM((2, S, pair), BF16),
                        pltpu.VMEM((2, 2 * bs, bs), F32), pltpu.VMEM((2, 2 * bs, bs), F32),
                        pltpu.VMEM((2, 2 * bs, pair), F32),
                        pltpu.VMEM((2, 2 * bs, pair), F32)],
        compiler_params=_params("arbitrary", "arbitrary", "arbitrary"), name="moba")(
            q, k, v, jnp.tile(q_g, 2).reshape(1, pair), jnp.tile(k_g, 2).reshape(1, pair), bias_tiles)


def _outproj_body(h_ref, ys_ref, ya_ref, wa_ref, wb_ref, g_ref, wrh_ref, wrl_ref, br_ref, tri_ref,
                  hn_ref, xn_ref, route_ref, routet_ref, counts_ref, carry_ref):
    i = pl.program_id(0)

    @pl.when(i == 0)
    def _():
        carry_ref[...] = jnp.zeros_like(carry_ref)

    _outproj_rows(0, OUT_ROWS, h_ref, ys_ref, ya_ref, wa_ref, wb_ref, g_ref, wrh_ref, wrl_ref,
                  br_ref, tri_ref, hn_ref, xn_ref, route_ref, routet_ref, carry_ref)
    counts_ref[...] = carry_ref[...]


def _outproj_rows(r0, n, h_ref, ys_ref, ya_ref, wa_ref, wb_ref, g_ref, wrh_ref, wrl_ref, br_ref, tri_ref,
                  hn_ref, xn_ref, route_ref, routet_ref, carry_ref):
    rs = slice(r0, r0 + n)
    hn = (h_ref[rs, :]
          + jnp.dot(ys_ref[rs, :].astype(BF16), wa_ref[...], preferred_element_type=F32)
          + jnp.dot(ya_ref[rs, :].astype(BF16), wb_ref[...], preferred_element_type=F32))
    hn_ref[rs, :] = hn
    xn = hn * lax.rsqrt(jnp.mean(hn * hn, axis=-1, keepdims=True) + EPS) * g_ref[...]
    for c in range(SUBLANES):
        xn_ref[pl.ds(r0 * SUBLANES + c, n, stride=SUBLANES), :] = xn[:, c * LANES:(c + 1) * LANES]
    x_hi = xn.astype(BF16)
    x_lo = (xn - x_hi.astype(F32)).astype(BF16)
    hi_terms = jnp.dot(x_hi, wrl_ref[...], preferred_element_type=F32)
    logits = (hi_terms[:, :LANES] + hi_terms[:, LANES:]
              + jnp.dot(x_lo, wrh_ref[...], preferred_element_type=F32)) + br_ref[...]
    lt = logits.T[0:ROUTER_ROWS, :]
    row = lax.broadcasted_iota(jnp.int32, (ROUTER_ROWS, n), 0)
    first = lambda hit: jnp.min(jnp.where(hit, row, LANES), axis=0, keepdims=True)
    l1 = jnp.where(row < MOE_GROUPS, lt, -jnp.inf)
    mx = jnp.max(l1, axis=0, keepdims=True)
    gval = 1.0 / jnp.sum(jnp.exp(l1 - mx), axis=0, keepdims=True)
    gidx = first(l1 == mx)
    e_row = row - ROUTER_LANE0
    in_grp = (e_row >= 0) & (e_row < MOE_EXPERTS) & ((e_row >> 3) == gidx)
    l2 = jnp.where(in_grp, lt, -jnp.inf)
    m1 = jnp.max(l2, axis=0, keepdims=True)
    i1 = first(l2 == m1)
    l2b = jnp.where(row == i1, -jnp.inf, l2)
    m2 = jnp.max(l2b, axis=0, keepdims=True)
    i2 = first(l2b == m2)
    r = jnp.exp(m2 - m1)
    g1 = gval / (1.0 + r)
    g2 = gval * r / (1.0 + r)
    oh1 = (row == i1)
    oh2 = (row == i2)
    both = jnp.where(oh1 | oh2, 1.0, 0.0)
    cx = jnp.dot(both.astype(BF16), tri_ref[...], preferred_element_type=F32) + carry_ref[...]
    rank1 = jnp.sum(jnp.where(oh1, cx, 0.0), axis=0, keepdims=True)
    rank2 = jnp.sum(jnp.where(oh2, cx, 0.0), axis=0, keepdims=True)
    carry_ref[...] = carry_ref[...] + jnp.sum(both, axis=1, keepdims=True)
    routet = jnp.concatenate(
        [(i1 - ROUTER_LANE0).astype(F32), (i2 - ROUTER_LANE0).astype(F32), g1, g2, rank1, rank2,
         jnp.zeros((SUBLANES - 6, n), F32)], axis=0)
    routet_ref[:, rs] = routet
    route_ref[rs, :] = jnp.concatenate([routet, jnp.zeros((LANES - SUBLANES, n), F32)], axis=0).T


def _outproj_route(h2, y_ssd, y_att, w_out, norm_g, w_r1, b_r1, w_r2, b_r2):
    T = h2.shape[0]
    R = OUT_ROWS
    wb = w_out.astype(BF16)
    pad = LANES - MOE_GROUPS - MOE_EXPERTS
    wr = jnp.pad(jnp.concatenate([w_r1, w_r2], axis=1), ((0, 0), (0, pad)))
    wr_hi = wr.astype(BF16)
    wr_lo = (wr - wr_hi.astype(F32)).astype(BF16)
    br = jnp.pad(jnp.concatenate([b_r1, b_r2]), (0, pad)).reshape(1, LANES)
    tri = jnp.asarray(np.triu(np.ones((R, R), np.float32), 1), dtype=BF16)
    ins = (h2, y_ssd, y_att, wb[:SSD_INNER], wb[SSD_INNER:], norm_g.reshape(1, D_MODEL),
           wr_hi, jnp.concatenate([wr_hi, wr_lo], axis=1), br, tri)
    full = lambda a: pl.BlockSpec(a.shape, lambda i: (0,) * a.ndim)
    rows = lambda n: pl.BlockSpec((R, n), lambda i: (i, 0))
    return pl.pallas_call(
        _outproj_body, grid=(T // R,),
        in_specs=[rows(D_MODEL), rows(SSD_INNER), rows(ATT_INNER)] + [full(a) for a in ins[3:]],
        out_specs=(rows(D_MODEL), pl.BlockSpec((R * SUBLANES, LANES), lambda i: (i, 0)), rows(LANES),
                   pl.BlockSpec((SUBLANES, R), lambda i: (0, i)),
                   pl.BlockSpec((ROUTER_ROWS, 1), lambda i: (0, 0))),
        out_shape=(jax.ShapeDtypeStruct((T, D_MODEL), F32),
                   jax.ShapeDtypeStruct((T * SUBLANES, LANES), F32),
                   jax.ShapeDtypeStruct((T, LANES), F32), jax.ShapeDtypeStruct((SUBLANES, T), F32),
                   jax.ShapeDtypeStruct((ROUTER_ROWS, 1), F32)),
        scratch_shapes=[pltpu.VMEM((ROUTER_ROWS, 1), F32)],
        compiler_params=_params("arbitrary"), name="outproj_route")(*ins)


def _slots_body(routet_ref, pstart_ref, dest_ref):
    cols = routet_ref.shape[1]
    expert = lax.broadcasted_iota(jnp.int32, (MOE_EXPERTS, cols), 0).astype(F32)
    pstart = pstart_ref[...]
    for j in range(2):
        hit = expert == routet_ref[j:j + 1, :]
        base = jnp.sum(jnp.where(hit, pstart, 0.0), axis=0, keepdims=True)
        dest_ref[j:j + 1, :] = (base + routet_ref[4 + j:5 + j, :]).astype(jnp.int32)


def _slots(routet, pstarts):
    T = routet.shape[1]
    cols = min(T, 4096)
    assert T % cols == 0
    return pl.pallas_call(
        _slots_body, grid=(T // cols,),
        in_specs=[pl.BlockSpec((SUBLANES, cols), lambda i: (0, i)),
                  pl.BlockSpec((MOE_EXPERTS, 1), lambda i: (0, 0))],
        out_specs=pl.BlockSpec((2, cols), lambda i: (0, i)),
        out_shape=jax.ShapeDtypeStruct((2, T), jnp.int32),
        compiler_params=_params("arbitrary"), name="slots")(
            routet, pstarts.astype(F32).reshape(MOE_EXPERTS, 1))


def _tile_copy(src_ref, src_row, dst_ref, dst_row, sem):
    def tile(ref, row):
        if len(ref.shape) == 3:
            return ref.at[row]
        return ref.at[pl.ds(pl.multiple_of(row * SUBLANES, SUBLANES), SUBLANES)]
    return pltpu.make_async_copy(tile(src_ref, src_row), tile(dst_ref, dst_row), sem)


def _dispatch_body(pad_ref, d1_ref, d2_ref, xn_ref, xb_ref, zero_ref, stage_ref, sem, lsem, ssem):
    n_blocks = xb_ref.shape[0] // EXPERT_ROWS
    n_used = pad_ref[MOE_EXPERTS]

    def zero_copy(row):
        return pltpu.make_async_copy(zero_ref, xb_ref.at[pl.ds(row, EXPERT_ROWS)], sem)

    @pl.when(pl.program_id(0) == 0)
    def _():
        zero_ref[...] = jnp.zeros_like(zero_ref)
        for e in range(MOE_EXPERTS):
            zero_copy(pad_ref[e]).start()
        for e in range(MOE_EXPERTS):
            zero_copy(pad_ref[e]).wait()
        for b in range(n_blocks - MOE_EXPERTS - 1, n_blocks):
            @pl.when(b >= n_used)
            def _():
                zero_copy(b * EXPERT_ROWS).start()
        for b in range(n_blocks - MOE_EXPERTS - 1, n_blocks):
            @pl.when(b >= n_used)
            def _():
                zero_copy(b * EXPERT_ROWS).wait()

    i = pl.program_id(0)
    n = pl.num_programs(0)
    rows = MOVE_ROWS * SUBLANES

    def load(step, slot):
        src = xn_ref.at[pl.ds(pl.multiple_of(step * rows, rows), rows)]
        return pltpu.make_async_copy(src, stage_ref.at[slot], lsem.at[slot])

    def wait_scatters(slot):
        for _ in range(2):
            pltpu.make_async_copy(stage_ref.at[slot], stage_ref.at[slot], ssem.at[slot]).wait()

    @pl.when(i == 0)
    def _():
        load(0, 0).start()

    @pl.when(i >= 1)
    def _():
        wait_scatters((i - 1) & 1)

    @pl.when(i + 1 < n)
    def _():
        load(i + 1, (i + 1) & 1).start()

    slot = i & 1
    load(i, slot).wait()
    base = i * MOVE_ROWS

    def scatter(t, _):
        _tile_copy(stage_ref.at[slot], t, xb_ref, d1_ref[base + t], ssem.at[slot]).start()
        _tile_copy(stage_ref.at[slot], t, xb_ref, d2_ref[base + t], ssem.at[slot]).start(priority=1)
        return 0

    lax.fori_loop(0, MOVE_ROWS, scatter, 0)

    @pl.when(i == n - 1)
    def _():
        wait_scatters(slot)


def _dispatch(xn, dest1, dest2, pad_start, n_slots):
    T = dest1.shape[0]
    grid_spec = pltpu.PrefetchScalarGridSpec(
        num_scalar_prefetch=3, grid=(T // MOVE_ROWS,),
        in_specs=[pl.BlockSpec(memory_space=pl.ANY)],
        out_specs=pl.BlockSpec(memory_space=pl.ANY),
        scratch_shapes=[pltpu.VMEM((EXPERT_ROWS, SUBLANES, LANES), F32),
                        pltpu.VMEM((2, MOVE_ROWS * SUBLANES, LANES), F32),
                        pltpu.SemaphoreType.DMA(()), pltpu.SemaphoreType.DMA((2,)),
                        pltpu.SemaphoreType.DMA((2,))])
    xb = pl.pallas_call(
        _dispatch_body, grid_spec=grid_spec,
        out_shape=jax.ShapeDtypeStruct((n_slots + EXPERT_ROWS, SUBLANES, LANES), F32),
        compiler_params=_params("arbitrary"), name="dispatch")(pad_start, dest1, dest2, xn)
    return xb.reshape(-1, LANES)


def _expert_body(be_ref, nused_ref, xb_ref, wg_ref, wu_ref, wd_ref, yb_ref,
                 x_ref, wgb_ref, wub_ref, wdb_ref):
    i = pl.program_id(0)
    used = i < nused_ref[0]
    new_expert = (i == 0) | (be_ref[i] != be_ref[jnp.maximum(i - 1, 0)])

    @pl.when(used & new_expert)
    def _():
        wgb_ref[...] = wg_ref[0, 0].astype(BF16)
        wub_ref[...] = wu_ref[0, 0].astype(BF16)
        wdb_ref[...] = wd_ref[0, 0].astype(BF16)

    @pl.when(used)
    def _():
        for c in range(SUBLANES):
            x_ref[:, c * LANES:(c + 1) * LANES] = xb_ref[pl.ds(c, EXPERT_ROWS, stride=SUBLANES), :].astype(BF16)
        x = x_ref[...]
        gate = jnp.dot(x, wgb_ref[...], preferred_element_type=F32)
        up = jnp.dot(x, wub_ref[...], preferred_element_type=F32)
        hmid = (_silu(gate) * up).astype(BF16)
        y = jnp.dot(hmid, wdb_ref[...], preferred_element_type=F32)
        for c in range(SUBLANES):
            yb_ref[pl.ds(c, EXPERT_ROWS, stride=SUBLANES), :] = y[:, c * LANES:(c + 1) * LANES]

    @pl.when(jnp.logical_not(used))
    def _():
        yb_ref[...] = jnp.zeros_like(yb_ref)


def _experts(xb, blk_exp, n_used, n_blk, layer, w_gate, w_up, w_down):
    M = EXPERT_ROWS
    tile = lambda index: pl.BlockSpec((M * SUBLANES, LANES), index)
    weight = lambda a: pl.BlockSpec((1, 1) + a.shape[2:], lambda i, be, nu: (layer, be[i], 0, 0))
    grid_spec = pltpu.PrefetchScalarGridSpec(
        num_scalar_prefetch=2, grid=(n_blk,),
        in_specs=[tile(lambda i, be, nu: (jnp.minimum(i, nu[0] - 1), 0)),
                  weight(w_gate), weight(w_up), weight(w_down)],
        out_specs=tile(lambda i, be, nu: (i, 0)),
        scratch_shapes=[pltpu.VMEM((M, D_MODEL), BF16), pltpu.VMEM((D_MODEL, MOE_HIDDEN), BF16),
                        pltpu.VMEM((D_MODEL, MOE_HIDDEN), BF16), pltpu.VMEM((MOE_HIDDEN, D_MODEL), BF16)])
    return pl.pallas_call(
        _expert_body, grid_spec=grid_spec,
        out_shape=jax.ShapeDtypeStruct((n_blk * M * SUBLANES, LANES), F32),
        compiler_params=_params("arbitrary"), name="experts")(
            blk_exp, n_used, xb, w_gate, w_up, w_down)


def _combine_body(d1_ref, d2_ref, h_ref, route_ref, yb_ref, o_ref, buf_ref, sem):
    i = pl.program_id(0)
    n = pl.num_programs(0)

    def gather(step, slot):
        base = step * MOVE_ROWS

        def body(t, _):
            _tile_copy(yb_ref, d1_ref[base + t], buf_ref.at[slot, 0], t, sem.at[slot]).start()
            _tile_copy(yb_ref, d2_ref[base + t], buf_ref.at[slot, 1], t, sem.at[slot]).start(priority=1)
            return 0

        lax.fori_loop(0, MOVE_ROWS, body, 0)

    @pl.when(i == 0)
    def _():
        gather(0, 0)

    @pl.when(i + 1 < n)
    def _():
        gather(i + 1, (i + 1) & 1)

    slot = i & 1
    pltpu.make_async_copy(buf_ref.at[slot], buf_ref.at[slot], sem.at[slot]).wait()
    route = route_ref[...]
    g1 = route[:, 2:3]
    g2 = route[:, 3:4]
    for c in range(SUBLANES):
        cs = slice(c * LANES, (c + 1) * LANES)
        rows = pl.ds(c, MOVE_ROWS, stride=SUBLANES)
        o_ref[:, cs] = h_ref[:, cs] + (buf_ref[slot, 0, rows, :] * g1 + buf_ref[slot, 1, rows, :] * g2)


def _combine(h2, route, yb, dest1, dest2):
    T = h2.shape[0]
    R = MOVE_ROWS
    grid_spec = pltpu.PrefetchScalarGridSpec(
        num_scalar_prefetch=2, grid=(T // R,),
        in_specs=[pl.BlockSpec((R, D_MODEL), lambda i, d1, d2: (i, 0)),
                  pl.BlockSpec((R, LANES), lambda i, d1, d2: (i, 0)), pl.BlockSpec(memory_space=pl.ANY)],
        out_specs=pl.BlockSpec((R, D_MODEL), lambda i, d1, d2: (i, 0)),
        scratch_shapes=[pltpu.VMEM((2, 2, R * SUBLANES, LANES), F32), pltpu.SemaphoreType.DMA((2,))])
    return pl.pallas_call(
        _combine_body, grid_spec=grid_spec, out_shape=jax.ShapeDtypeStruct((T, D_MODEL), F32),
        compiler_params=_params("arbitrary"), name="combine")(dest1, dest2, h2, route, yb)


def _moe(h2, xn, route, routet, counts, layer, w_gate, w_up, w_down):
    T = h2.shape[0]
    M = EXPERT_ROWS
    n_blk = (2 * T) // M + MOE_EXPERTS
    cnt = counts[ROUTER_LANE0:ROUTER_LANE0 + MOE_EXPERTS, 0].astype(jnp.int32)
    padded = (cnt + M - 1) // M * M
    ends = jnp.cumsum(padded)
    pstarts = ends - padded
    n_used = (ends[-1] // M).astype(jnp.int32)
    blk = jnp.minimum(jnp.arange(n_blk, dtype=jnp.int32), n_used - 1) * M
    blk_exp = jnp.sum((ends[None, :] <= blk[:, None]).astype(jnp.int32), axis=1)
    blk_exp = jnp.minimum(blk_exp, MOE_EXPERTS - 1)
    dest = _slots(routet, pstarts)
    pad_info = jnp.concatenate([pstarts + cnt, n_used.reshape(1)])
    xb = _dispatch(xn, dest[0], dest[1], pad_info, n_blk * M)
    yb = _experts(xb, blk_exp, n_used.reshape(1), n_blk, layer, w_gate, w_up, w_down)
    return _combine(h2, route, yb.reshape(-1, SUBLANES, LANES), dest[0], dest[1])


def kernel(x, rel_bias, norm1_g, w_in, conv_w, conv_b, dt_bias, a_log, d_skip, ssd_norm_g,
           q_norm_g, k_norm_g, w_out, norm2_g, w_r1, b_r1, w_r2, b_r2, w_gate, w_up, w_down):
    B, S, D = x.shape
    T = B * S
    h2 = x.reshape(T, D)
    depth = w_in.shape[0]
    bias_tiles = _bias_tiles(rel_bias)
    for l in range(depth):
        z, xbc, dt, dtt, q, k, v = _inproj(h2, norm1_g[l], w_in[l])
        r3 = lambda a: a.reshape(B, S, a.shape[-1])
        y_ssd = _ssd(r3(z), r3(xbc), r3(dt), dtt, conv_w[l], conv_b[l], dt_bias[l], a_log[l],
                     d_skip[l], ssd_norm_g[l])
        y_att = _attention(r3(q), r3(k), r3(v), q_norm_g[l], k_norm_g[l], bias_tiles)
        h2, xn, route, routet, counts = _outproj_route(
            h2, y_ssd.reshape(T, SSD_INNER), y_att.reshape(T, ATT_INNER), w_out[l], norm2_g[l],
            w_r1[l], b_r1[l], w_r2[l], b_r2[l])
        h2 = _moe(h2, xn, route, routet, counts, l, w_gate, w_up, w_down)
    return h2.reshape(B, S, D)
```

```python
import math

import numpy as np
import jax
import jax.numpy as jnp
from jax import lax
from jax.experimental import pallas as pl
from jax.experimental.pallas import tpu as pltpu

F32 = jnp.float32
BF16 = jnp.bfloat16
HIGHEST = lax.Precision.HIGHEST

LANES = 128
SUBLANES = 8

D_MODEL = 1024
SSD_HEADS = 8
SSD_HEAD_DIM = 64
SSD_INNER = SSD_HEADS * SSD_HEAD_DIM
SSD_GROUPS = 2
SSD_HEADS_PER_GROUP = SSD_HEADS // SSD_GROUPS
SSD_STATE = 64
SSD_CONV = 4
SSD_CHUNK = 128
CONV_CH = SSD_INNER + 2 * SSD_GROUPS * SSD_STATE
ATT_HEADS = 8
ATT_HEAD_DIM = 64
ATT_INNER = ATT_HEADS * ATT_HEAD_DIM
MOBA_BLOCK = 256
MOBA_TOPK = 3
REL_BUCKETS = 32
REL_MAX_DIST = 128
MOE_GROUPS = 4
MOE_EXPERTS_PER_GROUP = 8
MOE_EXPERTS = MOE_GROUPS * MOE_EXPERTS_PER_GROUP
MOE_HIDDEN = 512
EPS = 1e-6

MASK_VALUE = -1e30
LOG2E = math.log2(math.e)
IN_ROWS = 1024
SSD_STEP_CHUNKS = 16
OUT_ROWS = 1024
EXPERT_ROWS = 512
MOVE_ROWS = 256
VMEM_LIMIT = 56 * 1024 * 1024
ROUTER_LANE0 = MOE_GROUPS
ROUTER_ROWS = 48


def _silu(x):
    return x * (1.0 / (1.0 + jnp.exp(-x)))


def _softplus(x):
    return jnp.maximum(x, 0.0) + jnp.log(1.0 + jnp.exp(-jnp.abs(x)))


def _params(*sem):
    return pltpu.CompilerParams(dimension_semantics=sem, vmem_limit_bytes=VMEM_LIMIT)


def _inproj_body(h_ref, g_ref, wz_ref, wx_ref, wdt_ref, wq_ref, wk_ref, wv_ref,
                 z_ref, xbc_ref, dt_ref, dtt_ref, q_ref, k_ref, v_ref):
    x = h_ref[...]
    xn = x * lax.rsqrt(jnp.mean(x * x, axis=-1, keepdims=True) + EPS) * g_ref[...]
    xb = xn.astype(BF16)
    z_ref[...] = jnp.dot(xb, wz_ref[...], preferred_element_type=F32)
    xbc_ref[...] = jnp.dot(xb, wx_ref[...], preferred_element_type=F32)
    dt = jnp.dot(xb, wdt_ref[...], preferred_element_type=F32)
    dt_ref[...] = dt
    dtt_ref[...] = dt.T[0:SSD_HEADS, :]
    q_ref[...] = jnp.dot(xb, wq_ref[...], preferred_element_type=F32)
    k_ref[...] = jnp.dot(xb, wk_ref[...], preferred_element_type=F32)
    v_ref[...] = jnp.dot(xb, wv_ref[...], preferred_element_type=F32)


def _inproj(h2, g, w_in):
    T = h2.shape[0]
    c0, c1, c2 = SSD_INNER, SSD_INNER + CONV_CH, SSD_INNER + CONV_CH + SSD_HEADS
    wb = w_in.astype(BF16)
    wz, wx, wdt = wb[:, :c0], wb[:, c0:c1], wb[:, c1:c2]
    wq, wk, wv = (wb[:, c2 + i * ATT_INNER:c2 + (i + 1) * ATT_INNER] for i in range(3))
    wdt_pad = jnp.pad(wdt, ((0, 0), (0, LANES - SSD_HEADS)))
    full = lambda a: pl.BlockSpec(a.shape, lambda i: (0,) * a.ndim)
    rows = lambda n: pl.BlockSpec((IN_ROWS, n), lambda i: (i, 0))
    ins = (h2, g.reshape(1, D_MODEL), wz, wx, wdt_pad, wq, wk, wv)
    out_shape = (
        jax.ShapeDtypeStruct((T, SSD_INNER), F32), jax.ShapeDtypeStruct((T, CONV_CH), F32),
        jax.ShapeDtypeStruct((T, LANES), F32), jax.ShapeDtypeStruct((SSD_HEADS, T), F32),
        jax.ShapeDtypeStruct((T, ATT_INNER), F32), jax.ShapeDtypeStruct((T, ATT_INNER), F32),
        jax.ShapeDtypeStruct((T, ATT_INNER), F32))
    out_specs = (rows(SSD_INNER), rows(CONV_CH), rows(LANES),
                 pl.BlockSpec((SSD_HEADS, IN_ROWS), lambda i: (0, i)),
                 rows(ATT_INNER), rows(ATT_INNER), rows(ATT_INNER))
    return pl.pallas_call(
        _inproj_body, grid=(T // IN_ROWS,),
        in_specs=[rows(D_MODEL)] + [full(a) for a in ins[1:]],
        out_specs=out_specs, out_shape=out_shape,
        compiler_params=_params("arbitrary"), name="inproj")(*ins)


def _ssd_body(z_ref, xbc_ref, dt_ref, dtt_ref, cw_ref, cb_ref, dtb_ref, dtbt_ref, alog_ref,
              alogt_ref, dskip_ref, ng_ref, tril_ref, expand_ref, y_ref, ext_ref, state_ref):
    L, P, N, E = SSD_CHUNK, SSD_HEAD_DIM, SSD_STATE, SSD_HEADS_PER_GROUP
    R = SSD_STEP_CHUNKS * L
    c = pl.program_id(1)

    @pl.when(c == 0)
    def _():
        state_ref[...] = jnp.zeros_like(state_ref)
        ext_ref[0:SUBLANES, :] = jnp.zeros((SUBLANES, CONV_CH), F32)

    ext_ref[SUBLANES:SUBLANES + R, :] = xbc_ref[0]
    tril = tril_ref[...]
    expand = expand_ref[...]
    row = lax.broadcasted_iota(jnp.int32, (L, L), 0)
    col = lax.broadcasted_iota(jnp.int32, (L, L), 1)
    causal = row >= col
    for sub in range(SSD_STEP_CHUNKS):
        _ssd_chunk(sub * L, z_ref, dt_ref, dtt_ref, cw_ref, cb_ref, dtb_ref, dtbt_ref, alog_ref,
                   alogt_ref, dskip_ref, ng_ref, tril, expand, causal, y_ref, ext_ref, state_ref)
    ext_ref[0:SUBLANES, :] = ext_ref[R:R + SUBLANES, :]


def _ssd_chunk(r0, z_ref, dt_ref, dtt_ref, cw_ref, cb_ref, dtb_ref, dtbt_ref, alog_ref, alogt_ref,
               dskip_ref, ng_ref, tril, expand, causal, y_ref, ext_ref, state_ref):
    L, P, N, E = SSD_CHUNK, SSD_HEAD_DIM, SSD_STATE, SSD_HEADS_PER_GROUP
    conv = cb_ref[...]
    for j in range(SSD_CONV):
        off = r0 + SUBLANES - (SSD_CONV - 1) + j
        conv = conv + cw_ref[j:j + 1, :] * ext_ref[off:off + L, :]
    act = _silu(conv)
    xs = act[:, :SSD_INNER]
    b_in = act[:, SSD_INNER:SSD_INNER + SSD_GROUPS * N]
    c_out = act[:, SSD_INNER + SSD_GROUPS * N:]

    dt = _softplus(dt_ref[0, r0:r0 + L, :] + dtb_ref[...])
    a_dt = dt * (-jnp.exp(alog_ref[...]))
    a_cs = jnp.dot(tril, a_dt, precision=HIGHEST, preferred_element_type=F32)
    dt_t = _softplus(dtt_ref[:, r0:r0 + L] + dtbt_ref[...])
    a_dt_t = dt_t * (-jnp.exp(alogt_ref[...]))
    a_cs_t = lax.dot_general(a_dt_t, tril, (((1,), (1,)), ((), ())), precision=HIGHEST,
                             preferred_element_type=F32)
    last = a_cs[L - 1:L, :]
    per_head = jnp.concatenate(
        [dt, jnp.exp(a_cs), jnp.exp(last - a_cs), jnp.broadcast_to(jnp.exp(last), (SUBLANES, LANES))],
        axis=0)
    hi = per_head.astype(BF16)
    lo = (per_head - hi.astype(F32)).astype(BF16)
    on_lanes = jnp.dot(jnp.concatenate([hi, lo], axis=-1), expand, preferred_element_type=F32)
    dt_x = on_lanes[0:L]
    eacs_x = on_lanes[L:2 * L]
    dec_x = on_lanes[2 * L:3 * L]
    cdec_x = on_lanes[3 * L:3 * L + 1]
    xdt = xs * dt_x
    xdtd = (xdt * dec_x).astype(BF16)
    xdt_b = xdt.astype(BF16)

    y_parts = []
    groups = []
    for g in range(SSD_GROUPS):
        bg = b_in[:, g * N:(g + 1) * N].astype(BF16)
        cg = c_out[:, g * N:(g + 1) * N].astype(BF16)
        cb = lax.dot_general(cg, bg, (((1,), (1,)), ((), ())), preferred_element_type=F32)
        gs = slice(g * E * P, (g + 1) * E * P)
        upd = lax.dot_general(bg, xdtd[:, gs], (((0,), (0,)), ((), ())),
                              preferred_element_type=F32)
        groups.append((cg, gs, upd))
        for e in range(E):
            h = g * E + e
            diff = a_cs[:, h:h + 1] - a_cs_t[h:h + 1, :]
            l_mat = jnp.exp(jnp.where(causal, diff, -jnp.inf))
            m = (cb * l_mat).astype(BF16)
            y_parts.append(jnp.dot(m, xdt_b[:, h * P:(h + 1) * P], preferred_element_type=F32))
    y_off = []
    for g, (cg, gs, upd) in enumerate(groups):
        st = state_ref[g]
        y_off.append(jnp.dot(cg, st.astype(BF16), preferred_element_type=F32))
        state_ref[g] = cdec_x[:, gs] * st + upd
    y = (jnp.concatenate(y_parts, axis=-1) + jnp.concatenate(y_off, axis=-1) * eacs_x
         + dskip_ref[...] * xs)
    y = y * _silu(z_ref[0, r0:r0 + L, :])
    y = y * lax.rsqrt(jnp.mean(y * y, axis=-1, keepdims=True) + EPS) * ng_ref[...]
    y_ref[0, r0:r0 + L, :] = y


def _ssd(z, xbc, dt, dtt, conv_w, conv_b, dt_bias, a_log, d_skip, norm_g):
    B, S, _ = z.shape
    L = SSD_CHUNK
    R = SSD_STEP_CHUNKS * L
    nc = S // R
    pad_row = lambda a: jnp.pad(a.reshape(1, SSD_HEADS), ((0, 0), (0, LANES - SSD_HEADS)))
    tril = jnp.asarray(np.tril(np.ones((L, L), np.float32)))
    expand_np = np.zeros((LANES, SSD_INNER), np.float32)
    for h in range(SSD_HEADS):
        expand_np[h, h * SSD_HEAD_DIM:(h + 1) * SSD_HEAD_DIM] = 1.0
    ins = (z, xbc, dt, dtt, conv_w, conv_b.reshape(1, CONV_CH), pad_row(dt_bias),
           dt_bias.reshape(SSD_HEADS, 1), pad_row(a_log), a_log.reshape(SSD_HEADS, 1),
           jnp.repeat(d_skip, SSD_HEAD_DIM).reshape(1, SSD_INNER), norm_g.reshape(1, SSD_INNER),
           tril, jnp.asarray(np.concatenate([expand_np, expand_np], axis=0), dtype=BF16))
    full = lambda a: pl.BlockSpec(a.shape, lambda b, c: (0,) * a.ndim)
    chunk = lambda n: pl.BlockSpec((1, R, n), lambda b, c: (b, c, 0))
    in_specs = [chunk(SSD_INNER), chunk(CONV_CH), chunk(LANES),
                pl.BlockSpec((SSD_HEADS, R), lambda b, c: (0, b * nc + c))]
    in_specs += [full(a) for a in ins[4:]]
    return pl.pallas_call(
        _ssd_body, grid=(B, nc), in_specs=in_specs, out_specs=chunk(SSD_INNER),
        out_shape=jax.ShapeDtypeStruct((B, S, SSD_INNER), F32),
        scratch_shapes=[pltpu.VMEM((R + SUBLANES, CONV_CH), F32),
                        pltpu.VMEM((SSD_GROUPS, SSD_STATE, SSD_HEADS_PER_GROUP * SSD_HEAD_DIM), F32)],
        compiler_params=_params("arbitrary", "arbitrary"), name="ssd")(*ins)


def _bucket_tile():
    bs = MOBA_BLOCK
    dist = np.arange(bs)[:, None] - np.arange(2 * bs)[None, :] + bs
    max_exact = REL_BUCKETS // 2
    d = np.maximum(dist, max_exact).astype(np.float32)
    large = max_exact + (np.log(d / np.float32(max_exact)) / np.float32(math.log(REL_MAX_DIST / max_exact))
                         * np.float32(REL_BUCKETS - max_exact)).astype(np.int32)
    large = np.minimum(large, REL_BUCKETS - 1)
    bucket = np.where(dist < max_exact, dist, large)
    return np.where(dist >= 0, bucket, -1).astype(np.int32)


def _bias_body(rb_ref, bucket_ref, o_ref):
    head = pl.program_id(0)
    bucket = bucket_ref[...]
    far = rb_ref[REL_BUCKETS - 1, head]
    bias = jnp.where(bucket < 0, MASK_VALUE, 0.0)
    for i in range(REL_BUCKETS - 1):
        bias = jnp.where(bucket == i, (rb_ref[i, head] - far) * LOG2E, bias)
    o_ref[0] = bias


def _bias_tiles(rel_bias):
    bs = MOBA_BLOCK
    bucket = jnp.asarray(_bucket_tile())
    return pl.pallas_call(
        _bias_body, grid=(ATT_HEADS,),
        in_specs=[pl.BlockSpec(memory_space=pltpu.SMEM), pl.BlockSpec((bs, 2 * bs), lambda h: (0, 0))],
        out_specs=pl.BlockSpec((1, bs, 2 * bs), lambda h: (h, 0, 0)),
        out_shape=jax.ShapeDtypeStruct((ATT_HEADS, bs, 2 * bs), F32),
        compiler_params=_params("arbitrary"), name="bias_tiles")(rel_bias, bucket)


def _attn_body(q_ref, k_ref, v_ref, qg_ref, kg_ref, bias_ref, o_ref,
               kaug_ref, vaug_ref, qaug_ref, sa_ref, sb_ref, m_ref, acc_ref):
    bs, hd = MOBA_BLOCK, ATT_HEAD_DIM
    nb = k_ref.shape[1] // bs
    jp = pl.program_id(2)
    nt_dims = (((1,), (1,)), ((), ()))

    @pl.when(jp == 0)
    def _():
        S = nb * bs
        pair = 2 * hd
        r = lax.broadcasted_iota(jnp.int32, (pair, pair), 0) // hd
        c = lax.broadcasted_iota(jnp.int32, (pair, pair), 1) // hd
        head_mean = jnp.where(r == c, 1.0 / hd, 0.0).astype(BF16)

        def head_norm(x, gain):
            ms = jnp.dot((x * x).astype(BF16), head_mean, preferred_element_type=F32)
            return x * lax.rsqrt(ms + EPS) * gain

        kn = head_norm(k_ref[0], kg_ref[...])
        kmean = jnp.mean(kn.reshape(nb, bs, pair), axis=1)
        kn_b = kn.astype(BF16)
        v_b = v_ref[0].astype(BF16)
        lane = lax.broadcasted_iota(jnp.int32, (S, pair), 1)
        blk = lax.broadcasted_iota(jnp.int32, (S, pair), 0) // bs
        lane_m = lax.broadcasted_iota(jnp.int32, (nb, pair), 1)
        rowb = lax.broadcasted_iota(jnp.int32, (nb, bs), 0)
        lane_q = lax.broadcasted_iota(jnp.int32, (bs, pair), 1)
        kmean_b = []
        for h in range(2):
            own = (lane // hd) == h
            other0 = (1 - h) * hd
            kaug_ref[h] = jnp.where(own, kn_b, (lane - other0 == blk).astype(BF16))
            vaug_ref[h] = jnp.where(own, v_b, jnp.ones((S, pair), BF16))
            kmean_b.append(jnp.where((lane_m // hd) == h, kmean, 0.0).astype(BF16))
        blocks = [slice(j * bs, (j + 1) * bs) for j in range(nb)]
        qn_all = [head_norm(q_ref[0, rows, :], qg_ref[...]) for rows in blocks]
        masks_all = []
        for j in range(nb):
            qn_b = qn_all[j].astype(BF16)
            masks = []
            for h in range(2):
                if j <= MOBA_TOPK:
                    sel = rowb <= j
                else:
                    gate = lax.dot_general(kmean_b[h], qn_b, nt_dims,
                                           preferred_element_type=F32)
                    beats = []
                    for m in range(j):
                        gm = gate[m:m + 1, :]
                        beats.append(((gm > gate) | ((gm == gate) & (rowb > m))).astype(jnp.int32))
                    while len(beats) > 1:
                        beats = [a + b for a, b in zip(beats[::2], beats[1::2])] + beats[len(beats) & ~1:]
                    sel = ((rowb < j) & (beats[0] < MOBA_TOPK)) | (rowb == j)
                masks.append(jnp.where(sel, 0.0, MASK_VALUE))
            masks_all.append(masks)
        gap = jnp.zeros((hd - nb, bs), F32)
        for j, rows in enumerate(blocks):
            mask_t = jnp.concatenate([masks_all[j][1], gap, masks_all[j][0], gap], axis=0).T
            qs = qn_all[j] * (ATT_HEAD_DIM ** -0.5 * LOG2E)
            for h in range(2):
                qaug_ref[h, rows, :] = jnp.where((lane_q // hd) == h, qs, mask_t).astype(BF16)

    qt = 2 * bs
    q0 = pl.multiple_of(jp * qt, qt)
    all_rows = slice(0, qt)
    late_rows = slice(bs, qt)
    for h in range(2):
        m_ref[h] = jnp.full((qt, 2 * hd), -jnp.inf, F32)
        acc_ref[h] = jnp.zeros((qt, 2 * hd), F32)

    def scores(n, s_ref, rows=all_rows):
        start = pl.multiple_of(n * bs, bs)
        q_rows = pl.ds(q0 + rows.start, rows.stop - rows.start)
        for h in range(2):
            s_ref[h, rows, :] = lax.dot_general(qaug_ref[h, q_rows, :], kaug_ref[h, pl.ds(start, bs), :],
                                                nt_dims, preferred_element_type=F32)

    def update(n, s_ref, bias, rows=all_rows):
        start = pl.multiple_of(n * bs, bs)
        for r0 in range(rows.start, rows.stop, bs):
            sub = slice(r0, r0 + bs)
            for h in range(2):
                s = s_ref[h, sub, :]
                tile = None if bias is None else bias(h, r0 // bs)
                if tile is not None:
                    s = s + tile
                m_i = m_ref[h, sub, :]
                m_new = jnp.maximum(m_i, jnp.max(s, axis=-1, keepdims=True))
                alpha = jnp.exp2(m_i - m_new)
                p = jnp.exp2((s - jnp.concatenate([m_new, m_new], axis=-1)).astype(BF16))
                pv = jnp.dot(p, vaug_ref[h, pl.ds(start, bs), :], preferred_element_type=F32)
                acc_ref[h, sub, :] = alpha * acc_ref[h, sub, :] + pv
                m_ref[h, sub, :] = m_new

    def far_pair(n):
        scores(n + 1, sb_ref)
        update(n, sa_ref, None)
        scores(n + 2, sa_ref)
        update(n + 1, sb_ref, None)

    def far_quad(i, _):
        far_pair(4 * i)
        far_pair(4 * i + 2)
        return 0

    n_far = jnp.maximum(2 * jp - 1, 0)
    n_loop = jnp.maximum(n_far - 3, 0)
    scores(0, sa_ref)
    lax.fori_loop(0, n_loop >> 2, far_quad, 0)

    @pl.when((n_loop & 2) != 0)
    def _():
        far_pair(n_loop & ~3)

    prev = lambda h: bias_ref[h, :, 0:bs]
    own = lambda h, qblk=1: bias_ref[h, :, bs:2 * bs]
    bias_p = lambda h, qblk: prev(h) if qblk == 0 else None
    bias_o = lambda h, qblk: own(h) if qblk == 0 else prev(h)
    blk_p = jnp.maximum(2 * jp - 1, 0)
    blk_o = 2 * jp
    blk_n = 2 * jp + 1

    def finish():
        first_half = lax.broadcasted_iota(jnp.int32, (qt, 2 * hd), 1) < hd
        numer = jnp.where(first_half, acc_ref[0], acc_ref[1])
        denom = pltpu.roll(jnp.where(first_half, acc_ref[1], acc_ref[0]), hd, 1)
        o_ref[0] = numer / denom

    def tail():
        scores(blk_p, sb_ref)
        update(n_far - 1, sa_ref, None)
        scores(blk_o, sa_ref)
        update(blk_p, sb_ref, bias_p)
        scores(blk_n, sb_ref, late_rows)
        update(blk_o, sa_ref, bias_o)
        update(blk_n, sb_ref, own, late_rows)

    @pl.when(jp >= 2)
    def _():
        far_pair(n_far - 3)
        tail()
        finish()

    @pl.when(jp == 1)
    def _():
        tail()
        finish()

    @pl.when(jp == 0)
    def _():
        scores(blk_n, sb_ref, late_rows)
        update(blk_o, sa_ref, bias_o)
        update(blk_n, sb_ref, own, late_rows)
        finish()


def _attention(q, k, v, q_g, k_g, bias_tiles):
    B, S, _ = q.shape
    bs = MOBA_BLOCK
    nb = S // bs
    assert S % (2 * bs) == 0 and nb <= ATT_HEAD_DIM
    pair = 2 * ATT_HEAD_DIM
    return pl.pallas_call(
        _attn_body, grid=(B, ATT_HEADS // 2, nb // 2),
        in_specs=[
            pl.BlockSpec((1, S, pair), lambda b, hp, i: (b, 0, hp)),
            pl.BlockSpec((1, S, pair), lambda b, hp, i: (b, 0, hp)),
            pl.BlockSpec((1, S, pair), lambda b, hp, i: (b, 0, hp)),
            pl.BlockSpec((1, pair), lambda b, hp, i: (0, 0)),
            pl.BlockSpec((1, pair), lambda b, hp, i: (0, 0)),
            pl.BlockSpec((2, bs, 2 * bs), lambda b, hp, i: (hp, 0, 0)),
        ],
        out_specs=pl.BlockSpec((1, 2 * bs, pair), lambda b, hp, i: (b, i, hp)),
        out_shape=jax.ShapeDtypeStruct((B, S, ATT_INNER), F32),
        scratch_shapes=[pltpu.VMEM((2, S, pair), BF16), pltpu.VMEM((2, S, pair), BF16),
                        pltpu.VMEM((2, S, pair), BF16),
                        pltpu.VMEM((2, 2 * bs, bs), F32), pltpu.VMEM((2, 2 * bs, bs), F32),
                        pltpu.VMEM((2, 2 * bs, pair), F32),
                        pltpu.VMEM((2, 2 * bs, pair), F32)],
        compiler_params=_params("arbitrary", "arbitrary", "arbitrary"), name="moba")(
            q, k, v, jnp.tile(q_g, 2).reshape(1, pair), jnp.tile(k_g, 2).reshape(1, pair), bias_tiles)


def _outproj_body(h_ref, ys_ref, ya_ref, wa_ref, wb_ref, g_ref, wrh_ref, wrl_ref, br_ref, tri_ref,
                  hn_ref, xn_ref, route_ref, routet_ref, counts_ref, carry_ref):
    i = pl.program_id(0)

    @pl.when(i == 0)
    def _():
        carry_ref[...] = jnp.zeros_like(carry_ref)

    _outproj_rows(0, OUT_ROWS, h_ref, ys_ref, ya_ref, wa_ref, wb_ref, g_ref, wrh_ref, wrl_ref,
                  br_ref, tri_ref, hn_ref, xn_ref, route_ref, routet_ref, carry_ref)
    counts_ref[...] = carry_ref[...]


def _outproj_rows(r0, n, h_ref, ys_ref, ya_ref, wa_ref, wb_ref, g_ref, wrh_ref, wrl_ref, br_ref, tri_ref,
                  hn_ref, xn_ref, route_ref, routet_ref, carry_ref):
    rs = slice(r0, r0 + n)
    hn = (h_ref[rs, :]
          + jnp.dot(ys_ref[rs, :].astype(BF16), wa_ref[...], preferred_element_type=F32)
          + jnp.dot(ya_ref[rs, :].astype(BF16), wb_ref[...], preferred_element_type=F32))
    hn_ref[rs, :] = hn
    xn = hn * lax.rsqrt(jnp.mean(hn * hn, axis=-1, keepdims=True) + EPS) * g_ref[...]
    for c in range(SUBLANES):
        xn_ref[pl.ds(r0 * SUBLANES + c, n, stride=SUBLANES), :] = xn[:, c * LANES:(c + 1) * LANES]
    x_hi = xn.astype(BF16)
    x_lo = (xn - x_hi.astype(F32)).astype(BF16)
    hi_terms = jnp.dot(x_hi, wrl_ref[...], preferred_element_type=F32)
    logits = (hi_terms[:, :LANES] + hi_terms[:, LANES:]
              + jnp.dot(x_lo, wrh_ref[...], preferred_element_type=F32)) + br_ref[...]
    lt = logits.T[0:ROUTER_ROWS, :]
    row = lax.broadcasted_iota(jnp.int32, (ROUTER_ROWS, n), 0)
    first = lambda hit: jnp.min(jnp.where(hit, row, LANES), axis=0, keepdims=True)
    l1 = jnp.where(row < MOE_GROUPS, lt, -jnp.inf)
    mx = jnp.max(l1, axis=0, keepdims=True)
    gval = 1.0 / jnp.sum(jnp.exp(l1 - mx), axis=0, keepdims=True)
    gidx = first(l1 == mx)
    e_row = row - ROUTER_LANE0
    in_grp = (e_row >= 0) & (e_row < MOE_EXPERTS) & ((e_row >> 3) == gidx)
    l2 = jnp.where(in_grp, lt, -jnp.inf)
    m1 = jnp.max(l2, axis=0, keepdims=True)
    i1 = first(l2 == m1)
    l2b = jnp.where(row == i1, -jnp.inf, l2)
    m2 = jnp.max(l2b, axis=0, keepdims=True)
    i2 = first(l2b == m2)
    r = jnp.exp(m2 - m1)
    g1 = gval / (1.0 + r)
    g2 = gval * r / (1.0 + r)
    oh1 = (row == i1)
    oh2 = (row == i2)
    both = jnp.where(oh1 | oh2, 1.0, 0.0)
    cx = jnp.dot(both.astype(BF16), tri_ref[...], preferred_element_type=F32) + carry_ref[...]
    rank1 = jnp.sum(jnp.where(oh1, cx, 0.0), axis=0, keepdims=True)
    rank2 = jnp.sum(jnp.where(oh2, cx, 0.0), axis=0, keepdims=True)
    carry_ref[...] = carry_ref[...] + jnp.sum(both, axis=1, keepdims=True)
    routet = jnp.concatenate(
        [(i1 - ROUTER_LANE0).astype(F32), (i2 - ROUTER_LANE0).astype(F32), g1, g2, rank1, rank2,
         jnp.zeros((SUBLANES - 6, n), F32)], axis=0)
    routet_ref[:, rs] = routet
    route_ref[rs, :] = jnp.concatenate([routet, jnp.zeros((LANES - SUBLANES, n), F32)], axis=0).T


def _outproj_route(h2, y_ssd, y_att, w_out, norm_g, w_r1, b_r1, w_r2, b_r2):
    T = h2.shape[0]
    R = OUT_ROWS
    wb = w_out.astype(BF16)
    pad = LANES - MOE_GROUPS - MOE_EXPERTS
    wr = jnp.pad(jnp.concatenate([w_r1, w_r2], axis=1), ((0, 0), (0, pad)))
    wr_hi = wr.astype(BF16)
    wr_lo = (wr - wr_hi.astype(F32)).astype(BF16)
    br = jnp.pad(jnp.concatenate([b_r1, b_r2]), (0, pad)).reshape(1, LANES)
    tri = jnp.asarray(np.triu(np.ones((R, R), np.float32), 1), dtype=BF16)
    ins = (h2, y_ssd, y_att, wb[:SSD_INNER], wb[SSD_INNER:], norm_g.reshape(1, D_MODEL),
           wr_hi, jnp.concatenate([wr_hi, wr_lo], axis=1), br, tri)
    full = lambda a: pl.BlockSpec(a.shape, lambda i: (0,) * a.ndim)
    rows = lambda n: pl.BlockSpec((R, n), lambda i: (i, 0))
    return pl.pallas_call(
        _outproj_body, grid=(T // R,),
        in_specs=[rows(D_MODEL), rows(SSD_INNER), rows(ATT_INNER)] + [full(a) for a in ins[3:]],
        out_specs=(rows(D_MODEL), pl.BlockSpec((R * SUBLANES, LANES), lambda i: (i, 0)), rows(LANES),
                   pl.BlockSpec((SUBLANES, R), lambda i: (0, i)),
                   pl.BlockSpec((ROUTER_ROWS, 1), lambda i: (0, 0))),
        out_shape=(jax.ShapeDtypeStruct((T, D_MODEL), F32),
                   jax.ShapeDtypeStruct((T * SUBLANES, LANES), F32),
                   jax.ShapeDtypeStruct((T, LANES), F32), jax.ShapeDtypeStruct((SUBLANES, T), F32),
                   jax.ShapeDtypeStruct((ROUTER_ROWS, 1), F32)),
        scratch_shapes=[pltpu.VMEM((ROUTER_ROWS, 1), F32)],
        compiler_params=_params("arbitrary"), name="outproj_route")(*ins)


def _slots_body(routet_ref, pstart_ref, dest_ref):
    cols = routet_ref.shape[1]
    expert = lax.broadcasted_iota(jnp.int32, (MOE_EXPERTS, cols), 0).astype(F32)
    pstart = pstart_ref[...]
    for j in range(2):
        hit = expert == routet_ref[j:j + 1, :]
        base = jnp.sum(jnp.where(hit, pstart, 0.0), axis=0, keepdims=True)
        dest_ref[j:j + 1, :] = (base + routet_ref[4 + j:5 + j, :]).astype(jnp.int32)


def _slots(routet, pstarts):
    T = routet.shape[1]
    cols = min(T, 4096)
    assert T % cols == 0
    return pl.pallas_call(
        _slots_body, grid=(T // cols,),
        in_specs=[pl.BlockSpec((SUBLANES, cols), lambda i: (0, i)),
                  pl.BlockSpec((MOE_EXPERTS, 1), lambda i: (0, 0))],
        out_specs=pl.BlockSpec((2, cols), lambda i: (0, i)),
        out_shape=jax.ShapeDtypeStruct((2, T), jnp.int32),
        compiler_params=_params("arbitrary"), name="slots")(
            routet, pstarts.astype(F32).reshape(MOE_EXPERTS, 1))


def _tile_copy(src_ref, src_row, dst_ref, dst_row, sem):
    def tile(ref, row):
        if len(ref.shape) == 3:
            return ref.at[row]
        return ref.at[pl.ds(pl.multiple_of(row * SUBLANES, SUBLANES), SUBLANES)]
    return pltpu.make_async_copy(tile(src_ref, src_row), tile(dst_ref, dst_row), sem)


def _dispatch_body(pad_ref, d1_ref, d2_ref, xn_ref, xb_ref, zero_ref, stage_ref, sem, lsem, ssem):
    n_blocks = xb_ref.shape[0] // EXPERT_ROWS
    n_used = pad_ref[MOE_EXPERTS]

    def zero_copy(row):
        return pltpu.make_async_copy(zero_ref, xb_ref.at[pl.ds(row, EXPERT_ROWS)], sem)

    @pl.when(pl.program_id(0) == 0)
    def _():
        zero_ref[...] = jnp.zeros_like(zero_ref)
        for e in range(MOE_EXPERTS):
            zero_copy(pad_ref[e]).start()
        for e in range(MOE_EXPERTS):
            zero_copy(pad_ref[e]).wait()
        for b in range(n_blocks - MOE_EXPERTS - 1, n_blocks):
            @pl.when(b >= n_used)
            def _():
                zero_copy(b * EXPERT_ROWS).start()
        for b in range(n_blocks - MOE_EXPERTS - 1, n_blocks):
            @pl.when(b >= n_used)
            def _():
                zero_copy(b * EXPERT_ROWS).wait()

    i = pl.program_id(0)
    n = pl.num_programs(0)
    rows = MOVE_ROWS * SUBLANES

    def load(step, slot):
        src = xn_ref.at[pl.ds(pl.multiple_of(step * rows, rows), rows)]
        return pltpu.make_async_copy(src, stage_ref.at[slot], lsem.at[slot])

    def wait_scatters(slot):
        for _ in range(2):
            pltpu.make_async_copy(stage_ref.at[slot], stage_ref.at[slot], ssem.at[slot]).wait()

    @pl.when(i == 0)
    def _():
        load(0, 0).start()

    @pl.when(i >= 1)
    def _():
        wait_scatters((i - 1) & 1)

    @pl.when(i + 1 < n)
    def _():
        load(i + 1, (i + 1) & 1).start()

    slot = i & 1
    load(i, slot).wait()
    base = i * MOVE_ROWS

    def scatter(t, _):
        _tile_copy(stage_ref.at[slot], t, xb_ref, d1_ref[base + t], ssem.at[slot]).start()
        _tile_copy(stage_ref.at[slot], t, xb_ref, d2_ref[base + t], ssem.at[slot]).start(priority=1)
        return 0

    lax.fori_loop(0, MOVE_ROWS, scatter, 0)

    @pl.when(i == n - 1)
    def _():
        wait_scatters(slot)


def _dispatch(xn, dest1, dest2, pad_start, n_slots):
    T = dest1.shape[0]
    grid_spec = pltpu.PrefetchScalarGridSpec(
        num_scalar_prefetch=3, grid=(T // MOVE_ROWS,),
        in_specs=[pl.BlockSpec(memory_space=pl.ANY)],
        out_specs=pl.BlockSpec(memory_space=pl.ANY),
        scratch_shapes=[pltpu.VMEM((EXPERT_ROWS, SUBLANES, LANES), F32),
                        pltpu.VMEM((2, MOVE_ROWS * SUBLANES, LANES), F32),
                        pltpu.SemaphoreType.DMA(()), pltpu.SemaphoreType.DMA((2,)),
                        pltpu.SemaphoreType.DMA((2,))])
    xb = pl.pallas_call(
        _dispatch_body, grid_spec=grid_spec,
        out_shape=jax.ShapeDtypeStruct((n_slots + EXPERT_ROWS, SUBLANES, LANES), F32),
        compiler_params=_params("arbitrary"), name="dispatch")(pad_start, dest1, dest2, xn)
    return xb.reshape(-1, LANES)


def _expert_body(be_ref, nused_ref, xb_ref, wg_ref, wu_ref, wd_ref, yb_ref,
                 x_ref, wgb_ref, wub_ref, wdb_ref):
    i = pl.program_id(0)
    used = i < nused_ref[0]
    new_expert = (i == 0) | (be_ref[i] != be_ref[jnp.maximum(i - 1, 0)])

    @pl.when(used & new_expert)
    def _():
        wgb_ref[...] = wg_ref[0, 0].astype(BF16)
        wub_ref[...] = wu_ref[0, 0].astype(BF16)
        wdb_ref[...] = wd_ref[0, 0].astype(BF16)

    @pl.when(used)
    def _():
        for c in range(SUBLANES):
            x_ref[:, c * LANES:(c + 1) * LANES] = xb_ref[pl.ds(c, EXPERT_ROWS, stride=SUBLANES), :].astype(BF16)
        x = x_ref[...]
        gate = jnp.dot(x, wgb_ref[...], preferred_element_type=F32)
        up = jnp.dot(x, wub_ref[...], preferred_element_type=F32)
        hmid = (_silu(gate) * up).astype(BF16)
        y = jnp.dot(hmid, wdb_ref[...], preferred_element_type=F32)
        for c in range(SUBLANES):
            yb_ref[pl.ds(c, EXPERT_ROWS, stride=SUBLANES), :] = y[:, c * LANES:(c + 1) * LANES]

    @pl.when(jnp.logical_not(used))
    def _():
        yb_ref[...] = jnp.zeros_like(yb_ref)


def _experts(xb, blk_exp, n_used, n_blk, layer, w_gate, w_up, w_down):
    M = EXPERT_ROWS
    tile = lambda index: pl.BlockSpec((M * SUBLANES, LANES), index)
    weight = lambda a: pl.BlockSpec((1, 1) + a.shape[2:], lambda i, be, nu: (layer, be[i], 0, 0))
    grid_spec = pltpu.PrefetchScalarGridSpec(
        num_scalar_prefetch=2, grid=(n_blk,),
        in_specs=[tile(lambda i, be, nu: (jnp.minimum(i, nu[0] - 1), 0)),
                  weight(w_gate), weight(w_up), weight(w_down)],
        out_specs=tile(lambda i, be, nu: (i, 0)),
        scratch_shapes=[pltpu.VMEM((M, D_MODEL), BF16), pltpu.VMEM((D_MODEL, MOE_HIDDEN), BF16),
                        pltpu.VMEM((D_MODEL, MOE_HIDDEN), BF16), pltpu.VMEM((MOE_HIDDEN, D_MODEL), BF16)])
    return pl.pallas_call(
        _expert_body, grid_spec=grid_spec,
        out_shape=jax.ShapeDtypeStruct((n_blk * M * SUBLANES, LANES), F32),
        compiler_params=_params("arbitrary"), name="experts")(
            blk_exp, n_used, xb, w_gate, w_up, w_down)


def _combine_body(d1_ref, d2_ref, h_ref, route_ref, yb_ref, o_ref, buf_ref, sem):
    i = pl.program_id(0)
    n = pl.num_programs(0)

    def gather(step, slot):
        base = step * MOVE_ROWS

        def body(t, _):
            _tile_copy(yb_ref, d1_ref[base + t], buf_ref.at[slot, 0], t, sem.at[slot]).start()
            _tile_copy(yb_ref, d2_ref[base + t], buf_ref.at[slot, 1], t, sem.at[slot]).start(priority=1)
            return 0

        lax.fori_loop(0, MOVE_ROWS, body, 0)

    @pl.when(i == 0)
    def _():
        gather(0, 0)

    @pl.when(i + 1 < n)
    def _():
        gather(i + 1, (i + 1) & 1)

    slot = i & 1
    pltpu.make_async_copy(buf_ref.at[slot], buf_ref.at[slot], sem.at[slot]).wait()
    route = route_ref[...]
    g1 = route[:, 2:3]
    g2 = route[:, 3:4]
    for c in range(SUBLANES):
        cs = slice(c * LANES, (c + 1) * LANES)
        rows = pl.ds(c, MOVE_ROWS, stride=SUBLANES)
        o_ref[:, cs] = h_ref[:, cs] + (buf_ref[slot, 0, rows, :] * g1 + buf_ref[slot, 1, rows, :] * g2)


def _combine(h2, route, yb, dest1, dest2):
    T = h2.shape[0]
    R = MOVE_ROWS
    grid_spec = pltpu.PrefetchScalarGridSpec(
        num_scalar_prefetch=2, grid=(T // R,),
        in_specs=[pl.BlockSpec((R, D_MODEL), lambda i, d1, d2: (i, 0)),
                  pl.BlockSpec((R, LANES), lambda i, d1, d2: (i, 0)), pl.BlockSpec(memory_space=pl.ANY)],
        out_specs=pl.BlockSpec((R, D_MODEL), lambda i, d1, d2: (i, 0)),
        scratch_shapes=[pltpu.VMEM((2, 2, R * SUBLANES, LANES), F32), pltpu.SemaphoreType.DMA((2,))])
    return pl.pallas_call(
        _combine_body, grid_spec=grid_spec, out_shape=jax.ShapeDtypeStruct((T, D_MODEL), F32),
        compiler_params=_params("arbitrary"), name="combine")(dest1, dest2, h2, route, yb)


def _moe(h2, xn, route, routet, counts, layer, w_gate, w_up, w_down):
    T = h2.shape[0]
    M = EXPERT_ROWS
    n_blk = (2 * T) // M + MOE_EXPERTS
    cnt = counts[ROUTER_LANE0:ROUTER_LANE0 + MOE_EXPERTS, 0].astype(jnp.int32)
    padded = (cnt + M - 1) // M * M
    ends = jnp.cumsum(padded)
    pstarts = ends - padded
    n_used = (ends[-1] // M).astype(jnp.int32)
    blk = jnp.minimum(jnp.arange(n_blk, dtype=jnp.int32), n_used - 1) * M
    blk_exp = jnp.sum((ends[None, :] <= blk[:, None]).astype(jnp.int32), axis=1)
    blk_exp = jnp.minimum(blk_exp, MOE_EXPERTS - 1)
    dest = _slots(routet, pstarts)
    pad_info = jnp.concatenate([pstarts + cnt, n_used.reshape(1)])
    xb = _dispatch(xn, dest[0], dest[1], pad_info, n_blk * M)
    yb = _experts(xb, blk_exp, n_used.reshape(1), n_blk, layer, w_gate, w_up, w_down)
    return _combine(h2, route, yb.reshape(-1, SUBLANES, LANES), dest[0], dest[1])


def kernel(x, rel_bias, norm1_g, w_in, conv_w, conv_b, dt_bias, a_log, d_skip, ssd_norm_g,
           q_norm_g, k_norm_g, w_out, norm2_g, w_r1, b_r1, w_r2, b_r2, w_gate, w_up, w_down):
    B, S, D = x.shape
    T = B * S
    h2 = x.reshape(T, D)
    depth = w_in.shape[0]
    bias_tiles = _bias_tiles(rel_bias)
    for l in range(depth):
        z, xbc, dt, dtt, q, k, v = _inproj(h2, norm1_g[l], w_in[l])
        r3 = lambda a: a.reshape(B, S, a.shape[-1])
        y_ssd = _ssd(r3(z), r3(xbc), r3(dt), dtt, conv_w[l], conv_b[l], dt_bias[l], a_log[l],
                     d_skip[l], ssd_norm_g[l])
        y_att = _attention(r3(q), r3(k), r3(v), q_norm_g[l], k_norm_g[l], bias_tiles)
        h2, xn, route, routet, counts = _outproj_route(
            h2, y_ssd.reshape(T, SSD_INNER), y_att.reshape(T, ATT_INNER), w_out[l], norm2_g[l],
            w_r1[l], b_r1[l], w_r2[l], b_r2[l])
        h2 = _moe(h2, xn, route, routet, counts, l, w_gate, w_up, w_down)
    return h2.reshape(B, S, D)
```

```python
import math

import numpy as np
import jax
import jax.numpy as jnp
from jax import lax
from jax.experimental import pallas as pl
from jax.experimental.pallas import tpu as pltpu

F32 = jnp.float32
BF16 = jnp.bfloat16
HIGHEST = lax.Precision.HIGHEST

LANES = 128
SUBLANES = 8

D_MODEL = 1024
SSD_HEADS = 8
SSD_HEAD_DIM = 64
SSD_INNER = SSD_HEADS * SSD_HEAD_DIM
SSD_GROUPS = 2
SSD_HEADS_PER_GROUP = SSD_HEADS // SSD_GROUPS
SSD_STATE = 64
SSD_CONV = 4
SSD_CHUNK = 128
CONV_CH = SSD_INNER + 2 * SSD_GROUPS * SSD_STATE
ATT_HEADS = 8
ATT_HEAD_DIM = 64
ATT_INNER = ATT_HEADS * ATT_HEAD_DIM
MOBA_BLOCK = 256
MOBA_TOPK = 3
REL_BUCKETS = 32
REL_MAX_DIST = 128
MOE_GROUPS = 4
MOE_EXPERTS_PER_GROUP = 8
MOE_EXPERTS = MOE_GROUPS * MOE_EXPERTS_PER_GROUP
MOE_HIDDEN = 512
EPS = 1e-6

MASK_VALUE = -1e30
LOG2E = math.log2(math.e)
IN_ROWS = 1024
SSD_STEP_CHUNKS = 16
OUT_ROWS = 1024
EXPERT_ROWS = 512
MOVE_ROWS = 256
VMEM_LIMIT = 56 * 1024 * 1024
ROUTER_LANE0 = MOE_GROUPS
ROUTER_ROWS = 48


def _silu(x):
    return x * (1.0 / (1.0 + jnp.exp(-x)))


def _softplus(x):
    return jnp.maximum(x, 0.0) + jnp.log(1.0 + jnp.exp(-jnp.abs(x)))


def _params(*sem):
    return pltpu.CompilerParams(dimension_semantics=sem, vmem_limit_bytes=VMEM_LIMIT)


def _inproj_body(h_ref, g_ref, wz_ref, wx_ref, wdt_ref, wq_ref, wk_ref, wv_ref,
                 z_ref, xbc_ref, dt_ref, dtt_ref, q_ref, k_ref, v_ref):
    x = h_ref[...]
    xn = x * lax.rsqrt(jnp.mean(x * x, axis=-1, keepdims=True) + EPS) * g_ref[...]
    xb = xn.astype(BF16)
    z_ref[...] = jnp.dot(xb, wz_ref[...], preferred_element_type=F32)
    xbc_ref[...] = jnp.dot(xb, wx_ref[...], preferred_element_type=F32)
    dt = jnp.dot(xb, wdt_ref[...], preferred_element_type=F32)
    dt_ref[...] = dt
    dtt_ref[...] = dt.T[0:SSD_HEADS, :]
    q_ref[...] = jnp.dot(xb, wq_ref[...], preferred_element_type=F32)
    k_ref[...] = jnp.dot(xb, wk_ref[...], preferred_element_type=F32)
    v_ref[...] = jnp.dot(xb, wv_ref[...], preferred_element_type=F32)


def _inproj(h2, g, w_in):
    T = h2.shape[0]
    c0, c1, c2 = SSD_INNER, SSD_INNER + CONV_CH, SSD_INNER + CONV_CH + SSD_HEADS
    wb = w_in.astype(BF16)
    wz, wx, wdt = wb[:, :c0], wb[:, c0:c1], wb[:, c1:c2]
    wq, wk, wv = (wb[:, c2 + i * ATT_INNER:c2 + (i + 1) * ATT_INNER] for i in range(3))
    wdt_pad = jnp.pad(wdt, ((0, 0), (0, LANES - SSD_HEADS)))
    full = lambda a: pl.BlockSpec(a.shape, lambda i: (0,) * a.ndim)
    rows = lambda n: pl.BlockSpec((IN_ROWS, n), lambda i: (i, 0))
    ins = (h2, g.reshape(1, D_MODEL), wz, wx, wdt_pad, wq, wk, wv)
    out_shape = (
        jax.ShapeDtypeStruct((T, SSD_INNER), F32), jax.ShapeDtypeStruct((T, CONV_CH), F32),
        jax.ShapeDtypeStruct((T, LANES), F32), jax.ShapeDtypeStruct((SSD_HEADS, T), F32),
        jax.ShapeDtypeStruct((T, ATT_INNER), F32), jax.ShapeDtypeStruct((T, ATT_INNER), F32),
        jax.ShapeDtypeStruct((T, ATT_INNER), F32))
    out_specs = (rows(SSD_INNER), rows(CONV_CH), rows(LANES),
                 pl.BlockSpec((SSD_HEADS, IN_ROWS), lambda i: (0, i)),
                 rows(ATT_INNER), rows(ATT_INNER), rows(ATT_INNER))
    return pl.pallas_call(
        _inproj_body, grid=(T // IN_ROWS,),
        in_specs=[rows(D_MODEL)] + [full(a) for a in ins[1:]],
        out_specs=out_specs, out_shape=out_shape,
        compiler_params=_params("arbitrary"), name="inproj")(*ins)


def _ssd_body(z_ref, xbc_ref, dt_ref, dtt_ref, cw_ref, cb_ref, dtb_ref, dtbt_ref, alog_ref,
              alogt_ref, dskip_ref, ng_ref, tril_ref, expand_ref, y_ref, ext_ref, state_ref):
    L, P, N, E = SSD_CHUNK, SSD_HEAD_DIM, SSD_STATE, SSD_HEADS_PER_GROUP
    R = SSD_STEP_CHUNKS * L
    c = pl.program_id(1)

    @pl.when(c == 0)
    def _():
        state_ref[...] = jnp.zeros_like(state_ref)
        ext_ref[0:SUBLANES, :] = jnp.zeros((SUBLANES, CONV_CH), F32)

    ext_ref[SUBLANES:SUBLANES + R, :] = xbc_ref[0]
    tril = tril_ref[...]
    expand = expand_ref[...]
    row = lax.broadcasted_iota(jnp.int32, (L, L), 0)
    col = lax.broadcasted_iota(jnp.int32, (L, L), 1)
    causal = row >= col
    for sub in range(SSD_STEP_CHUNKS):
        _ssd_chunk(sub * L, z_ref, dt_ref, dtt_ref, cw_ref, cb_ref, dtb_ref, dtbt_ref, alog_ref,
                   alogt_ref, dskip_ref, ng_ref, tril, expand, causal, y_ref, ext_ref, state_ref)
    ext_ref[0:SUBLANES, :] = ext_ref[R:R + SUBLANES, :]


def _ssd_chunk(r0, z_ref, dt_ref, dtt_ref, cw_ref, cb_ref, dtb_ref, dtbt_ref, alog_ref, alogt_ref,
               dskip_ref, ng_ref, tril, expand, causal, y_ref, ext_ref, state_ref):
    L, P, N, E = SSD_CHUNK, SSD_HEAD_DIM, SSD_STATE, SSD_HEADS_PER_GROUP
    conv = cb_ref[...]
    for j in range(SSD_CONV):
        off = r0 + SUBLANES - (SSD_CONV - 1) + j
        conv = conv + cw_ref[j:j + 1, :] * ext_ref[off:off + L, :]
    act = _silu(conv)
    xs = act[:, :SSD_INNER]
    b_in = act[:, SSD_INNER:SSD_INNER + SSD_GROUPS * N]
    c_out = act[:, SSD_INNER + SSD_GROUPS * N:]

    dt = _softplus(dt_ref[0, r0:r0 + L, :] + dtb_ref[...])
    a_dt = dt * (-jnp.exp(alog_ref[...]))
    a_cs = jnp.dot(tril, a_dt, precision=HIGHEST, preferred_element_type=F32)
    dt_t = _softplus(dtt_ref[:, r0:r0 + L] + dtbt_ref[...])
    a_dt_t = dt_t * (-jnp.exp(alogt_ref[...]))
    a_cs_t = lax.dot_general(a_dt_t, tril, (((1,), (1,)), ((), ())), precision=HIGHEST,
                             preferred_element_type=F32)
    last = a_cs[L - 1:L, :]
    per_head = jnp.concatenate(
        [dt, jnp.exp(a_cs), jnp.exp(last - a_cs), jnp.broadcast_to(jnp.exp(last), (SUBLANES, LANES))],
        axis=0)
    hi = per_head.astype(BF16)
    lo = (per_head - hi.astype(F32)).astype(BF16)
    on_lanes = jnp.dot(jnp.concatenate([hi, lo], axis=-1), expand, preferred_element_type=F32)
    dt_x = on_lanes[0:L]
    eacs_x = on_lanes[L:2 * L]
    dec_x = on_lanes[2 * L:3 * L]
    cdec_x = on_lanes[3 * L:3 * L + 1]
    xdt = xs * dt_x
    xdtd = (xdt * dec_x).astype(BF16)
    xdt_b = xdt.astype(BF16)

    y_parts = []
    groups = []
    for g in range(SSD_GROUPS):
        bg = b_in[:, g * N:(g + 1) * N].astype(BF16)
        cg = c_out[:, g * N:(g + 1) * N].astype(BF16)
        cb = lax.dot_general(cg, bg, (((1,), (1,)), ((), ())), preferred_element_type=F32)
        gs = slice(g * E * P, (g + 1) * E * P)
        upd = lax.dot_general(bg, xdtd[:, gs], (((0,), (0,)), ((), ())),
                              preferred_element_type=F32)
        groups.append((cg, gs, upd))
        for e in range(E):
            h = g * E + e
            diff = a_cs[:, h:h + 1] - a_cs_t[h:h + 1, :]
            l_mat = jnp.exp(jnp.where(causal, diff, -jnp.inf))
            m = (cb * l_mat).astype(BF16)
            y_parts.append(jnp.dot(m, xdt_b[:, h * P:(h + 1) * P], preferred_element_type=F32))
    y_off = []
    for g, (cg, gs, upd) in enumerate(groups):
        st = state_ref[g]
        y_off.append(jnp.dot(cg, st.astype(BF16), preferred_element_type=F32))
        state_ref[g] = cdec_x[:, gs] * st + upd
    y = (jnp.concatenate(y_parts, axis=-1) + jnp.concatenate(y_off, axis=-1) * eacs_x
         + dskip_ref[...] * xs)
    y = y * _silu(z_ref[0, r0:r0 + L, :])
    y = y * lax.rsqrt(jnp.mean(y * y, axis=-1, keepdims=True) + EPS) * ng_ref[...]
    y_ref[0, r0:r0 + L, :] = y


def _ssd(z, xbc, dt, dtt, conv_w, conv_b, dt_bias, a_log, d_skip, norm_g):
    B, S, _ = z.shape
    L = SSD_CHUNK
    R = SSD_STEP_CHUNKS * L
    nc = S // R
    pad_row = lambda a: jnp.pad(a.reshape(1, SSD_HEADS), ((0, 0), (0, LANES - SSD_HEADS)))
    tril = jnp.asarray(np.tril(np.ones((L, L), np.float32)))
    expand_np = np.zeros((LANES, SSD_INNER), np.float32)
    for h in range(SSD_HEADS):
        expand_np[h, h * SSD_HEAD_DIM:(h + 1) * SSD_HEAD_DIM] = 1.0
    ins = (z, xbc, dt, dtt, conv_w, conv_b.reshape(1, CONV_CH), pad_row(dt_bias),
           dt_bias.reshape(SSD_HEADS, 1), pad_row(a_log), a_log.reshape(SSD_HEADS, 1),
           jnp.repeat(d_skip, SSD_HEAD_DIM).reshape(1, SSD_INNER), norm_g.reshape(1, SSD_INNER),
           tril, jnp.asarray(np.concatenate([expand_np, expand_np], axis=0), dtype=BF16))
    full = lambda a: pl.BlockSpec(a.shape, lambda b, c: (0,) * a.ndim)
    chunk = lambda n: pl.BlockSpec((1, R, n), lambda b, c: (b, c, 0))
    in_specs = [chunk(SSD_INNER), chunk(CONV_CH), chunk(LANES),
                pl.BlockSpec((SSD_HEADS, R), lambda b, c: (0, b * nc + c))]
    in_specs += [full(a) for a in ins[4:]]
    return pl.pallas_call(
        _ssd_body, grid=(B, nc), in_specs=in_specs, out_specs=chunk(SSD_INNER),
        out_shape=jax.ShapeDtypeStruct((B, S, SSD_INNER), F32),
        scratch_shapes=[pltpu.VMEM((R + SUBLANES, CONV_CH), F32),
                        pltpu.VMEM((SSD_GROUPS, SSD_STATE, SSD_HEADS_PER_GROUP * SSD_HEAD_DIM), F32)],
        compiler_params=_params("arbitrary", "arbitrary"), name="ssd")(*ins)


def _bucket_tile():
    bs = MOBA_BLOCK
    dist = np.arange(bs)[:, None] - np.arange(2 * bs)[None, :] + bs
    max_exact = REL_BUCKETS // 2
    d = np.maximum(dist, max_exact).astype(np.float32)
    large = max_exact + (np.log(d / np.float32(max_exact)) / np.float32(math.log(REL_MAX_DIST / max_exact))
                         * np.float32(REL_BUCKETS - max_exact)).astype(np.int32)
    large = np.minimum(large, REL_BUCKETS - 1)
    bucket = np.where(dist < max_exact, dist, large)
    return np.where(dist >= 0, bucket, -1).astype(np.int32)


def _bias_body(rb_ref, bucket_ref, o_ref):
    head = pl.program_id(0)
    bucket = bucket_ref[...]
    far = rb_ref[REL_BUCKETS - 1, head]
    bias = jnp.where(bucket < 0, MASK_VALUE, 0.0)
    for i in range(REL_BUCKETS - 1):
        bias = jnp.where(bucket == i, (rb_ref[i, head] - far) * LOG2E, bias)
    o_ref[0] = bias


def _bias_tiles(rel_bias):
    bs = MOBA_BLOCK
    bucket = jnp.asarray(_bucket_tile())
    return pl.pallas_call(
        _bias_body, grid=(ATT_HEADS,),
        in_specs=[pl.BlockSpec(memory_space=pltpu.SMEM), pl.BlockSpec((bs, 2 * bs), lambda h: (0, 0))],
        out_specs=pl.BlockSpec((1, bs, 2 * bs), lambda h: (h, 0, 0)),
        out_shape=jax.ShapeDtypeStruct((ATT_HEADS, bs, 2 * bs), F32),
        compiler_params=_params("arbitrary"), name="bias_tiles")(rel_bias, bucket)


def _attn_body(q_ref, k_ref, v_ref, qg_ref, kg_ref, bias_ref, o_ref,
               kaug_ref, vaug_ref, qaug_ref, sa_ref, sb_ref, m_ref, acc_ref):
    bs, hd = MOBA_BLOCK, ATT_HEAD_DIM
    nb = k_ref.shape[1] // bs
    jp = pl.program_id(2)
    nt_dims = (((1,), (1,)), ((), ()))

    @pl.when(jp == 0)
    def _():
        S = nb * bs
        pair = 2 * hd
        r = lax.broadcasted_iota(jnp.int32, (pair, pair), 0) // hd
        c = lax.broadcasted_iota(jnp.int32, (pair, pair), 1) // hd
        head_mean = jnp.where(r == c, 1.0 / hd, 0.0).astype(BF16)

        def head_norm(x, gain):
            ms = jnp.dot((x * x).astype(BF16), head_mean, preferred_element_type=F32)
            return x * lax.rsqrt(ms + EPS) * gain

        kn = head_norm(k_ref[0], kg_ref[...])
        kmean = jnp.mean(kn.reshape(nb, bs, pair), axis=1)
        kn_b = kn.astype(BF16)
        v_b = v_ref[0].astype(BF16)
        lane = lax.broadcasted_iota(jnp.int32, (S, pair), 1)
        blk = lax.broadcasted_iota(jnp.int32, (S, pair), 0) // bs
        lane_m = lax.broadcasted_iota(jnp.int32, (nb, pair), 1)
        rowb = lax.broadcasted_iota(jnp.int32, (nb, bs), 0)
        lane_q = lax.broadcasted_iota(jnp.int32, (bs, pair), 1)
        kmean_b = []
        for h in range(2):
            own = (lane // hd) == h
            other0 = (1 - h) * hd
            kaug_ref[h] = jnp.where(own, kn_b, (lane - other0 == blk).astype(BF16))
            vaug_ref[h] = jnp.where(own, v_b, jnp.ones((S, pair), BF16))
            kmean_b.append(jnp.where((lane_m // hd) == h, kmean, 0.0).astype(BF16))
        blocks = [slice(j * bs, (j + 1) * bs) for j in range(nb)]
        qn_all = [head_norm(q_ref[0, rows, :], qg_ref[...]) for rows in blocks]
        masks_all = []
        for j in range(nb):
            qn_b = qn_all[j].astype(BF16)
            masks = []
            for h in range(2):
                if j <= MOBA_TOPK:
                    sel = rowb <= j
                else:
                    gate = lax.dot_general(kmean_b[h], qn_b, nt_dims,
                                           preferred_element_type=F32)
                    beats = []
                    for m in range(j):
                        gm = gate[m:m + 1, :]
                        beats.append(((gm > gate) | ((gm == gate) & (rowb > m))).astype(jnp.int32))
                    while len(beats) > 1:
                        beats = [a + b for a, b in zip(beats[::2], beats[1::2])] + beats[len(beats) & ~1:]
                    sel = ((rowb < j) & (beats[0] < MOBA_TOPK)) | (rowb == j)
                masks.append(jnp.where(sel, 0.0, MASK_VALUE))
            masks_all.append(masks)
        gap = jnp.zeros((hd - nb, bs), F32)
        for j, rows in enumerate(blocks):
            mask_t = jnp.concatenate([masks_all[j][1], gap, masks_all[j][0], gap], axis=0).T
            qs = qn_all[j] * (ATT_HEAD_DIM ** -0.5 * LOG2E)
            for h in range(2):
                qaug_ref[h, rows, :] = jnp.where((lane_q // hd) == h, qs, mask_t).astype(BF16)

    qt = 2 * bs
    q0 = pl.multiple_of(jp * qt, qt)
    all_rows = slice(0, qt)
    late_rows = slice(bs, qt)
    for h in range(2):
        m_ref[h] = jnp.full((qt, 2 * hd), -jnp.inf, F32)
        acc_ref[h] = jnp.zeros((qt, 2 * hd), F32)

    def scores(n, s_ref, rows=all_rows, q_base=q0):
        start = pl.multiple_of(n * bs, bs)
        q_rows = pl.ds(q_base + rows.start, rows.stop - rows.start)
        for h in range(2):
            s_ref[h, rows, :] = lax.dot_general(qaug_ref[h, q_rows, :], kaug_ref[h, pl.ds(start, bs), :],
                                                nt_dims, preferred_element_type=F32)

    def update(n, s_ref, bias, rows=all_rows):
        start = pl.multiple_of(n * bs, bs)
        for r0 in range(rows.start, rows.stop, bs):
            sub = slice(r0, r0 + bs)
            for h in range(2):
                s = s_ref[h, sub, :]
                tile = None if bias is None else bias(h, r0 // bs)
                if tile is not None:
                    s = s + tile
                m_i = m_ref[h, sub, :]
                m_new = jnp.maximum(m_i, jnp.max(s, axis=-1, keepdims=True))
                alpha = jnp.exp2(m_i - m_new)
                p = jnp.exp2((s - jnp.concatenate([m_new, m_new], axis=-1)).astype(BF16))
                pv = jnp.dot(p, vaug_ref[h, pl.ds(start, bs), :], preferred_element_type=F32)
                acc_ref[h, sub, :] = alpha * acc_ref[h, sub, :] + pv
                m_ref[h, sub, :] = m_new

    def far_pair(n):
        scores(n + 1, sb_ref)
        update(n, sa_ref, None)
        scores(n + 2, sa_ref)
        update(n + 1, sb_ref, None)

    def far_quad(i, _):
        far_pair(4 * i)
        far_pair(4 * i + 2)
        return 0

    n_far = jnp.maximum(2 * jp - 1, 0)
    n_loop = jnp.maximum(n_far - 3, 0)
    @pl.when(jp == 0)
    def _():
        scores(0, sa_ref)

    def next_scores():
        q_next = pl.multiple_of(jnp.minimum(jp + 1, nb // 2 - 1) * qt, qt)
        scores(0, sa_ref, q_base=q_next)

    lax.fori_loop(0, n_loop >> 2, far_quad, 0)

    @pl.when((n_loop & 2) != 0)
    def _():
        far_pair(n_loop & ~3)

    prev = lambda h: bias_ref[h, :, 0:bs]
    own = lambda h, qblk=1: bias_ref[h, :, bs:2 * bs]
    bias_p = lambda h, qblk: prev(h) if qblk == 0 else None
    bias_o = lambda h, qblk: own(h) if qblk == 0 else prev(h)
    blk_p = jnp.maximum(2 * jp - 1, 0)
    blk_o = 2 * jp
    blk_n = 2 * jp + 1

    def finish():
        first_half = lax.broadcasted_iota(jnp.int32, (qt, 2 * hd), 1) < hd
        numer = jnp.where(first_half, acc_ref[0], acc_ref[1])
        denom = pltpu.roll(jnp.where(first_half, acc_ref[1], acc_ref[0]), hd, 1)
        o_ref[0] = numer / denom

    def tail():
        scores(blk_p, sb_ref)
        update(n_far - 1, sa_ref, None)
        scores(blk_o, sa_ref)
        update(blk_p, sb_ref, bias_p)
        scores(blk_n, sb_ref, late_rows)
        update(blk_o, sa_ref, bias_o)
        next_scores()
        update(blk_n, sb_ref, own, late_rows)

    @pl.when(jp >= 2)
    def _():
        far_pair(n_far - 3)
        tail()
        finish()

    @pl.when(jp == 1)
    def _():
        tail()
        finish()

    @pl.when(jp == 0)
    def _():
        scores(blk_n, sb_ref, late_rows)
        update(blk_o, sa_ref, bias_o)
        next_scores()
        update(blk_n, sb_ref, own, late_rows)
        finish()


def _attention(q, k, v, q_g, k_g, bias_tiles):
    B, S, _ = q.shape
    bs = MOBA_BLOCK
    nb = S // bs
    assert S % (2 * bs) == 0 and nb <= ATT_HEAD_DIM
    pair = 2 * ATT_HEAD_DIM
    return pl.pallas_call(
        _attn_body, grid=(B, ATT_HEADS // 2, nb // 2),
        in_specs=[
            pl.BlockSpec((1, S, pair), lambda b, hp, i: (b, 0, hp)),
            pl.BlockSpec((1, S, pair), lambda b, hp, i: (b, 0, hp)),
            pl.BlockSpec((1, S, pair), lambda b, hp, i: (b, 0, hp)),
            pl.BlockSpec((1, pair), lambda b, hp, i: (0, 0)),
            pl.BlockSpec((1, pair), lambda b, hp, i: (0, 0)),
            pl.BlockSpec((2, bs, 2 * bs), lambda b, hp, i: (hp, 0, 0)),
        ],
        out_specs=pl.BlockSpec((1, 2 * bs, pair), lambda b, hp, i: (b, i, hp)),
        out_shape=jax.ShapeDtypeStruct((B, S, ATT_INNER), F32),
        scratch_shapes=[pltpu.VMEM((2, S, pair), BF16), pltpu.VMEM((2, S, pair), BF16),
                        pltpu.VMEM((2, S, pair), BF16),
                        pltpu.VMEM((2, 2 * bs, bs), F32), pltpu.VMEM((2, 2 * bs, bs), F32),
                        pltpu.VMEM((2, 2 * bs, pair), F32),
                        pltpu.VMEM((2, 2 * bs, pair), F32)],
        compiler_params=_params("arbitrary", "arbitrary", "arbitrary"), name="moba")(
            q, k, v, jnp.tile(q_g, 2).reshape(1, pair), jnp.tile(k_g, 2).reshape(1, pair), bias_tiles)


def _outproj_body(h_ref, ys_ref, ya_ref, wa_ref, wb_ref, g_ref, wrh_ref, wrl_ref, br_ref, tri_ref,
                  hn_ref, xn_ref, route_ref, routet_ref, counts_ref, carry_ref):
    i = pl.program_id(0)

    @pl.when(i == 0)
    def _():
        carry_ref[...] = jnp.zeros_like(carry_ref)

    _outproj_rows(0, OUT_ROWS, h_ref, ys_ref, ya_ref, wa_ref, wb_ref, g_ref, wrh_ref, wrl_ref,
                  br_ref, tri_ref, hn_ref, xn_ref, route_ref, routet_ref, carry_ref)
    counts_ref[...] = carry_ref[...]


def _outproj_rows(r0, n, h_ref, ys_ref, ya_ref, wa_ref, wb_ref, g_ref, wrh_ref, wrl_ref, br_ref, tri_ref,
                  hn_ref, xn_ref, route_ref, routet_ref, carry_ref):
    rs = slice(r0, r0 + n)
    hn = (h_ref[rs, :]
          + jnp.dot(ys_ref[rs, :].astype(BF16), wa_ref[...], preferred_element_type=F32)
          + jnp.dot(ya_ref[rs, :].astype(BF16), wb_ref[...], preferred_element_type=F32))
    hn_ref[rs, :] = hn
    xn = hn * lax.rsqrt(jnp.mean(hn * hn, axis=-1, keepdims=True) + EPS) * g_ref[...]
    for c in range(SUBLANES):
        xn_ref[pl.ds(r0 * SUBLANES + c, n, stride=SUBLANES), :] = xn[:, c * LANES:(c + 1) * LANES]
    x_hi = xn.astype(BF16)
    x_lo = (xn - x_hi.astype(F32)).astype(BF16)
    hi_terms = jnp.dot(x_hi, wrl_ref[...], preferred_element_type=F32)
    logits = (hi_terms[:, :LANES] + hi_terms[:, LANES:]
              + jnp.dot(x_lo, wrh_ref[...], preferred_element_type=F32)) + br_ref[...]
    lt = logits.T[0:ROUTER_ROWS, :]
    row = lax.broadcasted_iota(jnp.int32, (ROUTER_ROWS, n), 0)
    first = lambda hit: jnp.min(jnp.where(hit, row, LANES), axis=0, keepdims=True)
    l1 = jnp.where(row < MOE_GROUPS, lt, -jnp.inf)
    mx = jnp.max(l1, axis=0, keepdims=True)
    gval = 1.0 / jnp.sum(jnp.exp(l1 - mx), axis=0, keepdims=True)
    gidx = first(l1 == mx)
    e_row = row - ROUTER_LANE0
    in_grp = (e_row >= 0) & (e_row < MOE_EXPERTS) & ((e_row >> 3) == gidx)
    l2 = jnp.where(in_grp, lt, -jnp.inf)
    m1 = jnp.max(l2, axis=0, keepdims=True)
    i1 = first(l2 == m1)
    l2b = jnp.where(row == i1, -jnp.inf, l2)
    m2 = jnp.max(l2b, axis=0, keepdims=True)
    i2 = first(l2b == m2)
    r = jnp.exp(m2 - m1)
    g1 = gval / (1.0 + r)
    g2 = gval * r / (1.0 + r)
    oh1 = (row == i1)
    oh2 = (row == i2)
    both = jnp.where(oh1 | oh2, 1.0, 0.0)
    cx = jnp.dot(both.astype(BF16), tri_ref[...], preferred_element_type=F32) + carry_ref[...]
    rank1 = jnp.sum(jnp.where(oh1, cx, 0.0), axis=0, keepdims=True)
    rank2 = jnp.sum(jnp.where(oh2, cx, 0.0), axis=0, keepdims=True)
    carry_ref[...] = carry_ref[...] + jnp.sum(both, axis=1, keepdims=True)
    routet = jnp.concatenate(
        [(i1 - ROUTER_LANE0).astype(F32), (i2 - ROUTER_LANE0).astype(F32), g1, g2, rank1, rank2,
         jnp.zeros((SUBLANES - 6, n), F32)], axis=0)
    routet_ref[:, rs] = routet
    route_ref[rs, :] = jnp.concatenate([routet, jnp.zeros((LANES - SUBLANES, n), F32)], axis=0).T


def _outproj_route(h2, y_ssd, y_att, w_out, norm_g, w_r1, b_r1, w_r2, b_r2):
    T = h2.shape[0]
    R = OUT_ROWS
    wb = w_out.astype(BF16)
    pad = LANES - MOE_GROUPS - MOE_EXPERTS
    wr = jnp.pad(jnp.concatenate([w_r1, w_r2], axis=1), ((0, 0), (0, pad)))
    wr_hi = wr.astype(BF16)
    wr_lo = (wr - wr_hi.astype(F32)).astype(BF16)
    br = jnp.pad(jnp.concatenate([b_r1, b_r2]), (0, pad)).reshape(1, LANES)
    tri = jnp.asarray(np.triu(np.ones((R, R), np.float32), 1), dtype=BF16)
    ins = (h2, y_ssd, y_att, wb[:SSD_INNER], wb[SSD_INNER:], norm_g.reshape(1, D_MODEL),
           wr_hi, jnp.concatenate([wr_hi, wr_lo], axis=1), br, tri)
    full = lambda a: pl.BlockSpec(a.shape, lambda i: (0,) * a.ndim)
    rows = lambda n: pl.BlockSpec((R, n), lambda i: (i, 0))
    return pl.pallas_call(
        _outproj_body, grid=(T // R,),
        in_specs=[rows(D_MODEL), rows(SSD_INNER), rows(ATT_INNER)] + [full(a) for a in ins[3:]],
        out_specs=(rows(D_MODEL), pl.BlockSpec((R * SUBLANES, LANES), lambda i: (i, 0)), rows(LANES),
                   pl.BlockSpec((SUBLANES, R), lambda i: (0, i)),
                   pl.BlockSpec((ROUTER_ROWS, 1), lambda i: (0, 0))),
        out_shape=(jax.ShapeDtypeStruct((T, D_MODEL), F32),
                   jax.ShapeDtypeStruct((T * SUBLANES, LANES), F32),
                   jax.ShapeDtypeStruct((T, LANES), F32), jax.ShapeDtypeStruct((SUBLANES, T), F32),
                   jax.ShapeDtypeStruct((ROUTER_ROWS, 1), F32)),
        scratch_shapes=[pltpu.VMEM((ROUTER_ROWS, 1), F32)],
        compiler_params=_params("arbitrary"), name="outproj_route")(*ins)


def _slots_body(routet_ref, pstart_ref, dest_ref):
    cols = routet_ref.shape[1]
    expert = lax.broadcasted_iota(jnp.int32, (MOE_EXPERTS, cols), 0).astype(F32)
    pstart = pstart_ref[...]
    for j in range(2):
        hit = expert == routet_ref[j:j + 1, :]
        base = jnp.sum(jnp.where(hit, pstart, 0.0), axis=0, keepdims=True)
        dest_ref[j:j + 1, :] = (base + routet_ref[4 + j:5 + j, :]).astype(jnp.int32)


def _slots(routet, pstarts):
    T = routet.shape[1]
    cols = min(T, 4096)
    assert T % cols == 0
    return pl.pallas_call(
        _slots_body, grid=(T // cols,),
        in_specs=[pl.BlockSpec((SUBLANES, cols), lambda i: (0, i)),
                  pl.BlockSpec((MOE_EXPERTS, 1), lambda i: (0, 0))],
        out_specs=pl.BlockSpec((2, cols), lambda i: (0, i)),
        out_shape=jax.ShapeDtypeStruct((2, T), jnp.int32),
        compiler_params=_params("arbitrary"), name="slots")(
            routet, pstarts.astype(F32).reshape(MOE_EXPERTS, 1))


def _tile_copy(src_ref, src_row, dst_ref, dst_row, sem):
    def tile(ref, row):
        if len(ref.shape) == 3:
            return ref.at[row]
        return ref.at[pl.ds(pl.multiple_of(row * SUBLANES, SUBLANES), SUBLANES)]
    return pltpu.make_async_copy(tile(src_ref, src_row), tile(dst_ref, dst_row), sem)


def _dispatch_body(pad_ref, d1_ref, d2_ref, xn_ref, xb_ref, zero_ref, stage_ref, sem, lsem, ssem):
    n_blocks = xb_ref.shape[0] // EXPERT_ROWS
    n_used = pad_ref[MOE_EXPERTS]

    def zero_copy(row):
        return pltpu.make_async_copy(zero_ref, xb_ref.at[pl.ds(row, EXPERT_ROWS)], sem)

    @pl.when(pl.program_id(0) == 0)
    def _():
        zero_ref[...] = jnp.zeros_like(zero_ref)
        for e in range(MOE_EXPERTS):
            zero_copy(pad_ref[e]).start()
        for e in range(MOE_EXPERTS):
            zero_copy(pad_ref[e]).wait()
        for b in range(n_blocks - MOE_EXPERTS - 1, n_blocks):
            @pl.when(b >= n_used)
            def _():
                zero_copy(b * EXPERT_ROWS).start()
        for b in range(n_blocks - MOE_EXPERTS - 1, n_blocks):
            @pl.when(b >= n_used)
            def _():
                zero_copy(b * EXPERT_ROWS).wait()

    i = pl.program_id(0)
    n = pl.num_programs(0)
    rows = MOVE_ROWS * SUBLANES

    def load(step, slot):
        src = xn_ref.at[pl.ds(pl.multiple_of(step * rows, rows), rows)]
        return pltpu.make_async_copy(src, stage_ref.at[slot], lsem.at[slot])

    def wait_scatters(slot):
        for _ in range(2):
            pltpu.make_async_copy(stage_ref.at[slot], stage_ref.at[slot], ssem.at[slot]).wait()

    @pl.when(i == 0)
    def _():
        load(0, 0).start()

    @pl.when(i >= 1)
    def _():
        wait_scatters((i - 1) & 1)

    @pl.when(i + 1 < n)
    def _():
        load(i + 1, (i + 1) & 1).start()

    slot = i & 1
    load(i, slot).wait()
    base = i * MOVE_ROWS

    def scatter(t, _):
        _tile_copy(stage_ref.at[slot], t, xb_ref, d1_ref[base + t], ssem.at[slot]).start()
        _tile_copy(stage_ref.at[slot], t, xb_ref, d2_ref[base + t], ssem.at[slot]).start(priority=1)
        return 0

    lax.fori_loop(0, MOVE_ROWS, scatter, 0)

    @pl.when(i == n - 1)
    def _():
        wait_scatters(slot)


def _dispatch(xn, dest1, dest2, pad_start, n_slots):
    T = dest1.shape[0]
    grid_spec = pltpu.PrefetchScalarGridSpec(
        num_scalar_prefetch=3, grid=(T // MOVE_ROWS,),
        in_specs=[pl.BlockSpec(memory_space=pl.ANY)],
        out_specs=pl.BlockSpec(memory_space=pl.ANY),
        scratch_shapes=[pltpu.VMEM((EXPERT_ROWS, SUBLANES, LANES), F32),
                        pltpu.VMEM((2, MOVE_ROWS * SUBLANES, LANES), F32),
                        pltpu.SemaphoreType.DMA(()), pltpu.SemaphoreType.DMA((2,)),
                        pltpu.SemaphoreType.DMA((2,))])
    xb = pl.pallas_call(
        _dispatch_body, grid_spec=grid_spec,
        out_shape=jax.ShapeDtypeStruct((n_slots + EXPERT_ROWS, SUBLANES, LANES), F32),
        compiler_params=_params("arbitrary"), name="dispatch")(pad_start, dest1, dest2, xn)
    return xb.reshape(-1, LANES)


def _expert_body(be_ref, nused_ref, xb_ref, wg_ref, wu_ref, wd_ref, yb_ref,
                 x_ref, wgb_ref, wub_ref, wdb_ref):
    i = pl.program_id(0)
    used = i < nused_ref[0]
    new_expert = (i == 0) | (be_ref[i] != be_ref[jnp.maximum(i - 1, 0)])

    @pl.when(used & new_expert)
    def _():
        wgb_ref[...] = wg_ref[0, 0].astype(BF16)
        wub_ref[...] = wu_ref[0, 0].astype(BF16)
        wdb_ref[...] = wd_ref[0, 0].astype(BF16)

    @pl.when(used)
    def _():
        for c in range(SUBLANES):
            x_ref[:, c * LANES:(c + 1) * LANES] = xb_ref[pl.ds(c, EXPERT_ROWS, stride=SUBLANES), :].astype(BF16)
        x = x_ref[...]
        gate = jnp.dot(x, wgb_ref[...], preferred_element_type=F32)
        up = jnp.dot(x, wub_ref[...], preferred_element_type=F32)
        hmid = (_silu(gate) * up).astype(BF16)
        y = jnp.dot(hmid, wdb_ref[...], preferred_element_type=F32)
        for c in range(SUBLANES):
            yb_ref[pl.ds(c, EXPERT_ROWS, stride=SUBLANES), :] = y[:, c * LANES:(c + 1) * LANES]

    @pl.when(jnp.logical_not(used))
    def _():
        yb_ref[...] = jnp.zeros_like(yb_ref)


def _experts(xb, blk_exp, n_used, n_blk, layer, w_gate, w_up, w_down):
    M = EXPERT_ROWS
    tile = lambda index: pl.BlockSpec((M * SUBLANES, LANES), index)
    weight = lambda a: pl.BlockSpec((1, 1) + a.shape[2:], lambda i, be, nu: (layer, be[i], 0, 0))
    grid_spec = pltpu.PrefetchScalarGridSpec(
        num_scalar_prefetch=2, grid=(n_blk,),
        in_specs=[tile(lambda i, be, nu: (jnp.minimum(i, nu[0] - 1), 0)),
                  weight(w_gate), weight(w_up), weight(w_down)],
        out_specs=tile(lambda i, be, nu: (i, 0)),
        scratch_shapes=[pltpu.VMEM((M, D_MODEL), BF16), pltpu.VMEM((D_MODEL, MOE_HIDDEN), BF16),
                        pltpu.VMEM((D_MODEL, MOE_HIDDEN), BF16), pltpu.VMEM((MOE_HIDDEN, D_MODEL), BF16)])
    return pl.pallas_call(
        _expert_body, grid_spec=grid_spec,
        out_shape=jax.ShapeDtypeStruct((n_blk * M * SUBLANES, LANES), F32),
        compiler_params=_params("arbitrary"), name="experts")(
            blk_exp, n_used, xb, w_gate, w_up, w_down)


def _combine_body(d1_ref, d2_ref, h_ref, route_ref, yb_ref, o_ref, buf_ref, sem):
    i = pl.program_id(0)
    n = pl.num_programs(0)

    def gather(step, slot):
        base = step * MOVE_ROWS

        def body(t, _):
            _tile_copy(yb_ref, d1_ref[base + t], buf_ref.at[slot, 0], t, sem.at[slot]).start()
            _tile_copy(yb_ref, d2_ref[base + t], buf_ref.at[slot, 1], t, sem.at[slot]).start(priority=1)
            return 0

        lax.fori_loop(0, MOVE_ROWS, body, 0)

    @pl.when(i == 0)
    def _():
        gather(0, 0)

    @pl.when(i + 1 < n)
    def _():
        gather(i + 1, (i + 1) & 1)

    slot = i & 1
    pltpu.make_async_copy(buf_ref.at[slot], buf_ref.at[slot], sem.at[slot]).wait()
    route = route_ref[...]
    g1 = route[:, 2:3]
    g2 = route[:, 3:4]
    for c in range(SUBLANES):
        cs = slice(c * LANES, (c + 1) * LANES)
        rows = pl.ds(c, MOVE_ROWS, stride=SUBLANES)
        o_ref[:, cs] = h_ref[:, cs] + (buf_ref[slot, 0, rows, :] * g1 + buf_ref[slot, 1, rows, :] * g2)


def _combine(h2, route, yb, dest1, dest2):
    T = h2.shape[0]
    R = MOVE_ROWS
    grid_spec = pltpu.PrefetchScalarGridSpec(
        num_scalar_prefetch=2, grid=(T // R,),
        in_specs=[pl.BlockSpec((R, D_MODEL), lambda i, d1, d2: (i, 0)),
                  pl.BlockSpec((R, LANES), lambda i, d1, d2: (i, 0)), pl.BlockSpec(memory_space=pl.ANY)],
        out_specs=pl.BlockSpec((R, D_MODEL), lambda i, d1, d2: (i, 0)),
        scratch_shapes=[pltpu.VMEM((2, 2, R * SUBLANES, LANES), F32), pltpu.SemaphoreType.DMA((2,))])
    return pl.pallas_call(
        _combine_body, grid_spec=grid_spec, out_shape=jax.ShapeDtypeStruct((T, D_MODEL), F32),
        compiler_params=_params("arbitrary"), name="combine")(dest1, dest2, h2, route, yb)


def _moe(h2, xn, route, routet, counts, layer, w_gate, w_up, w_down):
    T = h2.shape[0]
    M = EXPERT_ROWS
    n_blk = (2 * T) // M + MOE_EXPERTS
    cnt = counts[ROUTER_LANE0:ROUTER_LANE0 + MOE_EXPERTS, 0].astype(jnp.int32)
    padded = (cnt + M - 1) // M * M
    ends = jnp.cumsum(padded)
    pstarts = ends - padded
    n_used = (ends[-1] // M).astype(jnp.int32)
    blk = jnp.minimum(jnp.arange(n_blk, dtype=jnp.int32), n_used - 1) * M
    blk_exp = jnp.sum((ends[None, :] <= blk[:, None]).astype(jnp.int32), axis=1)
    blk_exp = jnp.minimum(blk_exp, MOE_EXPERTS - 1)
    dest = _slots(routet, pstarts)
    pad_info = jnp.concatenate([pstarts + cnt, n_used.reshape(1)])
    xb = _dispatch(xn, dest[0], dest[1], pad_info, n_blk * M)
    yb = _experts(xb, blk_exp, n_used.reshape(1), n_blk, layer, w_gate, w_up, w_down)
    return _combine(h2, route, yb.reshape(-1, SUBLANES, LANES), dest[0], dest[1])


def kernel(x, rel_bias, norm1_g, w_in, conv_w, conv_b, dt_bias, a_log, d_skip, ssd_norm_g,
           q_norm_g, k_norm_g, w_out, norm2_g, w_r1, b_r1, w_r2, b_r2, w_gate, w_up, w_down):
    B, S, D = x.shape
    T = B * S
    h2 = x.reshape(T, D)
    depth = w_in.shape[0]
    bias_tiles = _bias_tiles(rel_bias)
    for l in range(depth):
        z, xbc, dt, dtt, q, k, v = _inproj(h2, norm1_g[l], w_in[l])
        r3 = lambda a: a.reshape(B, S, a.shape[-1])
        y_ssd = _ssd(r3(z), r3(xbc), r3(dt), dtt, conv_w[l], conv_b[l], dt_bias[l], a_log[l],
                     d_skip[l], ssd_norm_g[l])
        y_att = _attention(r3(q), r3(k), r3(v), q_norm_g[l], k_norm_g[l], bias_tiles)
        h2, xn, route, routet, counts = _outproj_route(
            h2, y_ssd.reshape(T, SSD_INNER), y_att.reshape(T, ATT_INNER), w_out[l], norm2_g[l],
            w_r1[l], b_r1[l], w_r2[l], b_r2[l])
        h2 = _moe(h2, xn, route, routet, counts, l, w_gate, w_up, w_down)
    return h2.reshape(B, S, D)
```
